```python
import math
import jax, jax.numpy as jnp
from jax import lax
import numpy as np

D_MODEL = 2048
BATCH = 8
SEQ = 8192
DEPTH = 2

N_MIXERS = 2
N_ATTN = (DEPTH + 1) // 2
N_HGRN = DEPTH // 2

ATTN_HEAD_DIM = 64
ATTN_HEADS = D_MODEL // ATTN_HEAD_DIM
ATTN_KV_HEADS = ATTN_HEADS // 8
ATTN_GROUP = ATTN_HEADS // ATTN_KV_HEADS
WINDOW = 128
BLOCK = 128
ATTN_Q_DIM = ATTN_HEADS * ATTN_HEAD_DIM
ATTN_KV_DIM = ATTN_KV_HEADS * ATTN_HEAD_DIM
ATTN_IN_DIM = ATTN_Q_DIM + 2 * ATTN_KV_DIM
ATTN_SCALE = 1.0 / math.sqrt(ATTN_HEAD_DIM)

HGRN_EXPAND = 128
HGRN_HEADS = D_MODEL // HGRN_EXPAND
HGRN_DK = HGRN_EXPAND
HGRN_DV = D_MODEL // HGRN_HEADS
HGRN_FDIM = HGRN_HEADS * HGRN_DK
HGRN_IDIM = HGRN_HEADS * HGRN_DV
HGRN_IN_DIM = 2 * HGRN_FDIM + 2 * HGRN_IDIM
HGRN_SCALE = 1.0 / math.sqrt(HGRN_DK)
CHUNK = 64

D_FF = 4 * D_MODEL
N_MOD = 6
EPS = 1e-6

kernel_name = "hybrid_swa_hgrn2_block"


def rms_norm(x, gain):
    xf = x.astype(jnp.float32)
    y = xf * lax.rsqrt(jnp.mean(xf * xf, axis=-1, keepdims=True) + EPS)
    return (y * gain.astype(jnp.float32)).astype(x.dtype)


def modulate(h, shift, scale):
    return h * (1.0 + scale[:, None, :]) + shift[:, None, :]


def alibi_slopes(n_heads):
    return jnp.exp2(-8.0 * jnp.arange(1, n_heads + 1, dtype=jnp.float32) / n_heads)


def swa_sink_attention(h, w_in, w_out, q_gain, k_gain, sinks):
    B, T, _ = h.shape
    nb = T // BLOCK
    proj = h @ w_in
    q, k, v = jnp.split(proj, [ATTN_Q_DIM, ATTN_Q_DIM + ATTN_KV_DIM], axis=-1)
    q = rms_norm(q.reshape(B, T, ATTN_HEADS, ATTN_HEAD_DIM), q_gain)
    k = rms_norm(k.reshape(B, T, ATTN_KV_HEADS, ATTN_HEAD_DIM), k_gain)
    v = v.reshape(B, T, ATTN_KV_HEADS, ATTN_HEAD_DIM)
    q = q.reshape(B, nb, BLOCK, ATTN_KV_HEADS, ATTN_GROUP, ATTN_HEAD_DIM)

    def band(a):
        ap = jnp.pad(a, ((0, 0), (BLOCK, 0), (0, 0), (0, 0)))
        ap = ap.reshape(B, nb + 1, BLOCK, ATTN_KV_HEADS, ATTN_HEAD_DIM)
        return jnp.concatenate([ap[:, :-1], ap[:, 1:]], axis=2)

    kb, vb = band(k), band(v)
    logits = jnp.einsum('bnqkgd,bnskd->bnkgqs', q, kb,
                        preferred_element_type=jnp.float32) * ATTN_SCALE

    kpos = jnp.arange(2 * BLOCK)
    dist = (jnp.arange(BLOCK) + BLOCK)[:, None] - kpos[None, :]
    in_band = (dist >= 0) & (dist < WINDOW)
    s_abs = (jnp.arange(nb) * BLOCK - BLOCK)[:, None, None] + kpos[None, None, :]
    valid = in_band[None] & (s_abs >= 0)
    slopes = alibi_slopes(ATTN_HEADS).reshape(ATTN_KV_HEADS, ATTN_GROUP)
    alibi = -slopes[:, :, None, None] * jnp.abs(dist).astype(jnp.float32)
    logits = jnp.where(valid[None, :, None, None], logits + alibi, -jnp.inf)

    sink = jnp.broadcast_to(
        sinks.astype(jnp.float32).reshape(1, 1, ATTN_KV_HEADS, ATTN_GROUP, 1, 1),
        logits.shape[:-1] + (1,))
    probs = jax.nn.softmax(jnp.concatenate([logits, sink], axis=-1), axis=-1)[..., :-1]
    out = jnp.einsum('bnkgqs,bnskd->bnqkgd', probs.astype(vb.dtype), vb)
    return out.reshape(B, T, ATTN_Q_DIM) @ w_out


def hgrn2_mixer(h, w_in, w_out, o_gain, lower_bound):
    B, T, _ = h.shape
    nc = T // CHUNK
    proj = (h @ w_in).astype(jnp.float32)
    q, f, v, g = jnp.split(proj, [HGRN_FDIM, 2 * HGRN_FDIM, 2 * HGRN_FDIM + HGRN_IDIM], axis=-1)
    q = jax.nn.silu(q) * HGRN_SCALE
    forget = lower_bound + (1.0 - lower_bound) * jax.nn.sigmoid(f)
    k = 1.0 - forget
    logf = jnp.log(forget)

    def to_chunks(a, d):
        return a.reshape(B, nc, CHUNK, HGRN_HEADS, d)

    q, k, logf = to_chunks(q, HGRN_DK), to_chunks(k, HGRN_DK), to_chunks(logf, HGRN_DK)
    v = to_chunks(v, HGRN_DV)
    b = jnp.cumsum(logf, axis=2)
    piv = b[:, :, CHUNK // 2 - 1:CHUNK // 2]

    causal = jnp.tril(jnp.ones((CHUNK, CHUNK), dtype=bool))
    a = jnp.einsum('bnchd,bnshd->bnhcs', q * jnp.exp(b - piv), k * jnp.exp(piv - b))
    a = jnp.where(causal, a, 0.0)
    o_intra = jnp.einsum('bnhcs,bnshv->bnchv', a, v)

    b_last = b[:, :, -1]
    upd = jnp.einsum('bnshd,bnshv->nbhdv', k * jnp.exp(b_last[:, :, None] - b), v)
    decay = jnp.exp(b_last).transpose(1, 0, 2, 3)

    def step(state, inp):
        dec, u = inp
        return dec[..., None] * state + u, state

    s0 = jnp.zeros((B, HGRN_HEADS, HGRN_DK, HGRN_DV), jnp.float32)
    _, s_before = lax.scan(step, s0, (decay, upd))
    o_inter = jnp.einsum('bnchd,nbhdv->bnchv', q * jnp.exp(b), s_before)

    o = (o_intra + o_inter).reshape(B, T, HGRN_HEADS, HGRN_DV)
    o = rms_norm(o, o_gain) * jax.nn.silu(g).reshape(B, T, HGRN_HEADS, HGRN_DV)
    return o.reshape(B, T, HGRN_IDIM).astype(h.dtype) @ w_out


def squared_relu_mlp(h, w1, w2):
    a = jax.nn.relu(h @ w1)
    return (a * a) @ w2


def _fwd_setup_inputs(seed: int = 0) -> dict:
    key = jax.random.key(seed)
    ks = jax.random.split(key, 17)
    nrm = jax.random.normal
    f32 = jnp.float32
    x = nrm(ks[0], (BATCH, SEQ, D_MODEL), f32)
    c = nrm(ks[1], (BATCH, D_MODEL), f32)
    mod_w = nrm(ks[2], (DEPTH, D_MODEL, N_MOD * D_MODEL), f32) * (0.5 * D_MODEL ** -0.5)
    mod_b = nrm(ks[3], (DEPTH, N_MOD * D_MODEL), f32) * 0.02
    norm_mix = 1.0 + 0.05 * nrm(ks[4], (DEPTH, D_MODEL), f32)
    norm_mlp = 1.0 + 0.05 * nrm(ks[5], (DEPTH, D_MODEL), f32)
    attn_w_in = nrm(ks[6], (N_ATTN, D_MODEL, ATTN_IN_DIM), f32) * D_MODEL ** -0.5
    attn_w_out = nrm(ks[7], (N_ATTN, ATTN_Q_DIM, D_MODEL), f32) * ATTN_Q_DIM ** -0.5
    attn_q_gain = 1.0 + 0.05 * nrm(ks[8], (N_ATTN, ATTN_HEAD_DIM), f32)
    attn_k_gain = 1.0 + 0.05 * nrm(ks[9], (N_ATTN, ATTN_HEAD_DIM), f32)
    attn_sinks = nrm(ks[10], (N_ATTN, ATTN_HEADS), f32)
    hgrn_w_in = nrm(ks[11], (N_HGRN, D_MODEL, HGRN_IN_DIM), f32) * D_MODEL ** -0.5
    hgrn_w_out = nrm(ks[12], (N_HGRN, HGRN_IDIM, D_MODEL), f32) * HGRN_IDIM ** -0.5
    hgrn_o_gain = 1.0 + 0.05 * nrm(ks[13], (N_HGRN, HGRN_HEADS, HGRN_DV), f32)
    hgrn_lb_logits = 0.5 * nrm(ks[14], (DEPTH, HGRN_FDIM), f32)
    mlp_w1 = nrm(ks[15], (DEPTH, D_MODEL, D_FF), f32) * D_MODEL ** -0.5
    mlp_w2 = nrm(ks[16], (DEPTH, D_FF, D_MODEL), f32) * D_FF ** -0.5
    return {"x": x, "c": c, "mod_w": mod_w, "mod_b": mod_b,
            "norm_mix": norm_mix, "norm_mlp": norm_mlp,
            "attn_w_in": attn_w_in, "attn_w_out": attn_w_out,
            "attn_q_gain": attn_q_gain, "attn_k_gain": attn_k_gain, "attn_sinks": attn_sinks,
            "hgrn_w_in": hgrn_w_in, "hgrn_w_out": hgrn_w_out, "hgrn_o_gain": hgrn_o_gain,
            "hgrn_lb_logits": hgrn_lb_logits, "mlp_w1": mlp_w1, "mlp_w2": mlp_w2}


def _fwd_reference(x, c, mod_w, mod_b, norm_mix, norm_mlp, attn_w_in, attn_w_out,
              attn_q_gain, attn_k_gain, attn_sinks, hgrn_w_in, hgrn_w_out, hgrn_o_gain,
              hgrn_lb_logits, mlp_w1, mlp_w2):
    lb_p = jax.nn.softmax(hgrn_lb_logits.astype(jnp.float32), axis=0)
    lower_bounds = jnp.cumsum(lb_p, axis=0) - lb_p[0]
    cond = jax.nn.silu(c)
    for i in range(DEPTH):
        mod = cond @ mod_w[i] + mod_b[i]
        sh1, sc1, g1, sh2, sc2, g2 = jnp.split(mod, N_MOD, axis=-1)
        h = modulate(rms_norm(x, norm_mix[i]), sh1, sc1)
        j = i // N_MIXERS
        if i % N_MIXERS == 0:
            y = swa_sink_attention(h, attn_w_in[j], attn_w_out[j], attn_q_gain[j],
                                   attn_k_gain[j], attn_sinks[j])
        else:
            y = hgrn2_mixer(h, hgrn_w_in[j], hgrn_w_out[j], hgrn_o_gain[j], lower_bounds[i])
        x = x + g1[:, None, :] * y
        h = modulate(rms_norm(x, norm_mlp[i]), sh2, sc2)
        x = x + g2[:, None, :] * squared_relu_mlp(h, mlp_w1[i], mlp_w2[i])
    return x


import jax as _jax
import jax.numpy as _jnp

TWIN_FORMAT = 'train_step'
FWD_PARAMS = ['x', 'c', 'mod_w', 'mod_b', 'norm_mix', 'norm_mlp', 'attn_w_in', 'attn_w_out', 'attn_q_gain', 'attn_k_gain', 'attn_sinks', 'hgrn_w_in', 'hgrn_w_out', 'hgrn_o_gain', 'hgrn_lb_logits', 'mlp_w1', 'mlp_w2']
TWIN_WEIGHTS = ['mod_w', 'mod_b', 'norm_mix', 'norm_mlp', 'attn_w_in', 'attn_w_out', 'attn_q_gain', 'attn_k_gain', 'attn_sinks', 'hgrn_w_in', 'hgrn_w_out', 'hgrn_o_gain', 'hgrn_lb_logits', 'mlp_w1', 'mlp_w2']
TWIN_DIFF_INPUT = 'x'
TWIN_INPUTS = ['x', 'c', 'mod_w', 'mod_b', 'norm_mix', 'norm_mlp', 'attn_w_in', 'attn_w_out', 'attn_q_gain', 'attn_k_gain', 'attn_sinks', 'hgrn_w_in', 'hgrn_w_out', 'hgrn_o_gain', 'hgrn_lb_logits', 'mlp_w1', 'mlp_w2', 'loss_target', 'm_mod_w', 'm_mod_b', 'm_norm_mix', 'm_norm_mlp', 'm_attn_w_in', 'm_attn_w_out', 'm_attn_q_gain', 'm_attn_k_gain', 'm_attn_sinks', 'm_hgrn_w_in', 'm_hgrn_w_out', 'm_hgrn_o_gain', 'm_hgrn_lb_logits', 'm_mlp_w1', 'm_mlp_w2', 'v_mod_w', 'v_mod_b', 'v_norm_mix', 'v_norm_mlp', 'v_attn_w_in', 'v_attn_w_out', 'v_attn_q_gain', 'v_attn_k_gain', 'v_attn_sinks', 'v_hgrn_w_in', 'v_hgrn_w_out', 'v_hgrn_o_gain', 'v_hgrn_lb_logits', 'v_mlp_w1', 'v_mlp_w2']
TWIN_OUTPUTS = ['loss', 'grad_x', 'grad_mod_w', 'grad_mod_b', 'grad_norm_mix', 'grad_norm_mlp', 'grad_attn_w_in', 'grad_attn_w_out', 'grad_attn_q_gain', 'grad_attn_k_gain', 'grad_attn_sinks', 'grad_hgrn_w_in', 'grad_hgrn_w_out', 'grad_hgrn_o_gain', 'grad_hgrn_lb_logits', 'grad_mlp_w1', 'grad_mlp_w2', 'delta_mod_w', 'delta_mod_b', 'delta_norm_mix', 'delta_norm_mlp', 'delta_attn_w_in', 'delta_attn_w_out', 'delta_attn_q_gain', 'delta_attn_k_gain', 'delta_attn_sinks', 'delta_hgrn_w_in', 'delta_hgrn_w_out', 'delta_hgrn_o_gain', 'delta_hgrn_lb_logits', 'delta_mlp_w1', 'delta_mlp_w2', 'new_m_mod_w', 'new_m_mod_b', 'new_m_norm_mix', 'new_m_norm_mlp', 'new_m_attn_w_in', 'new_m_attn_w_out', 'new_m_attn_q_gain', 'new_m_attn_k_gain', 'new_m_attn_sinks', 'new_m_hgrn_w_in', 'new_m_hgrn_w_out', 'new_m_hgrn_o_gain', 'new_m_hgrn_lb_logits', 'new_m_mlp_w1', 'new_m_mlp_w2', 'new_v_mod_w', 'new_v_mod_b', 'new_v_norm_mix', 'new_v_norm_mlp', 'new_v_attn_w_in', 'new_v_attn_w_out', 'new_v_attn_q_gain', 'new_v_attn_k_gain', 'new_v_attn_sinks', 'new_v_hgrn_w_in', 'new_v_hgrn_w_out', 'new_v_hgrn_o_gain', 'new_v_hgrn_lb_logits', 'new_v_mlp_w1', 'new_v_mlp_w2']
TWIN_LEAF_KINDS = {'loss': 'loss', 'grad_x': 'grad_x', 'grad_mod_w': 'grad_w', 'grad_mod_b': 'grad_w', 'grad_norm_mix': 'grad_w', 'grad_norm_mlp': 'grad_w', 'grad_attn_w_in': 'grad_w', 'grad_attn_w_out': 'grad_w', 'grad_attn_q_gain': 'grad_w', 'grad_attn_k_gain': 'grad_w', 'grad_attn_sinks': 'grad_w', 'grad_hgrn_w_in': 'grad_w', 'grad_hgrn_w_out': 'grad_w', 'grad_hgrn_o_gain': 'grad_w', 'grad_hgrn_lb_logits': 'grad_w', 'grad_mlp_w1': 'grad_w', 'grad_mlp_w2': 'grad_w', 'delta_mod_w': 'delta_w', 'delta_mod_b': 'delta_w', 'delta_norm_mix': 'delta_w', 'delta_norm_mlp': 'delta_w', 'delta_attn_w_in': 'delta_w', 'delta_attn_w_out': 'delta_w', 'delta_attn_q_gain': 'delta_w', 'delta_attn_k_gain': 'delta_w', 'delta_attn_sinks': 'delta_w', 'delta_hgrn_w_in': 'delta_w', 'delta_hgrn_w_out': 'delta_w', 'delta_hgrn_o_gain': 'delta_w', 'delta_hgrn_lb_logits': 'delta_w', 'delta_mlp_w1': 'delta_w', 'delta_mlp_w2': 'delta_w', 'new_m_mod_w': 'new_m', 'new_m_mod_b': 'new_m', 'new_m_norm_mix': 'new_m', 'new_m_norm_mlp': 'new_m', 'new_m_attn_w_in': 'new_m', 'new_m_attn_w_out': 'new_m', 'new_m_attn_q_gain': 'new_m', 'new_m_attn_k_gain': 'new_m', 'new_m_attn_sinks': 'new_m', 'new_m_hgrn_w_in': 'new_m', 'new_m_hgrn_w_out': 'new_m', 'new_m_hgrn_o_gain': 'new_m', 'new_m_hgrn_lb_logits': 'new_m', 'new_m_mlp_w1': 'new_m', 'new_m_mlp_w2': 'new_m', 'new_v_mod_w': 'new_v', 'new_v_mod_b': 'new_v', 'new_v_norm_mix': 'new_v', 'new_v_norm_mlp': 'new_v', 'new_v_attn_w_in': 'new_v', 'new_v_attn_w_out': 'new_v', 'new_v_attn_q_gain': 'new_v', 'new_v_attn_k_gain': 'new_v', 'new_v_attn_sinks': 'new_v', 'new_v_hgrn_w_in': 'new_v', 'new_v_hgrn_w_out': 'new_v', 'new_v_hgrn_o_gain': 'new_v', 'new_v_hgrn_lb_logits': 'new_v', 'new_v_mlp_w1': 'new_v', 'new_v_mlp_w2': 'new_v'}


def _forward(args):
    return _fwd_reference(*[args[k] for k in FWD_PARAMS])


def _output_shape():
    def fwd():
        inp = _fwd_setup_inputs(0)
        return _fwd_reference(*[inp[k] for k in FWD_PARAMS])
    out = _jax.eval_shape(fwd)
    return out.shape, out.dtype

N_MICROBATCH = 1
ADAM_LR = 0.001
ADAM_B1 = 0.9
ADAM_B2 = 0.999
ADAM_EPS = 1e-08
ADAM_WD = 0.01
ADAM_STEP = 10
PER_EXAMPLE_BATCH_AXIS = {'x': 0, 'c': 0, 'loss_target': 0}
SHARED_INPUTS = []
_WEIGHT_DTYPES = {'mod_w': _jnp.float32, 'mod_b': _jnp.float32, 'norm_mix': _jnp.float32, 'norm_mlp': _jnp.float32, 'attn_w_in': _jnp.float32, 'attn_w_out': _jnp.float32, 'attn_q_gain': _jnp.float32, 'attn_k_gain': _jnp.float32, 'attn_sinks': _jnp.float32, 'hgrn_w_in': _jnp.float32, 'hgrn_w_out': _jnp.float32, 'hgrn_o_gain': _jnp.float32, 'hgrn_lb_logits': _jnp.float32, 'mlp_w1': _jnp.float32, 'mlp_w2': _jnp.float32}
MOMENT_SCALE = {'mod_w': 2.441921e+00, 'mod_b': 6.279046e+00, 'norm_mix': 1.029469e+00, 'norm_mlp': 1.168012e+01, 'attn_w_in': 3.770699e-01, 'attn_w_out': 3.682959e-01, 'attn_q_gain': 3.577798e+00, 'attn_k_gain': 3.565577e+00, 'attn_sinks': 4.828703e+00, 'hgrn_w_in': 1.587850e-01, 'hgrn_w_out': 2.174268e-01, 'hgrn_o_gain': 1.365054e+00, 'hgrn_lb_logits': 4.698163e-03, 'mlp_w1': 3.071989e-01, 'mlp_w2': 1.331837e+00}


def _to_microbatches(a, axis):
    t = _jnp.moveaxis(a, axis, 0)
    t = t.reshape((N_MICROBATCH, t.shape[0] // N_MICROBATCH) + t.shape[1:])
    return _jnp.moveaxis(t, 1, axis + 1)


def setup_inputs(seed: int = 0) -> dict:
    inp = _fwd_setup_inputs(seed)
    key = _jax.random.fold_in(_jax.random.key(seed), 7919)
    shape, _ = _output_shape()
    out = dict(inp)
    out["loss_target"] = _jax.random.normal(_jax.random.fold_in(key, 0), shape, _jnp.float32)
    for i, name in enumerate(TWIN_WEIGHTS):
        w = inp[name].astype(_jnp.float32)
        if MOMENT_SCALE is None:
            s = _jnp.sqrt(_jnp.mean(_jnp.square(w)) + 1e-30)
        else:
            s = MOMENT_SCALE[name]
        km, kv = _jax.random.split(_jax.random.fold_in(key, i + 1))
        out[name] = w
        out["m_" + name] = s * _jax.random.normal(km, w.shape, _jnp.float32)
        out["v_" + name] = (s * s) * _jax.random.uniform(kv, w.shape, _jnp.float32, 0.5, 1.5)
    if N_MICROBATCH > 1:
        for name, axis in PER_EXAMPLE_BATCH_AXIS.items():
            out[name] = _to_microbatches(out[name], axis)
    return {'x': out['x'], 'c': out['c'], 'mod_w': out['mod_w'], 'mod_b': out['mod_b'], 'norm_mix': out['norm_mix'], 'norm_mlp': out['norm_mlp'], 'attn_w_in': out['attn_w_in'], 'attn_w_out': out['attn_w_out'], 'attn_q_gain': out['attn_q_gain'], 'attn_k_gain': out['attn_k_gain'], 'attn_sinks': out['attn_sinks'], 'hgrn_w_in': out['hgrn_w_in'], 'hgrn_w_out': out['hgrn_w_out'], 'hgrn_o_gain': out['hgrn_o_gain'], 'hgrn_lb_logits': out['hgrn_lb_logits'], 'mlp_w1': out['mlp_w1'], 'mlp_w2': out['mlp_w2'], 'loss_target': out['loss_target'], 'm_mod_w': out['m_mod_w'], 'm_mod_b': out['m_mod_b'], 'm_norm_mix': out['m_norm_mix'], 'm_norm_mlp': out['m_norm_mlp'], 'm_attn_w_in': out['m_attn_w_in'], 'm_attn_w_out': out['m_attn_w_out'], 'm_attn_q_gain': out['m_attn_q_gain'], 'm_attn_k_gain': out['m_attn_k_gain'], 'm_attn_sinks': out['m_attn_sinks'], 'm_hgrn_w_in': out['m_hgrn_w_in'], 'm_hgrn_w_out': out['m_hgrn_w_out'], 'm_hgrn_o_gain': out['m_hgrn_o_gain'], 'm_hgrn_lb_logits': out['m_hgrn_lb_logits'], 'm_mlp_w1': out['m_mlp_w1'], 'm_mlp_w2': out['m_mlp_w2'], 'v_mod_w': out['v_mod_w'], 'v_mod_b': out['v_mod_b'], 'v_norm_mix': out['v_norm_mix'], 'v_norm_mlp': out['v_norm_mlp'], 'v_attn_w_in': out['v_attn_w_in'], 'v_attn_w_out': out['v_attn_w_out'], 'v_attn_q_gain': out['v_attn_q_gain'], 'v_attn_k_gain': out['v_attn_k_gain'], 'v_attn_sinks': out['v_attn_sinks'], 'v_hgrn_w_in': out['v_hgrn_w_in'], 'v_hgrn_w_out': out['v_hgrn_w_out'], 'v_hgrn_o_gain': out['v_hgrn_o_gain'], 'v_hgrn_lb_logits': out['v_hgrn_lb_logits'], 'v_mlp_w1': out['v_mlp_w1'], 'v_mlp_w2': out['v_mlp_w2']}


def _loss(weights, diff, rest, loss_target):
    with _jax.named_scope("forward"):
        args = {**rest, TWIN_DIFF_INPUT: diff, **{k: w.astype(_WEIGHT_DTYPES[k]) for k, w in weights.items()}}
        y = _forward(args)
    with _jax.named_scope("loss_head"):
        err = _jnp.square(y.astype(_jnp.float32) - loss_target)
        return 0.5 * _jnp.sum(_jnp.mean(err, axis=-1)) if err.ndim else 0.5 * err


def _adamw(w, g, m, v):
    m = ADAM_B1 * m + (1.0 - ADAM_B1) * g
    v = ADAM_B2 * v + (1.0 - ADAM_B2) * _jnp.square(g)
    m_hat = m / (1.0 - ADAM_B1 ** ADAM_STEP)
    v_hat = v / (1.0 - ADAM_B2 ** ADAM_STEP)
    delta = -ADAM_LR * (m_hat / (_jnp.sqrt(v_hat) + ADAM_EPS) + ADAM_WD * w)
    return delta, m, v


def reference(x, c, mod_w, mod_b, norm_mix, norm_mlp, attn_w_in, attn_w_out, attn_q_gain, attn_k_gain, attn_sinks, hgrn_w_in, hgrn_w_out, hgrn_o_gain, hgrn_lb_logits, mlp_w1, mlp_w2, loss_target, m_mod_w, m_mod_b, m_norm_mix, m_norm_mlp, m_attn_w_in, m_attn_w_out, m_attn_q_gain, m_attn_k_gain, m_attn_sinks, m_hgrn_w_in, m_hgrn_w_out, m_hgrn_o_gain, m_hgrn_lb_logits, m_mlp_w1, m_mlp_w2, v_mod_w, v_mod_b, v_norm_mix, v_norm_mlp, v_attn_w_in, v_attn_w_out, v_attn_q_gain, v_attn_k_gain, v_attn_sinks, v_hgrn_w_in, v_hgrn_w_out, v_hgrn_o_gain, v_hgrn_lb_logits, v_mlp_w1, v_mlp_w2):
    given = dict(x=x, c=c, mod_w=mod_w, mod_b=mod_b, norm_mix=norm_mix, norm_mlp=norm_mlp, attn_w_in=attn_w_in, attn_w_out=attn_w_out, attn_q_gain=attn_q_gain, attn_k_gain=attn_k_gain, attn_sinks=attn_sinks, hgrn_w_in=hgrn_w_in, hgrn_w_out=hgrn_w_out, hgrn_o_gain=hgrn_o_gain, hgrn_lb_logits=hgrn_lb_logits, mlp_w1=mlp_w1, mlp_w2=mlp_w2, loss_target=loss_target, m_mod_w=m_mod_w, m_mod_b=m_mod_b, m_norm_mix=m_norm_mix, m_norm_mlp=m_norm_mlp, m_attn_w_in=m_attn_w_in, m_attn_w_out=m_attn_w_out, m_attn_q_gain=m_attn_q_gain, m_attn_k_gain=m_attn_k_gain, m_attn_sinks=m_attn_sinks, m_hgrn_w_in=m_hgrn_w_in, m_hgrn_w_out=m_hgrn_w_out, m_hgrn_o_gain=m_hgrn_o_gain, m_hgrn_lb_logits=m_hgrn_lb_logits, m_mlp_w1=m_mlp_w1, m_mlp_w2=m_mlp_w2, v_mod_w=v_mod_w, v_mod_b=v_mod_b, v_norm_mix=v_norm_mix, v_norm_mlp=v_norm_mlp, v_attn_w_in=v_attn_w_in, v_attn_w_out=v_attn_w_out, v_attn_q_gain=v_attn_q_gain, v_attn_k_gain=v_attn_k_gain, v_attn_sinks=v_attn_sinks, v_hgrn_w_in=v_hgrn_w_in, v_hgrn_w_out=v_hgrn_w_out, v_hgrn_o_gain=v_hgrn_o_gain, v_hgrn_lb_logits=v_hgrn_lb_logits, v_mlp_w1=v_mlp_w1, v_mlp_w2=v_mlp_w2)
    weights = {n: given[n] for n in TWIN_WEIGHTS}
    shared = {n: given[n] for n in SHARED_INPUTS}
    per_example = {n: given[n] for n in ['x', 'c']}
    grad_fn = _jax.value_and_grad(_loss, argnums=(0, 1))

    def one_microbatch(ex, loss_target):
        ex = dict(ex)
        diff = ex.pop(TWIN_DIFF_INPUT)
        return grad_fn(weights, diff, {**shared, **ex}, loss_target)

    if N_MICROBATCH == 1:
        loss, (grad_w, grad_x) = one_microbatch(per_example, given["loss_target"])
    else:
        def body(carry, xs):
            loss_sum, grad_sum = carry
            l_k, (gw_k, gx_k) = one_microbatch(xs[0], xs[1])
            with _jax.named_scope("update"):
                return (loss_sum + l_k, _jax.tree.map(_jnp.add, grad_sum, gw_k)), gx_k

        init = (_jnp.zeros((), _jnp.float32), _jax.tree.map(_jnp.zeros_like, weights))
        (loss, grad_w), grad_x = _jax.lax.scan(body, init, (per_example, given["loss_target"]))
    with _jax.named_scope("update"):
        delta_w, new_m, new_v = {}, {}, {}
        for n in TWIN_WEIGHTS:
            delta_w[n], new_m[n], new_v[n] = _adamw(weights[n], grad_w[n], given["m_" + n], given["v_" + n])
    return (loss, grad_x, *[grad_w[n] for n in TWIN_WEIGHTS], *[delta_w[n] for n in TWIN_WEIGHTS],
            *[new_m[n] for n in TWIN_WEIGHTS], *[new_v[n] for n in TWIN_WEIGHTS])
```

```python
import functools
import math

import jax
import jax.numpy as jnp
from jax import lax
from jax.experimental import pallas as pl
from jax.experimental.pallas import tpu as pltpu

F32, BF16 = jnp.float32, jnp.bfloat16
N_DEV = 8
D_MODEL = 2048
N_MOD = 6
EPS = 1e-6
ATTN_HD, ATTN_HEADS, ATTN_KV, ATTN_GROUP, ATTN_BLOCK = 64, 32, 4, 8, 128
ATTN_SCALE = 1.0 / math.sqrt(ATTN_HD)
HGRN_HEADS, HGRN_DK, HGRN_CHUNK = 16, 128, 64
HGRN_SCALE = 1.0 / math.sqrt(HGRN_DK)
D_FF = 4 * D_MODEL
ADAM_LR, ADAM_B1, ADAM_B2, ADAM_EPS, ADAM_WD, ADAM_STEP = 0.001, 0.9, 0.999, 1e-08, 0.01, 10
NEG_BIG = -1e30
VMEM_LIMIT = 56 * 1024 * 1024
MESH_ID = pl.DeviceIdType.MESH


def _params(*sem):
    return pltpu.CompilerParams(dimension_semantics=sem, vmem_limit_bytes=VMEM_LIMIT)


def _exchange(arrs, *, scatter, vmem, name):
    n = len(arrs)
    space = pltpu.VMEM if vmem else pltpu.HBM
    out_shape = [jax.ShapeDtypeStruct((N_DEV,) + (a.shape[1:] if scatter else a.shape), a.dtype) for a in arrs]

    def body(*refs):
        ins, outs = refs[:n], refs[n:2 * n]
        send_sems, recv_sems, loc_sems = refs[2 * n:]
        x, y, c = lax.axis_index("x"), lax.axis_index("y"), lax.axis_index("c")
        me = 4 * x + 2 * y + c

        def peer(d):
            px = 1 - x if d & 4 else x
            py = 1 - y if d & 2 else y
            pc = 1 - c if d & 1 else c
            return (px, py, pc), 4 * px + 2 * py + pc

        def remote(a, d, land):
            dev, pid = peer(d)
            return pltpu.make_async_remote_copy(
                src_ref=ins[a].at[pid] if scatter else ins[a],
                dst_ref=outs[a].at[pid if land else me],
                send_sem=send_sems.at[a * 7 + d - 1], recv_sem=recv_sems.at[a * 7 + d - 1],
                device_id=dev, device_id_type=MESH_ID)

        local = [pltpu.make_async_copy(ins[a].at[me] if scatter else ins[a], outs[a].at[me], loc_sems.at[a])
                 for a in range(n)]
        for cp in local:
            cp.start()
        for d in range(1, N_DEV):
            for a in range(n):
                remote(a, d, False).start()
        for d in range(1, N_DEV):
            for a in range(n):
                remote(a, d, True).wait()
        for cp in local:
            cp.wait()

    return pl.pallas_call(
        body, name=name, out_shape=out_shape,
        in_specs=[pl.BlockSpec(memory_space=space)] * n,
        out_specs=[pl.BlockSpec(memory_space=space)] * n,
        scratch_shapes=[pltpu.SemaphoreType.DMA((7 * n,)), pltpu.SemaphoreType.DMA((7 * n,)),
                        pltpu.SemaphoreType.DMA((n,))],
        compiler_params=pltpu.CompilerParams(vmem_limit_bytes=VMEM_LIMIT),
    )(*arrs)


_DIMS = {"nn": (((1,), (0,)), ((), ())), "nt": (((1,), (1,)), ((), ())), "tn": (((0,), (0,)), ((), ()))}


def _matmul(a, b, *, dims, tm, tn, tk, name, epilogue="plain", out_dtype=F32, extras=()):
    if dims == "tn":
        (K, M), N = a.shape, b.shape[1]
    else:
        (M, K), N = a.shape, (b.shape[1] if dims == "nn" else b.shape[0])
    tm, tn, tk = min(tm, M), min(tn, N), min(tk, K)
    assert M % tm == 0 and N % tn == 0 and K % tk == 0, (name, M, N, K, tm, tn, tk)
    if dims == "tn":
        a_spec = pl.BlockSpec((tk, tm), lambda i, j, k: (k, i))
    else:
        a_spec = pl.BlockSpec((tm, tk), lambda i, j, k: (i, k))
    if dims == "nt":
        b_spec = pl.BlockSpec((tn, tk), lambda i, j, k: (j, k))
    else:
        b_spec = pl.BlockSpec((tk, tn), lambda i, j, k: (k, j))
    nk = K // tk
    tile = pl.BlockSpec((tm, tn), lambda i, j, k: (i, j))
    row = pl.BlockSpec((1, tn), lambda i, j, k: (0, j))
    if epilogue == "plain":
        extra_specs, out_shape, out_specs = [], [jax.ShapeDtypeStruct((M, N), out_dtype)], [tile]
    elif epilogue == "relu2":
        extra_specs, out_shape, out_specs = [], [jax.ShapeDtypeStruct((M, N), BF16)] * 2, [tile, tile]
    elif epilogue == "resgate":
        extra_specs, out_shape, out_specs = [tile, row], [jax.ShapeDtypeStruct((M, N), F32)] * 2, [tile, tile]
    elif epilogue == "mul2a":
        extra_specs, out_shape, out_specs = [tile], [jax.ShapeDtypeStruct((M, N), BF16)], [tile]
    else:
        raise ValueError(epilogue)
    n_extra = len(extra_specs)

    def body(a_ref, b_ref, *rest):
        ex, outs, acc_ref = rest[:n_extra], rest[n_extra:-1], rest[-1]
        k = pl.program_id(2)

        @pl.when(k == 0)
        def _():
            acc_ref[...] = jnp.zeros_like(acc_ref)

        acc_ref[...] += lax.dot_general(a_ref[...], b_ref[...], _DIMS[dims], preferred_element_type=F32)

        @pl.when(k == nk - 1)
        def _():
            acc = acc_ref[...]
            if epilogue == "plain":
                outs[0][...] = acc.astype(out_dtype)
            elif epilogue == "relu2":
                act = jnp.maximum(acc, 0.0)
                outs[0][...] = act.astype(BF16)
                outs[1][...] = (act * act).astype(BF16)
            elif epilogue == "resgate":
                outs[0][...] = acc
                outs[1][...] = ex[0][...] + ex[1][...] * acc
            else:
                outs[0][...] = (acc * (2.0 * ex[0][...].astype(F32))).astype(BF16)

    res = pl.pallas_call(
        body, name=name, grid=(M // tm, N // tn, nk), out_shape=out_shape,
        in_specs=[a_spec, b_spec] + extra_specs, out_specs=out_specs,
        scratch_shapes=[pltpu.VMEM((tm, tn), F32)],
        compiler_params=_params("parallel", "parallel", "arbitrary"),
    )(a, b, *extras)
    return res[0] if len(res) == 1 else res


def _row_tile(T):
    return min(T, 256)


def _norm_mod_fwd(x, gain, sc, sh, name):
    T, D = x.shape
    tr = _row_tile(T)

    def body(x_ref, g_ref, sc_ref, sh_ref, h_ref):
        xv = x_ref[...]
        r = lax.rsqrt(jnp.mean(xv * xv, axis=-1, keepdims=True) + EPS)
        hn = (xv * r) * g_ref[...]
        h_ref[...] = (hn * (1.0 + sc_ref[...]) + sh_ref[...]).astype(BF16)

    vec = pl.BlockSpec((1, D), lambda i: (0, 0))
    return pl.pallas_call(
        body, name=name, grid=(T // tr,), out_shape=jax.ShapeDtypeStruct((T, D), BF16),
        in_specs=[pl.BlockSpec((tr, D), lambda i: (i, 0)), vec, vec, vec],
        out_specs=pl.BlockSpec((tr, D), lambda i: (i, 0)),
        compiler_params=_params("parallel"),
    )(x, gain, sc, sh)


def _norm_mod_bwd(x, dh, dres, gain, sc, name):
    T, D = x.shape
    tr = _row_tile(T)

    def body(x_ref, dh_ref, dres_ref, g_ref, sc_ref, dx_ref, st_ref):
        xv, dh_v, gain_v = x_ref[...], dh_ref[...], g_ref[...]
        r = lax.rsqrt(jnp.mean(xv * xv, axis=-1, keepdims=True) + EPS)
        xn = xv * r
        hn = xn * gain_v
        dhn = dh_v * (1.0 + sc_ref[...])
        dxn = dhn * gain_v
        dx_ref[...] = dres_ref[...] + r * (dxn - xn * jnp.mean(dxn * xn, axis=-1, keepdims=True))

        @pl.when(pl.program_id(0) == 0)
        def _():
            st_ref[...] = jnp.zeros_like(st_ref)

        st_ref[0:1, :] += jnp.sum(dh_v, axis=0, keepdims=True)
        st_ref[1:2, :] += jnp.sum(dh_v * hn, axis=0, keepdims=True)
        st_ref[2:3, :] += jnp.sum(dhn * xn, axis=0, keepdims=True)

    vec = pl.BlockSpec((1, D), lambda i: (0, 0))
    blk = pl.BlockSpec((tr, D), lambda i: (i, 0))
    return pl.pallas_call(
        body, name=name, grid=(T // tr,),
        out_shape=[jax.ShapeDtypeStruct((T, D), F32), jax.ShapeDtypeStruct((8, D), F32)],
        in_specs=[blk, blk, blk, vec, vec],
        out_specs=[blk, pl.BlockSpec((8, D), lambda i: (0, 0))],
        compiler_params=_params("arbitrary"),
    )(x, dh, dres, gain, sc)


def _gate_bwd(dx, y, gate, name):
    T, D = dx.shape
    tr = _row_tile(T)

    def body(dx_ref, y_ref, g_ref, dy_ref, st_ref):
        dxv = dx_ref[...]
        dy_ref[...] = (dxv * g_ref[...]).astype(BF16)

        @pl.when(pl.program_id(0) == 0)
        def _():
            st_ref[...] = jnp.zeros_like(st_ref)

        st_ref[0:1, :] += jnp.sum(dxv * y_ref[...], axis=0, keepdims=True)

    blk = pl.BlockSpec((tr, D), lambda i: (i, 0))
    return pl.pallas_call(
        body, name=name, grid=(T // tr,),
        out_shape=[jax.ShapeDtypeStruct((T, D), BF16), jax.ShapeDtypeStruct((8, D), F32)],
        in_specs=[blk, blk, pl.BlockSpec((1, D), lambda i: (0, 0))],
        out_specs=[blk, pl.BlockSpec((8, D), lambda i: (0, 0))],
        compiler_params=_params("arbitrary"),
    )(dx, y, gate)


def _loss_head(y, target, name):
    T, D = y.shape
    tr = _row_tile(T)

    def body(y_ref, t_ref, dy_ref, l_ref):
        err = y_ref[...] - t_ref[...]
        dy_ref[...] = err * (1.0 / D)

        @pl.when(pl.program_id(0) == 0)
        def _():
            l_ref[...] = jnp.zeros_like(l_ref)

        part = jnp.sum(jnp.mean(err * err, axis=-1, keepdims=True), axis=0, keepdims=True)
        l_ref[...] += jnp.broadcast_to(0.5 * part, l_ref.shape)

    blk = pl.BlockSpec((tr, D), lambda i: (i, 0))
    return pl.pallas_call(
        body, name=name, grid=(T // tr,),
        out_shape=[jax.ShapeDtypeStruct((T, D), F32), jax.ShapeDtypeStruct((8, 128), F32)],
        in_specs=[blk, blk], out_specs=[blk, pl.BlockSpec((8, 128), lambda i: (0, 0))],
        compiler_params=_params("arbitrary"),
    )(y, target)


def _silu(v):
    return v * jax.nn.sigmoid(v)


def _mod_fwd(c_all, mod_w, mod_b_cols, name):
    L, D, n = mod_w.shape
    tn = 512

    def body(c_ref, w_ref, b_ref, o_ref):
        cond = _silu(c_ref[...]).astype(BF16)
        o_ref[0] = jnp.dot(cond, w_ref[0].astype(BF16), preferred_element_type=F32) + b_ref[0]

    return pl.pallas_call(
        body, name=name, grid=(L, n // tn), out_shape=jax.ShapeDtypeStruct((L, N_DEV, n), F32),
        in_specs=[pl.BlockSpec((N_DEV, D), lambda l, j: (0, 0)), pl.BlockSpec((1, D, tn), lambda l, j: (l, 0, j)),
                  pl.BlockSpec((1, 1, tn), lambda l, j: (l, 0, j))],
        out_specs=pl.BlockSpec((1, N_DEV, tn), lambda l, j: (l, 0, j)),
        compiler_params=_params("parallel", "parallel"),
    )(c_all, mod_w, mod_b_cols)


def _adamw(g, w, m, v):
    m = ADAM_B1 * m + (1.0 - ADAM_B1) * g
    v = ADAM_B2 * v + (1.0 - ADAM_B2) * (g * g)
    m_hat = m / (1.0 - ADAM_B1 ** ADAM_STEP)
    v_hat = v / (1.0 - ADAM_B2 ** ADAM_STEP)
    delta = -ADAM_LR * (m_hat / (jnp.sqrt(v_hat) + ADAM_EPS) + ADAM_WD * w)
    return delta, m, v


def _mod_w_update(c_t, dmod_cols, w, m, v, name):
    L, D, n = w.shape
    tr = 256

    def body(c_ref, dm_ref, w_ref, m_ref, v_ref, g_ref, d_ref, nm_ref, nv_ref):
        cond = _silu(c_ref[...])
        dm = dm_ref[0]
        g = cond[:, 0:1] * dm[0:1, :]
        for b in range(1, N_DEV):
            g = g + cond[:, b:b + 1] * dm[b:b + 1, :]
        delta, nm, nv = _adamw(g, w_ref[0], m_ref[0], v_ref[0])
        g_ref[0], d_ref[0], nm_ref[0], nv_ref[0] = g, delta, nm, nv

    blk = pl.BlockSpec((1, tr, n), lambda l, i: (l, i, 0))
    return pl.pallas_call(
        body, name=name, grid=(L, D // tr), out_shape=[jax.ShapeDtypeStruct(w.shape, F32)] * 4,
        in_specs=[pl.BlockSpec((tr, N_DEV), lambda l, i: (i, 0)), pl.BlockSpec((1, N_DEV, n), lambda l, i: (l, 0, 0)),
                  blk, blk, blk],
        out_specs=[blk] * 4, compiler_params=_params("parallel", "parallel"),
    )(c_t, dmod_cols, w, m, v)


def _sum_adamw(parts, w, m, v, name):
    R, C = w.shape
    tr = min(R, 256 if C >= 1024 else 1024)
    assert R % tr == 0

    def body(p_ref, w_ref, m_ref, v_ref, g_ref, d_ref, nm_ref, nv_ref):
        g = p_ref[0].astype(F32)
        for s in range(1, N_DEV):
            g = g + p_ref[s].astype(F32)
        delta, nm, nv = _adamw(g, w_ref[...], m_ref[...], v_ref[...])
        g_ref[...], d_ref[...], nm_ref[...], nv_ref[...] = g, delta, nm, nv

    blk = pl.BlockSpec((tr, C), lambda i: (i, 0))
    return pl.pallas_call(
        body, name=name, grid=(R // tr,), out_shape=[jax.ShapeDtypeStruct((R, C), F32)] * 4,
        in_specs=[pl.BlockSpec((N_DEV, tr, C), lambda i: (0, i, 0)), blk, blk, blk], out_specs=[blk] * 4,
        compiler_params=_params("parallel"),
    )(parts, w, m, v)


def _lower_bound_row1(l0, l1):
    mx = lax.stop_gradient(jnp.maximum(l0, l1))
    e0, e1 = jnp.exp(l0 - mx), jnp.exp(l1 - mx)
    p0, p1 = e0 / (e0 + e1), e1 / (e0 + e1)
    return (p0 + p1) - p0


def _lb_fwd(logits, name):
    def body(l_ref, o_ref):
        o_ref[...] = _lower_bound_row1(l_ref[0:1, :], l_ref[1:2, :])

    return pl.pallas_call(body, name=name, out_shape=jax.ShapeDtypeStruct((1, logits.shape[1]), F32))(logits)


def _lb_bwd(logits, dlb, name):
    def body(l_ref, d_ref, o_ref):
        _, vjp = jax.vjp(_lower_bound_row1, l_ref[0:1, :], l_ref[1:2, :])
        d0, d1 = vjp(d_ref[...])
        o_ref[0:1, :] = d0
        o_ref[1:2, :] = d1

    return pl.pallas_call(body, name=name, out_shape=jax.ShapeDtypeStruct(logits.shape, F32))(logits, dlb)


def _bdot(a, b, dims):
    return lax.dot_general(a.astype(BF16), b.astype(BF16), _DIMS[dims], preferred_element_type=F32)


def _attn_probs(q_ref, kp_ref, kc_ref, qg, kg, slope, sink, n):
    rows = ATTN_GROUP * ATTN_BLOCK
    q = q_ref[...].reshape(rows, ATTN_HD)
    rq = lax.rsqrt(jnp.mean(q * q, axis=-1, keepdims=True) + EPS)
    qhat = q * rq
    k = jnp.concatenate([kp_ref[0], kc_ref[0]], axis=0)
    rk = lax.rsqrt(jnp.mean(k * k, axis=-1, keepdims=True) + EPS)
    khat = k * rk
    qn, kn = qhat * qg, khat * kg
    s = _bdot(qn, kn, "nt") * ATTN_SCALE
    qi = lax.broadcasted_iota(jnp.int32, s.shape, 0) & (ATTN_BLOCK - 1)
    ki = lax.broadcasted_iota(jnp.int32, s.shape, 1)
    dist = qi + ATTN_BLOCK - ki
    first_key = jnp.where(n > 0, 0, ATTN_BLOCK)
    valid = (dist >= 0) & (dist < ATTN_BLOCK) & (ki >= first_key)
    s = jnp.where(valid, s - slope * jnp.abs(dist).astype(F32), NEG_BIG)
    mx = jnp.maximum(jnp.max(s, axis=-1, keepdims=True), sink)
    e = jnp.exp(s - mx)
    es = jnp.exp(sink - mx)
    den = jnp.sum(e, axis=-1, keepdims=True) + es
    return e / den, es / den, (qhat, rq, qn), (khat, rk, kn)


def _attn_specs(T):
    nb = T // ATTN_BLOCK
    qspec = pl.BlockSpec((ATTN_GROUP, ATTN_BLOCK, ATTN_HD), lambda h, n: (h, n, 0))
    prev = pl.BlockSpec((1, ATTN_BLOCK, ATTN_HD), lambda h, n: (h, jnp.maximum(n - 1, 0), 0))
    cur = pl.BlockSpec((1, ATTN_BLOCK, ATTN_HD), lambda h, n: (h, n, 0))
    gain = pl.BlockSpec((1, ATTN_HD), lambda h, n: (0, 0))
    col = pl.BlockSpec((1, ATTN_GROUP * ATTN_BLOCK, 1), lambda h, n: (h, 0, 0))
    return nb, qspec, prev, cur, gain, col


def _attn_fwd(qh, kh, vh, qg, kg, slope_col, sink_col, name):
    T = qh.shape[1]
    nb, qspec, prev, cur, gain, col = _attn_specs(T)

    def body(q_ref, kp_ref, kc_ref, vp_ref, vc_ref, qg_ref, kg_ref, sl_ref, sk_ref, o_ref):
        n = pl.program_id(1)
        p, _, _, _ = _attn_probs(q_ref, kp_ref, kc_ref, qg_ref[...], kg_ref[...], sl_ref[0], sk_ref[0], n)
        v = jnp.concatenate([vp_ref[0], vc_ref[0]], axis=0)
        o_ref[...] = _bdot(p, v, "nn").reshape(ATTN_GROUP, ATTN_BLOCK, ATTN_HD).astype(BF16)

    return pl.pallas_call(
        body, name=name, grid=(ATTN_KV, nb), out_shape=jax.ShapeDtypeStruct(qh.shape, BF16),
        in_specs=[qspec, prev, cur, prev, cur, gain, gain, col, col], out_specs=qspec,
        compiler_params=_params("parallel", "parallel"),
    )(qh, kh, kh, vh, vh, qg, kg, slope_col, sink_col)


def _rms_bwd(dy, xhat, r, gain):
    dxh = dy * gain
    dx = r * (dxh - xhat * jnp.mean(dxh * xhat, axis=-1, keepdims=True))
    return dx, jnp.sum(dy * xhat, axis=0, keepdims=True)


def _attn_bwd(qh, kh, vh, doh, qg, kg, slope_col, sink_col, name):
    T = qh.shape[1]
    nb, qspec, prev, cur, gain, col = _attn_specs(T)
    rows = ATTN_GROUP * ATTN_BLOCK

    def body(q_ref, kp_ref, kc_ref, vp_ref, vc_ref, do_ref, qg_ref, kg_ref, sl_ref, sk_ref,
             dq_ref, dk_ref, dv_ref, dqg_ref, dkg_ref, dsk_ref, sk_acc):
        h, n = pl.program_id(0), pl.program_id(1)
        qg_v, kg_v = qg_ref[...], kg_ref[...]
        p, ps, (qhat, rq, qn), (khat, rk, kn) = _attn_probs(q_ref, kp_ref, kc_ref, qg_v, kg_v, sl_ref[0], sk_ref[0], n)
        v = jnp.concatenate([vp_ref[0], vc_ref[0]], axis=0)
        do = do_ref[...].reshape(rows, ATTN_HD)
        dp = _bdot(do, v, "nt")
        delta = jnp.sum(p * dp, axis=-1, keepdims=True)
        ds = p * (dp - delta)
        dqn = _bdot(ds, kn, "nn") * ATTN_SCALE
        dkn = _bdot(ds, qn, "tn") * ATTN_SCALE
        dv = _bdot(p, do, "tn")
        dq, dqg = _rms_bwd(dqn, qhat, rq, qg_v)
        dk, dkg = _rms_bwd(dkn, khat, rk, kg_v)
        dq_ref[...] = dq.reshape(ATTN_GROUP, ATTN_BLOCK, ATTN_HD).astype(BF16)

        @pl.when((h == 0) & (n == 0))
        def _():
            dqg_ref[...] = jnp.zeros_like(dqg_ref)
            dkg_ref[...] = jnp.zeros_like(dkg_ref)

        dqg_ref[0:1, :] += dqg
        dkg_ref[0:1, :] += dkg

        @pl.when(n == 0)
        def _():
            dk_ref[...] = jnp.zeros_like(dk_ref)
            dv_ref[...] = jnp.zeros_like(dv_ref)
            sk_acc[...] = jnp.zeros_like(sk_acc)
            dk_ref[0, 0:ATTN_BLOCK, :] += dk[ATTN_BLOCK:, :]
            dv_ref[0, 0:ATTN_BLOCK, :] += dv[ATTN_BLOCK:, :]

        @pl.when(n > 0)
        def _():
            band = pl.ds(pl.multiple_of((n - 1) * ATTN_BLOCK, ATTN_BLOCK), 2 * ATTN_BLOCK)
            dk_ref[0, band, :] += dk
            dv_ref[0, band, :] += dv

        sk_acc[...] += -ps * delta

        @pl.when(n == nb - 1)
        def _():
            for g in range(ATTN_GROUP):
                tot = jnp.sum(sk_acc[g * ATTN_BLOCK:(g + 1) * ATTN_BLOCK, :], axis=0, keepdims=True)
                dsk_ref[g:g + 1, :] = jnp.broadcast_to(tot, (1, 128))

    kv_full = pl.BlockSpec((1, T, ATTN_HD), lambda h, n: (h, 0, 0))
    acc = pl.BlockSpec((8, ATTN_HD), lambda h, n: (0, 0))
    return pl.pallas_call(
        body, name=name, grid=(ATTN_KV, nb),
        out_shape=[jax.ShapeDtypeStruct(qh.shape, BF16), jax.ShapeDtypeStruct(kh.shape, F32),
                   jax.ShapeDtypeStruct(kh.shape, F32), jax.ShapeDtypeStruct((8, ATTN_HD), F32),
                   jax.ShapeDtypeStruct((8, ATTN_HD), F32), jax.ShapeDtypeStruct((ATTN_HEADS, 128), F32)],
        in_specs=[qspec, prev, cur, prev, cur, qspec, gain, gain, col, col],
        out_specs=[qspec, kv_full, kv_full, acc, acc, pl.BlockSpec((ATTN_GROUP, 128), lambda h, n: (h, 0))],
        scratch_shapes=[pltpu.VMEM((rows, 1), F32)],
        compiler_params=_params("arbitrary", "arbitrary"),
    )(qh, kh, kh, vh, vh, doh, qg, kg, slope_col, sink_col)


@functools.partial(jax.custom_vjp, nondiff_argnums=(2,))
def _mm(a, b, dims):
    return _bdot(a, b, dims)


def _mm_fwd(a, b, dims):
    return _bdot(a, b, dims), (a, b)


def _mm_bwd(dims, res, ct):
    a, b = res
    if dims == "nn":
        return _bdot(ct, b, "nt"), _bdot(a, ct, "tn")
    if dims == "nt":
        return _bdot(ct, b, "nn"), _bdot(ct, a, "tn")
    return _bdot(b, ct, "nt"), _bdot(a, ct, "nn")


_mm.defvjp(_mm_fwd, _mm_bwd)


def _hgrn_chunk(st, qr, fr, v, gr, lb, og):
    row = lax.broadcasted_iota(jnp.int32, qr.shape, 0)
    causal = lax.broadcasted_iota(jnp.int32, (HGRN_CHUNK, HGRN_CHUNK), 0) >= lax.broadcasted_iota(
        jnp.int32, (HGRN_CHUNK, HGRN_CHUNK), 1)
    forget = lb + (1.0 - lb) * jax.nn.sigmoid(fr)
    k = 1.0 - forget
    logf = jnp.log(forget)
    b = jnp.dot(causal.astype(F32), logf, precision=lax.Precision.HIGHEST, preferred_element_type=F32)
    piv = jnp.sum(jnp.where(row == HGRN_CHUNK // 2 - 1, b, 0.0), axis=0, keepdims=True)
    b_last = jnp.sum(jnp.where(row == HGRN_CHUNK - 1, b, 0.0), axis=0, keepdims=True)
    q = _silu(qr) * HGRN_SCALE
    a = jnp.where(causal, _mm(q * jnp.exp(b - piv), k * jnp.exp(piv - b), "nt"), 0.0)
    o = _mm(a, v, "nn") + _mm(q * jnp.exp(b), st, "nt")
    st_new = st * jnp.exp(b_last) + _mm(v, k * jnp.exp(b_last - b), "tn")
    y = (o * lax.rsqrt(jnp.mean(o * o, axis=-1, keepdims=True) + EPS)) * og * _silu(gr)
    return y, st_new


def _hgrn_tile(T):
    return min(T, 512)


def _hgrn_fwd(proj, lb, og, name):
    T = proj.shape[0]
    tb = _hgrn_tile(T)
    ncb = tb // HGRN_CHUNK

    def body(q_ref, f_ref, v_ref, g_ref, lb_ref, og_ref, o_ref, s_ref, st_ref):
        @pl.when(pl.program_id(1) == 0)
        def _():
            st_ref[...] = jnp.zeros_like(st_ref)

        st = st_ref[...]
        for ci in range(ncb):
            sl = slice(ci * HGRN_CHUNK, (ci + 1) * HGRN_CHUNK)
            s_ref[0, ci] = st
            y, st = _hgrn_chunk(st, q_ref[sl, :], f_ref[sl, :], v_ref[sl, :], g_ref[sl, :], lb_ref[0], og_ref[0])
            o_ref[sl, :] = y.astype(BF16)
        st_ref[...] = st

    part = lambda p: pl.BlockSpec((tb, HGRN_DK), lambda h, t: (t, p * HGRN_HEADS + h))
    vec = pl.BlockSpec((1, 1, HGRN_DK), lambda h, t: (h, 0, 0))
    return pl.pallas_call(
        body, name=name, grid=(HGRN_HEADS, T // tb),
        out_shape=[jax.ShapeDtypeStruct((T, D_MODEL), BF16),
                   jax.ShapeDtypeStruct((HGRN_HEADS, T // HGRN_CHUNK, HGRN_DK, HGRN_DK), F32)],
        in_specs=[part(0), part(1), part(2), part(3), vec, vec],
        out_specs=[pl.BlockSpec((tb, HGRN_DK), lambda h, t: (t, h)),
                   pl.BlockSpec((1, ncb, HGRN_DK, HGRN_DK), lambda h, t: (h, t, 0, 0))],
        scratch_shapes=[pltpu.VMEM((HGRN_DK, HGRN_DK), F32)],
        compiler_params=_params("parallel", "arbitrary"),
    )(proj, proj, proj, proj, lb, og)


def _hgrn_bwd(proj, states, do, lb, og, name):
    T = proj.shape[0]
    tb = _hgrn_tile(T)
    ncb, nt = tb // HGRN_CHUNK, T // tb

    def body(q_ref, f_ref, v_ref, g_ref, s_ref, do_ref, lb_ref, og_ref,
             dq_ref, df_ref, dv_ref, dg_ref, dlb_ref, dog_ref, dst_ref):
        @pl.when(pl.program_id(1) == 0)
        def _():
            dst_ref[...] = jnp.zeros_like(dst_ref)
            dlb_ref[...] = jnp.zeros_like(dlb_ref)
            dog_ref[...] = jnp.zeros_like(dog_ref)

        dst = dst_ref[...]
        dlb, dog = jnp.zeros((1, HGRN_DK), F32), jnp.zeros((1, HGRN_DK), F32)
        for ci in reversed(range(ncb)):
            sl = slice(ci * HGRN_CHUNK, (ci + 1) * HGRN_CHUNK)
            _, vjp = jax.vjp(_hgrn_chunk, s_ref[0, ci], q_ref[sl, :], f_ref[sl, :], v_ref[sl, :], g_ref[sl, :],
                             lb_ref[0], og_ref[0])
            dst, dq, df, dv, dg, dlb_c, dog_c = vjp((do_ref[sl, :], dst))
            dq_ref[sl, :], df_ref[sl, :] = dq.astype(BF16), df.astype(BF16)
            dv_ref[sl, :], dg_ref[sl, :] = dv.astype(BF16), dg.astype(BF16)
            dlb, dog = dlb + dlb_c, dog + dog_c
        dst_ref[...] = dst
        dlb_ref[0] += dlb
        dog_ref[0] += dog

    part = lambda p: pl.BlockSpec((tb, HGRN_DK), lambda h, t: (nt - 1 - t, p * HGRN_HEADS + h))
    vec = pl.BlockSpec((1, 1, HGRN_DK), lambda h, t: (h, 0, 0))
    head = pl.BlockSpec((tb, HGRN_DK), lambda h, t: (nt - 1 - t, h))
    return pl.pallas_call(
        body, name=name, grid=(HGRN_HEADS, nt),
        out_shape=[jax.ShapeDtypeStruct((T, D_MODEL), BF16)] * 4 + [jax.ShapeDtypeStruct((HGRN_HEADS, 1, HGRN_DK), F32)] * 2,
        in_specs=[part(0), part(1), part(2), part(3),
                  pl.BlockSpec((1, ncb, HGRN_DK, HGRN_DK), lambda h, t: (h, nt - 1 - t, 0, 0)), head, vec, vec],
        out_specs=[head] * 4 + [vec, vec],
        scratch_shapes=[pltpu.VMEM((HGRN_DK, HGRN_DK), F32)],
        compiler_params=_params("parallel", "arbitrary"),
    )(proj, proj, proj, proj, states, do, lb, og)


def _cols_to_blocks(g, n8):
    K = g.shape[0]
    return g.reshape(K, N_DEV, n8).transpose(1, 0, 2)


def _blocks_to_cols(wg):
    _, K, n8 = wg.shape
    return wg.transpose(1, 0, 2).reshape(K, N_DEV * n8)


def _pack(parts):
    flat = []
    for p in parts:
        v = p.reshape(-1)
        flat.append(jnp.pad(v, (0, (-v.shape[0]) % 1024)))
    return jnp.concatenate(flat).reshape(-1, 128)


def _unpack(packed, like):
    flat, out, off = packed.reshape(-1), [], 0
    for p in like:
        size = math.prod(p.shape)
        out.append(flat[off:off + size].reshape(p.shape))
        off += size + (-size) % 1024
    return out


def _heads_major(a, heads):
    T = a.shape[0]
    return a.reshape(T, heads, ATTN_HD).transpose(1, 0, 2)


def _heads_minor(a):
    heads, T, _ = a.shape
    return a.transpose(1, 0, 2).reshape(T, heads * ATTN_HD)


def kernel(x, c, mod_w, mod_b, norm_mix, norm_mlp, attn_w_in, attn_w_out, attn_q_gain, attn_k_gain, attn_sinks, hgrn_w_in, hgrn_w_out, hgrn_o_gain, hgrn_lb_logits, mlp_w1, mlp_w2, loss_target, m_mod_w, m_mod_b, m_norm_mix, m_norm_mlp, m_attn_w_in, m_attn_w_out, m_attn_q_gain, m_attn_k_gain, m_attn_sinks, m_hgrn_w_in, m_hgrn_w_out, m_hgrn_o_gain, m_hgrn_lb_logits, m_mlp_w1, m_mlp_w2, v_mod_w, v_mod_b, v_norm_mix, v_norm_mlp, v_attn_w_in, v_attn_w_out, v_attn_q_gain, v_attn_k_gain, v_attn_sinks, v_hgrn_w_in, v_hgrn_w_out, v_hgrn_o_gain, v_hgrn_lb_logits, v_mlp_w1, v_mlp_w2):
    T = x.shape[1]
    me = 4 * lax.axis_index("x") + 2 * lax.axis_index("y") + lax.axis_index("c")
    x0, target = x[0], loss_target[0]
    n_mod = mod_w.shape[2]

    shards = [attn_w_in[0], attn_w_out[0], hgrn_w_in[0], hgrn_w_out[0], mlp_w1[0], mlp_w1[1], mlp_w2[0], mlp_w2[1]]
    gathered = _exchange([s.astype(BF16) for s in shards], scatter=False, vmem=False, name="gather_weights")
    w_attn_in, w_hgrn_in, w_mlp1 = _blocks_to_cols(gathered[0]), _blocks_to_cols(gathered[2]), [
        _blocks_to_cols(gathered[4]), _blocks_to_cols(gathered[5])]
    w_attn_out, w_hgrn_out = gathered[1].reshape(D_MODEL, D_MODEL), gathered[3].reshape(D_MODEL, D_MODEL)
    w_mlp2 = [gathered[6].reshape(D_FF, D_MODEL), gathered[7].reshape(D_FF, D_MODEL)]

    c_all = _exchange([c.reshape(16, 128)], scatter=False, vmem=True, name="gather_c")[0].reshape(N_DEV, D_MODEL)
    mod_b_cols = lax.dynamic_slice_in_dim(mod_b, me * n_mod, n_mod, axis=1).reshape(2, 1, n_mod)
    mod_cols = _mod_fwd(c_all, mod_w, mod_b_cols, "mod_fwd")
    mod_all = _exchange([mod_cols.reshape(-1, 128)], scatter=False, vmem=True, name="gather_mod")[0]
    mod_all = mod_all.reshape(N_DEV, 2, N_DEV, n_mod)
    mod_mine = lax.dynamic_index_in_dim(mod_all, me, axis=2, keepdims=False)
    mod_mine = mod_mine.transpose(1, 0, 2).reshape(2, N_MOD, 1, D_MODEL)

    lb = _lb_fwd(hgrn_lb_logits, "lb_fwd").reshape(HGRN_HEADS, 1, HGRN_DK)
    og = hgrn_o_gain.reshape(HGRN_HEADS, 1, HGRN_DK)
    slopes = jnp.exp2(-8.0 * jnp.arange(1, ATTN_HEADS + 1, dtype=F32) / ATTN_HEADS)
    per_row = lambda vals: jnp.repeat(vals.reshape(ATTN_KV, ATTN_GROUP), ATTN_BLOCK, axis=1).reshape(
        ATTN_KV, ATTN_GROUP * ATTN_BLOCK, 1)
    slope_col, sink_col = per_row(slopes), per_row(attn_sinks[0])

    saved = []
    xi = x0
    for i in range(2):
        sh1, sc1, g1, sh2, sc2, g2 = [mod_mine[i, j] for j in range(N_MOD)]
        h = _norm_mod_fwd(xi, norm_mix[i:i + 1], sc1, sh1, f"norm_mix_fwd{i}")
        if i == 0:
            proj = _matmul(h, w_attn_in, dims="nn", tm=1024, tn=640, tk=2048, name="attn_in_fwd")
            qh = _heads_major(proj[:, :D_MODEL], ATTN_HEADS)
            kh = _heads_major(proj[:, D_MODEL:D_MODEL + ATTN_KV * ATTN_HD], ATTN_KV)
            vh = _heads_major(proj[:, D_MODEL + ATTN_KV * ATTN_HD:], ATTN_KV)
            o = _heads_minor(_attn_fwd(qh, kh, vh, attn_q_gain, attn_k_gain, slope_col, sink_col, "attn_fwd"))
            mix = (qh, kh, vh)
            w_out = w_attn_out
        else:
            proj = _matmul(h, w_hgrn_in, dims="nn", tm=1024, tn=1024, tk=2048, name="hgrn_in_fwd")
            o, states = _hgrn_fwd(proj, lb, og, "hgrn_fwd")
            mix = (proj, states)
            w_out = w_hgrn_out
        y, x1 = _matmul(o, w_out, dims="nn", tm=1024, tn=512, tk=2048, name=f"mix_out_fwd{i}", epilogue="resgate",
                        extras=(xi, g1))
        h2 = _norm_mod_fwd(x1, norm_mlp[i:i + 1], sc2, sh2, f"norm_mlp_fwd{i}")
        act, act2 = _matmul(h2, w_mlp1[i], dims="nn", tm=1024, tn=1024, tk=2048, name=f"mlp1_fwd{i}", epilogue="relu2")
        z, x2 = _matmul(act2, w_mlp2[i], dims="nn", tm=1024, tn=512, tk=2048, name=f"mlp2_fwd{i}", epilogue="resgate",
                        extras=(x1, g2))
        saved.append((xi, h, o, y, x1, h2, act, act2, z, mix))
        xi = x2

    dx, loss_tile = _loss_head(xi, target, "loss_head")
    loss = lax.psum(loss_tile[0, 0], ("x", "y", "c"))

    grads, dmods, dnorm_mix, dnorm_mlp = {}, [None, None], [None, None], [None, None]
    for i in (1, 0):
        sh1, sc1, g1, sh2, sc2, g2 = [mod_mine[i, j] for j in range(N_MOD)]
        xin, h, o, y, x1, h2, act, act2, z, mix = saved[i]
        dz, st_g2 = _gate_bwd(dx, z, g2, f"gate_mlp_bwd{i}")
        dpre = _matmul(dz, w_mlp2[i], dims="nt", tm=1024, tn=1024, tk=2048, name=f"mlp2_bwd{i}", epilogue="mul2a",
                       extras=(act,))
        grads[f"mlp_w2_{i}"] = _matmul(act2, dz, dims="tn", tm=1024, tn=1024, tk=1024, name=f"mlp2_wgrad{i}",
                                       out_dtype=BF16)
        dh2 = _matmul(dpre, w_mlp1[i], dims="nt", tm=1024, tn=1024, tk=2048, name=f"mlp1_bwd{i}")
        grads[f"mlp_w1_{i}"] = _matmul(h2, dpre, dims="tn", tm=1024, tn=1024, tk=1024, name=f"mlp1_wgrad{i}",
                                       out_dtype=BF16)
        dx1, st_mlp = _norm_mod_bwd(x1, dh2, dx, norm_mlp[i:i + 1], sc2, f"norm_mlp_bwd{i}")
        dy, st_g1 = _gate_bwd(dx1, y, g1, f"gate_mix_bwd{i}")
        w_out = w_attn_out if i == 0 else w_hgrn_out
        do = _matmul(dy, w_out, dims="nt", tm=1024, tn=1024, tk=2048, name=f"mix_out_bwd{i}")
        g_out = _matmul(o, dy, dims="tn", tm=1024, tn=1024, tk=1024, name=f"mix_out_wgrad{i}", out_dtype=BF16)
        if i == 0:
            qh, kh, vh = mix
            dqh, dkh, dvh, dqg, dkg, dsk = _attn_bwd(qh, kh, vh, _heads_major(do, ATTN_HEADS), attn_q_gain, attn_k_gain,
                                                     slope_col, sink_col, "attn_bwd")
            dproj = jnp.concatenate([_heads_minor(dqh), _heads_minor(dkh).astype(BF16), _heads_minor(dvh).astype(BF16)],
                                    axis=1)
            grads["attn_w_out"], w_in = g_out, w_attn_in
            d_q_gain, d_k_gain, d_sinks = dqg[0:1], dkg[0:1], dsk[:, 0].reshape(1, ATTN_HEADS)
        else:
            proj, states = mix
            dq, df, dv, dg, dlb, d_o_gain = _hgrn_bwd(proj, states, do, lb, og, "hgrn_bwd")
            dproj = jnp.concatenate([dq, df, dv, dg], axis=1)
            grads["hgrn_w_out"], w_in = g_out, w_hgrn_in
            d_lb_logits = _lb_bwd(hgrn_lb_logits, dlb.reshape(1, D_MODEL), "lb_bwd")
        n_in = dproj.shape[1]
        dh = _matmul(dproj, w_in, dims="nt", tm=1024, tn=1024, tk=n_in // 2 if i == 0 else 2048,
                     name=f"mix_in_bwd{i}")
        g_in = _matmul(h, dproj, dims="tn", tm=1024, tn=640 if i == 0 else 1024, tk=1024, name=f"mix_in_wgrad{i}",
                       out_dtype=BF16)
        grads["attn_w_in" if i == 0 else "hgrn_w_in"] = g_in
        dx, st_mix = _norm_mod_bwd(xin, dh, dx1, norm_mix[i:i + 1], sc1, f"norm_mix_bwd{i}")
        dmods[i] = jnp.concatenate([st_mix[0:1], st_mix[1:2], st_g1[0:1], st_mlp[0:1], st_mlp[1:2], st_g2[0:1]], axis=1)
        dnorm_mix[i], dnorm_mlp[i] = st_mix[2:3], st_mlp[2:3]

    n_attn, n_hgrn, n_ff = attn_w_in.shape[2], hgrn_w_in.shape[2], mlp_w1.shape[2]
    blocks = [_cols_to_blocks(grads["attn_w_in"], n_attn), grads["attn_w_out"].reshape(N_DEV, -1, D_MODEL),
              _cols_to_blocks(grads["hgrn_w_in"], n_hgrn), grads["hgrn_w_out"].reshape(N_DEV, -1, D_MODEL),
              _cols_to_blocks(grads["mlp_w1_0"], n_ff), _cols_to_blocks(grads["mlp_w1_1"], n_ff),
              grads["mlp_w2_0"].reshape(N_DEV, -1, D_MODEL), grads["mlp_w2_1"].reshape(N_DEV, -1, D_MODEL)]
    shares = _exchange(blocks, scatter=True, vmem=False, name="scatter_grads")
    names = ["attn_w_in", "attn_w_out", "hgrn_w_in", "hgrn_w_out", "mlp_w1_0", "mlp_w1_1", "mlp_w2_0", "mlp_w2_1"]
    moments = [(m_attn_w_in[0], v_attn_w_in[0]), (m_attn_w_out[0], v_attn_w_out[0]), (m_hgrn_w_in[0], v_hgrn_w_in[0]),
               (m_hgrn_w_out[0], v_hgrn_w_out[0]), (m_mlp_w1[0], v_mlp_w1[0]), (m_mlp_w1[1], v_mlp_w1[1]),
               (m_mlp_w2[0], v_mlp_w2[0]), (m_mlp_w2[1], v_mlp_w2[1])]
    big = {nm: _sum_adamw(sh, w, m, v, f"adamw_{nm}") for nm, sh, w, (m, v) in zip(names, shares, shards, moments)}

    small_w = [mod_b, norm_mix, norm_mlp, attn_q_gain, attn_k_gain, attn_sinks, hgrn_o_gain, hgrn_lb_logits]
    small_m = [m_mod_b, m_norm_mix, m_norm_mlp, m_attn_q_gain, m_attn_k_gain, m_attn_sinks, m_hgrn_o_gain, m_hgrn_lb_logits]
    small_v = [v_mod_b, v_norm_mix, v_norm_mlp, v_attn_q_gain, v_attn_k_gain, v_attn_sinks, v_hgrn_o_gain, v_hgrn_lb_logits]
    small_g = [jnp.concatenate(dmods, axis=0), jnp.concatenate(dnorm_mix, axis=0), jnp.concatenate(dnorm_mlp, axis=0),
               d_q_gain, d_k_gain, d_sinks, d_o_gain.reshape(hgrn_o_gain.shape), d_lb_logits]
    packed_g = _pack(small_g)
    pad_rows = (-packed_g.shape[0]) % 8
    pad8 = lambda a: jnp.pad(a, ((0, pad_rows), (0, 0)))
    all_small = _exchange([pad8(packed_g)], scatter=False, vmem=True, name="gather_small_grads")[0]
    sg, sd, sm, sv = _sum_adamw(all_small, pad8(_pack(small_w)), pad8(_pack(small_m)), pad8(_pack(small_v)),
                                "adamw_small")
    small = [_unpack(t, small_w) for t in (sg, sd, sm, sv)]

    n_modb = N_MOD * D_MODEL
    dmod_all = all_small[:, :2 * n_modb // 128, :].reshape(N_DEV, 2, n_modb)
    dmod_cols = lax.dynamic_slice_in_dim(dmod_all, me * n_mod, n_mod, axis=2).transpose(1, 0, 2)
    modw = _mod_w_update(c_all.T, dmod_cols, mod_w, m_mod_w, v_mod_w, "adamw_mod_w")

    def leaf(k):
        stack = lambda a, b: jnp.stack([big[a][k], big[b][k]])
        one = lambda a: big[a][k][None]
        s = small[k]
        return [modw[k], s[0], s[1], s[2], one("attn_w_in"), one("attn_w_out"), s[3], s[4], s[5], one("hgrn_w_in"),
                one("hgrn_w_out"), s[6], s[7], stack("mlp_w1_0", "mlp_w1_1"), stack("mlp_w2_0", "mlp_w2_1")]

    return (loss, dx[None], *leaf(0), *leaf(1), *leaf(2), *leaf(3))
```

```python
import functools
import math

import jax
import jax.numpy as jnp
from jax import lax
from jax.experimental import pallas as pl
from jax.experimental.pallas import tpu as pltpu

F32, BF16 = jnp.float32, jnp.bfloat16
N_DEV = 8
D_MODEL = 2048
N_MOD = 6
EPS = 1e-6
ATTN_HD, ATTN_HEADS, ATTN_KV, ATTN_GROUP, ATTN_BLOCK = 64, 32, 4, 8, 128
ATTN_SCALE = 1.0 / math.sqrt(ATTN_HD)
HGRN_HEADS, HGRN_DK, HGRN_CHUNK = 16, 128, 64
HGRN_SCALE = 1.0 / math.sqrt(HGRN_DK)
D_FF = 4 * D_MODEL
ADAM_LR, ADAM_B1, ADAM_B2, ADAM_EPS, ADAM_WD, ADAM_STEP = 0.001, 0.9, 0.999, 1e-08, 0.01, 10
NEG_BIG = -1e30
VMEM_LIMIT = 56 * 1024 * 1024
MESH_ID = pl.DeviceIdType.MESH


def _params(*sem):
    return pltpu.CompilerParams(dimension_semantics=sem, vmem_limit_bytes=VMEM_LIMIT)


class _Exchange:
    def __init__(self, items):
        self.items = items
        self.n = len(items)
        self.out_shape = []
        for kind, a, axis in items:
            if kind == "gather":
                shape = (N_DEV,) + a.shape if axis is None else tuple(
                    d * N_DEV if i == axis else d for i, d in enumerate(a.shape))
            else:
                shape = a.shape if axis is None else (N_DEV,) + tuple(
                    d // N_DEV if i == axis else d for i, d in enumerate(a.shape))
            self.out_shape.append(jax.ShapeDtypeStruct(shape, a.dtype))
        self.scratch = [pltpu.SemaphoreType.DMA((7 * self.n,)), pltpu.SemaphoreType.DMA((7 * self.n,)),
                        pltpu.SemaphoreType.DMA((self.n,))]
        self.arrays = [a for _, a, _ in items]

    @staticmethod
    def _block(ref, b, axis, size):
        if axis is None:
            return ref.at[b]
        sl = pl.ds(pl.multiple_of(b * size, size), size)
        return ref.at[sl, :] if axis == 0 else ref.at[:, sl]

    def _copies(self, ins, outs, sems):
        send_sems, recv_sems, loc_sems = sems
        x, y, c = lax.axis_index("x"), lax.axis_index("y"), lax.axis_index("c")
        me = 4 * x + 2 * y + c
        local, sends, lands = [], [], []
        for a, (kind, arr, axis) in enumerate(self.items):
            gather = kind == "gather"
            size = None if axis is None else (arr.shape[axis] if gather else arr.shape[axis] // N_DEV)

            def src(b):
                return ins[a] if gather else self._block(ins[a], b, axis, size)

            def dst(b):
                return self._block(outs[a], b, axis, size) if gather else outs[a].at[b]

            local.append(pltpu.make_async_copy(src(me), dst(me), loc_sems.at[a]))
            for d in range(1, N_DEV):
                px = 1 - x if d & 4 else x
                py = 1 - y if d & 2 else y
                pc = 1 - c if d & 1 else c
                pid = 4 * px + 2 * py + pc
                for land, group in ((False, sends), (True, lands)):
                    group.append(pltpu.make_async_remote_copy(
                        src_ref=src(pid), dst_ref=dst(pid if land else me),
                        send_sem=send_sems.at[a * 7 + d - 1], recv_sem=recv_sems.at[a * 7 + d - 1],
                        device_id=(px, py, pc), device_id_type=MESH_ID))
        return local, sends, lands

    def start(self, ins, outs, sems):
        local, sends, _ = self._copies(ins, outs, sems)
        for cp in local + sends:
            cp.start()

    def wait(self, ins, outs, sems):
        local, _, lands = self._copies(ins, outs, sems)
        for cp in lands:
            cp.wait()
        for cp in local:
            cp.wait()


def _call(body, *, name, grid, in_specs, out_specs, out_shape, args, scratch_shapes=(), sem=None, comm=None):
    n_in, n_out, n_scr = len(in_specs), len(out_specs), len(scratch_shapes)
    if comm is None:
        return pl.pallas_call(
            body, name=name, grid=grid, out_shape=list(out_shape), in_specs=list(in_specs), out_specs=list(out_specs),
            scratch_shapes=list(scratch_shapes), compiler_params=_params(*sem))(*args)
    hbm = pl.BlockSpec(memory_space=pltpu.HBM)

    def carrier(*refs):
        bounds = [0, n_in, n_in + comm.n, n_in + comm.n + n_out, n_in + 2 * comm.n + n_out, len(refs) - 3, len(refs)]
        ins, cin, outs, cout, scr, sems = [refs[lo:hi] for lo, hi in zip(bounds[:-1], bounds[1:])]
        assert len(scr) == n_scr
        ids = [pl.program_id(ax) for ax in range(len(grid))]
        first = functools.reduce(lambda p, q: p & q, [i == 0 for i in ids])
        last = functools.reduce(lambda p, q: p & q, [i == g - 1 for i, g in zip(ids, grid)])

        @pl.when(first)
        def _():
            comm.start(cin, cout, sems)

        body(*ins, *outs, *scr)

        @pl.when(last)
        def _():
            comm.wait(cin, cout, sems)

    return pl.pallas_call(
        carrier, name=name, grid=grid, out_shape=list(out_shape) + comm.out_shape,
        in_specs=list(in_specs) + [hbm] * comm.n, out_specs=list(out_specs) + [hbm] * comm.n,
        scratch_shapes=list(scratch_shapes) + comm.scratch,
        compiler_params=_params(*["arbitrary"] * len(grid)))(*args, *comm.arrays)


def _exchange(items, *, name, vmem=False):
    comm = _Exchange(items)
    space = pl.BlockSpec(memory_space=pltpu.VMEM if vmem else pltpu.HBM)

    def body(*refs):
        ins, outs, sems = refs[:comm.n], refs[comm.n:2 * comm.n], refs[2 * comm.n:]
        comm.start(ins, outs, sems)
        comm.wait(ins, outs, sems)

    return pl.pallas_call(
        body, name=name, out_shape=comm.out_shape, in_specs=[space] * comm.n, out_specs=[space] * comm.n,
        scratch_shapes=comm.scratch, compiler_params=pltpu.CompilerParams(vmem_limit_bytes=VMEM_LIMIT))(*comm.arrays)


_DIMS = {"nn": (((1,), (0,)), ((), ())), "nt": (((1,), (1,)), ((), ())), "tn": (((0,), (0,)), ((), ()))}


def _matmul(a, b, *, dims, tm, tn, tk, name, epilogue="plain", out_dtype=F32, extras=(), comm=None):
    a_parts = a.shape[0] if a.ndim == 3 else 0
    b_parts = b.shape[0] if b.ndim == 3 else 0
    assert not (a_parts and dims != "nt") and not (b_parts and dims != "tn")
    a2 = (a.shape[1], a.shape[2] * a_parts) if a_parts else a.shape
    b2 = (b.shape[1], b.shape[2] * b_parts) if b_parts else b.shape
    if dims == "tn":
        (K, M), N = a2, b2[1]
    else:
        (M, K), N = a2, (b2[1] if dims == "nn" else b2[0])
    tm, tn, tk = min(tm, M), min(tn, N), min(tk, K)
    assert M % tm == 0 and N % tn == 0 and K % tk == 0, (name, M, N, K, tm, tn, tk)
    if a_parts:
        per = K // a_parts // tk
        a_spec = pl.BlockSpec((None, tm, tk), lambda i, j, k: (k // per, i, k % per))
    elif dims == "tn":
        a_spec = pl.BlockSpec((tk, tm), lambda i, j, k: (k, i))
    else:
        a_spec = pl.BlockSpec((tm, tk), lambda i, j, k: (i, k))
    if b_parts:
        per_n = N // b_parts // tn
        b_spec = pl.BlockSpec((None, tk, tn), lambda i, j, k: (j // per_n, k, j % per_n))
    elif dims == "nt":
        b_spec = pl.BlockSpec((tn, tk), lambda i, j, k: (j, k))
    else:
        b_spec = pl.BlockSpec((tk, tn), lambda i, j, k: (k, j))
    nk = K // tk
    tile = pl.BlockSpec((tm, tn), lambda i, j, k: (i, j))
    row = pl.BlockSpec((1, tn), lambda i, j, k: (0, j))
    if epilogue == "plain":
        extra_specs, out_shape, out_specs = [], [jax.ShapeDtypeStruct((M, N), out_dtype)], [tile]
    elif epilogue == "relu2":
        extra_specs, out_shape, out_specs = [], [jax.ShapeDtypeStruct((M, N), BF16)] * 2, [tile, tile]
    elif epilogue == "resgate":
        extra_specs, out_shape, out_specs = [tile, row], [jax.ShapeDtypeStruct((M, N), F32)] * 2, [tile, tile]
    elif epilogue == "mul2a":
        extra_specs, out_shape, out_specs = [tile], [jax.ShapeDtypeStruct((M, N), BF16)], [tile]
    else:
        raise ValueError(epilogue)
    n_extra = len(extra_specs)

    def body(a_ref, b_ref, *rest):
        ex, outs, acc_ref = rest[:n_extra], rest[n_extra:-1], rest[-1]
        k = pl.program_id(2)

        @pl.when(k == 0)
        def _():
            acc_ref[...] = jnp.zeros_like(acc_ref)

        acc_ref[...] += lax.dot_general(a_ref[...], b_ref[...], _DIMS[dims], preferred_element_type=F32)

        @pl.when(k == nk - 1)
        def _():
            acc = acc_ref[...]
            if epilogue == "plain":
                outs[0][...] = acc.astype(out_dtype)
            elif epilogue == "relu2":
                act = jnp.maximum(acc, 0.0)
                outs[0][...] = act.astype(BF16)
                outs[1][...] = (act * act).astype(BF16)
            elif epilogue == "resgate":
                outs[0][...] = acc
                outs[1][...] = ex[0][...] + ex[1][...] * acc
            else:
                outs[0][...] = (acc * (2.0 * ex[0][...].astype(F32))).astype(BF16)

    res = _call(body, name=name, grid=(M // tm, N // tn, nk), out_shape=out_shape,
                in_specs=[a_spec, b_spec] + extra_specs, out_specs=out_specs,
                scratch_shapes=[pltpu.VMEM((tm, tn), F32)], sem=("parallel", "parallel", "arbitrary"),
                args=(a, b, *extras), comm=comm)
    return res[0] if len(res) == 1 else res


def _row_tile(T):
    return min(T, 256)


def _norm_mod_fwd(x, gain, sc, sh, name, comm=None):
    T, D = x.shape
    tr = _row_tile(T)

    def body(x_ref, g_ref, sc_ref, sh_ref, h_ref):
        xv = x_ref[...]
        r = lax.rsqrt(jnp.mean(xv * xv, axis=-1, keepdims=True) + EPS)
        hn = (xv * r) * g_ref[...]
        h_ref[...] = (hn * (1.0 + sc_ref[...]) + sh_ref[...]).astype(BF16)

    vec = pl.BlockSpec((1, D), lambda i: (0, 0))
    res = _call(body, name=name, grid=(T // tr,), out_shape=[jax.ShapeDtypeStruct((T, D), BF16)],
                in_specs=[pl.BlockSpec((tr, D), lambda i: (i, 0)), vec, vec, vec],
                out_specs=[pl.BlockSpec((tr, D), lambda i: (i, 0))], sem=("parallel",), args=(x, gain, sc, sh),
                comm=comm)
    return res[0] if comm is None else res


def _norm_mod_bwd(x, dh, dres, gain, sc, name):
    T, D = x.shape
    tr = _row_tile(T)

    def body(x_ref, dh_ref, dres_ref, g_ref, sc_ref, dx_ref, st_ref):
        xv, dh_v, gain_v = x_ref[...], dh_ref[...], g_ref[...]
        r = lax.rsqrt(jnp.mean(xv * xv, axis=-1, keepdims=True) + EPS)
        xn = xv * r
        hn = xn * gain_v
        dhn = dh_v * (1.0 + sc_ref[...])
        dxn = dhn * gain_v
        dx_ref[...] = dres_ref[...] + r * (dxn - xn * jnp.mean(dxn * xn, axis=-1, keepdims=True))

        @pl.when(pl.program_id(0) == 0)
        def _():
            st_ref[...] = jnp.zeros_like(st_ref)

        st_ref[0:1, :] += jnp.sum(dh_v, axis=0, keepdims=True)
        st_ref[1:2, :] += jnp.sum(dh_v * hn, axis=0, keepdims=True)
        st_ref[2:3, :] += jnp.sum(dhn * xn, axis=0, keepdims=True)

    vec = pl.BlockSpec((1, D), lambda i: (0, 0))
    blk = pl.BlockSpec((tr, D), lambda i: (i, 0))
    return pl.pallas_call(
        body, name=name, grid=(T // tr,),
        out_shape=[jax.ShapeDtypeStruct((T, D), F32), jax.ShapeDtypeStruct((8, D), F32)],
        in_specs=[blk, blk, blk, vec, vec],
        out_specs=[blk, pl.BlockSpec((8, D), lambda i: (0, 0))],
        compiler_params=_params("arbitrary"),
    )(x, dh, dres, gain, sc)


def _gate_bwd(dx, y, gate, name):
    T, D = dx.shape
    tr = _row_tile(T)

    def body(dx_ref, y_ref, g_ref, dy_ref, st_ref):
        dxv = dx_ref[...]
        dy_ref[...] = (dxv * g_ref[...]).astype(BF16)

        @pl.when(pl.program_id(0) == 0)
        def _():
            st_ref[...] = jnp.zeros_like(st_ref)

        st_ref[0:1, :] += jnp.sum(dxv * y_ref[...], axis=0, keepdims=True)

    blk = pl.BlockSpec((tr, D), lambda i: (i, 0))
    return pl.pallas_call(
        body, name=name, grid=(T // tr,),
        out_shape=[jax.ShapeDtypeStruct((T, D), BF16), jax.ShapeDtypeStruct((8, D), F32)],
        in_specs=[blk, blk, pl.BlockSpec((1, D), lambda i: (0, 0))],
        out_specs=[blk, pl.BlockSpec((8, D), lambda i: (0, 0))],
        compiler_params=_params("arbitrary"),
    )(dx, y, gate)


def _loss_head(y, target, name):
    T, D = y.shape
    tr = _row_tile(T)

    def body(y_ref, t_ref, dy_ref, l_ref):
        err = y_ref[...] - t_ref[...]
        dy_ref[...] = err * (1.0 / D)

        @pl.when(pl.program_id(0) == 0)
        def _():
            l_ref[...] = jnp.zeros_like(l_ref)

        part = jnp.sum(jnp.mean(err * err, axis=-1, keepdims=True), axis=0, keepdims=True)
        l_ref[...] += jnp.broadcast_to(0.5 * part, l_ref.shape)

    blk = pl.BlockSpec((tr, D), lambda i: (i, 0))
    return pl.pallas_call(
        body, name=name, grid=(T // tr,),
        out_shape=[jax.ShapeDtypeStruct((T, D), F32), jax.ShapeDtypeStruct((8, 128), F32)],
        in_specs=[blk, blk], out_specs=[blk, pl.BlockSpec((8, 128), lambda i: (0, 0))],
        compiler_params=_params("arbitrary"),
    )(y, target)


def _silu(v):
    return v * jax.nn.sigmoid(v)


def _mod_fwd(c_all, mod_w, mod_b_cols, name):
    L, D, n = mod_w.shape
    tn = 512

    def body(c_ref, w_ref, b_ref, o_ref):
        cond = _silu(c_ref[...]).astype(BF16)
        o_ref[0] = jnp.dot(cond, w_ref[0].astype(BF16), preferred_element_type=F32) + b_ref[0]

    return pl.pallas_call(
        body, name=name, grid=(L, n // tn), out_shape=jax.ShapeDtypeStruct((L, N_DEV, n), F32),
        in_specs=[pl.BlockSpec((N_DEV, D), lambda l, j: (0, 0)), pl.BlockSpec((1, D, tn), lambda l, j: (l, 0, j)),
                  pl.BlockSpec((1, 1, tn), lambda l, j: (l, 0, j))],
        out_specs=pl.BlockSpec((1, N_DEV, tn), lambda l, j: (l, 0, j)),
        compiler_params=_params("parallel", "parallel"),
    )(c_all, mod_w, mod_b_cols)


def _adamw(g, w, m, v):
    m = ADAM_B1 * m + (1.0 - ADAM_B1) * g
    v = ADAM_B2 * v + (1.0 - ADAM_B2) * (g * g)
    m_hat = m / (1.0 - ADAM_B1 ** ADAM_STEP)
    v_hat = v / (1.0 - ADAM_B2 ** ADAM_STEP)
    delta = -ADAM_LR * (m_hat / (jnp.sqrt(v_hat) + ADAM_EPS) + ADAM_WD * w)
    return delta, m, v


def _mod_w_update(c_t, dmod_cols, w, m, v, name):
    L, D, n = w.shape
    tr = 256

    def body(c_ref, dm_ref, w_ref, m_ref, v_ref, g_ref, d_ref, nm_ref, nv_ref):
        cond = _silu(c_ref[...])
        dm = dm_ref[0]
        g = cond[:, 0:1] * dm[0:1, :]
        for b in range(1, N_DEV):
            g = g + cond[:, b:b + 1] * dm[b:b + 1, :]
        delta, nm, nv = _adamw(g, w_ref[0], m_ref[0], v_ref[0])
        g_ref[0], d_ref[0], nm_ref[0], nv_ref[0] = g, delta, nm, nv

    blk = pl.BlockSpec((1, tr, n), lambda l, i: (l, i, 0))
    return pl.pallas_call(
        body, name=name, grid=(L, D // tr), out_shape=[jax.ShapeDtypeStruct(w.shape, F32)] * 4,
        in_specs=[pl.BlockSpec((tr, N_DEV), lambda l, i: (i, 0)), pl.BlockSpec((1, N_DEV, n), lambda l, i: (l, 0, 0)),
                  blk, blk, blk],
        out_specs=[blk] * 4, compiler_params=_params("parallel", "parallel"),
    )(c_t, dmod_cols, w, m, v)


def _sum_adamw(parts, w, m, v, name):
    R, C = w.shape
    tr = min(R, 256 if C >= 1024 else 1024)
    assert R % tr == 0

    def body(p_ref, w_ref, m_ref, v_ref, g_ref, d_ref, nm_ref, nv_ref):
        g = p_ref[0].astype(F32)
        for s in range(1, N_DEV):
            g = g + p_ref[s].astype(F32)
        delta, nm, nv = _adamw(g, w_ref[...], m_ref[...], v_ref[...])
        g_ref[...], d_ref[...], nm_ref[...], nv_ref[...] = g, delta, nm, nv

    blk = pl.BlockSpec((tr, C), lambda i: (i, 0))
    return pl.pallas_call(
        body, name=name, grid=(R // tr,), out_shape=[jax.ShapeDtypeStruct((R, C), F32)] * 4,
        in_specs=[pl.BlockSpec((N_DEV, tr, C), lambda i: (0, i, 0)), blk, blk, blk], out_specs=[blk] * 4,
        compiler_params=_params("parallel"),
    )(parts, w, m, v)


def _lower_bound_row1(l0, l1):
    mx = lax.stop_gradient(jnp.maximum(l0, l1))
    e0, e1 = jnp.exp(l0 - mx), jnp.exp(l1 - mx)
    p0, p1 = e0 / (e0 + e1), e1 / (e0 + e1)
    return (p0 + p1) - p0


def _lb_fwd(logits, name):
    def body(l_ref, o_ref):
        o_ref[...] = _lower_bound_row1(l_ref[0:1, :], l_ref[1:2, :])

    return pl.pallas_call(body, name=name, out_shape=jax.ShapeDtypeStruct((1, logits.shape[1]), F32))(logits)


def _lb_bwd(logits, dlb, name):
    def body(l_ref, d_ref, o_ref):
        _, vjp = jax.vjp(_lower_bound_row1, l_ref[0:1, :], l_ref[1:2, :])
        d0, d1 = vjp(d_ref[...])
        o_ref[0:1, :] = d0
        o_ref[1:2, :] = d1

    return pl.pallas_call(body, name=name, out_shape=jax.ShapeDtypeStruct(logits.shape, F32))(logits, dlb)


def _bdot(a, b, dims):
    return lax.dot_general(a.astype(BF16), b.astype(BF16), _DIMS[dims], preferred_element_type=F32)


def _attn_probs(q_ref, kp_ref, kc_ref, qg, kg, slope, sink, n):
    rows = ATTN_GROUP * ATTN_BLOCK
    q = q_ref[...].reshape(rows, ATTN_HD)
    rq = lax.rsqrt(jnp.mean(q * q, axis=-1, keepdims=True) + EPS)
    qhat = q * rq
    k = jnp.concatenate([kp_ref[0], kc_ref[0]], axis=0)
    rk = lax.rsqrt(jnp.mean(k * k, axis=-1, keepdims=True) + EPS)
    khat = k * rk
    qn, kn = qhat * qg, khat * kg
    s = _bdot(qn, kn, "nt") * ATTN_SCALE
    qi = lax.broadcasted_iota(jnp.int32, s.shape, 0) & (ATTN_BLOCK - 1)
    ki = lax.broadcasted_iota(jnp.int32, s.shape, 1)
    dist = qi + ATTN_BLOCK - ki
    first_key = jnp.where(n > 0, 0, ATTN_BLOCK)
    valid = (dist >= 0) & (dist < ATTN_BLOCK) & (ki >= first_key)
    s = jnp.where(valid, s - slope * jnp.abs(dist).astype(F32), NEG_BIG)
    mx = jnp.maximum(jnp.max(s, axis=-1, keepdims=True), sink)
    e = jnp.exp(s - mx)
    es = jnp.exp(sink - mx)
    den = jnp.sum(e, axis=-1, keepdims=True) + es
    return e / den, es / den, (qhat, rq, qn), (khat, rk, kn)


def _attn_specs(T):
    nb = T // ATTN_BLOCK
    qspec = pl.BlockSpec((ATTN_GROUP, ATTN_BLOCK, ATTN_HD), lambda h, n: (h, n, 0))
    prev = pl.BlockSpec((1, ATTN_BLOCK, ATTN_HD), lambda h, n: (h, jnp.maximum(n - 1, 0), 0))
    cur = pl.BlockSpec((1, ATTN_BLOCK, ATTN_HD), lambda h, n: (h, n, 0))
    gain = pl.BlockSpec((1, ATTN_HD), lambda h, n: (0, 0))
    col = pl.BlockSpec((1, ATTN_GROUP * ATTN_BLOCK, 1), lambda h, n: (h, 0, 0))
    return nb, qspec, prev, cur, gain, col


def _attn_fwd(qh, kh, vh, qg, kg, slope_col, sink_col, name, comm=None):
    T = qh.shape[1]
    nb, qspec, prev, cur, gain, col = _attn_specs(T)

    def body(q_ref, kp_ref, kc_ref, vp_ref, vc_ref, qg_ref, kg_ref, sl_ref, sk_ref, o_ref):
        n = pl.program_id(1)
        p, _, _, _ = _attn_probs(q_ref, kp_ref, kc_ref, qg_ref[...], kg_ref[...], sl_ref[0], sk_ref[0], n)
        v = jnp.concatenate([vp_ref[0], vc_ref[0]], axis=0)
        o_ref[...] = _bdot(p, v, "nn").reshape(ATTN_GROUP, ATTN_BLOCK, ATTN_HD).astype(BF16)

    return _call(body, name=name, grid=(ATTN_KV, nb), out_shape=[jax.ShapeDtypeStruct(qh.shape, BF16)],
                 in_specs=[qspec, prev, cur, prev, cur, gain, gain, col, col], out_specs=[qspec],
                 sem=("parallel", "parallel"), args=(qh, kh, kh, vh, vh, qg, kg, slope_col, sink_col), comm=comm)


def _rms_bwd(dy, xhat, r, gain):
    dxh = dy * gain
    dx = r * (dxh - xhat * jnp.mean(dxh * xhat, axis=-1, keepdims=True))
    return dx, jnp.sum(dy * xhat, axis=0, keepdims=True)


def _attn_bwd(qh, kh, vh, doh, qg, kg, slope_col, sink_col, name, comm=None):
    T = qh.shape[1]
    nb, qspec, prev, cur, gain, col = _attn_specs(T)
    rows = ATTN_GROUP * ATTN_BLOCK

    def body(q_ref, kp_ref, kc_ref, vp_ref, vc_ref, do_ref, qg_ref, kg_ref, sl_ref, sk_ref,
             dq_ref, dk_ref, dv_ref, dqg_ref, dkg_ref, dsk_ref, sk_acc):
        h, n = pl.program_id(0), pl.program_id(1)
        qg_v, kg_v = qg_ref[...], kg_ref[...]
        p, ps, (qhat, rq, qn), (khat, rk, kn) = _attn_probs(q_ref, kp_ref, kc_ref, qg_v, kg_v, sl_ref[0], sk_ref[0], n)
        v = jnp.concatenate([vp_ref[0], vc_ref[0]], axis=0)
        do = do_ref[...].reshape(rows, ATTN_HD)
        dp = _bdot(do, v, "nt")
        delta = jnp.sum(p * dp, axis=-1, keepdims=True)
        ds = p * (dp - delta)
        dqn = _bdot(ds, kn, "nn") * ATTN_SCALE
        dkn = _bdot(ds, qn, "tn") * ATTN_SCALE
        dv = _bdot(p, do, "tn")
        dq, dqg = _rms_bwd(dqn, qhat, rq, qg_v)
        dk, dkg = _rms_bwd(dkn, khat, rk, kg_v)
        dq_ref[...] = dq.reshape(ATTN_GROUP, ATTN_BLOCK, ATTN_HD).astype(BF16)

        @pl.when((h == 0) & (n == 0))
        def _():
            dqg_ref[...] = jnp.zeros_like(dqg_ref)
            dkg_ref[...] = jnp.zeros_like(dkg_ref)

        dqg_ref[0:1, :] += dqg
        dkg_ref[0:1, :] += dkg

        @pl.when(n == 0)
        def _():
            dk_ref[...] = jnp.zeros_like(dk_ref)
            dv_ref[...] = jnp.zeros_like(dv_ref)
            sk_acc[...] = jnp.zeros_like(sk_acc)
            dk_ref[0, 0:ATTN_BLOCK, :] += dk[ATTN_BLOCK:, :]
            dv_ref[0, 0:ATTN_BLOCK, :] += dv[ATTN_BLOCK:, :]

        @pl.when(n > 0)
        def _():
            band = pl.ds(pl.multiple_of((n - 1) * ATTN_BLOCK, ATTN_BLOCK), 2 * ATTN_BLOCK)
            dk_ref[0, band, :] += dk
            dv_ref[0, band, :] += dv

        sk_acc[...] += -ps * delta

        @pl.when(n == nb - 1)
        def _():
            for g in range(ATTN_GROUP):
                tot = jnp.sum(sk_acc[g * ATTN_BLOCK:(g + 1) * ATTN_BLOCK, :], axis=0, keepdims=True)
                dsk_ref[g:g + 1, :] = jnp.broadcast_to(tot, (1, 128))

    kv_full = pl.BlockSpec((1, T, ATTN_HD), lambda h, n: (h, 0, 0))
    acc = pl.BlockSpec((8, ATTN_HD), lambda h, n: (0, 0))
    return _call(
        body, name=name, grid=(ATTN_KV, nb),
        out_shape=[jax.ShapeDtypeStruct(qh.shape, BF16), jax.ShapeDtypeStruct(kh.shape, F32),
                   jax.ShapeDtypeStruct(kh.shape, F32), jax.ShapeDtypeStruct((8, ATTN_HD), F32),
                   jax.ShapeDtypeStruct((8, ATTN_HD), F32), jax.ShapeDtypeStruct((ATTN_HEADS, 128), F32)],
        in_specs=[qspec, prev, cur, prev, cur, qspec, gain, gain, col, col],
        out_specs=[qspec, kv_full, kv_full, acc, acc, pl.BlockSpec((ATTN_GROUP, 128), lambda h, n: (h, 0))],
        scratch_shapes=[pltpu.VMEM((rows, 1), F32)], sem=("arbitrary", "arbitrary"),
        args=(qh, kh, kh, vh, vh, doh, qg, kg, slope_col, sink_col), comm=comm)


@functools.partial(jax.custom_vjp, nondiff_argnums=(2,))
def _mm(a, b, dims):
    return _bdot(a, b, dims)


def _mm_fwd(a, b, dims):
    return _bdot(a, b, dims), (a, b)


def _mm_bwd(dims, res, ct):
    a, b = res
    if dims == "nn":
        return _bdot(ct, b, "nt"), _bdot(a, ct, "tn")
    if dims == "nt":
        return _bdot(ct, b, "nn"), _bdot(ct, a, "tn")
    return _bdot(b, ct, "nt"), _bdot(a, ct, "nn")


_mm.defvjp(_mm_fwd, _mm_bwd)


def _hgrn_chunk(st, qr, fr, v, gr, lb, og):
    row = lax.broadcasted_iota(jnp.int32, qr.shape, 0)
    causal = lax.broadcasted_iota(jnp.int32, (HGRN_CHUNK, HGRN_CHUNK), 0) >= lax.broadcasted_iota(
        jnp.int32, (HGRN_CHUNK, HGRN_CHUNK), 1)
    forget = lb + (1.0 - lb) * jax.nn.sigmoid(fr)
    k = 1.0 - forget
    logf = jnp.log(forget)
    b = jnp.dot(causal.astype(F32), logf, precision=lax.Precision.HIGHEST, preferred_element_type=F32)
    piv = jnp.sum(jnp.where(row == HGRN_CHUNK // 2 - 1, b, 0.0), axis=0, keepdims=True)
    b_last = jnp.sum(jnp.where(row == HGRN_CHUNK - 1, b, 0.0), axis=0, keepdims=True)
    q = _silu(qr) * HGRN_SCALE
    a = jnp.where(causal, _mm(q * jnp.exp(b - piv), k * jnp.exp(piv - b), "nt"), 0.0)
    o = _mm(a, v, "nn") + _mm(q * jnp.exp(b), st, "nt")
    st_new = st * jnp.exp(b_last) + _mm(v, k * jnp.exp(b_last - b), "tn")
    y = (o * lax.rsqrt(jnp.mean(o * o, axis=-1, keepdims=True) + EPS)) * og * _silu(gr)
    return y, st_new


def _hgrn_tile(T):
    return min(T, 512)


def _hgrn_fwd(proj, lb, og, name, comm=None):
    T = proj.shape[0]
    tb = _hgrn_tile(T)
    ncb = tb // HGRN_CHUNK

    def body(q_ref, f_ref, v_ref, g_ref, lb_ref, og_ref, o_ref, s_ref, st_ref):
        @pl.when(pl.program_id(1) == 0)
        def _():
            st_ref[...] = jnp.zeros_like(st_ref)

        st = st_ref[...]
        for ci in range(ncb):
            sl = slice(ci * HGRN_CHUNK, (ci + 1) * HGRN_CHUNK)
            s_ref[0, ci] = st
            y, st = _hgrn_chunk(st, q_ref[sl, :], f_ref[sl, :], v_ref[sl, :], g_ref[sl, :], lb_ref[0], og_ref[0])
            o_ref[sl, :] = y.astype(BF16)
        st_ref[...] = st

    part = lambda p: pl.BlockSpec((tb, HGRN_DK), lambda h, t: (t, p * HGRN_HEADS + h))
    vec = pl.BlockSpec((1, 1, HGRN_DK), lambda h, t: (h, 0, 0))
    return _call(
        body, name=name, grid=(HGRN_HEADS, T // tb),
        out_shape=[jax.ShapeDtypeStruct((T, D_MODEL), BF16),
                   jax.ShapeDtypeStruct((HGRN_HEADS, T // HGRN_CHUNK, HGRN_DK, HGRN_DK), F32)],
        in_specs=[part(0), part(1), part(2), part(3), vec, vec],
        out_specs=[pl.BlockSpec((tb, HGRN_DK), lambda h, t: (t, h)),
                   pl.BlockSpec((1, ncb, HGRN_DK, HGRN_DK), lambda h, t: (h, t, 0, 0))],
        scratch_shapes=[pltpu.VMEM((HGRN_DK, HGRN_DK), F32)], sem=("parallel", "arbitrary"),
        args=(proj, proj, proj, proj, lb, og), comm=comm)


def _hgrn_bwd(proj, states, do, lb, og, name, comm=None):
    T = proj.shape[0]
    tb = _hgrn_tile(T)
    ncb, nt = tb // HGRN_CHUNK, T // tb

    def body(q_ref, f_ref, v_ref, g_ref, s_ref, do_ref, lb_ref, og_ref, dp_ref, dlb_ref, dog_ref, dst_ref):
        @pl.when(pl.program_id(1) == 0)
        def _():
            dst_ref[...] = jnp.zeros_like(dst_ref)
            dlb_ref[...] = jnp.zeros_like(dlb_ref)
            dog_ref[...] = jnp.zeros_like(dog_ref)

        dst = dst_ref[...]
        dlb, dog = jnp.zeros((1, HGRN_DK), F32), jnp.zeros((1, HGRN_DK), F32)
        for ci in reversed(range(ncb)):
            sl = slice(ci * HGRN_CHUNK, (ci + 1) * HGRN_CHUNK)
            _, vjp = jax.vjp(_hgrn_chunk, s_ref[0, ci], q_ref[sl, :], f_ref[sl, :], v_ref[sl, :], g_ref[sl, :],
                             lb_ref[0], og_ref[0])
            dst, dq, df, dv, dg, dlb_c, dog_c = vjp((do_ref[sl, :], dst))
            for p, part_grad in enumerate((dq, df, dv, dg)):
                dp_ref[p, sl, :] = part_grad.astype(BF16)
            dlb, dog = dlb + dlb_c, dog + dog_c
        dst_ref[...] = dst
        dlb_ref[0] += dlb
        dog_ref[0] += dog

    part = lambda p: pl.BlockSpec((tb, HGRN_DK), lambda h, t: (nt - 1 - t, p * HGRN_HEADS + h))
    vec = pl.BlockSpec((1, 1, HGRN_DK), lambda h, t: (h, 0, 0))
    head = pl.BlockSpec((tb, HGRN_DK), lambda h, t: (nt - 1 - t, h))
    return _call(
        body, name=name, grid=(HGRN_HEADS, nt),
        out_shape=[jax.ShapeDtypeStruct((4, T, D_MODEL), BF16)] + [jax.ShapeDtypeStruct((HGRN_HEADS, 1, HGRN_DK), F32)] * 2,
        in_specs=[part(0), part(1), part(2), part(3),
                  pl.BlockSpec((1, ncb, HGRN_DK, HGRN_DK), lambda h, t: (h, nt - 1 - t, 0, 0)), head, vec, vec],
        out_specs=[pl.BlockSpec((4, tb, HGRN_DK), lambda h, t: (0, nt - 1 - t, h)), vec, vec],
        scratch_shapes=[pltpu.VMEM((HGRN_DK, HGRN_DK), F32)], sem=("parallel", "arbitrary"),
        args=(proj, proj, proj, proj, states, do, lb, og), comm=comm)


def _cols_to_blocks(g, n8):
    K = g.shape[0]
    return g.reshape(K, N_DEV, n8).transpose(1, 0, 2)


def _blocks_to_cols(wg):
    _, K, n8 = wg.shape
    return wg.transpose(1, 0, 2).reshape(K, N_DEV * n8)


def _pack(parts):
    flat = []
    for p in parts:
        v = p.reshape(-1)
        flat.append(jnp.pad(v, (0, (-v.shape[0]) % 1024)))
    return jnp.concatenate(flat).reshape(-1, 128)


def _unpack(packed, like):
    flat, out, off = packed.reshape(-1), [], 0
    for p in like:
        size = math.prod(p.shape)
        out.append(flat[off:off + size].reshape(p.shape))
        off += size + (-size) % 1024
    return out


def _heads_major(a, heads):
    T = a.shape[0]
    return a.reshape(T, heads, ATTN_HD).transpose(1, 0, 2)


def _heads_minor(a):
    heads, T, _ = a.shape
    return a.transpose(1, 0, 2).reshape(T, heads * ATTN_HD)


def kernel(x, c, mod_w, mod_b, norm_mix, norm_mlp, attn_w_in, attn_w_out, attn_q_gain, attn_k_gain, attn_sinks, hgrn_w_in, hgrn_w_out, hgrn_o_gain, hgrn_lb_logits, mlp_w1, mlp_w2, loss_target, m_mod_w, m_mod_b, m_norm_mix, m_norm_mlp, m_attn_w_in, m_attn_w_out, m_attn_q_gain, m_attn_k_gain, m_attn_sinks, m_hgrn_w_in, m_hgrn_w_out, m_hgrn_o_gain, m_hgrn_lb_logits, m_mlp_w1, m_mlp_w2, v_mod_w, v_mod_b, v_norm_mix, v_norm_mlp, v_attn_w_in, v_attn_w_out, v_attn_q_gain, v_attn_k_gain, v_attn_sinks, v_hgrn_w_in, v_hgrn_w_out, v_hgrn_o_gain, v_hgrn_lb_logits, v_mlp_w1, v_mlp_w2):
    T = x.shape[1]
    me = 4 * lax.axis_index("x") + 2 * lax.axis_index("y") + lax.axis_index("c")
    x0, target = x[0], loss_target[0]
    n_mod = mod_w.shape[2]

    shards = [attn_w_in[0], attn_w_out[0], hgrn_w_in[0], hgrn_w_out[0], mlp_w1[0], mlp_w1[1], mlp_w2[0], mlp_w2[1]]
    sb = [s.astype(BF16) for s in shards]
    gather = lambda *items: _Exchange([("gather", a, axis) for a, axis in items])

    c_all = _exchange([("gather", c.reshape(16, 128), None)], vmem=True, name="gather_c")[0].reshape(N_DEV, D_MODEL)
    mod_b_cols = lax.dynamic_slice_in_dim(mod_b, me * n_mod, n_mod, axis=1).reshape(2, 1, n_mod)
    mod_cols = _mod_fwd(c_all, mod_w, mod_b_cols, "mod_fwd")
    mod_all = _exchange([("gather", mod_cols.reshape(-1, 128), None)], vmem=True, name="gather_mod")[0]
    mod_all = mod_all.reshape(N_DEV, 2, N_DEV, n_mod)
    mod_mine = lax.dynamic_index_in_dim(mod_all, me, axis=2, keepdims=False)
    mod_mine = mod_mine.transpose(1, 0, 2).reshape(2, N_MOD, 1, D_MODEL)

    lb = _lb_fwd(hgrn_lb_logits, "lb_fwd").reshape(HGRN_HEADS, 1, HGRN_DK)
    og = hgrn_o_gain.reshape(HGRN_HEADS, 1, HGRN_DK)
    slopes = jnp.exp2(-8.0 * jnp.arange(1, ATTN_HEADS + 1, dtype=F32) / ATTN_HEADS)
    per_row = lambda vals: jnp.repeat(vals.reshape(ATTN_KV, ATTN_GROUP), ATTN_BLOCK, axis=1).reshape(
        ATTN_KV, ATTN_GROUP * ATTN_BLOCK, 1)
    slope_col, sink_col = per_row(slopes), per_row(attn_sinks[0])

    saved = []
    xi = x0
    w_mlp1, w_mlp2 = [None, None], [None, None]
    for i in range(2):
        sh1, sc1, g1, sh2, sc2, g2 = [mod_mine[i, j] for j in range(N_MOD)]
        if i == 0:
            h, w_attn_in = _norm_mod_fwd(xi, norm_mix[i:i + 1], sc1, sh1, "norm_mix_fwd0", comm=gather((sb[0], None)))
            w_attn_in = _blocks_to_cols(w_attn_in)
            proj, w_attn_out = _matmul(h, w_attn_in, dims="nn", tm=1024, tn=640, tk=2048, name="attn_in_fwd",
                                       comm=gather((sb[1], 0)))
            qh = _heads_major(proj[:, :D_MODEL], ATTN_HEADS)
            kh = _heads_major(proj[:, D_MODEL:D_MODEL + ATTN_KV * ATTN_HD], ATTN_KV)
            vh = _heads_major(proj[:, D_MODEL + ATTN_KV * ATTN_HD:], ATTN_KV)
            oh, w_mlp1[0] = _attn_fwd(qh, kh, vh, attn_q_gain, attn_k_gain, slope_col, sink_col, "attn_fwd",
                                      comm=gather((sb[4], 1)))
            o = _heads_minor(oh)
            mix = (qh, kh, vh)
            w_out = w_attn_out
        else:
            h = _norm_mod_fwd(xi, norm_mix[i:i + 1], sc1, sh1, "norm_mix_fwd1")
            proj, w_hgrn_out = _matmul(h, w_hgrn_in, dims="nn", tm=1024, tn=1024, tk=2048, name="hgrn_in_fwd",
                                       comm=gather((sb[3], 0)))
            o, states, w_mlp1[1], w_mlp2[1] = _hgrn_fwd(proj, lb, og, "hgrn_fwd", comm=gather((sb[5], 1), (sb[7], 0)))
            mix = (proj, states)
            w_out = w_hgrn_out
        y, x1 = _matmul(o, w_out, dims="nn", tm=1024, tn=1024, tk=2048, name=f"mix_out_fwd{i}", epilogue="resgate",
                        extras=(xi, g1))
        h2 = _norm_mod_fwd(x1, norm_mlp[i:i + 1], sc2, sh2, f"norm_mlp_fwd{i}")
        if i == 0:
            act, act2, w_mlp2[0] = _matmul(h2, w_mlp1[0], dims="nn", tm=1024, tn=1024, tk=2048, name="mlp1_fwd0",
                                           epilogue="relu2", comm=gather((sb[6], 0)))
            z, x2, w_hgrn_in = _matmul(act2, w_mlp2[0], dims="nn", tm=1024, tn=1024, tk=2048, name="mlp2_fwd0",
                                       epilogue="resgate", extras=(x1, g2), comm=gather((sb[2], 1)))
        else:
            act, act2 = _matmul(h2, w_mlp1[1], dims="nn", tm=1024, tn=1024, tk=2048, name="mlp1_fwd1", epilogue="relu2")
            z, x2 = _matmul(act2, w_mlp2[1], dims="nn", tm=1024, tn=1024, tk=2048, name="mlp2_fwd1", epilogue="resgate",
                            extras=(x1, g2))
        saved.append((xi, h, o, y, x1, h2, act, act2, z, mix))
        xi = x2

    dx, loss_tile = _loss_head(xi, target, "loss_head")
    loss = lax.psum(loss_tile[0, 0], ("x", "y", "c"))

    scatter = lambda *items: _Exchange([("scatter", a, axis) for a, axis in items])
    shares, dmods, dnorm_mix, dnorm_mlp = {}, [None, None], [None, None], [None, None]
    wgrad = lambda a, b, name, tn=1024: _matmul(a, b, dims="tn", tm=1024, tn=tn, tk=1024, name=name, out_dtype=BF16)
    for i in (1, 0):
        sh1, sc1, g1, sh2, sc2, g2 = [mod_mine[i, j] for j in range(N_MOD)]
        xin, h, o, y, x1, h2, act, act2, z, mix = saved[i]
        dz, st_g2 = _gate_bwd(dx, z, g2, f"gate_mlp_bwd{i}")
        if i == 1:
            dpre = _matmul(dz, w_mlp2[1], dims="nt", tm=1024, tn=1024, tk=2048, name="mlp2_bwd1", epilogue="mul2a",
                           extras=(act,))
        else:
            dpre, shares["hgrn_w_in"] = _matmul(dz, w_mlp2[0], dims="nt", tm=1024, tn=1024, tk=2048, name="mlp2_bwd0",
                                                epilogue="mul2a", extras=(act,), comm=scatter((g_hgrn_in, 1)))
        g_mlp2 = wgrad(act2, dz, f"mlp2_wgrad{i}")
        dh2, shares[f"mlp_w2_{i}"] = _matmul(dpre, w_mlp1[i], dims="nt", tm=1024, tn=1024, tk=2048, name=f"mlp1_bwd{i}",
                                             comm=scatter((g_mlp2, 0)))
        g_mlp1 = wgrad(h2, dpre, f"mlp1_wgrad{i}")
        dx1, st_mlp = _norm_mod_bwd(x1, dh2, dx, norm_mlp[i:i + 1], sc2, f"norm_mlp_bwd{i}")
        dy, st_g1 = _gate_bwd(dx1, y, g1, f"gate_mix_bwd{i}")
        w_out = w_attn_out if i == 0 else w_hgrn_out
        do = _matmul(dy, w_out, dims="nt", tm=1024, tn=1024, tk=2048, name=f"mix_out_bwd{i}")
        g_out = wgrad(o, dy, f"mix_out_wgrad{i}")
        if i == 0:
            qh, kh, vh = mix
            dqh, dkh, dvh, dqg, dkg, dsk, shares["mlp_w1_0"], shares["attn_w_out"] = _attn_bwd(
                qh, kh, vh, _heads_major(do, ATTN_HEADS), attn_q_gain, attn_k_gain, slope_col, sink_col, "attn_bwd",
                comm=scatter((g_mlp1, 1), (g_out, 0)))
            dproj = jnp.concatenate([_heads_minor(dqh), _heads_minor(dkh).astype(BF16), _heads_minor(dvh).astype(BF16)],
                                    axis=1)
            d_q_gain, d_k_gain, d_sinks = dqg[0:1], dkg[0:1], dsk[:, 0].reshape(1, ATTN_HEADS)
            g_attn_in = _cols_to_blocks(wgrad(h, dproj, "mix_in_wgrad0", tn=640), attn_w_in.shape[2])
            dh, shares["attn_w_in"] = _matmul(dproj, w_attn_in, dims="nt", tm=1024, tn=1024, tk=dproj.shape[1] // 2,
                                              name="mix_in_bwd0", comm=scatter((g_attn_in, None)))
        else:
            proj, states = mix
            dproj, dlb, d_o_gain, shares["mlp_w1_1"], shares["hgrn_w_out"] = _hgrn_bwd(
                proj, states, do, lb, og, "hgrn_bwd", comm=scatter((g_mlp1, 1), (g_out, 0)))
            d_lb_logits = _lb_bwd(hgrn_lb_logits, dlb.reshape(1, D_MODEL), "lb_bwd")
            dh = _matmul(dproj, w_hgrn_in, dims="nt", tm=1024, tn=1024, tk=2048, name="mix_in_bwd1")
            g_hgrn_in = wgrad(h, dproj, "mix_in_wgrad1")
        dx, st_mix = _norm_mod_bwd(xin, dh, dx1, norm_mix[i:i + 1], sc1, f"norm_mix_bwd{i}")
        dmods[i] = jnp.concatenate([st_mix[0:1], st_mix[1:2], st_g1[0:1], st_mlp[0:1], st_mlp[1:2], st_g2[0:1]], axis=1)
        dnorm_mix[i], dnorm_mlp[i] = st_mix[2:3], st_mlp[2:3]

    names = ["attn_w_in", "attn_w_out", "hgrn_w_in", "hgrn_w_out", "mlp_w1_0", "mlp_w1_1", "mlp_w2_0", "mlp_w2_1"]
    shares = [shares[nm] for nm in names]
    moments = [(m_attn_w_in[0], v_attn_w_in[0]), (m_attn_w_out[0], v_attn_w_out[0]), (m_hgrn_w_in[0], v_hgrn_w_in[0]),
               (m_hgrn_w_out[0], v_hgrn_w_out[0]), (m_mlp_w1[0], v_mlp_w1[0]), (m_mlp_w1[1], v_mlp_w1[1]),
               (m_mlp_w2[0], v_mlp_w2[0]), (m_mlp_w2[1], v_mlp_w2[1])]
    big = {nm: _sum_adamw(sh, w, m, v, f"adamw_{nm}") for nm, sh, w, (m, v) in zip(names, shares, shards, moments)}

    small_w = [mod_b, norm_mix, norm_mlp, attn_q_gain, attn_k_gain, attn_sinks, hgrn_o_gain, hgrn_lb_logits]
    small_m = [m_mod_b, m_norm_mix, m_norm_mlp, m_attn_q_gain, m_attn_k_gain, m_attn_sinks, m_hgrn_o_gain, m_hgrn_lb_logits]
    small_v = [v_mod_b, v_norm_mix, v_norm_mlp, v_attn_q_gain, v_attn_k_gain, v_attn_sinks, v_hgrn_o_gain, v_hgrn_lb_logits]
    small_g = [jnp.concatenate(dmods, axis=0), jnp.concatenate(dnorm_mix, axis=0), jnp.concatenate(dnorm_mlp, axis=0),
               d_q_gain, d_k_gain, d_sinks, d_o_gain.reshape(hgrn_o_gain.shape), d_lb_logits]
    packed_g = _pack(small_g)
    pad_rows = (-packed_g.shape[0]) % 8
    pad8 = lambda a: jnp.pad(a, ((0, pad_rows), (0, 0)))
    all_small = _exchange([("gather", pad8(packed_g), None)], vmem=True, name="gather_small_grads")[0]
    sg, sd, sm, sv = _sum_adamw(all_small, pad8(_pack(small_w)), pad8(_pack(small_m)), pad8(_pack(small_v)),
                                "adamw_small")
    small = [_unpack(t, small_w) for t in (sg, sd, sm, sv)]

    n_modb = N_MOD * D_MODEL
    dmod_all = all_small[:, :2 * n_modb // 128, :].reshape(N_DEV, 2, n_modb)
    dmod_cols = lax.dynamic_slice_in_dim(dmod_all, me * n_mod, n_mod, axis=2).transpose(1, 0, 2)
    modw = _mod_w_update(c_all.T, dmod_cols, mod_w, m_mod_w, v_mod_w, "adamw_mod_w")

    def leaf(k):
        stack = lambda a, b: jnp.stack([big[a][k], big[b][k]])
        one = lambda a: big[a][k][None]
        s = small[k]
        return [modw[k], s[0], s[1], s[2], one("attn_w_in"), one("attn_w_out"), s[3], s[4], s[5], one("hgrn_w_in"),
                one("hgrn_w_out"), s[6], s[7], stack("mlp_w1_0", "mlp_w1_1"), stack("mlp_w2_0", "mlp_w2_1")]

    return (loss, dx[None], *leaf(0), *leaf(1), *leaf(2), *leaf(3))
```

```python
import functools
import math

import jax
import jax.numpy as jnp
from jax import lax
from jax.experimental import pallas as pl
from jax.experimental.pallas import tpu as pltpu

F32, BF16 = jnp.float32, jnp.bfloat16
N_DEV = 8
D_MODEL = 2048
N_MOD = 6
EPS = 1e-6
ATTN_HD, ATTN_HEADS, ATTN_KV, ATTN_GROUP, ATTN_BLOCK = 64, 32, 4, 8, 128
ATTN_SCALE = 1.0 / math.sqrt(ATTN_HD)
HGRN_HEADS, HGRN_DK, HGRN_CHUNK = 16, 128, 64
HGRN_SCALE = 1.0 / math.sqrt(HGRN_DK)
HGRN_CHUNK_SHIFT = HGRN_CHUNK.bit_length() - 1
assert 1 << HGRN_CHUNK_SHIFT == HGRN_CHUNK
D_FF = 4 * D_MODEL
ADAM_LR, ADAM_B1, ADAM_B2, ADAM_EPS, ADAM_WD, ADAM_STEP = 0.001, 0.9, 0.999, 1e-08, 0.01, 10
NEG_BIG = -1e30
VMEM_LIMIT = 56 * 1024 * 1024
MESH_ID = pl.DeviceIdType.MESH


def _params(*sem):
    return pltpu.CompilerParams(dimension_semantics=sem, vmem_limit_bytes=VMEM_LIMIT)


class _Exchange:
    def __init__(self, items):
        self.items = items
        self.n = len(items)
        self.out_shape = []
        for kind, a, axis in items:
            if kind == "gather":
                shape = (N_DEV,) + a.shape if axis is None else tuple(
                    d * N_DEV if i == axis else d for i, d in enumerate(a.shape))
            else:
                shape = a.shape if axis is None else (N_DEV,) + tuple(
                    d // N_DEV if i == axis else d for i, d in enumerate(a.shape))
            self.out_shape.append(jax.ShapeDtypeStruct(shape, a.dtype))
        self.scratch = [pltpu.SemaphoreType.DMA((7 * self.n,)), pltpu.SemaphoreType.DMA((7 * self.n,)),
                        pltpu.SemaphoreType.DMA((self.n,))]
        self.arrays = [a for _, a, _ in items]

    @staticmethod
    def _block(ref, b, axis, size):
        if axis is None:
            return ref.at[b]
        sl = pl.ds(pl.multiple_of(b * size, size), size)
        return ref.at[sl, :] if axis == 0 else ref.at[:, sl]

    def _copies(self, ins, outs, sems):
        send_sems, recv_sems, loc_sems = sems
        x, y, c = lax.axis_index("x"), lax.axis_index("y"), lax.axis_index("c")
        me = 4 * x + 2 * y + c
        local, sends, lands = [], [], []
        for a, (kind, arr, axis) in enumerate(self.items):
            gather = kind == "gather"
            size = None if axis is None else (arr.shape[axis] if gather else arr.shape[axis] // N_DEV)

            def src(b):
                return ins[a] if gather else self._block(ins[a], b, axis, size)

            def dst(b):
                return self._block(outs[a], b, axis, size) if gather else outs[a].at[b]

            local.append(pltpu.make_async_copy(src(me), dst(me), loc_sems.at[a]))
            for d in range(1, N_DEV):
                px = 1 - x if d & 4 else x
                py = 1 - y if d & 2 else y
                pc = 1 - c if d & 1 else c
                pid = 4 * px + 2 * py + pc
                for land, group in ((False, sends), (True, lands)):
                    group.append(pltpu.make_async_remote_copy(
                        src_ref=src(pid), dst_ref=dst(pid if land else me),
                        send_sem=send_sems.at[a * 7 + d - 1], recv_sem=recv_sems.at[a * 7 + d - 1],
                        device_id=(px, py, pc), device_id_type=MESH_ID))
        return local, sends, lands

    def start(self, ins, outs, sems):
        local, sends, _ = self._copies(ins, outs, sems)
        for cp in local + sends:
            cp.start()

    def wait(self, ins, outs, sems):
        local, _, lands = self._copies(ins, outs, sems)
        for cp in lands:
            cp.wait()
        for cp in local:
            cp.wait()


def _call(body, *, name, grid, in_specs, out_specs, out_shape, args, scratch_shapes=(), sem=None, comm=None):
    n_in, n_out, n_scr = len(in_specs), len(out_specs), len(scratch_shapes)
    if comm is None:
        return pl.pallas_call(
            body, name=name, grid=grid, out_shape=list(out_shape), in_specs=list(in_specs), out_specs=list(out_specs),
            scratch_shapes=list(scratch_shapes), compiler_params=_params(*sem))(*args)
    hbm = pl.BlockSpec(memory_space=pltpu.HBM)

    def carrier(*refs):
        bounds = [0, n_in, n_in + comm.n, n_in + comm.n + n_out, n_in + 2 * comm.n + n_out, len(refs) - 3, len(refs)]
        ins, cin, outs, cout, scr, sems = [refs[lo:hi] for lo, hi in zip(bounds[:-1], bounds[1:])]
        assert len(scr) == n_scr
        ids = [pl.program_id(ax) for ax in range(len(grid))]
        first = functools.reduce(lambda p, q: p & q, [i == 0 for i in ids])
        last = functools.reduce(lambda p, q: p & q, [i == g - 1 for i, g in zip(ids, grid)])

        @pl.when(first)
        def _():
            comm.start(cin, cout, sems)

        body(*ins, *outs, *scr)

        @pl.when(last)
        def _():
            comm.wait(cin, cout, sems)

    return pl.pallas_call(
        carrier, name=name, grid=grid, out_shape=list(out_shape) + comm.out_shape,
        in_specs=list(in_specs) + [hbm] * comm.n, out_specs=list(out_specs) + [hbm] * comm.n,
        scratch_shapes=list(scratch_shapes) + comm.scratch,
        compiler_params=_params(*["arbitrary"] * len(grid)))(*args, *comm.arrays)


def _exchange(items, *, name, vmem=False):
    comm = _Exchange(items)
    space = pl.BlockSpec(memory_space=pltpu.VMEM if vmem else pltpu.HBM)

    def body(*refs):
        ins, outs, sems = refs[:comm.n], refs[comm.n:2 * comm.n], refs[2 * comm.n:]
        comm.start(ins, outs, sems)
        comm.wait(ins, outs, sems)

    return pl.pallas_call(
        body, name=name, out_shape=comm.out_shape, in_specs=[space] * comm.n, out_specs=[space] * comm.n,
        scratch_shapes=comm.scratch, compiler_params=pltpu.CompilerParams(vmem_limit_bytes=VMEM_LIMIT))(*comm.arrays)


_DIMS = {"nn": (((1,), (0,)), ((), ())), "nt": (((1,), (1,)), ((), ())), "tn": (((0,), (0,)), ((), ()))}


def _matmul(a, b, *, dims, tm, tn, tk, name, epilogue="plain", out_dtype=F32, extras=(), comm=None):
    a_parts = a.shape[0] if a.ndim == 3 else 0
    b_parts = b.shape[0] if b.ndim == 3 else 0
    assert not (a_parts and dims != "nt") and not (b_parts and dims != "tn")
    a2 = (a.shape[1], a.shape[2] * a_parts) if a_parts else a.shape
    b2 = (b.shape[1], b.shape[2] * b_parts) if b_parts else b.shape
    if dims == "tn":
        (K, M), N = a2, b2[1]
    else:
        (M, K), N = a2, (b2[1] if dims == "nn" else b2[0])
    tm, tn, tk = min(tm, M), min(tn, N), min(tk, K)
    assert M % tm == 0 and N % tn == 0 and K % tk == 0, (name, M, N, K, tm, tn, tk)
    if a_parts:
        per = K // a_parts // tk
        a_spec = pl.BlockSpec((None, tm, tk), lambda i, j, k: (k // per, i, k % per))
    elif dims == "tn":
        a_spec = pl.BlockSpec((tk, tm), lambda i, j, k: (k, i))
    else:
        a_spec = pl.BlockSpec((tm, tk), lambda i, j, k: (i, k))
    if b_parts:
        per_n = N // b_parts // tn
        b_spec = pl.BlockSpec((None, tk, tn), lambda i, j, k: (j // per_n, k, j % per_n))
    elif dims == "nt":
        b_spec = pl.BlockSpec((tn, tk), lambda i, j, k: (j, k))
    else:
        b_spec = pl.BlockSpec((tk, tn), lambda i, j, k: (k, j))
    nk = K // tk
    tile = pl.BlockSpec((tm, tn), lambda i, j, k: (i, j))
    row = pl.BlockSpec((1, tn), lambda i, j, k: (0, j))
    if epilogue == "plain":
        extra_specs, out_shape, out_specs = [], [jax.ShapeDtypeStruct((M, N), out_dtype)], [tile]
    elif epilogue == "relu2":
        extra_specs, out_shape, out_specs = [], [jax.ShapeDtypeStruct((M, N), BF16)] * 2, [tile, tile]
    elif epilogue == "resgate":
        extra_specs, out_shape, out_specs = [tile, row], [jax.ShapeDtypeStruct((M, N), F32)] * 2, [tile, tile]
    elif epilogue == "mul2a":
        extra_specs, out_shape, out_specs = [tile], [jax.ShapeDtypeStruct((M, N), BF16)], [tile]
    else:
        raise ValueError(epilogue)
    n_extra = len(extra_specs)

    def body(a_ref, b_ref, *rest):
        ex, outs, acc_ref = rest[:n_extra], rest[n_extra:-1], rest[-1]
        k = pl.program_id(2)

        @pl.when(k == 0)
        def _():
            acc_ref[...] = jnp.zeros_like(acc_ref)

        acc_ref[...] += lax.dot_general(a_ref[...], b_ref[...], _DIMS[dims], preferred_element_type=F32)

        @pl.when(k == nk - 1)
        def _():
            acc = acc_ref[...]
            if epilogue == "plain":
                outs[0][...] = acc.astype(out_dtype)
            elif epilogue == "relu2":
                act = jnp.maximum(acc, 0.0)
                outs[0][...] = act.astype(BF16)
                outs[1][...] = (act * act).astype(BF16)
            elif epilogue == "resgate":
                outs[0][...] = acc
                outs[1][...] = ex[0][...] + ex[1][...] * acc
            else:
                outs[0][...] = (acc * (2.0 * ex[0][...].astype(F32))).astype(BF16)

    res = _call(body, name=name, grid=(M // tm, N // tn, nk), out_shape=out_shape,
                in_specs=[a_spec, b_spec] + extra_specs, out_specs=out_specs,
                scratch_shapes=[pltpu.VMEM((tm, tn), F32)], sem=("parallel", "parallel", "arbitrary"),
                args=(a, b, *extras), comm=comm)
    return res[0] if len(res) == 1 else res


def _row_tile(T):
    return min(T, 256)


def _norm_mod_fwd(x, gain, sc, sh, name, comm=None):
    T, D = x.shape
    tr = _row_tile(T)

    def body(x_ref, g_ref, sc_ref, sh_ref, h_ref):
        xv = x_ref[...]
        r = lax.rsqrt(jnp.mean(xv * xv, axis=-1, keepdims=True) + EPS)
        hn = (xv * r) * g_ref[...]
        h_ref[...] = (hn * (1.0 + sc_ref[...]) + sh_ref[...]).astype(BF16)

    vec = pl.BlockSpec((1, D), lambda i: (0, 0))
    res = _call(body, name=name, grid=(T // tr,), out_shape=[jax.ShapeDtypeStruct((T, D), BF16)],
                in_specs=[pl.BlockSpec((tr, D), lambda i: (i, 0)), vec, vec, vec],
                out_specs=[pl.BlockSpec((tr, D), lambda i: (i, 0))], sem=("parallel",), args=(x, gain, sc, sh),
                comm=comm)
    return res[0] if comm is None else res


def _norm_mod_bwd(x, dh, dres, gain, sc, name):
    T, D = x.shape
    tr = _row_tile(T)

    def body(x_ref, dh_ref, dres_ref, g_ref, sc_ref, dx_ref, st_ref):
        xv, dh_v, gain_v = x_ref[...], dh_ref[...], g_ref[...]
        r = lax.rsqrt(jnp.mean(xv * xv, axis=-1, keepdims=True) + EPS)
        xn = xv * r
        hn = xn * gain_v
        dhn = dh_v * (1.0 + sc_ref[...])
        dxn = dhn * gain_v
        dx_ref[...] = dres_ref[...] + r * (dxn - xn * jnp.mean(dxn * xn, axis=-1, keepdims=True))

        @pl.when(pl.program_id(0) == 0)
        def _():
            st_ref[...] = jnp.zeros_like(st_ref)

        st_ref[0:1, :] += jnp.sum(dh_v, axis=0, keepdims=True)
        st_ref[1:2, :] += jnp.sum(dh_v * hn, axis=0, keepdims=True)
        st_ref[2:3, :] += jnp.sum(dhn * xn, axis=0, keepdims=True)

    vec = pl.BlockSpec((1, D), lambda i: (0, 0))
    blk = pl.BlockSpec((tr, D), lambda i: (i, 0))
    return pl.pallas_call(
        body, name=name, grid=(T // tr,),
        out_shape=[jax.ShapeDtypeStruct((T, D), F32), jax.ShapeDtypeStruct((8, D), F32)],
        in_specs=[blk, blk, blk, vec, vec],
        out_specs=[blk, pl.BlockSpec((8, D), lambda i: (0, 0))],
        compiler_params=_params("arbitrary"),
    )(x, dh, dres, gain, sc)


def _gate_bwd(dx, y, gate, name):
    T, D = dx.shape
    tr = _row_tile(T)

    def body(dx_ref, y_ref, g_ref, dy_ref, st_ref):
        dxv = dx_ref[...]
        dy_ref[...] = (dxv * g_ref[...]).astype(BF16)

        @pl.when(pl.program_id(0) == 0)
        def _():
            st_ref[...] = jnp.zeros_like(st_ref)

        st_ref[0:1, :] += jnp.sum(dxv * y_ref[...], axis=0, keepdims=True)

    blk = pl.BlockSpec((tr, D), lambda i: (i, 0))
    return pl.pallas_call(
        body, name=name, grid=(T // tr,),
        out_shape=[jax.ShapeDtypeStruct((T, D), BF16), jax.ShapeDtypeStruct((8, D), F32)],
        in_specs=[blk, blk, pl.BlockSpec((1, D), lambda i: (0, 0))],
        out_specs=[blk, pl.BlockSpec((8, D), lambda i: (0, 0))],
        compiler_params=_params("arbitrary"),
    )(dx, y, gate)


def _loss_head(y, target, name):
    T, D = y.shape
    tr = _row_tile(T)

    def body(y_ref, t_ref, dy_ref, l_ref):
        err = y_ref[...] - t_ref[...]
        dy_ref[...] = err * (1.0 / D)

        @pl.when(pl.program_id(0) == 0)
        def _():
            l_ref[...] = jnp.zeros_like(l_ref)

        part = jnp.sum(jnp.mean(err * err, axis=-1, keepdims=True), axis=0, keepdims=True)
        l_ref[...] += jnp.broadcast_to(0.5 * part, l_ref.shape)

    blk = pl.BlockSpec((tr, D), lambda i: (i, 0))
    return pl.pallas_call(
        body, name=name, grid=(T // tr,),
        out_shape=[jax.ShapeDtypeStruct((T, D), F32), jax.ShapeDtypeStruct((8, 128), F32)],
        in_specs=[blk, blk], out_specs=[blk, pl.BlockSpec((8, 128), lambda i: (0, 0))],
        compiler_params=_params("arbitrary"),
    )(y, target)


def _silu(v):
    return v * jax.nn.sigmoid(v)


def _mod_fwd(c_all, mod_w, mod_b_cols, name):
    L, D, n = mod_w.shape
    tn = 512

    def body(c_ref, w_ref, b_ref, o_ref):
        cond = _silu(c_ref[...]).astype(BF16)
        o_ref[0] = jnp.dot(cond, w_ref[0].astype(BF16), preferred_element_type=F32) + b_ref[0]

    return pl.pallas_call(
        body, name=name, grid=(L, n // tn), out_shape=jax.ShapeDtypeStruct((L, N_DEV, n), F32),
        in_specs=[pl.BlockSpec((N_DEV, D), lambda l, j: (0, 0)), pl.BlockSpec((1, D, tn), lambda l, j: (l, 0, j)),
                  pl.BlockSpec((1, 1, tn), lambda l, j: (l, 0, j))],
        out_specs=pl.BlockSpec((1, N_DEV, tn), lambda l, j: (l, 0, j)),
        compiler_params=_params("parallel", "parallel"),
    )(c_all, mod_w, mod_b_cols)


def _adamw(g, w, m, v):
    m = ADAM_B1 * m + (1.0 - ADAM_B1) * g
    v = ADAM_B2 * v + (1.0 - ADAM_B2) * (g * g)
    m_hat = m / (1.0 - ADAM_B1 ** ADAM_STEP)
    v_hat = v / (1.0 - ADAM_B2 ** ADAM_STEP)
    delta = -ADAM_LR * (m_hat / (jnp.sqrt(v_hat) + ADAM_EPS) + ADAM_WD * w)
    return delta, m, v


def _mod_w_update(c_t, dmod_cols, w, m, v, name):
    L, D, n = w.shape
    tr = 256

    def body(c_ref, dm_ref, w_ref, m_ref, v_ref, g_ref, d_ref, nm_ref, nv_ref):
        cond = _silu(c_ref[...])
        dm = dm_ref[0]
        g = cond[:, 0:1] * dm[0:1, :]
        for b in range(1, N_DEV):
            g = g + cond[:, b:b + 1] * dm[b:b + 1, :]
        delta, nm, nv = _adamw(g, w_ref[0], m_ref[0], v_ref[0])
        g_ref[0], d_ref[0], nm_ref[0], nv_ref[0] = g, delta, nm, nv

    blk = pl.BlockSpec((1, tr, n), lambda l, i: (l, i, 0))
    return pl.pallas_call(
        body, name=name, grid=(L, D // tr), out_shape=[jax.ShapeDtypeStruct(w.shape, F32)] * 4,
        in_specs=[pl.BlockSpec((tr, N_DEV), lambda l, i: (i, 0)), pl.BlockSpec((1, N_DEV, n), lambda l, i: (l, 0, 0)),
                  blk, blk, blk],
        out_specs=[blk] * 4, compiler_params=_params("parallel", "parallel"),
    )(c_t, dmod_cols, w, m, v)


def _sum_adamw(parts, w, m, v, name):
    R, C = w.shape
    tr = min(R, 256 if C >= 1024 else 1024)
    assert R % tr == 0

    def body(p_ref, w_ref, m_ref, v_ref, g_ref, d_ref, nm_ref, nv_ref):
        g = p_ref[0].astype(F32)
        for s in range(1, N_DEV):
            g = g + p_ref[s].astype(F32)
        delta, nm, nv = _adamw(g, w_ref[...], m_ref[...], v_ref[...])
        g_ref[...], d_ref[...], nm_ref[...], nv_ref[...] = g, delta, nm, nv

    blk = pl.BlockSpec((tr, C), lambda i: (i, 0))
    return pl.pallas_call(
        body, name=name, grid=(R // tr,), out_shape=[jax.ShapeDtypeStruct((R, C), F32)] * 4,
        in_specs=[pl.BlockSpec((N_DEV, tr, C), lambda i: (0, i, 0)), blk, blk, blk], out_specs=[blk] * 4,
        compiler_params=_params("parallel"),
    )(parts, w, m, v)


def _lower_bound_row1(l0, l1):
    mx = lax.stop_gradient(jnp.maximum(l0, l1))
    e0, e1 = jnp.exp(l0 - mx), jnp.exp(l1 - mx)
    p0, p1 = e0 / (e0 + e1), e1 / (e0 + e1)
    return (p0 + p1) - p0


def _lb_fwd(logits, name):
    def body(l_ref, o_ref):
        o_ref[...] = _lower_bound_row1(l_ref[0:1, :], l_ref[1:2, :])

    return pl.pallas_call(body, name=name, out_shape=jax.ShapeDtypeStruct((1, logits.shape[1]), F32))(logits)


def _lb_bwd(logits, dlb, name):
    def body(l_ref, d_ref, o_ref):
        _, vjp = jax.vjp(_lower_bound_row1, l_ref[0:1, :], l_ref[1:2, :])
        d0, d1 = vjp(d_ref[...])
        o_ref[0:1, :] = d0
        o_ref[1:2, :] = d1

    return pl.pallas_call(body, name=name, out_shape=jax.ShapeDtypeStruct(logits.shape, F32))(logits, dlb)


def _bdot(a, b, dims):
    return lax.dot_general(a.astype(BF16), b.astype(BF16), _DIMS[dims], preferred_element_type=F32)


def _attn_bias(slopes):
    rows = ATTN_GROUP * ATTN_BLOCK
    qi = lax.broadcasted_iota(jnp.int32, (rows, 2 * ATTN_BLOCK), 0) % ATTN_BLOCK
    ki = lax.broadcasted_iota(jnp.int32, (rows, 2 * ATTN_BLOCK), 1)
    dist = qi + ATTN_BLOCK - ki
    in_band = (dist >= 0) & (dist < ATTN_BLOCK)
    slope_rows = jnp.repeat(slopes.reshape(ATTN_KV, ATTN_GROUP), ATTN_BLOCK, axis=1)[:, :, None]
    alibi = -slope_rows * jnp.abs(dist).astype(F32)[None]
    later = jnp.where(in_band[None], alibi, NEG_BIG)
    first = jnp.where((in_band & (ki >= ATTN_BLOCK))[None], alibi, NEG_BIG)
    return jnp.stack([first, later])


def _attn_probs(q_ref, kp_ref, kc_ref, qg, kg, bias, sink):
    rows = ATTN_GROUP * ATTN_BLOCK
    q = q_ref[...].reshape(rows, ATTN_HD)
    rq = lax.rsqrt(jnp.mean(q * q, axis=-1, keepdims=True) + EPS)
    qhat = q * rq
    k = jnp.concatenate([kp_ref[0], kc_ref[0]], axis=0)
    rk = lax.rsqrt(jnp.mean(k * k, axis=-1, keepdims=True) + EPS)
    khat = k * rk
    qn, kn = qhat * qg, khat * kg
    s = _bdot(qn, kn, "nt") * ATTN_SCALE + bias
    mx = jnp.maximum(jnp.max(s, axis=-1, keepdims=True), sink)
    e = jnp.exp(s - mx)
    es = jnp.exp(sink - mx)
    inv = 1.0 / (jnp.sum(e, axis=-1, keepdims=True) + es)
    return e * inv, es * inv, (qhat, rq, qn), (khat, rk, kn)


def _attn_specs(T):
    nb = T // ATTN_BLOCK
    rows = ATTN_GROUP * ATTN_BLOCK
    qspec = pl.BlockSpec((ATTN_GROUP, ATTN_BLOCK, ATTN_HD), lambda h, n: (h, n, 0))
    prev = pl.BlockSpec((1, ATTN_BLOCK, ATTN_HD), lambda h, n: (h, jnp.maximum(n - 1, 0), 0))
    cur = pl.BlockSpec((1, ATTN_BLOCK, ATTN_HD), lambda h, n: (h, n, 0))
    gain = pl.BlockSpec((1, ATTN_HD), lambda h, n: (0, 0))
    bias = pl.BlockSpec((None, None, rows, 2 * ATTN_BLOCK), lambda h, n: (jnp.minimum(n, 1), h, 0, 0))
    col = pl.BlockSpec((1, rows, 1), lambda h, n: (h, 0, 0))
    return nb, qspec, prev, cur, gain, bias, col


def _attn_fwd(qh, kh, vh, qg, kg, bias, sink_col, name, comm=None):
    T = qh.shape[1]
    nb, qspec, prev, cur, gain, bspec, col = _attn_specs(T)

    def body(q_ref, kp_ref, kc_ref, vp_ref, vc_ref, qg_ref, kg_ref, b_ref, sk_ref, o_ref):
        p, _, _, _ = _attn_probs(q_ref, kp_ref, kc_ref, qg_ref[...], kg_ref[...], b_ref[...], sk_ref[0])
        v = jnp.concatenate([vp_ref[0], vc_ref[0]], axis=0)
        o_ref[...] = _bdot(p, v, "nn").reshape(ATTN_GROUP, ATTN_BLOCK, ATTN_HD).astype(BF16)

    return _call(body, name=name, grid=(ATTN_KV, nb), out_shape=[jax.ShapeDtypeStruct(qh.shape, BF16)],
                 in_specs=[qspec, prev, cur, prev, cur, gain, gain, bspec, col], out_specs=[qspec],
                 sem=("parallel", "parallel"), args=(qh, kh, kh, vh, vh, qg, kg, bias, sink_col), comm=comm)


def _rms_bwd(dy, xhat, r, gain):
    dxh = dy * gain
    dx = r * (dxh - xhat * jnp.mean(dxh * xhat, axis=-1, keepdims=True))
    return dx, jnp.sum(dy * xhat, axis=0, keepdims=True)


def _attn_bwd(qh, kh, vh, doh, qg, kg, bias, sink_col, name, comm=None):
    T = qh.shape[1]
    nb, qspec, prev, cur, gain, bspec, col = _attn_specs(T)
    rows = ATTN_GROUP * ATTN_BLOCK

    def body(q_ref, kp_ref, kc_ref, vp_ref, vc_ref, do_ref, qg_ref, kg_ref, b_ref, sk_ref,
             dq_ref, dk_ref, dv_ref, dqg_ref, dkg_ref, dsk_ref, sk_acc):
        h, n = pl.program_id(0), pl.program_id(1)
        qg_v, kg_v = qg_ref[...], kg_ref[...]
        p, ps, (qhat, rq, qn), (khat, rk, kn) = _attn_probs(q_ref, kp_ref, kc_ref, qg_v, kg_v, b_ref[...], sk_ref[0])
        v = jnp.concatenate([vp_ref[0], vc_ref[0]], axis=0)
        do = do_ref[...].reshape(rows, ATTN_HD)
        dp = _bdot(do, v, "nt")
        delta = jnp.sum(p * dp, axis=-1, keepdims=True)
        ds = p * (dp - delta)
        dqn = _bdot(ds, kn, "nn") * ATTN_SCALE
        dkn = _bdot(ds, qn, "tn") * ATTN_SCALE
        dv = _bdot(p, do, "tn")
        dq, dqg = _rms_bwd(dqn, qhat, rq, qg_v)
        dk, dkg = _rms_bwd(dkn, khat, rk, kg_v)
        dq_ref[...] = dq.reshape(ATTN_GROUP, ATTN_BLOCK, ATTN_HD).astype(BF16)

        @pl.when((h == 0) & (n == 0))
        def _():
            dqg_ref[...] = jnp.zeros_like(dqg_ref)
            dkg_ref[...] = jnp.zeros_like(dkg_ref)

        dqg_ref[0:1, :] += dqg
        dkg_ref[0:1, :] += dkg

        @pl.when(n == 0)
        def _():
            dk_ref[...] = jnp.zeros_like(dk_ref)
            dv_ref[...] = jnp.zeros_like(dv_ref)
            sk_acc[...] = jnp.zeros_like(sk_acc)
            dk_ref[0, 0:ATTN_BLOCK, :] += dk[ATTN_BLOCK:, :]
            dv_ref[0, 0:ATTN_BLOCK, :] += dv[ATTN_BLOCK:, :]

        @pl.when(n > 0)
        def _():
            band = pl.ds(pl.multiple_of((n - 1) * ATTN_BLOCK, ATTN_BLOCK), 2 * ATTN_BLOCK)
            dk_ref[0, band, :] += dk
            dv_ref[0, band, :] += dv

        sk_acc[...] += -ps * delta

        @pl.when(n == nb - 1)
        def _():
            for g in range(ATTN_GROUP):
                tot = jnp.sum(sk_acc[g * ATTN_BLOCK:(g + 1) * ATTN_BLOCK, :], axis=0, keepdims=True)
                dsk_ref[g:g + 1, :] = jnp.broadcast_to(tot, (1, 128))

    kv_full = pl.BlockSpec((1, T, ATTN_HD), lambda h, n: (h, 0, 0))
    acc = pl.BlockSpec((8, ATTN_HD), lambda h, n: (0, 0))
    return _call(
        body, name=name, grid=(ATTN_KV, nb),
        out_shape=[jax.ShapeDtypeStruct(qh.shape, BF16), jax.ShapeDtypeStruct(kh.shape, F32),
                   jax.ShapeDtypeStruct(kh.shape, F32), jax.ShapeDtypeStruct((8, ATTN_HD), F32),
                   jax.ShapeDtypeStruct((8, ATTN_HD), F32), jax.ShapeDtypeStruct((ATTN_HEADS, 128), F32)],
        in_specs=[qspec, prev, cur, prev, cur, qspec, gain, gain, bspec, col],
        out_specs=[qspec, kv_full, kv_full, acc, acc, pl.BlockSpec((ATTN_GROUP, 128), lambda h, n: (h, 0))],
        scratch_shapes=[pltpu.VMEM((rows, 1), F32)], sem=("arbitrary", "arbitrary"),
        args=(qh, kh, kh, vh, vh, doh, qg, kg, bias, sink_col), comm=comm)


@functools.partial(jax.custom_vjp, nondiff_argnums=(2,))
def _mm(a, b, dims):
    return _bdot(a, b, dims)


def _mm_fwd(a, b, dims):
    return _bdot(a, b, dims), (a, b)


def _mm_bwd(dims, res, ct):
    a, b = res
    if dims == "nn":
        return _bdot(ct, b, "nt"), _bdot(a, ct, "tn")
    if dims == "nt":
        return _bdot(ct, b, "nn"), _bdot(ct, a, "tn")
    return _bdot(b, ct, "nt"), _bdot(a, ct, "nn")


_mm.defvjp(_mm_fwd, _mm_bwd)


def _same_chunk_mask(rows, upper):
    ri = lax.broadcasted_iota(jnp.int32, (rows, rows), 0)
    ci = lax.broadcasted_iota(jnp.int32, (rows, rows), 1)
    same = (ri >> HGRN_CHUNK_SHIFT) == (ci >> HGRN_CHUNK_SHIFT)
    return same & ((ri <= ci) if upper else (ri >= ci))


@functools.partial(jax.custom_vjp, nondiff_argnums=(1,))
def _chunk_cumsum(x, reverse):
    tri = _same_chunk_mask(x.shape[0], reverse).astype(BF16)
    hi = x.astype(BF16)
    rest = x - hi.astype(F32)
    mid = rest.astype(BF16)
    lo = (rest - mid.astype(F32)).astype(BF16)
    out = jnp.dot(tri, jnp.concatenate([hi, mid, lo], axis=1), preferred_element_type=F32)
    w = x.shape[1]
    return out[:, :w] + out[:, w:2 * w] + out[:, 2 * w:]


def _chunk_cumsum_fwd(x, reverse):
    return _chunk_cumsum(x, reverse), None


def _chunk_cumsum_bwd(reverse, _, ct):
    return (_chunk_cumsum(ct, not reverse),)


_chunk_cumsum.defvjp(_chunk_cumsum_fwd, _chunk_cumsum_bwd)


def _hgrn_block(st, qr, fr, v, gr, lb, og):
    rows = qr.shape[0]
    nc = rows // HGRN_CHUNK
    chunk_of_row = lax.broadcasted_iota(jnp.int32, qr.shape, 0) >> HGRN_CHUNK_SHIFT
    row_in_chunk = lax.broadcasted_iota(jnp.int32, (nc, HGRN_CHUNK, HGRN_DK), 1)
    per_chunk = lambda m: m.reshape(nc, HGRN_CHUNK, HGRN_DK)
    flat = lambda m: m.reshape(rows, HGRN_DK)
    by_chunk = lambda m: jnp.concatenate([jnp.where(chunk_of_row == c, m, 0.0) for c in range(nc)], axis=1)

    forget = lb + (1.0 - lb) * jax.nn.sigmoid(fr)
    k = 1.0 - forget
    b = _chunk_cumsum(jnp.log(forget), False)
    b3 = per_chunk(b)
    piv = jnp.sum(jnp.where(row_in_chunk == HGRN_CHUNK // 2 - 1, b3, 0.0), axis=1, keepdims=True)
    b_last = jnp.sum(jnp.where(row_in_chunk == HGRN_CHUNK - 1, b3, 0.0), axis=1, keepdims=True)
    q = _silu(qr) * HGRN_SCALE
    a = _mm(q * flat(jnp.exp(b3 - piv)), k * flat(jnp.exp(piv - b3)), "nt")
    o = _mm(jnp.where(_same_chunk_mask(rows, False), a, 0.0), v, "nn")
    updates = _mm(v, by_chunk(k * flat(jnp.exp(b_last - b3))), "tn")
    decay = jnp.exp(b_last)
    before = []
    for c in range(nc):
        before.append(st)
        st = st * decay[c] + updates[:, c * HGRN_DK:(c + 1) * HGRN_DK]
    o = o + _mm(by_chunk(q * jnp.exp(b)), jnp.concatenate(before, axis=1), "nt")
    y = (o * lax.rsqrt(jnp.mean(o * o, axis=-1, keepdims=True) + EPS)) * og * _silu(gr)
    return y, st


def _hgrn_tile(T):
    return min(T, 256)


HGRN_HEADS_PER_STEP = 4
HGRN_GROUPS = HGRN_HEADS // HGRN_HEADS_PER_STEP


def _hgrn_fwd(proj, lb, og, name, comm=None):
    T = proj.shape[0]
    tb = _hgrn_tile(T)
    hp, wide = HGRN_HEADS_PER_STEP, HGRN_HEADS_PER_STEP * HGRN_DK

    def body(q_ref, f_ref, v_ref, g_ref, lb_ref, og_ref, o_ref, s_ref, st_ref):
        @pl.when(pl.program_id(1) == 0)
        def _():
            st_ref[...] = jnp.zeros_like(st_ref)

        for j in range(hp):
            ln = slice(j * HGRN_DK, (j + 1) * HGRN_DK)
            st = st_ref[j]
            s_ref[j, 0] = st
            y, st_ref[j] = _hgrn_block(st, q_ref[:, ln], f_ref[:, ln], v_ref[:, ln], g_ref[:, ln], lb_ref[j], og_ref[j])
            o_ref[:, ln] = y.astype(BF16)

    part = lambda p: pl.BlockSpec((tb, wide), lambda h, t: (t, p * HGRN_GROUPS + h))
    vec = pl.BlockSpec((hp, 1, HGRN_DK), lambda h, t: (h, 0, 0))
    return _call(
        body, name=name, grid=(HGRN_GROUPS, T // tb),
        out_shape=[jax.ShapeDtypeStruct((T, D_MODEL), BF16),
                   jax.ShapeDtypeStruct((HGRN_HEADS, T // tb, HGRN_DK, HGRN_DK), F32)],
        in_specs=[part(0), part(1), part(2), part(3), vec, vec],
        out_specs=[pl.BlockSpec((tb, wide), lambda h, t: (t, h)),
                   pl.BlockSpec((hp, 1, HGRN_DK, HGRN_DK), lambda h, t: (h, t, 0, 0))],
        scratch_shapes=[pltpu.VMEM((hp, HGRN_DK, HGRN_DK), F32)], sem=("parallel", "arbitrary"),
        args=(proj, proj, proj, proj, lb, og), comm=comm)


def _hgrn_bwd(proj, states, do, lb, og, name, comm=None):
    T = proj.shape[0]
    tb = _hgrn_tile(T)
    nt, hp, wide = T // tb, HGRN_HEADS_PER_STEP, HGRN_HEADS_PER_STEP * HGRN_DK

    def body(q_ref, f_ref, v_ref, g_ref, s_ref, do_ref, lb_ref, og_ref, dp_ref, dlb_ref, dog_ref, dst_ref):
        @pl.when(pl.program_id(1) == 0)
        def _():
            dst_ref[...] = jnp.zeros_like(dst_ref)
            dlb_ref[...] = jnp.zeros_like(dlb_ref)
            dog_ref[...] = jnp.zeros_like(dog_ref)

        for j in range(hp):
            ln = slice(j * HGRN_DK, (j + 1) * HGRN_DK)
            _, vjp = jax.vjp(_hgrn_block, s_ref[j, 0], q_ref[:, ln], f_ref[:, ln], v_ref[:, ln], g_ref[:, ln],
                             lb_ref[j], og_ref[j])
            dst_ref[j], dq, df, dv, dg, dlb, dog = vjp((do_ref[:, ln], dst_ref[j]))
            for p, part_grad in enumerate((dq, df, dv, dg)):
                dp_ref[p, :, ln] = part_grad.astype(BF16)
            dlb_ref[j] += dlb
            dog_ref[j] += dog

    part = lambda p: pl.BlockSpec((tb, wide), lambda h, t: (nt - 1 - t, p * HGRN_GROUPS + h))
    vec = pl.BlockSpec((hp, 1, HGRN_DK), lambda h, t: (h, 0, 0))
    head = pl.BlockSpec((tb, wide), lambda h, t: (nt - 1 - t, h))
    return _call(
        body, name=name, grid=(HGRN_GROUPS, nt),
        out_shape=[jax.ShapeDtypeStruct((4, T, D_MODEL), BF16)] + [jax.ShapeDtypeStruct((HGRN_HEADS, 1, HGRN_DK), F32)] * 2,
        in_specs=[part(0), part(1), part(2), part(3),
                  pl.BlockSpec((hp, 1, HGRN_DK, HGRN_DK), lambda h, t: (h, nt - 1 - t, 0, 0)), head, vec, vec],
        out_specs=[pl.BlockSpec((4, tb, wide), lambda h, t: (0, nt - 1 - t, h)), vec, vec],
        scratch_shapes=[pltpu.VMEM((hp, HGRN_DK, HGRN_DK), F32)], sem=("parallel", "arbitrary"),
        args=(proj, proj, proj, proj, states, do, lb, og), comm=comm)


def _cols_to_blocks(g, n8):
    K = g.shape[0]
    return g.reshape(K, N_DEV, n8).transpose(1, 0, 2)


def _blocks_to_cols(wg):
    _, K, n8 = wg.shape
    return wg.transpose(1, 0, 2).reshape(K, N_DEV * n8)


def _pack(parts):
    flat = []
    for p in parts:
        v = p.reshape(-1)
        flat.append(jnp.pad(v, (0, (-v.shape[0]) % 1024)))
    return jnp.concatenate(flat).reshape(-1, 128)


def _unpack(packed, like):
    flat, out, off = packed.reshape(-1), [], 0
    for p in like:
        size = math.prod(p.shape)
        out.append(flat[off:off + size].reshape(p.shape))
        off += size + (-size) % 1024
    return out


def _heads_major(a, heads):
    T = a.shape[0]
    return a.reshape(T, heads, ATTN_HD).transpose(1, 0, 2)


def _heads_minor(a):
    heads, T, _ = a.shape
    return a.transpose(1, 0, 2).reshape(T, heads * ATTN_HD)


def kernel(x, c, mod_w, mod_b, norm_mix, norm_mlp, attn_w_in, attn_w_out, attn_q_gain, attn_k_gain, attn_sinks, hgrn_w_in, hgrn_w_out, hgrn_o_gain, hgrn_lb_logits, mlp_w1, mlp_w2, loss_target, m_mod_w, m_mod_b, m_norm_mix, m_norm_mlp, m_attn_w_in, m_attn_w_out, m_attn_q_gain, m_attn_k_gain, m_attn_sinks, m_hgrn_w_in, m_hgrn_w_out, m_hgrn_o_gain, m_hgrn_lb_logits, m_mlp_w1, m_mlp_w2, v_mod_w, v_mod_b, v_norm_mix, v_norm_mlp, v_attn_w_in, v_attn_w_out, v_attn_q_gain, v_attn_k_gain, v_attn_sinks, v_hgrn_w_in, v_hgrn_w_out, v_hgrn_o_gain, v_hgrn_lb_logits, v_mlp_w1, v_mlp_w2):
    T = x.shape[1]
    me = 4 * lax.axis_index("x") + 2 * lax.axis_index("y") + lax.axis_index("c")
    x0, target = x[0], loss_target[0]
    n_mod = mod_w.shape[2]

    shards = [attn_w_in[0], attn_w_out[0], hgrn_w_in[0], hgrn_w_out[0], mlp_w1[0], mlp_w1[1], mlp_w2[0], mlp_w2[1]]
    sb = [s.astype(BF16) for s in shards]
    gather = lambda *items: _Exchange([("gather", a, axis) for a, axis in items])

    c_all = _exchange([("gather", c.reshape(16, 128), None)], vmem=True, name="gather_c")[0].reshape(N_DEV, D_MODEL)
    mod_b_cols = lax.dynamic_slice_in_dim(mod_b, me * n_mod, n_mod, axis=1).reshape(2, 1, n_mod)
    mod_cols = _mod_fwd(c_all, mod_w, mod_b_cols, "mod_fwd")
    mod_all = _exchange([("gather", mod_cols.reshape(-1, 128), None)], vmem=True, name="gather_mod")[0]
    mod_all = mod_all.reshape(N_DEV, 2, N_DEV, n_mod)
    mod_mine = lax.dynamic_index_in_dim(mod_all, me, axis=2, keepdims=False)
    mod_mine = mod_mine.transpose(1, 0, 2).reshape(2, N_MOD, 1, D_MODEL)

    lb = _lb_fwd(hgrn_lb_logits, "lb_fwd").reshape(HGRN_HEADS, 1, HGRN_DK)
    og = hgrn_o_gain.reshape(HGRN_HEADS, 1, HGRN_DK)
    slopes = jnp.exp2(-8.0 * jnp.arange(1, ATTN_HEADS + 1, dtype=F32) / ATTN_HEADS)
    attn_bias = _attn_bias(slopes)
    sink_col = jnp.repeat(attn_sinks[0].reshape(ATTN_KV, ATTN_GROUP), ATTN_BLOCK, axis=1).reshape(
        ATTN_KV, ATTN_GROUP * ATTN_BLOCK, 1)

    saved = []
    xi = x0
    w_mlp1, w_mlp2 = [None, None], [None, None]
    for i in range(2):
        sh1, sc1, g1, sh2, sc2, g2 = [mod_mine[i, j] for j in range(N_MOD)]
        if i == 0:
            h, w_attn_in = _norm_mod_fwd(xi, norm_mix[i:i + 1], sc1, sh1, "norm_mix_fwd0", comm=gather((sb[0], None)))
            w_attn_in = _blocks_to_cols(w_attn_in)
            proj, w_attn_out = _matmul(h, w_attn_in, dims="nn", tm=1024, tn=640, tk=2048, name="attn_in_fwd",
                                       comm=gather((sb[1], 0)))
            qh = _heads_major(proj[:, :D_MODEL], ATTN_HEADS)
            kh = _heads_major(proj[:, D_MODEL:D_MODEL + ATTN_KV * ATTN_HD], ATTN_KV)
            vh = _heads_major(proj[:, D_MODEL + ATTN_KV * ATTN_HD:], ATTN_KV)
            oh, w_mlp1[0] = _attn_fwd(qh, kh, vh, attn_q_gain, attn_k_gain, attn_bias, sink_col, "attn_fwd",
                                      comm=gather((sb[4], 1)))
            o = _heads_minor(oh)
            mix = (qh, kh, vh)
            w_out = w_attn_out
        else:
            h = _norm_mod_fwd(xi, norm_mix[i:i + 1], sc1, sh1, "norm_mix_fwd1")
            proj, w_hgrn_out = _matmul(h, w_hgrn_in, dims="nn", tm=1024, tn=1024, tk=2048, name="hgrn_in_fwd",
                                       comm=gather((sb[3], 0)))
            o, states, w_mlp1[1] = _hgrn_fwd(proj, lb, og, "hgrn_fwd", comm=gather((sb[5], 1)))
            mix = (proj, states)
            w_out = w_hgrn_out
        y, x1 = _matmul(o, w_out, dims="nn", tm=1024, tn=1024, tk=2048, name=f"mix_out_fwd{i}", epilogue="resgate",
                        extras=(xi, g1))
        h2 = _norm_mod_fwd(x1, norm_mlp[i:i + 1], sc2, sh2, f"norm_mlp_fwd{i}")
        if i == 0:
            act, act2, w_mlp2[0] = _matmul(h2, w_mlp1[0], dims="nn", tm=1024, tn=1024, tk=2048, name="mlp1_fwd0",
                                           epilogue="relu2", comm=gather((sb[6], 0)))
            z, x2, w_hgrn_in = _matmul(act2, w_mlp2[0], dims="nn", tm=1024, tn=1024, tk=2048, name="mlp2_fwd0",
                                       epilogue="resgate", extras=(x1, g2), comm=gather((sb[2], 1)))
        else:
            act, act2, w_mlp2[1] = _matmul(h2, w_mlp1[1], dims="nn", tm=1024, tn=1024, tk=2048, name="mlp1_fwd1",
                                           epilogue="relu2", comm=gather((sb[7], 0)))
            z, x2 = _matmul(act2, w_mlp2[1], dims="nn", tm=1024, tn=1024, tk=2048, name="mlp2_fwd1", epilogue="resgate",
                            extras=(x1, g2))
        saved.append((xi, h, o, y, x1, h2, act, act2, z, mix))
        xi = x2

    dx, loss_tile = _loss_head(xi, target, "loss_head")
    loss = lax.psum(loss_tile[0, 0], ("x", "y", "c"))

    scatter = lambda *items: _Exchange([("scatter", a, axis) for a, axis in items])
    shares, dmods, dnorm_mix, dnorm_mlp = {}, [None, None], [None, None], [None, None]
    wgrad = lambda a, b, name, tn=1024: _matmul(a, b, dims="tn", tm=1024, tn=tn, tk=1024, name=name, out_dtype=BF16)
    for i in (1, 0):
        sh1, sc1, g1, sh2, sc2, g2 = [mod_mine[i, j] for j in range(N_MOD)]
        xin, h, o, y, x1, h2, act, act2, z, mix = saved[i]
        dz, st_g2 = _gate_bwd(dx, z, g2, f"gate_mlp_bwd{i}")
        if i == 1:
            dpre = _matmul(dz, w_mlp2[1], dims="nt", tm=1024, tn=1024, tk=2048, name="mlp2_bwd1", epilogue="mul2a",
                           extras=(act,))
        else:
            dpre, shares["hgrn_w_in"] = _matmul(dz, w_mlp2[0], dims="nt", tm=1024, tn=1024, tk=2048, name="mlp2_bwd0",
                                                epilogue="mul2a", extras=(act,), comm=scatter((g_hgrn_in, 1)))
        g_mlp2 = wgrad(act2, dz, f"mlp2_wgrad{i}")
        dh2, shares[f"mlp_w2_{i}"] = _matmul(dpre, w_mlp1[i], dims="nt", tm=1024, tn=1024, tk=2048, name=f"mlp1_bwd{i}",
                                             comm=scatter((g_mlp2, 0)))
        g_mlp1 = wgrad(h2, dpre, f"mlp1_wgrad{i}")
        dx1, st_mlp = _norm_mod_bwd(x1, dh2, dx, norm_mlp[i:i + 1], sc2, f"norm_mlp_bwd{i}")
        dy, st_g1 = _gate_bwd(dx1, y, g1, f"gate_mix_bwd{i}")
        w_out = w_attn_out if i == 0 else w_hgrn_out
        do = _matmul(dy, w_out, dims="nt", tm=1024, tn=1024, tk=2048, name=f"mix_out_bwd{i}")
        g_out = wgrad(o, dy, f"mix_out_wgrad{i}")
        if i == 0:
            qh, kh, vh = mix
            dqh, dkh, dvh, dqg, dkg, dsk, shares["mlp_w1_0"], shares["attn_w_out"] = _attn_bwd(
                qh, kh, vh, _heads_major(do, ATTN_HEADS), attn_q_gain, attn_k_gain, attn_bias, sink_col, "attn_bwd",
                comm=scatter((g_mlp1, 1), (g_out, 0)))
            dproj = jnp.concatenate([_heads_minor(dqh), _heads_minor(dkh).astype(BF16), _heads_minor(dvh).astype(BF16)],
                                    axis=1)
            d_q_gain, d_k_gain, d_sinks = dqg[0:1], dkg[0:1], dsk[:, 0].reshape(1, ATTN_HEADS)
            g_attn_in = _cols_to_blocks(wgrad(h, dproj, "mix_in_wgrad0", tn=640), attn_w_in.shape[2])
            dh, shares["attn_w_in"] = _matmul(dproj, w_attn_in, dims="nt", tm=1024, tn=1024, tk=dproj.shape[1] // 2,
                                              name="mix_in_bwd0", comm=scatter((g_attn_in, None)))
        else:
            proj, states = mix
            dproj, dlb, d_o_gain, shares["mlp_w1_1"], shares["hgrn_w_out"] = _hgrn_bwd(
                proj, states, do, lb, og, "hgrn_bwd", comm=scatter((g_mlp1, 1), (g_out, 0)))
            d_lb_logits = _lb_bwd(hgrn_lb_logits, dlb.reshape(1, D_MODEL), "lb_bwd")
            dh = _matmul(dproj, w_hgrn_in, dims="nt", tm=1024, tn=1024, tk=2048, name="mix_in_bwd1")
            g_hgrn_in = wgrad(h, dproj, "mix_in_wgrad1")
        dx, st_mix = _norm_mod_bwd(xin, dh, dx1, norm_mix[i:i + 1], sc1, f"norm_mix_bwd{i}")
        dmods[i] = jnp.concatenate([st_mix[0:1], st_mix[1:2], st_g1[0:1], st_mlp[0:1], st_mlp[1:2], st_g2[0:1]], axis=1)
        dnorm_mix[i], dnorm_mlp[i] = st_mix[2:3], st_mlp[2:3]

    names = ["attn_w_in", "attn_w_out", "hgrn_w_in", "hgrn_w_out", "mlp_w1_0", "mlp_w1_1", "mlp_w2_0", "mlp_w2_1"]
    shares = [shares[nm] for nm in names]
    moments = [(m_attn_w_in[0], v_attn_w_in[0]), (m_attn_w_out[0], v_attn_w_out[0]), (m_hgrn_w_in[0], v_hgrn_w_in[0]),
               (m_hgrn_w_out[0], v_hgrn_w_out[0]), (m_mlp_w1[0], v_mlp_w1[0]), (m_mlp_w1[1], v_mlp_w1[1]),
               (m_mlp_w2[0], v_mlp_w2[0]), (m_mlp_w2[1], v_mlp_w2[1])]
    big = {nm: _sum_adamw(sh, w, m, v, f"adamw_{nm}") for nm, sh, w, (m, v) in zip(names, shares, shards, moments)}

    small_w = [mod_b, norm_mix, norm_mlp, attn_q_gain, attn_k_gain, attn_sinks, hgrn_o_gain, hgrn_lb_logits]
    small_m = [m_mod_b, m_norm_mix, m_norm_mlp, m_attn_q_gain, m_attn_k_gain, m_attn_sinks, m_hgrn_o_gain, m_hgrn_lb_logits]
    small_v = [v_mod_b, v_norm_mix, v_norm_mlp, v_attn_q_gain, v_attn_k_gain, v_attn_sinks, v_hgrn_o_gain, v_hgrn_lb_logits]
    small_g = [jnp.concatenate(dmods, axis=0), jnp.concatenate(dnorm_mix, axis=0), jnp.concatenate(dnorm_mlp, axis=0),
               d_q_gain, d_k_gain, d_sinks, d_o_gain.reshape(hgrn_o_gain.shape), d_lb_logits]
    packed_g = _pack(small_g)
    pad_rows = (-packed_g.shape[0]) % 8
    pad8 = lambda a: jnp.pad(a, ((0, pad_rows), (0, 0)))
    all_small = _exchange([("gather", pad8(packed_g), None)], vmem=True, name="gather_small_grads")[0]
    sg, sd, sm, sv = _sum_adamw(all_small, pad8(_pack(small_w)), pad8(_pack(small_m)), pad8(_pack(small_v)),
                                "adamw_small")
    small = [_unpack(t, small_w) for t in (sg, sd, sm, sv)]

    n_modb = N_MOD * D_MODEL
    dmod_all = all_small[:, :2 * n_modb // 128, :].reshape(N_DEV, 2, n_modb)
    dmod_cols = lax.dynamic_slice_in_dim(dmod_all, me * n_mod, n_mod, axis=2).transpose(1, 0, 2)
    modw = _mod_w_update(c_all.T, dmod_cols, mod_w, m_mod_w, v_mod_w, "adamw_mod_w")

    def leaf(k):
        stack = lambda a, b: jnp.stack([big[a][k], big[b][k]])
        one = lambda a: big[a][k][None]
        s = small[k]
        return [modw[k], s[0], s[1], s[2], one("attn_w_in"), one("attn_w_out"), s[3], s[4], s[5], one("hgrn_w_in"),
                one("hgrn_w_out"), s[6], s[7], stack("mlp_w1_0", "mlp_w1_1"), stack("mlp_w2_0", "mlp_w2_1")]

    return (loss, dx[None], *leaf(0), *leaf(1), *leaf(2), *leaf(3))
```

```python
import functools
import math

import jax
import jax.numpy as jnp
from jax import lax
from jax.experimental import pallas as pl
from jax.experimental.pallas import tpu as pltpu

F32, BF16 = jnp.float32, jnp.bfloat16
N_DEV = 8
D_MODEL = 2048
N_MOD = 6
EPS = 1e-6
ATTN_HD, ATTN_HEADS, ATTN_KV, ATTN_GROUP, ATTN_BLOCK = 64, 32, 4, 8, 128
ATTN_SCALE = 1.0 / math.sqrt(ATTN_HD)
HGRN_HEADS, HGRN_DK, HGRN_CHUNK = 16, 128, 64
HGRN_SCALE = 1.0 / math.sqrt(HGRN_DK)
HGRN_CHUNK_SHIFT = HGRN_CHUNK.bit_length() - 1
assert 1 << HGRN_CHUNK_SHIFT == HGRN_CHUNK
D_FF = 4 * D_MODEL
ADAM_LR, ADAM_B1, ADAM_B2, ADAM_EPS, ADAM_WD, ADAM_STEP = 0.001, 0.9, 0.999, 1e-08, 0.01, 10
NEG_BIG = -1e30
VMEM_LIMIT = 56 * 1024 * 1024
MESH_ID = pl.DeviceIdType.MESH


def _params(*sem):
    return pltpu.CompilerParams(dimension_semantics=sem, vmem_limit_bytes=VMEM_LIMIT)


class _Exchange:
    def __init__(self, items):
        self.items = items
        self.n = len(items)
        self.out_shape = []
        for kind, a, axis in items:
            if kind == "gather":
                shape = (N_DEV,) + a.shape if axis is None else tuple(
                    d * N_DEV if i == axis else d for i, d in enumerate(a.shape))
            else:
                shape = a.shape if axis is None else (N_DEV,) + tuple(
                    d // N_DEV if i == axis else d for i, d in enumerate(a.shape))
            self.out_shape.append(jax.ShapeDtypeStruct(shape, a.dtype))
        self.scratch = [pltpu.SemaphoreType.DMA((7 * self.n,)), pltpu.SemaphoreType.DMA((7 * self.n,)),
                        pltpu.SemaphoreType.DMA((self.n,))]
        self.arrays = [a for _, a, _ in items]

    @staticmethod
    def _block(ref, b, axis, size):
        if axis is None:
            return ref.at[b]
        sl = pl.ds(pl.multiple_of(b * size, size), size)
        return ref.at[sl, :] if axis == 0 else ref.at[:, sl]

    def _copies(self, ins, outs, sems):
        send_sems, recv_sems, loc_sems = sems
        x, y, c = lax.axis_index("x"), lax.axis_index("y"), lax.axis_index("c")
        me = 4 * x + 2 * y + c
        local, sends, lands = [], [], []
        for a, (kind, arr, axis) in enumerate(self.items):
            gather = kind == "gather"
            size = None if axis is None else (arr.shape[axis] if gather else arr.shape[axis] // N_DEV)

            def src(b):
                return ins[a] if gather else self._block(ins[a], b, axis, size)

            def dst(b):
                return self._block(outs[a], b, axis, size) if gather else outs[a].at[b]

            local.append(pltpu.make_async_copy(src(me), dst(me), loc_sems.at[a]))
            for d in range(1, N_DEV):
                px = 1 - x if d & 4 else x
                py = 1 - y if d & 2 else y
                pc = 1 - c if d & 1 else c
                pid = 4 * px + 2 * py + pc
                for land, group in ((False, sends), (True, lands)):
                    group.append(pltpu.make_async_remote_copy(
                        src_ref=src(pid), dst_ref=dst(pid if land else me),
                        send_sem=send_sems.at[a * 7 + d - 1], recv_sem=recv_sems.at[a * 7 + d - 1],
                        device_id=(px, py, pc), device_id_type=MESH_ID))
        return local, sends, lands

    def start(self, ins, outs, sems):
        local, sends, _ = self._copies(ins, outs, sems)
        for cp in local + sends:
            cp.start()

    def wait(self, ins, outs, sems):
        local, _, lands = self._copies(ins, outs, sems)
        for cp in lands:
            cp.wait()
        for cp in local:
            cp.wait()


def _call(body, *, name, grid, in_specs, out_specs, out_shape, args, scratch_shapes=(), sem=None, comm=None):
    n_in, n_out, n_scr = len(in_specs), len(out_specs), len(scratch_shapes)
    if comm is None:
        return pl.pallas_call(
            body, name=name, grid=grid, out_shape=list(out_shape), in_specs=list(in_specs), out_specs=list(out_specs),
            scratch_shapes=list(scratch_shapes), compiler_params=_params(*sem))(*args)
    hbm = pl.BlockSpec(memory_space=pltpu.HBM)

    def carrier(*refs):
        bounds = [0, n_in, n_in + comm.n, n_in + comm.n + n_out, n_in + 2 * comm.n + n_out, len(refs) - 3, len(refs)]
        ins, cin, outs, cout, scr, sems = [refs[lo:hi] for lo, hi in zip(bounds[:-1], bounds[1:])]
        assert len(scr) == n_scr
        ids = [pl.program_id(ax) for ax in range(len(grid))]
        first = functools.reduce(lambda p, q: p & q, [i == 0 for i in ids])
        last = functools.reduce(lambda p, q: p & q, [i == g - 1 for i, g in zip(ids, grid)])

        @pl.when(first)
        def _():
            comm.start(cin, cout, sems)

        body(*ins, *outs, *scr)

        @pl.when(last)
        def _():
            comm.wait(cin, cout, sems)

    return pl.pallas_call(
        carrier, name=name, grid=grid, out_shape=list(out_shape) + comm.out_shape,
        in_specs=list(in_specs) + [hbm] * comm.n, out_specs=list(out_specs) + [hbm] * comm.n,
        scratch_shapes=list(scratch_shapes) + comm.scratch,
        compiler_params=_params(*["arbitrary"] * len(grid)))(*args, *comm.arrays)


def _exchange(items, *, name, vmem=False):
    comm = _Exchange(items)
    space = pl.BlockSpec(memory_space=pltpu.VMEM if vmem else pltpu.HBM)

    def body(*refs):
        ins, outs, sems = refs[:comm.n], refs[comm.n:2 * comm.n], refs[2 * comm.n:]
        comm.start(ins, outs, sems)
        comm.wait(ins, outs, sems)

    return pl.pallas_call(
        body, name=name, out_shape=comm.out_shape, in_specs=[space] * comm.n, out_specs=[space] * comm.n,
        scratch_shapes=comm.scratch, compiler_params=pltpu.CompilerParams(vmem_limit_bytes=VMEM_LIMIT))(*comm.arrays)


_DIMS = {"nn": (((1,), (0,)), ((), ())), "nt": (((1,), (1,)), ((), ())), "tn": (((0,), (0,)), ((), ()))}


def _matmul(a, b, *, dims, tm, tn, tk, name, epilogue="plain", out_dtype=F32, extras=(), comm=None,
            a_heads=False, b_heads=False):
    a_parts = a.shape[0] if a.ndim == 3 else 0
    b_parts = b.shape[0] if b.ndim == 3 else 0
    assert not (a_parts and dims != "nt" and not a_heads) and not (b_parts and dims != "tn")
    a2 = (a.shape[1], a.shape[2] * a_parts) if a_parts else a.shape
    b2 = (b.shape[1], b.shape[2] * b_parts) if b_parts else b.shape
    if dims == "tn":
        (K, M), N = a2, b2[1]
    else:
        (M, K), N = a2, (b2[1] if dims == "nn" else b2[0])
    tm, tn, tk = min(tm, M), min(tn, N), min(tk, K)
    assert M % tm == 0 and N % tn == 0 and K % tk == 0, (name, M, N, K, tm, tn, tk)
    if a_heads and dims == "tn":
        a_spec = pl.BlockSpec((tm // ATTN_HD, tk, ATTN_HD), lambda i, j, k: (i, k, 0))
    elif a_heads:
        a_spec = pl.BlockSpec((tk // ATTN_HD, tm, ATTN_HD), lambda i, j, k: (k, i, 0))
    elif a_parts:
        per = K // a_parts // tk
        a_spec = pl.BlockSpec((None, tm, tk), lambda i, j, k: (k // per, i, k % per))
    elif dims == "tn":
        a_spec = pl.BlockSpec((tk, tm), lambda i, j, k: (k, i))
    else:
        a_spec = pl.BlockSpec((tm, tk), lambda i, j, k: (i, k))
    if b_heads:
        b_spec = pl.BlockSpec((tn // ATTN_HD, tk, ATTN_HD), lambda i, j, k: (j, k, 0))
    elif b_parts:
        per_n = N // b_parts // tn
        b_spec = pl.BlockSpec((None, tk, tn), lambda i, j, k: (j // per_n, k, j % per_n))
    elif dims == "nt":
        b_spec = pl.BlockSpec((tn, tk), lambda i, j, k: (j, k))
    else:
        b_spec = pl.BlockSpec((tk, tn), lambda i, j, k: (k, j))
    side_by_side = lambda ref: jnp.concatenate([ref[g] for g in range(ref.shape[0])], axis=1)
    nk = K // tk
    tile = pl.BlockSpec((tm, tn), lambda i, j, k: (i, j))
    row = pl.BlockSpec((1, tn), lambda i, j, k: (0, j))
    if epilogue == "plain":
        extra_specs, out_shape, out_specs = [], [jax.ShapeDtypeStruct((M, N), out_dtype)], [tile]
    elif epilogue == "relu2":
        extra_specs, out_shape, out_specs = [], [jax.ShapeDtypeStruct((M, N), BF16)] * 2, [tile, tile]
    elif epilogue == "resgate":
        extra_specs, out_shape, out_specs = [tile, row], [jax.ShapeDtypeStruct((M, N), F32)] * 2, [tile, tile]
    elif epilogue == "mul2a":
        extra_specs, out_shape, out_specs = [tile], [jax.ShapeDtypeStruct((M, N), BF16)], [tile]
    elif epilogue == "heads":
        per_tile = tn // ATTN_HD
        q_tiles = ATTN_HEADS // per_tile
        assert tn == 2 * ATTN_KV * ATTN_HD and N // tn in (q_tiles, q_tiles + 1)
        extra_specs = []
        out_shape = [jax.ShapeDtypeStruct((ATTN_HEADS, M, ATTN_HD), out_dtype)]
        out_specs = [pl.BlockSpec((per_tile, tm, ATTN_HD), lambda i, j, k: (jnp.minimum(j, q_tiles - 1), i, 0))]
        if N // tn > q_tiles:
            out_shape += [jax.ShapeDtypeStruct((ATTN_KV, M, ATTN_HD), out_dtype)] * 2
            out_specs += [pl.BlockSpec((ATTN_KV, tm, ATTN_HD), lambda i, j, k: (0, i, 0))] * 2
    else:
        raise ValueError(epilogue)
    n_extra = len(extra_specs)

    def body(a_ref, b_ref, *rest):
        ex, outs, acc_ref = rest[:n_extra], rest[n_extra:-1], rest[-1]
        k = pl.program_id(2)

        @pl.when(k == 0)
        def _():
            acc_ref[...] = jnp.zeros_like(acc_ref)

        a_tile = side_by_side(a_ref) if a_heads else a_ref[...]
        b_tile = side_by_side(b_ref) if b_heads else b_ref[...]
        acc_ref[...] += lax.dot_general(a_tile, b_tile, _DIMS[dims], preferred_element_type=F32)

        if epilogue == "heads":
            j = pl.program_id(1)

            @pl.when((k == nk - 1) & (j < q_tiles))
            def _():
                for g in range(per_tile):
                    outs[0][g] = acc_ref[:, g * ATTN_HD:(g + 1) * ATTN_HD].astype(out_dtype)

            if len(outs) > 1:
                @pl.when((k == nk - 1) & (j == q_tiles))
                def _():
                    for g in range(ATTN_KV):
                        outs[1][g] = acc_ref[:, g * ATTN_HD:(g + 1) * ATTN_HD].astype(out_dtype)
                        outs[2][g] = acc_ref[:, (ATTN_KV + g) * ATTN_HD:(ATTN_KV + g + 1) * ATTN_HD].astype(out_dtype)
            return

        @pl.when(k == nk - 1)
        def _():
            acc = acc_ref[...]
            if epilogue == "plain":
                outs[0][...] = acc.astype(out_dtype)
            elif epilogue == "relu2":
                act = jnp.maximum(acc, 0.0)
                outs[0][...] = act.astype(BF16)
                outs[1][...] = (act * act).astype(BF16)
            elif epilogue == "resgate":
                outs[0][...] = acc
                outs[1][...] = ex[0][...] + ex[1][...] * acc
            else:
                outs[0][...] = (acc * (2.0 * ex[0][...].astype(F32))).astype(BF16)

    res = _call(body, name=name, grid=(M // tm, N // tn, nk), out_shape=out_shape,
                in_specs=[a_spec, b_spec] + extra_specs, out_specs=out_specs,
                scratch_shapes=[pltpu.VMEM((tm, tn), F32)],
                sem=("parallel", "arbitrary" if epilogue == "heads" else "parallel", "arbitrary"),
                args=(a, b, *extras), comm=comm)
    return res[0] if len(res) == 1 else res


def _row_tile(T):
    return min(T, 256)


def _norm_mod_fwd(x, gain, sc, sh, name, comm=None):
    T, D = x.shape
    tr = _row_tile(T)

    def body(x_ref, g_ref, sc_ref, sh_ref, h_ref):
        xv = x_ref[...]
        r = lax.rsqrt(jnp.mean(xv * xv, axis=-1, keepdims=True) + EPS)
        hn = (xv * r) * g_ref[...]
        h_ref[...] = (hn * (1.0 + sc_ref[...]) + sh_ref[...]).astype(BF16)

    vec = pl.BlockSpec((1, D), lambda i: (0, 0))
    res = _call(body, name=name, grid=(T // tr,), out_shape=[jax.ShapeDtypeStruct((T, D), BF16)],
                in_specs=[pl.BlockSpec((tr, D), lambda i: (i, 0)), vec, vec, vec],
                out_specs=[pl.BlockSpec((tr, D), lambda i: (i, 0))], sem=("parallel",), args=(x, gain, sc, sh),
                comm=comm)
    return res[0] if comm is None else res


def _norm_mod_bwd(x, dh, dres, gain, sc, name):
    T, D = x.shape
    tr = _row_tile(T)

    def body(x_ref, dh_ref, dres_ref, g_ref, sc_ref, dx_ref, st_ref):
        xv, dh_v, gain_v = x_ref[...], dh_ref[...], g_ref[...]
        r = lax.rsqrt(jnp.mean(xv * xv, axis=-1, keepdims=True) + EPS)
        xn = xv * r
        hn = xn * gain_v
        dhn = dh_v * (1.0 + sc_ref[...])
        dxn = dhn * gain_v
        dx_ref[...] = dres_ref[...] + r * (dxn - xn * jnp.mean(dxn * xn, axis=-1, keepdims=True))

        @pl.when(pl.program_id(0) == 0)
        def _():
            st_ref[...] = jnp.zeros_like(st_ref)

        st_ref[0:1, :] += jnp.sum(dh_v, axis=0, keepdims=True)
        st_ref[1:2, :] += jnp.sum(dh_v * hn, axis=0, keepdims=True)
        st_ref[2:3, :] += jnp.sum(dhn * xn, axis=0, keepdims=True)

    vec = pl.BlockSpec((1, D), lambda i: (0, 0))
    blk = pl.BlockSpec((tr, D), lambda i: (i, 0))
    return pl.pallas_call(
        body, name=name, grid=(T // tr,),
        out_shape=[jax.ShapeDtypeStruct((T, D), F32), jax.ShapeDtypeStruct((8, D), F32)],
        in_specs=[blk, blk, blk, vec, vec],
        out_specs=[blk, pl.BlockSpec((8, D), lambda i: (0, 0))],
        compiler_params=_params("arbitrary"),
    )(x, dh, dres, gain, sc)


def _gate_bwd(dx, y, gate, name):
    T, D = dx.shape
    tr = _row_tile(T)

    def body(dx_ref, y_ref, g_ref, dy_ref, st_ref):
        dxv = dx_ref[...]
        dy_ref[...] = (dxv * g_ref[...]).astype(BF16)

        @pl.when(pl.program_id(0) == 0)
        def _():
            st_ref[...] = jnp.zeros_like(st_ref)

        st_ref[0:1, :] += jnp.sum(dxv * y_ref[...], axis=0, keepdims=True)

    blk = pl.BlockSpec((tr, D), lambda i: (i, 0))
    return pl.pallas_call(
        body, name=name, grid=(T // tr,),
        out_shape=[jax.ShapeDtypeStruct((T, D), BF16), jax.ShapeDtypeStruct((8, D), F32)],
        in_specs=[blk, blk, pl.BlockSpec((1, D), lambda i: (0, 0))],
        out_specs=[blk, pl.BlockSpec((8, D), lambda i: (0, 0))],
        compiler_params=_params("arbitrary"),
    )(dx, y, gate)


def _loss_head(y, target, name):
    T, D = y.shape
    tr = _row_tile(T)

    def body(y_ref, t_ref, dy_ref, l_ref):
        err = y_ref[...] - t_ref[...]
        dy_ref[...] = err * (1.0 / D)

        @pl.when(pl.program_id(0) == 0)
        def _():
            l_ref[...] = jnp.zeros_like(l_ref)

        part = jnp.sum(jnp.mean(err * err, axis=-1, keepdims=True), axis=0, keepdims=True)
        l_ref[...] += jnp.broadcast_to(0.5 * part, l_ref.shape)

    blk = pl.BlockSpec((tr, D), lambda i: (i, 0))
    return pl.pallas_call(
        body, name=name, grid=(T // tr,),
        out_shape=[jax.ShapeDtypeStruct((T, D), F32), jax.ShapeDtypeStruct((8, 128), F32)],
        in_specs=[blk, blk], out_specs=[blk, pl.BlockSpec((8, 128), lambda i: (0, 0))],
        compiler_params=_params("arbitrary"),
    )(y, target)


def _silu(v):
    return v * jax.nn.sigmoid(v)


def _mod_fwd(c_all, mod_w, mod_b_cols, name):
    L, D, n = mod_w.shape
    tn = 512

    def body(c_ref, w_ref, b_ref, o_ref):
        cond = _silu(c_ref[...]).astype(BF16)
        o_ref[0] = jnp.dot(cond, w_ref[0].astype(BF16), preferred_element_type=F32) + b_ref[0]

    return pl.pallas_call(
        body, name=name, grid=(L, n // tn), out_shape=jax.ShapeDtypeStruct((L, N_DEV, n), F32),
        in_specs=[pl.BlockSpec((N_DEV, D), lambda l, j: (0, 0)), pl.BlockSpec((1, D, tn), lambda l, j: (l, 0, j)),
                  pl.BlockSpec((1, 1, tn), lambda l, j: (l, 0, j))],
        out_specs=pl.BlockSpec((1, N_DEV, tn), lambda l, j: (l, 0, j)),
        compiler_params=_params("parallel", "parallel"),
    )(c_all, mod_w, mod_b_cols)


def _adamw(g, w, m, v):
    m = ADAM_B1 * m + (1.0 - ADAM_B1) * g
    v = ADAM_B2 * v + (1.0 - ADAM_B2) * (g * g)
    m_hat = m / (1.0 - ADAM_B1 ** ADAM_STEP)
    v_hat = v / (1.0 - ADAM_B2 ** ADAM_STEP)
    delta = -ADAM_LR * (m_hat / (jnp.sqrt(v_hat) + ADAM_EPS) + ADAM_WD * w)
    return delta, m, v


def _mod_w_update(c_t, dmod_cols, w, m, v, name):
    L, D, n = w.shape
    tr = 256

    def body(c_ref, dm_ref, w_ref, m_ref, v_ref, g_ref, d_ref, nm_ref, nv_ref):
        cond = _silu(c_ref[...])
        dm = dm_ref[0]
        g = cond[:, 0:1] * dm[0:1, :]
        for b in range(1, N_DEV):
            g = g + cond[:, b:b + 1] * dm[b:b + 1, :]
        delta, nm, nv = _adamw(g, w_ref[0], m_ref[0], v_ref[0])
        g_ref[0], d_ref[0], nm_ref[0], nv_ref[0] = g, delta, nm, nv

    blk = pl.BlockSpec((1, tr, n), lambda l, i: (l, i, 0))
    return pl.pallas_call(
        body, name=name, grid=(L, D // tr), out_shape=[jax.ShapeDtypeStruct(w.shape, F32)] * 4,
        in_specs=[pl.BlockSpec((tr, N_DEV), lambda l, i: (i, 0)), pl.BlockSpec((1, N_DEV, n), lambda l, i: (l, 0, 0)),
                  blk, blk, blk],
        out_specs=[blk] * 4, compiler_params=_params("parallel", "parallel"),
    )(c_t, dmod_cols, w, m, v)


def _sum_adamw(parts, w, m, v, name):
    R, C = w.shape
    tr = min(R, 256 if C >= 1024 else 1024)
    assert R % tr == 0

    def body(p_ref, w_ref, m_ref, v_ref, g_ref, d_ref, nm_ref, nv_ref):
        g = p_ref[0].astype(F32)
        for s in range(1, N_DEV):
            g = g + p_ref[s].astype(F32)
        delta, nm, nv = _adamw(g, w_ref[...], m_ref[...], v_ref[...])
        g_ref[...], d_ref[...], nm_ref[...], nv_ref[...] = g, delta, nm, nv

    blk = pl.BlockSpec((tr, C), lambda i: (i, 0))
    return pl.pallas_call(
        body, name=name, grid=(R // tr,), out_shape=[jax.ShapeDtypeStruct((R, C), F32)] * 4,
        in_specs=[pl.BlockSpec((N_DEV, tr, C), lambda i: (0, i, 0)), blk, blk, blk], out_specs=[blk] * 4,
        compiler_params=_params("parallel"),
    )(parts, w, m, v)


def _lower_bound_row1(l0, l1):
    mx = lax.stop_gradient(jnp.maximum(l0, l1))
    e0, e1 = jnp.exp(l0 - mx), jnp.exp(l1 - mx)
    p0, p1 = e0 / (e0 + e1), e1 / (e0 + e1)
    return (p0 + p1) - p0


def _lb_fwd(logits, name):
    def body(l_ref, o_ref):
        o_ref[...] = _lower_bound_row1(l_ref[0:1, :], l_ref[1:2, :])

    return pl.pallas_call(body, name=name, out_shape=jax.ShapeDtypeStruct((1, logits.shape[1]), F32))(logits)


def _lb_bwd(logits, dlb, name):
    def body(l_ref, d_ref, o_ref):
        _, vjp = jax.vjp(_lower_bound_row1, l_ref[0:1, :], l_ref[1:2, :])
        d0, d1 = vjp(d_ref[...])
        o_ref[0:1, :] = d0
        o_ref[1:2, :] = d1

    return pl.pallas_call(body, name=name, out_shape=jax.ShapeDtypeStruct(logits.shape, F32))(logits, dlb)


def _bdot(a, b, dims):
    return lax.dot_general(a.astype(BF16), b.astype(BF16), _DIMS[dims], preferred_element_type=F32)


def _rms(x, gain):
    r = lax.rsqrt(jnp.mean(x * x, axis=-1, keepdims=True) + EPS)
    xhat = x * r
    return xhat, r, xhat * gain


def _attn_band(n):
    qi = lax.broadcasted_iota(jnp.int32, (ATTN_BLOCK, 2 * ATTN_BLOCK), 0)
    ki = lax.broadcasted_iota(jnp.int32, (ATTN_BLOCK, 2 * ATTN_BLOCK), 1)
    dist = qi + ATTN_BLOCK - ki
    first_key = jnp.where(n > 0, 0, ATTN_BLOCK)
    valid = (dist >= 0) & (dist < ATTN_BLOCK) & (ki >= first_key)
    return valid, jnp.abs(dist).astype(F32)


def _attn_head_probs(qn, kn_b, valid, absdist, slope, sink):
    s = lax.dot_general(qn.astype(BF16), kn_b, _DIMS["nt"], preferred_element_type=F32) * ATTN_SCALE
    s = jnp.where(valid, s - slope * absdist, NEG_BIG)
    mx = jnp.maximum(jnp.max(s, axis=-1, keepdims=True), sink)
    e = jnp.exp(s - mx)
    es = jnp.exp(sink - mx)
    inv = 1.0 / (jnp.sum(e, axis=-1, keepdims=True) + es)
    return e * inv, es * inv


def _attn_specs(T):
    nb = T // ATTN_BLOCK
    qspec = pl.BlockSpec((ATTN_GROUP, ATTN_BLOCK, ATTN_HD), lambda h, n: (h, n, 0))
    prev = pl.BlockSpec((1, ATTN_BLOCK, ATTN_HD), lambda h, n: (h, jnp.maximum(n - 1, 0), 0))
    cur = pl.BlockSpec((1, ATTN_BLOCK, ATTN_HD), lambda h, n: (h, n, 0))
    gain = pl.BlockSpec((1, ATTN_HD), lambda h, n: (0, 0))
    scalars = pl.BlockSpec(memory_space=pltpu.SMEM)
    return nb, qspec, prev, cur, gain, scalars


def _attn_fwd(qh, kh, vh, qg, kg, slopes, sinks, name, comm=None):
    T = qh.shape[1]
    nb, qspec, prev, cur, gain, scalars = _attn_specs(T)

    def body(q_ref, kp_ref, kc_ref, vp_ref, vc_ref, qg_ref, kg_ref, sl_ref, sk_ref, o_ref):
        h, n = pl.program_id(0), pl.program_id(1)
        valid, absdist = _attn_band(n)
        _, _, kn = _rms(jnp.concatenate([kp_ref[0], kc_ref[0]], axis=0), kg_ref[...])
        kn_b = kn.astype(BF16)
        v_b = jnp.concatenate([vp_ref[0], vc_ref[0]], axis=0).astype(BF16)
        for g in range(ATTN_GROUP):
            head = h * ATTN_GROUP + g
            _, _, qn = _rms(q_ref[g], qg_ref[...])
            p, _ = _attn_head_probs(qn, kn_b, valid, absdist, sl_ref[head], sk_ref[head])
            o_ref[g] = jnp.dot(p.astype(BF16), v_b, preferred_element_type=F32).astype(BF16)

    return _call(body, name=name, grid=(ATTN_KV, nb), out_shape=[jax.ShapeDtypeStruct(qh.shape, BF16)],
                 in_specs=[qspec, prev, cur, prev, cur, gain, gain, scalars, scalars], out_specs=[qspec],
                 sem=("parallel", "parallel"), args=(qh, kh, kh, vh, vh, qg, kg, slopes, sinks), comm=comm)


def _rms_bwd(dy, xhat, r, gain):
    dxh = dy * gain
    dx = r * (dxh - xhat * jnp.mean(dxh * xhat, axis=-1, keepdims=True))
    return dx, jnp.sum(dy * xhat, axis=0, keepdims=True)


def _attn_bwd(qh, kh, vh, doh, qg, kg, slope_col, sink_col, name, comm=None):
    T = qh.shape[1]
    nb = T // ATTN_BLOCK
    rows = ATTN_GROUP * ATTN_BLOCK

    def body(q_ref, kp_ref, kc_ref, vp_ref, vc_ref, do_ref, qg_ref, kg_ref, sl_ref, sk_ref,
             dq_ref, dk_ref, dv_ref, dqg_ref, dkg_ref, dsk_ref, carry_ref, sk_acc):
        h, step = pl.program_id(0), pl.program_id(1)
        n = nb - 1 - step
        qg_v, kg_v = qg_ref[...], kg_ref[...]
        qhat, rq, qn = _rms(q_ref[...].reshape(rows, ATTN_HD), qg_v)
        khat, rk, kn = _rms(jnp.concatenate([kp_ref[0], kc_ref[0]], axis=0), kg_v)
        s = _bdot(qn, kn, "nt") * ATTN_SCALE
        qi = lax.broadcasted_iota(jnp.int32, s.shape, 0) & (ATTN_BLOCK - 1)
        ki = lax.broadcasted_iota(jnp.int32, s.shape, 1)
        dist = qi + ATTN_BLOCK - ki
        first_key = jnp.where(n > 0, 0, ATTN_BLOCK)
        valid = (dist >= 0) & (dist < ATTN_BLOCK) & (ki >= first_key)
        s = jnp.where(valid, s - sl_ref[0] * jnp.abs(dist).astype(F32), NEG_BIG)
        sink = sk_ref[0]
        mx = jnp.maximum(jnp.max(s, axis=-1, keepdims=True), sink)
        e = jnp.exp(s - mx)
        es = jnp.exp(sink - mx)
        den = jnp.sum(e, axis=-1, keepdims=True) + es
        p, ps = e / den, es / den
        v = jnp.concatenate([vp_ref[0], vc_ref[0]], axis=0)
        do = do_ref[...].reshape(rows, ATTN_HD)
        dp = _bdot(do, v, "nt")
        delta = jnp.sum(p * dp, axis=-1, keepdims=True)
        ds = p * (dp - delta)
        dqn = _bdot(ds, kn, "nn") * ATTN_SCALE
        dkn = _bdot(ds, qn, "tn") * ATTN_SCALE
        dv = _bdot(p, do, "tn")
        dq, dqg = _rms_bwd(dqn, qhat, rq, qg_v)
        dk, dkg = _rms_bwd(dkn, khat, rk, kg_v)
        dq_ref[...] = dq.reshape(ATTN_GROUP, ATTN_BLOCK, ATTN_HD).astype(BF16)

        @pl.when((h == 0) & (step == 0))
        def _():
            dqg_ref[...] = jnp.zeros_like(dqg_ref)
            dkg_ref[...] = jnp.zeros_like(dkg_ref)

        dqg_ref[0:1, :] += dqg
        dkg_ref[0:1, :] += dkg

        @pl.when(step == 0)
        def _():
            carry_ref[...] = jnp.zeros_like(carry_ref)
            sk_acc[...] = jnp.zeros_like(sk_acc)

        dk_ref[0] = dk[ATTN_BLOCK:, :] + carry_ref[0]
        dv_ref[0] = dv[ATTN_BLOCK:, :] + carry_ref[1]
        carry_ref[0] = dk[:ATTN_BLOCK, :]
        carry_ref[1] = dv[:ATTN_BLOCK, :]
        sk_acc[...] += -ps * delta

        @pl.when(step == nb - 1)
        def _():
            for g in range(ATTN_GROUP):
                tot = jnp.sum(sk_acc[g * ATTN_BLOCK:(g + 1) * ATTN_BLOCK, :], axis=0, keepdims=True)
                dsk_ref[g:g + 1, :] = jnp.broadcast_to(tot, (1, 128))

    qspec = pl.BlockSpec((ATTN_GROUP, ATTN_BLOCK, ATTN_HD), lambda h, s: (h, nb - 1 - s, 0))
    prev = pl.BlockSpec((1, ATTN_BLOCK, ATTN_HD), lambda h, s: (h, jnp.maximum(nb - 2 - s, 0), 0))
    cur = pl.BlockSpec((1, ATTN_BLOCK, ATTN_HD), lambda h, s: (h, nb - 1 - s, 0))
    gain = pl.BlockSpec((1, ATTN_HD), lambda h, s: (0, 0))
    col = pl.BlockSpec((1, rows, 1), lambda h, s: (h, 0, 0))
    acc = pl.BlockSpec((8, ATTN_HD), lambda h, s: (0, 0))
    return _call(
        body, name=name, grid=(ATTN_KV, nb),
        out_shape=[jax.ShapeDtypeStruct(qh.shape, BF16), jax.ShapeDtypeStruct(kh.shape, F32),
                   jax.ShapeDtypeStruct(kh.shape, F32), jax.ShapeDtypeStruct((8, ATTN_HD), F32),
                   jax.ShapeDtypeStruct((8, ATTN_HD), F32), jax.ShapeDtypeStruct((ATTN_HEADS, 128), F32)],
        in_specs=[qspec, prev, cur, prev, cur, qspec, gain, gain, col, col],
        out_specs=[qspec, cur, cur, acc, acc, pl.BlockSpec((ATTN_GROUP, 128), lambda h, s: (h, 0))],
        scratch_shapes=[pltpu.VMEM((2, ATTN_BLOCK, ATTN_HD), F32), pltpu.VMEM((rows, 1), F32)],
        sem=("arbitrary", "arbitrary"), args=(qh, kh, kh, vh, vh, doh, qg, kg, slope_col, sink_col), comm=comm)


@functools.partial(jax.custom_vjp, nondiff_argnums=(2,))
def _mm(a, b, dims):
    return _bdot(a, b, dims)


def _mm_fwd(a, b, dims):
    return _bdot(a, b, dims), (a, b)


def _mm_bwd(dims, res, ct):
    a, b = res
    if dims == "nn":
        return _bdot(ct, b, "nt"), _bdot(a, ct, "tn")
    if dims == "nt":
        return _bdot(ct, b, "nn"), _bdot(ct, a, "tn")
    return _bdot(b, ct, "nt"), _bdot(a, ct, "nn")


_mm.defvjp(_mm_fwd, _mm_bwd)


def _same_chunk_mask(rows, upper):
    ri = lax.broadcasted_iota(jnp.int32, (rows, rows), 0)
    ci = lax.broadcasted_iota(jnp.int32, (rows, rows), 1)
    same = (ri >> HGRN_CHUNK_SHIFT) == (ci >> HGRN_CHUNK_SHIFT)
    return same & ((ri <= ci) if upper else (ri >= ci))


@functools.partial(jax.custom_vjp, nondiff_argnums=(1,))
def _chunk_cumsum(x, reverse):
    tri = _same_chunk_mask(x.shape[0], reverse).astype(BF16)
    hi = x.astype(BF16)
    rest = x - hi.astype(F32)
    mid = rest.astype(BF16)
    lo = (rest - mid.astype(F32)).astype(BF16)
    out = jnp.dot(tri, jnp.concatenate([hi, mid, lo], axis=1), preferred_element_type=F32)
    w = x.shape[1]
    return out[:, :w] + out[:, w:2 * w] + out[:, 2 * w:]


def _chunk_cumsum_fwd(x, reverse):
    return _chunk_cumsum(x, reverse), None


def _chunk_cumsum_bwd(reverse, _, ct):
    return (_chunk_cumsum(ct, not reverse),)


_chunk_cumsum.defvjp(_chunk_cumsum_fwd, _chunk_cumsum_bwd)


def _hgrn_block(st, qr, fr, v, gr, lb, og):
    rows = qr.shape[0]
    nc = rows // HGRN_CHUNK
    chunk_of_row = lax.broadcasted_iota(jnp.int32, qr.shape, 0) >> HGRN_CHUNK_SHIFT
    row_in_chunk = lax.broadcasted_iota(jnp.int32, (nc, HGRN_CHUNK, HGRN_DK), 1)
    per_chunk = lambda m: m.reshape(nc, HGRN_CHUNK, HGRN_DK)
    flat = lambda m: m.reshape(rows, HGRN_DK)
    by_chunk = lambda m: jnp.concatenate([jnp.where(chunk_of_row == c, m, 0.0) for c in range(nc)], axis=1)

    forget = lb + (1.0 - lb) * jax.nn.sigmoid(fr)
    k = 1.0 - forget
    b = _chunk_cumsum(jnp.log(forget), False)
    b3 = per_chunk(b)
    piv = jnp.sum(jnp.where(row_in_chunk == HGRN_CHUNK // 2 - 1, b3, 0.0), axis=1, keepdims=True)
    b_last = jnp.sum(jnp.where(row_in_chunk == HGRN_CHUNK - 1, b3, 0.0), axis=1, keepdims=True)
    q = _silu(qr) * HGRN_SCALE
    a = _mm(q * flat(jnp.exp(b3 - piv)), k * flat(jnp.exp(piv - b3)), "nt")
    o = _mm(jnp.where(_same_chunk_mask(rows, False), a, 0.0), v, "nn")
    updates = _mm(v, by_chunk(k * flat(jnp.exp(b_last - b3))), "tn")
    decay = jnp.exp(b_last)
    before = []
    for c in range(nc):
        before.append(st)
        st = st * decay[c] + updates[:, c * HGRN_DK:(c + 1) * HGRN_DK]
    o = o + _mm(by_chunk(q * jnp.exp(b)), jnp.concatenate(before, axis=1), "nt")
    y = (o * lax.rsqrt(jnp.mean(o * o, axis=-1, keepdims=True) + EPS)) * og * _silu(gr)
    return y, st


def _hgrn_tile(T):
    return min(T, 256)


HGRN_HEADS_PER_STEP = 4
HGRN_GROUPS = HGRN_HEADS // HGRN_HEADS_PER_STEP


def _hgrn_fwd(proj, lb, og, name, comm=None):
    T = proj.shape[0]
    tb = _hgrn_tile(T)
    hp, wide = HGRN_HEADS_PER_STEP, HGRN_HEADS_PER_STEP * HGRN_DK

    def body(q_ref, f_ref, v_ref, g_ref, lb_ref, og_ref, o_ref, s_ref, st_ref):
        @pl.when(pl.program_id(1) == 0)
        def _():
            st_ref[...] = jnp.zeros_like(st_ref)

        for j in range(hp):
            ln = slice(j * HGRN_DK, (j + 1) * HGRN_DK)
            st = st_ref[j]
            s_ref[j, 0] = st
            y, st_ref[j] = _hgrn_block(st, q_ref[:, ln], f_ref[:, ln], v_ref[:, ln], g_ref[:, ln], lb_ref[j], og_ref[j])
            o_ref[:, ln] = y.astype(BF16)

    part = lambda p: pl.BlockSpec((tb, wide), lambda h, t: (t, p * HGRN_GROUPS + h))
    vec = pl.BlockSpec((hp, 1, HGRN_DK), lambda h, t: (h, 0, 0))
    return _call(
        body, name=name, grid=(HGRN_GROUPS, T // tb),
        out_shape=[jax.ShapeDtypeStruct((T, D_MODEL), BF16),
                   jax.ShapeDtypeStruct((HGRN_HEADS, T // tb, HGRN_DK, HGRN_DK), F32)],
        in_specs=[part(0), part(1), part(2), part(3), vec, vec],
        out_specs=[pl.BlockSpec((tb, wide), lambda h, t: (t, h)),
                   pl.BlockSpec((hp, 1, HGRN_DK, HGRN_DK), lambda h, t: (h, t, 0, 0))],
        scratch_shapes=[pltpu.VMEM((hp, HGRN_DK, HGRN_DK), F32)], sem=("parallel", "arbitrary"),
        args=(proj, proj, proj, proj, lb, og), comm=comm)


def _hgrn_bwd(proj, states, do, lb, og, name, comm=None):
    T = proj.shape[0]
    tb = _hgrn_tile(T)
    nt, hp, wide = T // tb, HGRN_HEADS_PER_STEP, HGRN_HEADS_PER_STEP * HGRN_DK

    def body(q_ref, f_ref, v_ref, g_ref, s_ref, do_ref, lb_ref, og_ref, dp_ref, dlb_ref, dog_ref, dst_ref):
        @pl.when(pl.program_id(1) == 0)
        def _():
            dst_ref[...] = jnp.zeros_like(dst_ref)
            dlb_ref[...] = jnp.zeros_like(dlb_ref)
            dog_ref[...] = jnp.zeros_like(dog_ref)

        for j in range(hp):
            ln = slice(j * HGRN_DK, (j + 1) * HGRN_DK)
            _, vjp = jax.vjp(_hgrn_block, s_ref[j, 0], q_ref[:, ln], f_ref[:, ln], v_ref[:, ln], g_ref[:, ln],
                             lb_ref[j], og_ref[j])
            dst_ref[j], dq, df, dv, dg, dlb, dog = vjp((do_ref[:, ln], dst_ref[j]))
            for p, part_grad in enumerate((dq, df, dv, dg)):
                dp_ref[p, :, ln] = part_grad.astype(BF16)
            dlb_ref[j] += dlb
            dog_ref[j] += dog

    part = lambda p: pl.BlockSpec((tb, wide), lambda h, t: (nt - 1 - t, p * HGRN_GROUPS + h))
    vec = pl.BlockSpec((hp, 1, HGRN_DK), lambda h, t: (h, 0, 0))
    head = pl.BlockSpec((tb, wide), lambda h, t: (nt - 1 - t, h))
    return _call(
        body, name=name, grid=(HGRN_GROUPS, nt),
        out_shape=[jax.ShapeDtypeStruct((4, T, D_MODEL), BF16)] + [jax.ShapeDtypeStruct((HGRN_HEADS, 1, HGRN_DK), F32)] * 2,
        in_specs=[part(0), part(1), part(2), part(3),
                  pl.BlockSpec((hp, 1, HGRN_DK, HGRN_DK), lambda h, t: (h, nt - 1 - t, 0, 0)), head, vec, vec],
        out_specs=[pl.BlockSpec((4, tb, wide), lambda h, t: (0, nt - 1 - t, h)), vec, vec],
        scratch_shapes=[pltpu.VMEM((hp, HGRN_DK, HGRN_DK), F32)], sem=("parallel", "arbitrary"),
        args=(proj, proj, proj, proj, states, do, lb, og), comm=comm)


def _cols_to_blocks(g, n8):
    K = g.shape[0]
    return g.reshape(K, N_DEV, n8).transpose(1, 0, 2)


def _blocks_to_cols(wg):
    _, K, n8 = wg.shape
    return wg.transpose(1, 0, 2).reshape(K, N_DEV * n8)


def _pack(parts):
    flat = []
    for p in parts:
        v = p.reshape(-1)
        flat.append(jnp.pad(v, (0, (-v.shape[0]) % 1024)))
    return jnp.concatenate(flat).reshape(-1, 128)


def _unpack(packed, like):
    flat, out, off = packed.reshape(-1), [], 0
    for p in like:
        size = math.prod(p.shape)
        out.append(flat[off:off + size].reshape(p.shape))
        off += size + (-size) % 1024
    return out


def _heads_major(a, heads):
    T = a.shape[0]
    return a.reshape(T, heads, ATTN_HD).transpose(1, 0, 2)


def _heads_minor(a):
    heads, T, _ = a.shape
    return a.transpose(1, 0, 2).reshape(T, heads * ATTN_HD)


def kernel(x, c, mod_w, mod_b, norm_mix, norm_mlp, attn_w_in, attn_w_out, attn_q_gain, attn_k_gain, attn_sinks, hgrn_w_in, hgrn_w_out, hgrn_o_gain, hgrn_lb_logits, mlp_w1, mlp_w2, loss_target, m_mod_w, m_mod_b, m_norm_mix, m_norm_mlp, m_attn_w_in, m_attn_w_out, m_attn_q_gain, m_attn_k_gain, m_attn_sinks, m_hgrn_w_in, m_hgrn_w_out, m_hgrn_o_gain, m_hgrn_lb_logits, m_mlp_w1, m_mlp_w2, v_mod_w, v_mod_b, v_norm_mix, v_norm_mlp, v_attn_w_in, v_attn_w_out, v_attn_q_gain, v_attn_k_gain, v_attn_sinks, v_hgrn_w_in, v_hgrn_w_out, v_hgrn_o_gain, v_hgrn_lb_logits, v_mlp_w1, v_mlp_w2):
    T = x.shape[1]
    me = 4 * lax.axis_index("x") + 2 * lax.axis_index("y") + lax.axis_index("c")
    x0, target = x[0], loss_target[0]
    n_mod = mod_w.shape[2]

    shards = [attn_w_in[0], attn_w_out[0], hgrn_w_in[0], hgrn_w_out[0], mlp_w1[0], mlp_w1[1], mlp_w2[0], mlp_w2[1]]
    sb = [s.astype(BF16) for s in shards]
    gather = lambda *items: _Exchange([("gather", a, axis) for a, axis in items])

    c_all = _exchange([("gather", c.reshape(16, 128), None)], vmem=True, name="gather_c")[0].reshape(N_DEV, D_MODEL)
    mod_b_cols = lax.dynamic_slice_in_dim(mod_b, me * n_mod, n_mod, axis=1).reshape(2, 1, n_mod)
    mod_cols = _mod_fwd(c_all, mod_w, mod_b_cols, "mod_fwd")
    mod_all = _exchange([("gather", mod_cols.reshape(-1, 128), None)], vmem=True, name="gather_mod")[0]
    mod_all = mod_all.reshape(N_DEV, 2, N_DEV, n_mod)
    mod_mine = lax.dynamic_index_in_dim(mod_all, me, axis=2, keepdims=False)
    mod_mine = mod_mine.transpose(1, 0, 2).reshape(2, N_MOD, 1, D_MODEL)

    lb = _lb_fwd(hgrn_lb_logits, "lb_fwd").reshape(HGRN_HEADS, 1, HGRN_DK)
    og = hgrn_o_gain.reshape(HGRN_HEADS, 1, HGRN_DK)
    slopes = jnp.exp2(-8.0 * jnp.arange(1, ATTN_HEADS + 1, dtype=F32) / ATTN_HEADS)
    sinks = attn_sinks[0]
    per_row = lambda vals: jnp.repeat(vals.reshape(ATTN_KV, ATTN_GROUP), ATTN_BLOCK, axis=1).reshape(
        ATTN_KV, ATTN_GROUP * ATTN_BLOCK, 1)
    slope_col, sink_col = per_row(slopes), per_row(sinks)

    saved = []
    xi = x0
    w_mlp1, w_mlp2 = [None, None], [None, None]
    for i in range(2):
        sh1, sc1, g1, sh2, sc2, g2 = [mod_mine[i, j] for j in range(N_MOD)]
        if i == 0:
            h, w_attn_in = _norm_mod_fwd(xi, norm_mix[i:i + 1], sc1, sh1, "norm_mix_fwd0", comm=gather((sb[0], None)))
            w_attn_in = _blocks_to_cols(w_attn_in)
            qh, kh, vh, w_attn_out = _matmul(h, w_attn_in, dims="nn", tm=1024, tn=512, tk=2048, name="attn_in_fwd",
                                             epilogue="heads", comm=gather((sb[1], 0)))
            o, w_mlp1[0] = _attn_fwd(qh, kh, vh, attn_q_gain, attn_k_gain, slopes, sinks, "attn_fwd",
                                     comm=gather((sb[4], 1)))
            mix = (qh, kh, vh)
            w_out = w_attn_out
        else:
            h = _norm_mod_fwd(xi, norm_mix[i:i + 1], sc1, sh1, "norm_mix_fwd1")
            proj, w_hgrn_out = _matmul(h, w_hgrn_in, dims="nn", tm=1024, tn=1024, tk=2048, name="hgrn_in_fwd",
                                       comm=gather((sb[3], 0)))
            o, states, w_mlp1[1] = _hgrn_fwd(proj, lb, og, "hgrn_fwd", comm=gather((sb[5], 1)))
            mix = (proj, states)
            w_out = w_hgrn_out
        y, x1 = _matmul(o, w_out, dims="nn", tm=1024, tn=1024, tk=2048, name=f"mix_out_fwd{i}", epilogue="resgate",
                        extras=(xi, g1), a_heads=i == 0)
        h2 = _norm_mod_fwd(x1, norm_mlp[i:i + 1], sc2, sh2, f"norm_mlp_fwd{i}")
        if i == 0:
            act, act2, w_mlp2[0] = _matmul(h2, w_mlp1[0], dims="nn", tm=1024, tn=1024, tk=2048, name="mlp1_fwd0",
                                           epilogue="relu2", comm=gather((sb[6], 0)))
            z, x2, w_hgrn_in = _matmul(act2, w_mlp2[0], dims="nn", tm=1024, tn=1024, tk=2048, name="mlp2_fwd0",
                                       epilogue="resgate", extras=(x1, g2), comm=gather((sb[2], 1)))
        else:
            act, act2, w_mlp2[1] = _matmul(h2, w_mlp1[1], dims="nn", tm=1024, tn=1024, tk=2048, name="mlp1_fwd1",
                                           epilogue="relu2", comm=gather((sb[7], 0)))
            z, x2 = _matmul(act2, w_mlp2[1], dims="nn", tm=1024, tn=1024, tk=2048, name="mlp2_fwd1", epilogue="resgate",
                            extras=(x1, g2))
        saved.append((xi, h, o, y, x1, h2, act, act2, z, mix))
        xi = x2

    dx, loss_tile = _loss_head(xi, target, "loss_head")
    loss = lax.psum(loss_tile[0, 0], ("x", "y", "c"))

    scatter = lambda *items: _Exchange([("scatter", a, axis) for a, axis in items])
    shares, dmods, dnorm_mix, dnorm_mlp = {}, [None, None], [None, None], [None, None]
    wgrad = lambda a, b, name, tn=1024: _matmul(a, b, dims="tn", tm=1024, tn=tn, tk=1024, name=name, out_dtype=BF16)
    for i in (1, 0):
        sh1, sc1, g1, sh2, sc2, g2 = [mod_mine[i, j] for j in range(N_MOD)]
        xin, h, o, y, x1, h2, act, act2, z, mix = saved[i]
        dz, st_g2 = _gate_bwd(dx, z, g2, f"gate_mlp_bwd{i}")
        if i == 1:
            dpre = _matmul(dz, w_mlp2[1], dims="nt", tm=1024, tn=1024, tk=2048, name="mlp2_bwd1", epilogue="mul2a",
                           extras=(act,))
        else:
            dpre, shares["hgrn_w_in"] = _matmul(dz, w_mlp2[0], dims="nt", tm=1024, tn=1024, tk=2048, name="mlp2_bwd0",
                                                epilogue="mul2a", extras=(act,), comm=scatter((g_hgrn_in, 1)))
        g_mlp2 = wgrad(act2, dz, f"mlp2_wgrad{i}")
        dh2, shares[f"mlp_w2_{i}"] = _matmul(dpre, w_mlp1[i], dims="nt", tm=1024, tn=1024, tk=2048, name=f"mlp1_bwd{i}",
                                             comm=scatter((g_mlp2, 0)))
        g_mlp1 = wgrad(h2, dpre, f"mlp1_wgrad{i}")
        dx1, st_mlp = _norm_mod_bwd(x1, dh2, dx, norm_mlp[i:i + 1], sc2, f"norm_mlp_bwd{i}")
        dy, st_g1 = _gate_bwd(dx1, y, g1, f"gate_mix_bwd{i}")
        w_out = w_attn_out if i == 0 else w_hgrn_out
        if i == 0:
            qh, kh, vh = mix
            doh = _matmul(dy, w_out, dims="nt", tm=1024, tn=512, tk=2048, name="mix_out_bwd0", epilogue="heads")
            g_out = _matmul(o, dy, dims="tn", tm=1024, tn=1024, tk=1024, name="mix_out_wgrad0", out_dtype=BF16,
                            a_heads=True)
            dqh, dkh, dvh, dqg, dkg, dsk, shares["mlp_w1_0"], shares["attn_w_out"] = _attn_bwd(
                qh, kh, vh, doh, attn_q_gain, attn_k_gain, slope_col, sink_col, "attn_bwd",
                comm=scatter((g_mlp1, 1), (g_out, 0)))
            dproj = jnp.concatenate([dqh, dkh.astype(BF16), dvh.astype(BF16)], axis=0)
            d_q_gain, d_k_gain, d_sinks = dqg[0:1], dkg[0:1], dsk[:, 0].reshape(1, ATTN_HEADS)
            g_attn_in = _matmul(h, dproj, dims="tn", tm=1024, tn=640, tk=1024, name="mix_in_wgrad0", out_dtype=BF16,
                                b_heads=True)
            g_attn_in = _cols_to_blocks(g_attn_in, attn_w_in.shape[2])
            dh, shares["attn_w_in"] = _matmul(dproj, w_attn_in, dims="nt", tm=1024, tn=1024, tk=1280,
                                              name="mix_in_bwd0", comm=scatter((g_attn_in, None)), a_heads=True)
        else:
            do = _matmul(dy, w_out, dims="nt", tm=1024, tn=1024, tk=2048, name="mix_out_bwd1")
            g_out = wgrad(o, dy, "mix_out_wgrad1")
            proj, states = mix
            dproj, dlb, d_o_gain, shares["mlp_w1_1"], shares["hgrn_w_out"] = _hgrn_bwd(
                proj, states, do, lb, og, "hgrn_bwd", comm=scatter((g_mlp1, 1), (g_out, 0)))
            d_lb_logits = _lb_bwd(hgrn_lb_logits, dlb.reshape(1, D_MODEL), "lb_bwd")
            dh = _matmul(dproj, w_hgrn_in, dims="nt", tm=1024, tn=1024, tk=2048, name="mix_in_bwd1")
            g_hgrn_in = wgrad(h, dproj, "mix_in_wgrad1")
        dx, st_mix = _norm_mod_bwd(xin, dh, dx1, norm_mix[i:i + 1], sc1, f"norm_mix_bwd{i}")
        dmods[i] = jnp.concatenate([st_mix[0:1], st_mix[1:2], st_g1[0:1], st_mlp[0:1], st_mlp[1:2], st_g2[0:1]], axis=1)
        dnorm_mix[i], dnorm_mlp[i] = st_mix[2:3], st_mlp[2:3]

    names = ["attn_w_in", "attn_w_out", "hgrn_w_in", "hgrn_w_out", "mlp_w1_0", "mlp_w1_1", "mlp_w2_0", "mlp_w2_1"]
    shares = [shares[nm] for nm in names]
    moments = [(m_attn_w_in[0], v_attn_w_in[0]), (m_attn_w_out[0], v_attn_w_out[0]), (m_hgrn_w_in[0], v_hgrn_w_in[0]),
               (m_hgrn_w_out[0], v_hgrn_w_out[0]), (m_mlp_w1[0], v_mlp_w1[0]), (m_mlp_w1[1], v_mlp_w1[1]),
               (m_mlp_w2[0], v_mlp_w2[0]), (m_mlp_w2[1], v_mlp_w2[1])]
    big = {nm: _sum_adamw(sh, w, m, v, f"adamw_{nm}") for nm, sh, w, (m, v) in zip(names, shares, shards, moments)}

    small_w = [mod_b, norm_mix, norm_mlp, attn_q_gain, attn_k_gain, attn_sinks, hgrn_o_gain, hgrn_lb_logits]
    small_m = [m_mod_b, m_norm_mix, m_norm_mlp, m_attn_q_gain, m_attn_k_gain, m_attn_sinks, m_hgrn_o_gain, m_hgrn_lb_logits]
    small_v = [v_mod_b, v_norm_mix, v_norm_mlp, v_attn_q_gain, v_attn_k_gain, v_attn_sinks, v_hgrn_o_gain, v_hgrn_lb_logits]
    small_g = [jnp.concatenate(dmods, axis=0), jnp.concatenate(dnorm_mix, axis=0), jnp.concatenate(dnorm_mlp, axis=0),
               d_q_gain, d_k_gain, d_sinks, d_o_gain.reshape(hgrn_o_gain.shape), d_lb_logits]
    packed_g = _pack(small_g)
    pad_rows = (-packed_g.shape[0]) % 8
    pad8 = lambda a: jnp.pad(a, ((0, pad_rows), (0, 0)))
    all_small = _exchange([("gather", pad8(packed_g), None)], vmem=True, name="gather_small_grads")[0]
    sg, sd, sm, sv = _sum_adamw(all_small, pad8(_pack(small_w)), pad8(_pack(small_m)), pad8(_pack(small_v)),
                                "adamw_small")
    small = [_unpack(t, small_w) for t in (sg, sd, sm, sv)]

    n_modb = N_MOD * D_MODEL
    dmod_all = all_small[:, :2 * n_modb // 128, :].reshape(N_DEV, 2, n_modb)
    dmod_cols = lax.dynamic_slice_in_dim(dmod_all, me * n_mod, n_mod, axis=2).transpose(1, 0, 2)
    modw = _mod_w_update(c_all.T, dmod_cols, mod_w, m_mod_w, v_mod_w, "adamw_mod_w")

    def leaf(k):
        stack = lambda a, b: jnp.stack([big[a][k], big[b][k]])
        one = lambda a: big[a][k][None]
        s = small[k]
        return [modw[k], s[0], s[1], s[2], one("attn_w_in"), one("attn_w_out"), s[3], s[4], s[5], one("hgrn_w_in"),
                one("hgrn_w_out"), s[6], s[7], stack("mlp_w1_0", "mlp_w1_1"), stack("mlp_w2_0", "mlp_w2_1")]

    return (loss, dx[None], *leaf(0), *leaf(1), *leaf(2), *leaf(3))
```

```python
import functools
import math

import jax
import jax.numpy as jnp
from jax import lax
from jax.experimental import pallas as pl
from jax.experimental.pallas import tpu as pltpu

F32, BF16 = jnp.float32, jnp.bfloat16
N_DEV = 8
D_MODEL = 2048
N_MOD = 6
EPS = 1e-6
ATTN_HD, ATTN_HEADS, ATTN_KV, ATTN_GROUP, ATTN_BLOCK = 64, 32, 4, 8, 128
ATTN_SCALE = 1.0 / math.sqrt(ATTN_HD)
HGRN_HEADS, HGRN_DK, HGRN_CHUNK = 16, 128, 64
HGRN_SCALE = 1.0 / math.sqrt(HGRN_DK)
HGRN_CHUNK_SHIFT = HGRN_CHUNK.bit_length() - 1
assert 1 << HGRN_CHUNK_SHIFT == HGRN_CHUNK
D_FF = 4 * D_MODEL
ADAM_LR, ADAM_B1, ADAM_B2, ADAM_EPS, ADAM_WD, ADAM_STEP = 0.001, 0.9, 0.999, 1e-08, 0.01, 10
NEG_BIG = -1e30
VMEM_LIMIT = 56 * 1024 * 1024
MESH_ID = pl.DeviceIdType.MESH


def _params(*sem):
    return pltpu.CompilerParams(dimension_semantics=sem, vmem_limit_bytes=VMEM_LIMIT)


class _Exchange:
    def __init__(self, items):
        self.items = items
        self.n = len(items)
        self.out_shape = []
        for kind, a, axis in items:
            assert kind in ("gather", "gather_by_chip", "scatter")
            if kind != "scatter":
                shape = (N_DEV,) + a.shape if axis is None else tuple(
                    d * N_DEV if i == axis else d for i, d in enumerate(a.shape))
            else:
                shape = a.shape if axis is None else (N_DEV,) + tuple(
                    d // N_DEV if i == axis else d for i, d in enumerate(a.shape))
            self.out_shape.append(jax.ShapeDtypeStruct(shape, a.dtype))
        self.scratch = [pltpu.SemaphoreType.DMA((7 * self.n,)), pltpu.SemaphoreType.DMA((7 * self.n,)),
                        pltpu.SemaphoreType.DMA((self.n,))]
        self.arrays = [a for _, a, _ in items]

    @staticmethod
    def _block(ref, b, axis, size):
        if axis is None:
            return ref.at[b]
        sl = pl.ds(pl.multiple_of(b * size, size), size)
        return ref.at[sl, :] if axis == 0 else ref.at[:, sl]

    def _plan(self, ins, outs, sems):
        send_sems, recv_sems, loc_sems = sems
        x, y, c = lax.axis_index("x"), lax.axis_index("y"), lax.axis_index("c")
        me = 4 * x + 2 * y + c
        begin, middle, end = [], [], []

        def peer(d):
            px = 1 - x if d & 4 else x
            py = 1 - y if d & 2 else y
            pc = 1 - c if d & 1 else c
            return (px, py, pc), 4 * px + 2 * py + pc

        for a, (kind, arr, axis) in enumerate(self.items):
            gather = kind != "scatter"
            size = None if axis is None else (arr.shape[axis] if gather else arr.shape[axis] // N_DEV)

            def src(b):
                return ins[a] if gather else self._block(ins[a], b, axis, size)

            def dst(b):
                return self._block(outs[a], b, axis, size) if gather else outs[a].at[b]

            def remote(src_ref, dst_ref, slot, dev):
                return pltpu.make_async_remote_copy(
                    src_ref=src_ref, dst_ref=dst_ref, send_sem=send_sems.at[a * 7 + slot],
                    recv_sem=recv_sems.at[a * 7 + slot], device_id=dev, device_id_type=MESH_ID)

            local = pltpu.make_async_copy(src(me), dst(me), loc_sems.at[a])
            begin.append(local)
            end.append(local.wait)
            if kind == "gather_by_chip":
                sib_dev, sib_id = peer(1)
                to_sib = remote(ins[a], dst(me), 0, sib_dev)
                begin.append(to_sib)
                end += [to_sib.wait_send, remote(ins[a], dst(sib_id), 0, sib_dev).wait_recv]
                for j, d in enumerate((2, 4, 6)):
                    dev, pid = peer(d)
                    over_ici = remote(ins[a], dst(me), 1 + j, dev)
                    begin.append(over_ici)
                    passed_on = remote(dst(pid), dst(pid), 4 + j, sib_dev)
                    middle.append((remote(ins[a], dst(pid), 1 + j, dev), passed_on))
                    end += [over_ici.wait_send, passed_on.wait_send,
                            remote(ins[a], dst(peer(d ^ 1)[1]), 4 + j, sib_dev).wait_recv]
            else:
                for d in range(1, N_DEV):
                    dev, pid = peer(d)
                    begin.append(remote(src(pid), dst(me), d - 1, dev))
                    end.append(remote(src(pid), dst(pid), d - 1, dev).wait)
        return begin, middle, end

    def start(self, ins, outs, sems):
        for cp in self._plan(ins, outs, sems)[0]:
            cp.start()

    def pass_on(self, ins, outs, sems):
        for arrived, onward in self._plan(ins, outs, sems)[1]:
            arrived.wait_recv()
            onward.start()

    def wait(self, ins, outs, sems):
        for wait in self._plan(ins, outs, sems)[2]:
            wait()


def _call(body, *, name, grid, in_specs, out_specs, out_shape, args, scratch_shapes=(), sem=None, comm=None):
    n_in, n_out, n_scr = len(in_specs), len(out_specs), len(scratch_shapes)
    if comm is None:
        return pl.pallas_call(
            body, name=name, grid=grid, out_shape=list(out_shape), in_specs=list(in_specs), out_specs=list(out_specs),
            scratch_shapes=list(scratch_shapes), compiler_params=_params(*sem))(*args)
    hbm = pl.BlockSpec(memory_space=pltpu.HBM)

    def carrier(*refs):
        bounds = [0, n_in, n_in + comm.n, n_in + comm.n + n_out, n_in + 2 * comm.n + n_out, len(refs) - 3, len(refs)]
        ins, cin, outs, cout, scr, sems = [refs[lo:hi] for lo, hi in zip(bounds[:-1], bounds[1:])]
        assert len(scr) == n_scr
        step = functools.reduce(lambda lin, ax: lin * grid[ax] + pl.program_id(ax), range(len(grid)), 0)
        steps = math.prod(grid)

        @pl.when(step == 0)
        def _():
            comm.start(cin, cout, sems)

        body(*ins, *outs, *scr)

        @pl.when(step == (2 * steps) // 3)
        def _():
            comm.pass_on(cin, cout, sems)

        @pl.when(step == steps - 1)
        def _():
            comm.wait(cin, cout, sems)

    return pl.pallas_call(
        carrier, name=name, grid=grid, out_shape=list(out_shape) + comm.out_shape,
        in_specs=list(in_specs) + [hbm] * comm.n, out_specs=list(out_specs) + [hbm] * comm.n,
        scratch_shapes=list(scratch_shapes) + comm.scratch,
        compiler_params=_params(*["arbitrary"] * len(grid)))(*args, *comm.arrays)


def _exchange(items, *, name, vmem=False):
    comm = _Exchange(items)
    space = pl.BlockSpec(memory_space=pltpu.VMEM if vmem else pltpu.HBM)

    def body(*refs):
        ins, outs, sems = refs[:comm.n], refs[comm.n:2 * comm.n], refs[2 * comm.n:]
        comm.start(ins, outs, sems)
        comm.pass_on(ins, outs, sems)
        comm.wait(ins, outs, sems)

    return pl.pallas_call(
        body, name=name, out_shape=comm.out_shape, in_specs=[space] * comm.n, out_specs=[space] * comm.n,
        scratch_shapes=comm.scratch, compiler_params=pltpu.CompilerParams(vmem_limit_bytes=VMEM_LIMIT))(*comm.arrays)


_DIMS = {"nn": (((1,), (0,)), ((), ())), "nt": (((1,), (1,)), ((), ())), "tn": (((0,), (0,)), ((), ()))}


def _matmul(a, b, *, dims, tm, tn, tk, name, epilogue="plain", out_dtype=F32, extras=(), comm=None,
            a_heads=False, b_heads=False):
    a_parts = a.shape[0] if a.ndim == 3 else 0
    b_parts = b.shape[0] if b.ndim == 3 else 0
    assert not (a_parts and dims != "nt" and not a_heads) and not (b_parts and dims != "tn")
    a2 = (a.shape[1], a.shape[2] * a_parts) if a_parts else a.shape
    b2 = (b.shape[1], b.shape[2] * b_parts) if b_parts else b.shape
    if dims == "tn":
        (K, M), N = a2, b2[1]
    else:
        (M, K), N = a2, (b2[1] if dims == "nn" else b2[0])
    tm, tn, tk = min(tm, M), min(tn, N), min(tk, K)
    assert M % tm == 0 and N % tn == 0 and K % tk == 0, (name, M, N, K, tm, tn, tk)
    if a_heads and dims == "tn":
        a_spec = pl.BlockSpec((tm // ATTN_HD, tk, ATTN_HD), lambda i, j, k: (i, k, 0))
    elif a_heads:
        a_spec = pl.BlockSpec((tk // ATTN_HD, tm, ATTN_HD), lambda i, j, k: (k, i, 0))
    elif a_parts:
        per = K // a_parts // tk
        a_spec = pl.BlockSpec((None, tm, tk), lambda i, j, k: (k // per, i, k % per))
    elif dims == "tn":
        a_spec = pl.BlockSpec((tk, tm), lambda i, j, k: (k, i))
    else:
        a_spec = pl.BlockSpec((tm, tk), lambda i, j, k: (i, k))
    if b_heads:
        b_spec = pl.BlockSpec((tn // ATTN_HD, tk, ATTN_HD), lambda i, j, k: (j, k, 0))
    elif b_parts:
        per_n = N // b_parts // tn
        b_spec = pl.BlockSpec((None, tk, tn), lambda i, j, k: (j // per_n, k, j % per_n))
    elif dims == "nt":
        b_spec = pl.BlockSpec((tn, tk), lambda i, j, k: (j, k))
    else:
        b_spec = pl.BlockSpec((tk, tn), lambda i, j, k: (k, j))
    side_by_side = lambda ref: jnp.concatenate([ref[g] for g in range(ref.shape[0])], axis=1)
    nk = K // tk
    tile = pl.BlockSpec((tm, tn), lambda i, j, k: (i, j))
    row = pl.BlockSpec((1, tn), lambda i, j, k: (0, j))
    if epilogue == "plain":
        extra_specs, out_shape, out_specs = [], [jax.ShapeDtypeStruct((M, N), out_dtype)], [tile]
    elif epilogue == "relu2":
        extra_specs, out_shape, out_specs = [], [jax.ShapeDtypeStruct((M, N), BF16)] * 2, [tile, tile]
    elif epilogue == "resgate":
        extra_specs, out_specs = [tile, row], [tile, tile]
        out_shape = [jax.ShapeDtypeStruct((M, N), BF16), jax.ShapeDtypeStruct((M, N), F32)]
    elif epilogue == "mul2a":
        extra_specs, out_shape, out_specs = [tile], [jax.ShapeDtypeStruct((M, N), BF16)], [tile]
    elif epilogue == "heads":
        per_tile = tn // ATTN_HD
        q_tiles = ATTN_HEADS // per_tile
        assert tn == 2 * ATTN_KV * ATTN_HD and N // tn in (q_tiles, q_tiles + 1)
        extra_specs = []
        out_shape = [jax.ShapeDtypeStruct((ATTN_HEADS, M, ATTN_HD), out_dtype)]
        out_specs = [pl.BlockSpec((per_tile, tm, ATTN_HD), lambda i, j, k: (jnp.minimum(j, q_tiles - 1), i, 0))]
        if N // tn > q_tiles:
            out_shape += [jax.ShapeDtypeStruct((ATTN_KV, M, ATTN_HD), out_dtype)] * 2
            out_specs += [pl.BlockSpec((ATTN_KV, tm, ATTN_HD), lambda i, j, k: (0, i, 0))] * 2
    else:
        raise ValueError(epilogue)
    n_extra = len(extra_specs)

    def body(a_ref, b_ref, *rest):
        ex, outs, acc_ref = rest[:n_extra], rest[n_extra:-1], rest[-1]
        k = pl.program_id(2)

        @pl.when(k == 0)
        def _():
            acc_ref[...] = jnp.zeros_like(acc_ref)

        a_tile = side_by_side(a_ref) if a_heads else a_ref[...]
        b_tile = side_by_side(b_ref) if b_heads else b_ref[...]
        acc_ref[...] += lax.dot_general(a_tile, b_tile, _DIMS[dims], preferred_element_type=F32)

        if epilogue == "heads":
            j = pl.program_id(1)

            @pl.when((k == nk - 1) & (j < q_tiles))
            def _():
                for g in range(per_tile):
                    outs[0][g] = acc_ref[:, g * ATTN_HD:(g + 1) * ATTN_HD].astype(out_dtype)

            if len(outs) > 1:
                @pl.when((k == nk - 1) & (j == q_tiles))
                def _():
                    for g in range(ATTN_KV):
                        outs[1][g] = acc_ref[:, g * ATTN_HD:(g + 1) * ATTN_HD].astype(out_dtype)
                        outs[2][g] = acc_ref[:, (ATTN_KV + g) * ATTN_HD:(ATTN_KV + g + 1) * ATTN_HD].astype(out_dtype)
            return

        @pl.when(k == nk - 1)
        def _():
            acc = acc_ref[...]
            if epilogue == "plain":
                outs[0][...] = acc.astype(out_dtype)
            elif epilogue == "relu2":
                act = jnp.maximum(acc, 0.0)
                outs[0][...] = act.astype(BF16)
                outs[1][...] = (act * act).astype(BF16)
            elif epilogue == "resgate":
                outs[0][...] = acc.astype(BF16)
                outs[1][...] = ex[0][...] + ex[1][...] * acc
            else:
                outs[0][...] = (acc * (2.0 * ex[0][...].astype(F32))).astype(BF16)

    res = _call(body, name=name, grid=(M // tm, N // tn, nk), out_shape=out_shape,
                in_specs=[a_spec, b_spec] + extra_specs, out_specs=out_specs,
                scratch_shapes=[pltpu.VMEM((tm, tn), F32)],
                sem=("parallel", "arbitrary" if epilogue == "heads" else "parallel", "arbitrary"),
                args=(a, b, *extras), comm=comm)
    return res[0] if len(res) == 1 else res


def _row_tile(T):
    return min(T, 256)


def _norm_mod_fwd(x, gain, sc, sh, name, comm=None):
    T, D = x.shape
    tr = _row_tile(T)

    def body(x_ref, g_ref, sc_ref, sh_ref, h_ref):
        xv = x_ref[...]
        r = lax.rsqrt(jnp.mean(xv * xv, axis=-1, keepdims=True) + EPS)
        hn = (xv * r) * g_ref[...]
        h_ref[...] = (hn * (1.0 + sc_ref[...]) + sh_ref[...]).astype(BF16)

    vec = pl.BlockSpec((1, D), lambda i: (0, 0))
    res = _call(body, name=name, grid=(T // tr,), out_shape=[jax.ShapeDtypeStruct((T, D), BF16)],
                in_specs=[pl.BlockSpec((tr, D), lambda i: (i, 0)), vec, vec, vec],
                out_specs=[pl.BlockSpec((tr, D), lambda i: (i, 0))], sem=("parallel",), args=(x, gain, sc, sh),
                comm=comm)
    return res[0] if comm is None else res


def _through_gate(dx, branch_ref, gate_ref, dbranch_ref, st_ref, row):
    dbranch_ref[...] = (dx * gate_ref[...]).astype(BF16)
    st_ref[row:row + 1, :] += jnp.sum(dx * branch_ref[...].astype(F32), axis=0, keepdims=True)


def _norm_mod_bwd(x, dh, dres, gain, sc, name, below=None):
    T, D = x.shape
    tr = _row_tile(T)

    def body(x_ref, dh_ref, dres_ref, g_ref, sc_ref, *rest):
        dx_ref, st_ref = rest[-3:-1] if below else rest[-2:]
        xv, dh_v, gain_v = x_ref[...], dh_ref[...], g_ref[...]
        r = lax.rsqrt(jnp.mean(xv * xv, axis=-1, keepdims=True) + EPS)
        xn = xv * r
        hn = xn * gain_v
        dhn = dh_v * (1.0 + sc_ref[...])
        dxn = dhn * gain_v
        dx = dres_ref[...] + r * (dxn - xn * jnp.mean(dxn * xn, axis=-1, keepdims=True))
        dx_ref[...] = dx

        @pl.when(pl.program_id(0) == 0)
        def _():
            st_ref[...] = jnp.zeros_like(st_ref)

        st_ref[0:1, :] += jnp.sum(dh_v, axis=0, keepdims=True)
        st_ref[1:2, :] += jnp.sum(dh_v * hn, axis=0, keepdims=True)
        st_ref[2:3, :] += jnp.sum(dhn * xn, axis=0, keepdims=True)
        if below:
            _through_gate(dx, rest[0], rest[1], rest[-1], st_ref, 3)

    vec = pl.BlockSpec((1, D), lambda i: (0, 0))
    blk = pl.BlockSpec((tr, D), lambda i: (i, 0))
    return pl.pallas_call(
        body, name=name, grid=(T // tr,),
        out_shape=[jax.ShapeDtypeStruct((T, D), F32), jax.ShapeDtypeStruct((8, D), F32)]
        + ([jax.ShapeDtypeStruct((T, D), BF16)] if below else []),
        in_specs=[blk, blk, blk, vec, vec] + ([blk, vec] if below else []),
        out_specs=[blk, pl.BlockSpec((8, D), lambda i: (0, 0))] + ([blk] if below else []),
        compiler_params=_params("arbitrary"),
    )(x, dh, dres, gain, sc, *(below or ()))


def _loss_head(y, target, below, name):
    T, D = y.shape
    tr = _row_tile(T)

    def body(y_ref, t_ref, b_ref, g_ref, dy_ref, l_ref, st_ref, db_ref):
        err = y_ref[...] - t_ref[...]
        dy = err * (1.0 / D)
        dy_ref[...] = dy

        @pl.when(pl.program_id(0) == 0)
        def _():
            l_ref[...] = jnp.zeros_like(l_ref)
            st_ref[...] = jnp.zeros_like(st_ref)

        part = jnp.sum(jnp.mean(err * err, axis=-1, keepdims=True), axis=0, keepdims=True)
        l_ref[...] += jnp.broadcast_to(0.5 * part, l_ref.shape)
        _through_gate(dy, b_ref, g_ref, db_ref, st_ref, 0)

    blk = pl.BlockSpec((tr, D), lambda i: (i, 0))
    return pl.pallas_call(
        body, name=name, grid=(T // tr,),
        out_shape=[jax.ShapeDtypeStruct((T, D), F32), jax.ShapeDtypeStruct((8, 128), F32),
                   jax.ShapeDtypeStruct((8, D), F32), jax.ShapeDtypeStruct((T, D), BF16)],
        in_specs=[blk, blk, blk, pl.BlockSpec((1, D), lambda i: (0, 0))],
        out_specs=[blk, pl.BlockSpec((8, 128), lambda i: (0, 0)), pl.BlockSpec((8, D), lambda i: (0, 0)), blk],
        compiler_params=_params("arbitrary"),
    )(y, target, *below)


def _silu(v):
    return v * jax.nn.sigmoid(v)


def _mod_fwd(c_all, mod_w, mod_b_cols, name):
    L, D, n = mod_w.shape
    tn = 512

    def body(c_ref, w_ref, b_ref, o_ref):
        cond = _silu(c_ref[...]).astype(BF16)
        o_ref[0] = jnp.dot(cond, w_ref[0].astype(BF16), preferred_element_type=F32) + b_ref[0]

    return pl.pallas_call(
        body, name=name, grid=(L, n // tn), out_shape=jax.ShapeDtypeStruct((L, N_DEV, n), F32),
        in_specs=[pl.BlockSpec((N_DEV, D), lambda l, j: (0, 0)), pl.BlockSpec((1, D, tn), lambda l, j: (l, 0, j)),
                  pl.BlockSpec((1, 1, tn), lambda l, j: (l, 0, j))],
        out_specs=pl.BlockSpec((1, N_DEV, tn), lambda l, j: (l, 0, j)),
        compiler_params=_params("parallel", "parallel"),
    )(c_all, mod_w, mod_b_cols)


def _adamw(g, w, m, v):
    m = ADAM_B1 * m + (1.0 - ADAM_B1) * g
    v = ADAM_B2 * v + (1.0 - ADAM_B2) * (g * g)
    m_hat = m / (1.0 - ADAM_B1 ** ADAM_STEP)
    v_hat = v / (1.0 - ADAM_B2 ** ADAM_STEP)
    delta = -ADAM_LR * (m_hat / (jnp.sqrt(v_hat) + ADAM_EPS) + ADAM_WD * w)
    return delta, m, v


def _mod_w_update(c_t, dmod_cols, w, m, v, name):
    L, D, n = w.shape
    tr = 256

    def body(c_ref, dm_ref, w_ref, m_ref, v_ref, g_ref, d_ref, nm_ref, nv_ref):
        cond = _silu(c_ref[...])
        dm = dm_ref[0]
        g = cond[:, 0:1] * dm[0:1, :]
        for b in range(1, N_DEV):
            g = g + cond[:, b:b + 1] * dm[b:b + 1, :]
        delta, nm, nv = _adamw(g, w_ref[0], m_ref[0], v_ref[0])
        g_ref[0], d_ref[0], nm_ref[0], nv_ref[0] = g, delta, nm, nv

    blk = pl.BlockSpec((1, tr, n), lambda l, i: (l, i, 0))
    return pl.pallas_call(
        body, name=name, grid=(L, D // tr), out_shape=[jax.ShapeDtypeStruct(w.shape, F32)] * 4,
        in_specs=[pl.BlockSpec((tr, N_DEV), lambda l, i: (i, 0)), pl.BlockSpec((1, N_DEV, n), lambda l, i: (l, 0, 0)),
                  blk, blk, blk],
        out_specs=[blk] * 4, compiler_params=_params("parallel", "parallel"),
    )(c_t, dmod_cols, w, m, v)


def _sum_adamw(parts, w, m, v, name):
    R, C = w.shape
    tr = min(R, 256 if C >= 1024 else 1024)
    assert R % tr == 0

    def body(p_ref, w_ref, m_ref, v_ref, g_ref, d_ref, nm_ref, nv_ref):
        g = p_ref[0].astype(F32)
        for s in range(1, N_DEV):
            g = g + p_ref[s].astype(F32)
        delta, nm, nv = _adamw(g, w_ref[...], m_ref[...], v_ref[...])
        g_ref[...], d_ref[...], nm_ref[...], nv_ref[...] = g, delta, nm, nv

    blk = pl.BlockSpec((tr, C), lambda i: (i, 0))
    return pl.pallas_call(
        body, name=name, grid=(R // tr,), out_shape=[jax.ShapeDtypeStruct((R, C), F32)] * 4,
        in_specs=[pl.BlockSpec((N_DEV, tr, C), lambda i: (0, i, 0)), blk, blk, blk], out_specs=[blk] * 4,
        compiler_params=_params("parallel"),
    )(parts, w, m, v)


def _lower_bound_row1(l0, l1):
    mx = lax.stop_gradient(jnp.maximum(l0, l1))
    e0, e1 = jnp.exp(l0 - mx), jnp.exp(l1 - mx)
    p0, p1 = e0 / (e0 + e1), e1 / (e0 + e1)
    return (p0 + p1) - p0


def _lb_fwd(logits, name):
    def body(l_ref, o_ref):
        o_ref[...] = _lower_bound_row1(l_ref[0:1, :], l_ref[1:2, :])

    return pl.pallas_call(body, name=name, out_shape=jax.ShapeDtypeStruct((1, logits.shape[1]), F32))(logits)


def _lb_bwd(logits, dlb, name):
    def body(l_ref, d_ref, o_ref):
        _, vjp = jax.vjp(_lower_bound_row1, l_ref[0:1, :], l_ref[1:2, :])
        d0, d1 = vjp(d_ref[...])
        o_ref[0:1, :] = d0
        o_ref[1:2, :] = d1

    return pl.pallas_call(body, name=name, out_shape=jax.ShapeDtypeStruct(logits.shape, F32))(logits, dlb)


def _bdot(a, b, dims):
    return lax.dot_general(a.astype(BF16), b.astype(BF16), _DIMS[dims], preferred_element_type=F32)


def _rms(x, gain):
    r = lax.rsqrt(jnp.mean(x * x, axis=-1, keepdims=True) + EPS)
    xhat = x * r
    return xhat, r, xhat * gain


def _attn_band(n):
    qi = lax.broadcasted_iota(jnp.int32, (ATTN_BLOCK, 2 * ATTN_BLOCK), 0)
    ki = lax.broadcasted_iota(jnp.int32, (ATTN_BLOCK, 2 * ATTN_BLOCK), 1)
    dist = qi + ATTN_BLOCK - ki
    first_key = jnp.where(n > 0, 0, ATTN_BLOCK)
    valid = (dist >= 0) & (dist < ATTN_BLOCK) & (ki >= first_key)
    return valid, jnp.abs(dist).astype(F32)


def _attn_head_probs(qn, kn_b, valid, absdist, slope, sink):
    s = lax.dot_general(qn.astype(BF16), kn_b, _DIMS["nt"], preferred_element_type=F32) * ATTN_SCALE
    s = jnp.where(valid, s - slope * absdist, NEG_BIG)
    mx = jnp.maximum(jnp.max(s, axis=-1, keepdims=True), sink)
    e = jnp.exp(s - mx)
    es = jnp.exp(sink - mx)
    inv = 1.0 / (jnp.sum(e, axis=-1, keepdims=True) + es)
    return e * inv, es * inv


def _attn_specs(T):
    nb = T // ATTN_BLOCK
    qspec = pl.BlockSpec((ATTN_GROUP, ATTN_BLOCK, ATTN_HD), lambda h, n: (h, n, 0))
    prev = pl.BlockSpec((1, ATTN_BLOCK, ATTN_HD), lambda h, n: (h, jnp.maximum(n - 1, 0), 0))
    cur = pl.BlockSpec((1, ATTN_BLOCK, ATTN_HD), lambda h, n: (h, n, 0))
    gain = pl.BlockSpec((1, ATTN_HD), lambda h, n: (0, 0))
    scalars = pl.BlockSpec(memory_space=pltpu.SMEM)
    return nb, qspec, prev, cur, gain, scalars


def _attn_fwd(qh, kh, vh, qg, kg, slopes, sinks, name, comm=None):
    T = qh.shape[1]
    nb, qspec, prev, cur, gain, scalars = _attn_specs(T)

    def body(q_ref, kp_ref, kc_ref, vp_ref, vc_ref, qg_ref, kg_ref, sl_ref, sk_ref, o_ref):
        h, n = pl.program_id(0), pl.program_id(1)
        valid, absdist = _attn_band(n)
        _, _, kn = _rms(jnp.concatenate([kp_ref[0], kc_ref[0]], axis=0), kg_ref[...])
        kn_b = kn.astype(BF16)
        v_b = jnp.concatenate([vp_ref[0], vc_ref[0]], axis=0).astype(BF16)
        for g in range(ATTN_GROUP):
            head = h * ATTN_GROUP + g
            _, _, qn = _rms(q_ref[g], qg_ref[...])
            p, _ = _attn_head_probs(qn, kn_b, valid, absdist, sl_ref[head], sk_ref[head])
            o_ref[g] = jnp.dot(p.astype(BF16), v_b, preferred_element_type=F32).astype(BF16)

    return _call(body, name=name, grid=(ATTN_KV, nb), out_shape=[jax.ShapeDtypeStruct(qh.shape, BF16)],
                 in_specs=[qspec, prev, cur, prev, cur, gain, gain, scalars, scalars], out_specs=[qspec],
                 sem=("parallel", "parallel"), args=(qh, kh, kh, vh, vh, qg, kg, slopes, sinks), comm=comm)


def _rms_bwd(dy, xhat, r, gain):
    dxh = dy * gain
    dx = r * (dxh - xhat * jnp.mean(dxh * xhat, axis=-1, keepdims=True))
    return dx, jnp.sum(dy * xhat, axis=0, keepdims=True)


def _attn_bwd(qh, kh, vh, doh, qg, kg, slope_col, sink_col, name, comm=None):
    T = qh.shape[1]
    nb = T // ATTN_BLOCK
    rows = ATTN_GROUP * ATTN_BLOCK

    def body(q_ref, kp_ref, kc_ref, vp_ref, vc_ref, do_ref, qg_ref, kg_ref, sl_ref, sk_ref,
             dq_ref, dk_ref, dv_ref, dqg_ref, dkg_ref, dsk_ref, carry_ref, sk_acc):
        h, step = pl.program_id(0), pl.program_id(1)
        n = nb - 1 - step
        qg_v, kg_v = qg_ref[...], kg_ref[...]
        qhat, rq, qn = _rms(q_ref[...].reshape(rows, ATTN_HD), qg_v)
        khat, rk, kn = _rms(jnp.concatenate([kp_ref[0], kc_ref[0]], axis=0), kg_v)
        s = _bdot(qn, kn, "nt") * ATTN_SCALE
        qi = lax.broadcasted_iota(jnp.int32, s.shape, 0) & (ATTN_BLOCK - 1)
        ki = lax.broadcasted_iota(jnp.int32, s.shape, 1)
        dist = qi + ATTN_BLOCK - ki
        first_key = jnp.where(n > 0, 0, ATTN_BLOCK)
        valid = (dist >= 0) & (dist < ATTN_BLOCK) & (ki >= first_key)
        s = jnp.where(valid, s - sl_ref[0] * jnp.abs(dist).astype(F32), NEG_BIG)
        sink = sk_ref[0]
        mx = jnp.maximum(jnp.max(s, axis=-1, keepdims=True), sink)
        e = jnp.exp(s - mx)
        es = jnp.exp(sink - mx)
        den = jnp.sum(e, axis=-1, keepdims=True) + es
        p, ps = e / den, es / den
        v = jnp.concatenate([vp_ref[0], vc_ref[0]], axis=0)
        do = do_ref[...].reshape(rows, ATTN_HD)
        dp = _bdot(do, v, "nt")
        delta = jnp.sum(p * dp, axis=-1, keepdims=True)
        ds = p * (dp - delta)
        dqn = _bdot(ds, kn, "nn") * ATTN_SCALE
        dkn = _bdot(ds, qn, "tn") * ATTN_SCALE
        dv = _bdot(p, do, "tn")
        dq, dqg = _rms_bwd(dqn, qhat, rq, qg_v)
        dk, dkg = _rms_bwd(dkn, khat, rk, kg_v)
        dq_ref[...] = dq.reshape(ATTN_GROUP, ATTN_BLOCK, ATTN_HD).astype(BF16)

        @pl.when((h == 0) & (step == 0))
        def _():
            dqg_ref[...] = jnp.zeros_like(dqg_ref)
            dkg_ref[...] = jnp.zeros_like(dkg_ref)

        dqg_ref[0:1, :] += dqg
        dkg_ref[0:1, :] += dkg

        @pl.when(step == 0)
        def _():
            carry_ref[...] = jnp.zeros_like(carry_ref)
            sk_acc[...] = jnp.zeros_like(sk_acc)

        dk_ref[0] = dk[ATTN_BLOCK:, :] + carry_ref[0]
        dv_ref[0] = dv[ATTN_BLOCK:, :] + carry_ref[1]
        carry_ref[0] = dk[:ATTN_BLOCK, :]
        carry_ref[1] = dv[:ATTN_BLOCK, :]
        sk_acc[...] += -ps * delta

        @pl.when(step == nb - 1)
        def _():
            for g in range(ATTN_GROUP):
                tot = jnp.sum(sk_acc[g * ATTN_BLOCK:(g + 1) * ATTN_BLOCK, :], axis=0, keepdims=True)
                dsk_ref[g:g + 1, :] = jnp.broadcast_to(tot, (1, 128))

    qspec = pl.BlockSpec((ATTN_GROUP, ATTN_BLOCK, ATTN_HD), lambda h, s: (h, nb - 1 - s, 0))
    prev = pl.BlockSpec((1, ATTN_BLOCK, ATTN_HD), lambda h, s: (h, jnp.maximum(nb - 2 - s, 0), 0))
    cur = pl.BlockSpec((1, ATTN_BLOCK, ATTN_HD), lambda h, s: (h, nb - 1 - s, 0))
    gain = pl.BlockSpec((1, ATTN_HD), lambda h, s: (0, 0))
    col = pl.BlockSpec((1, rows, 1), lambda h, s: (h, 0, 0))
    acc = pl.BlockSpec((8, ATTN_HD), lambda h, s: (0, 0))
    return _call(
        body, name=name, grid=(ATTN_KV, nb),
        out_shape=[jax.ShapeDtypeStruct(qh.shape, BF16), jax.ShapeDtypeStruct(kh.shape, F32),
                   jax.ShapeDtypeStruct(kh.shape, F32), jax.ShapeDtypeStruct((8, ATTN_HD), F32),
                   jax.ShapeDtypeStruct((8, ATTN_HD), F32), jax.ShapeDtypeStruct((ATTN_HEADS, 128), F32)],
        in_specs=[qspec, prev, cur, prev, cur, qspec, gain, gain, col, col],
        out_specs=[qspec, cur, cur, acc, acc, pl.BlockSpec((ATTN_GROUP, 128), lambda h, s: (h, 0))],
        scratch_shapes=[pltpu.VMEM((2, ATTN_BLOCK, ATTN_HD), F32), pltpu.VMEM((rows, 1), F32)],
        sem=("arbitrary", "arbitrary"), args=(qh, kh, kh, vh, vh, doh, qg, kg, slope_col, sink_col), comm=comm)


@functools.partial(jax.custom_vjp, nondiff_argnums=(2,))
def _mm(a, b, dims):
    return _bdot(a, b, dims)


def _mm_fwd(a, b, dims):
    return _bdot(a, b, dims), (a, b)


def _mm_bwd(dims, res, ct):
    a, b = res
    if dims == "nn":
        return _bdot(ct, b, "nt"), _bdot(a, ct, "tn")
    if dims == "nt":
        return _bdot(ct, b, "nn"), _bdot(ct, a, "tn")
    return _bdot(b, ct, "nt"), _bdot(a, ct, "nn")


_mm.defvjp(_mm_fwd, _mm_bwd)


def _same_chunk_mask(rows, upper):
    ri = lax.broadcasted_iota(jnp.int32, (rows, rows), 0)
    ci = lax.broadcasted_iota(jnp.int32, (rows, rows), 1)
    same = (ri >> HGRN_CHUNK_SHIFT) == (ci >> HGRN_CHUNK_SHIFT)
    return same & ((ri <= ci) if upper else (ri >= ci))


@functools.partial(jax.custom_vjp, nondiff_argnums=(1,))
def _chunk_cumsum(x, reverse):
    tri = _same_chunk_mask(x.shape[0], reverse).astype(BF16)
    hi = x.astype(BF16)
    rest = x - hi.astype(F32)
    mid = rest.astype(BF16)
    lo = (rest - mid.astype(F32)).astype(BF16)
    out = jnp.dot(tri, jnp.concatenate([hi, mid, lo], axis=1), preferred_element_type=F32)
    w = x.shape[1]
    return out[:, :w] + out[:, w:2 * w] + out[:, 2 * w:]


def _chunk_cumsum_fwd(x, reverse):
    return _chunk_cumsum(x, reverse), None


def _chunk_cumsum_bwd(reverse, _, ct):
    return (_chunk_cumsum(ct, not reverse),)


_chunk_cumsum.defvjp(_chunk_cumsum_fwd, _chunk_cumsum_bwd)


def _hgrn_block(st, qr, fr, v, gr, lb, og):
    rows = qr.shape[0]
    nc = rows // HGRN_CHUNK
    chunk_of_row = lax.broadcasted_iota(jnp.int32, qr.shape, 0) >> HGRN_CHUNK_SHIFT
    row_in_chunk = lax.broadcasted_iota(jnp.int32, (nc, HGRN_CHUNK, HGRN_DK), 1)
    per_chunk = lambda m: m.reshape(nc, HGRN_CHUNK, HGRN_DK)
    flat = lambda m: m.reshape(rows, HGRN_DK)
    by_chunk = lambda m: jnp.concatenate([jnp.where(chunk_of_row == c, m, 0.0) for c in range(nc)], axis=1)

    forget = lb + (1.0 - lb) * jax.nn.sigmoid(fr)
    k = 1.0 - forget
    b = _chunk_cumsum(jnp.log(forget), False)
    b3 = per_chunk(b)
    piv = jnp.sum(jnp.where(row_in_chunk == HGRN_CHUNK // 2 - 1, b3, 0.0), axis=1, keepdims=True)
    b_last = jnp.sum(jnp.where(row_in_chunk == HGRN_CHUNK - 1, b3, 0.0), axis=1, keepdims=True)
    q = _silu(qr) * HGRN_SCALE
    a = _mm(q * flat(jnp.exp(b3 - piv)), k * flat(jnp.exp(piv - b3)), "nt")
    o = _mm(jnp.where(_same_chunk_mask(rows, False), a, 0.0), v, "nn")
    updates = _mm(v, by_chunk(k * flat(jnp.exp(b_last - b3))), "tn")
    decay = jnp.exp(b_last)
    before = []
    for c in range(nc):
        before.append(st)
        st = st * decay[c] + updates[:, c * HGRN_DK:(c + 1) * HGRN_DK]
    o = o + _mm(by_chunk(q * jnp.exp(b)), jnp.concatenate(before, axis=1), "nt")
    y = (o * lax.rsqrt(jnp.mean(o * o, axis=-1, keepdims=True) + EPS)) * og * _silu(gr)
    return y, st


def _hgrn_tile(T):
    return min(T, 256)


HGRN_HEADS_PER_STEP = 4
HGRN_GROUPS = HGRN_HEADS // HGRN_HEADS_PER_STEP


def _hgrn_fwd(proj, lb, og, name, comm=None):
    T = proj.shape[0]
    tb = _hgrn_tile(T)
    hp, wide = HGRN_HEADS_PER_STEP, HGRN_HEADS_PER_STEP * HGRN_DK

    def body(q_ref, f_ref, v_ref, g_ref, lb_ref, og_ref, o_ref, s_ref, st_ref):
        @pl.when(pl.program_id(1) == 0)
        def _():
            st_ref[...] = jnp.zeros_like(st_ref)

        for j in range(hp):
            ln = slice(j * HGRN_DK, (j + 1) * HGRN_DK)
            st = st_ref[j]
            s_ref[j, 0] = st
            y, st_ref[j] = _hgrn_block(st, q_ref[:, ln], f_ref[:, ln], v_ref[:, ln], g_ref[:, ln], lb_ref[j], og_ref[j])
            o_ref[:, ln] = y.astype(BF16)

    part = lambda p: pl.BlockSpec((tb, wide), lambda h, t: (t, p * HGRN_GROUPS + h))
    vec = pl.BlockSpec((hp, 1, HGRN_DK), lambda h, t: (h, 0, 0))
    return _call(
        body, name=name, grid=(HGRN_GROUPS, T // tb),
        out_shape=[jax.ShapeDtypeStruct((T, D_MODEL), BF16),
                   jax.ShapeDtypeStruct((HGRN_HEADS, T // tb, HGRN_DK, HGRN_DK), F32)],
        in_specs=[part(0), part(1), part(2), part(3), vec, vec],
        out_specs=[pl.BlockSpec((tb, wide), lambda h, t: (t, h)),
                   pl.BlockSpec((hp, 1, HGRN_DK, HGRN_DK), lambda h, t: (h, t, 0, 0))],
        scratch_shapes=[pltpu.VMEM((hp, HGRN_DK, HGRN_DK), F32)], sem=("parallel", "arbitrary"),
        args=(proj, proj, proj, proj, lb, og), comm=comm)


def _hgrn_bwd(proj, states, do, lb, og, name, comm=None):
    T = proj.shape[0]
    tb = _hgrn_tile(T)
    nt, hp, wide = T // tb, HGRN_HEADS_PER_STEP, HGRN_HEADS_PER_STEP * HGRN_DK

    def body(q_ref, f_ref, v_ref, g_ref, s_ref, do_ref, lb_ref, og_ref, dp_ref, dlb_ref, dog_ref, dst_ref):
        @pl.when(pl.program_id(1) == 0)
        def _():
            dst_ref[...] = jnp.zeros_like(dst_ref)
            dlb_ref[...] = jnp.zeros_like(dlb_ref)
            dog_ref[...] = jnp.zeros_like(dog_ref)

        for j in range(hp):
            ln = slice(j * HGRN_DK, (j + 1) * HGRN_DK)
            _, vjp = jax.vjp(_hgrn_block, s_ref[j, 0], q_ref[:, ln], f_ref[:, ln], v_ref[:, ln], g_ref[:, ln],
                             lb_ref[j], og_ref[j])
            dst_ref[j], dq, df, dv, dg, dlb, dog = vjp((do_ref[:, ln], dst_ref[j]))
            for p, part_grad in enumerate((dq, df, dv, dg)):
                dp_ref[p, :, ln] = part_grad.astype(BF16)
            dlb_ref[j] += dlb
            dog_ref[j] += dog

    part = lambda p: pl.BlockSpec((tb, wide), lambda h, t: (nt - 1 - t, p * HGRN_GROUPS + h))
    vec = pl.BlockSpec((hp, 1, HGRN_DK), lambda h, t: (h, 0, 0))
    head = pl.BlockSpec((tb, wide), lambda h, t: (nt - 1 - t, h))
    return _call(
        body, name=name, grid=(HGRN_GROUPS, nt),
        out_shape=[jax.ShapeDtypeStruct((4, T, D_MODEL), BF16)] + [jax.ShapeDtypeStruct((HGRN_HEADS, 1, HGRN_DK), F32)] * 2,
        in_specs=[part(0), part(1), part(2), part(3),
                  pl.BlockSpec((hp, 1, HGRN_DK, HGRN_DK), lambda h, t: (h, nt - 1 - t, 0, 0)), head, vec, vec],
        out_specs=[pl.BlockSpec((4, tb, wide), lambda h, t: (0, nt - 1 - t, h)), vec, vec],
        scratch_shapes=[pltpu.VMEM((hp, HGRN_DK, HGRN_DK), F32)], sem=("parallel", "arbitrary"),
        args=(proj, proj, proj, proj, states, do, lb, og), comm=comm)


def _cols_to_blocks(g, n8):
    K = g.shape[0]
    return g.reshape(K, N_DEV, n8).transpose(1, 0, 2)


def _blocks_to_cols(wg):
    _, K, n8 = wg.shape
    return wg.transpose(1, 0, 2).reshape(K, N_DEV * n8)


def _pack(parts):
    flat = []
    for p in parts:
        v = p.reshape(-1)
        flat.append(jnp.pad(v, (0, (-v.shape[0]) % 1024)))
    return jnp.concatenate(flat).reshape(-1, 128)


def _unpack(packed, like):
    flat, out, off = packed.reshape(-1), [], 0
    for p in like:
        size = math.prod(p.shape)
        out.append(flat[off:off + size].reshape(p.shape))
        off += size + (-size) % 1024
    return out


def _heads_major(a, heads):
    T = a.shape[0]
    return a.reshape(T, heads, ATTN_HD).transpose(1, 0, 2)


def _heads_minor(a):
    heads, T, _ = a.shape
    return a.transpose(1, 0, 2).reshape(T, heads * ATTN_HD)


def kernel(x, c, mod_w, mod_b, norm_mix, norm_mlp, attn_w_in, attn_w_out, attn_q_gain, attn_k_gain, attn_sinks, hgrn_w_in, hgrn_w_out, hgrn_o_gain, hgrn_lb_logits, mlp_w1, mlp_w2, loss_target, m_mod_w, m_mod_b, m_norm_mix, m_norm_mlp, m_attn_w_in, m_attn_w_out, m_attn_q_gain, m_attn_k_gain, m_attn_sinks, m_hgrn_w_in, m_hgrn_w_out, m_hgrn_o_gain, m_hgrn_lb_logits, m_mlp_w1, m_mlp_w2, v_mod_w, v_mod_b, v_norm_mix, v_norm_mlp, v_attn_w_in, v_attn_w_out, v_attn_q_gain, v_attn_k_gain, v_attn_sinks, v_hgrn_w_in, v_hgrn_w_out, v_hgrn_o_gain, v_hgrn_lb_logits, v_mlp_w1, v_mlp_w2):
    T = x.shape[1]
    me = 4 * lax.axis_index("x") + 2 * lax.axis_index("y") + lax.axis_index("c")
    x0, target = x[0], loss_target[0]
    n_mod = mod_w.shape[2]

    shards = [attn_w_in[0], attn_w_out[0], hgrn_w_in[0], hgrn_w_out[0], mlp_w1[0], mlp_w1[1], mlp_w2[0], mlp_w2[1]]
    sb = [s.astype(BF16) for s in shards]
    gather = lambda *items: _Exchange([("gather_by_chip", a, axis) for a, axis in items])

    c_all = _exchange([("gather", c.reshape(16, 128), None)], vmem=True, name="gather_c")[0].reshape(N_DEV, D_MODEL)
    mod_b_cols = lax.dynamic_slice_in_dim(mod_b, me * n_mod, n_mod, axis=1).reshape(2, 1, n_mod)
    mod_cols = _mod_fwd(c_all, mod_w, mod_b_cols, "mod_fwd")
    mod_all = _exchange([("gather", mod_cols.reshape(-1, 128), None)], vmem=True, name="gather_mod")[0]
    mod_all = mod_all.reshape(N_DEV, 2, N_DEV, n_mod)
    mod_mine = lax.dynamic_index_in_dim(mod_all, me, axis=2, keepdims=False)
    mod_mine = mod_mine.transpose(1, 0, 2).reshape(2, N_MOD, 1, D_MODEL)

    lb = _lb_fwd(hgrn_lb_logits, "lb_fwd").reshape(HGRN_HEADS, 1, HGRN_DK)
    og = hgrn_o_gain.reshape(HGRN_HEADS, 1, HGRN_DK)
    slopes = jnp.exp2(-8.0 * jnp.arange(1, ATTN_HEADS + 1, dtype=F32) / ATTN_HEADS)
    sinks = attn_sinks[0]
    per_row = lambda vals: jnp.repeat(vals.reshape(ATTN_KV, ATTN_GROUP), ATTN_BLOCK, axis=1).reshape(
        ATTN_KV, ATTN_GROUP * ATTN_BLOCK, 1)
    slope_col, sink_col = per_row(slopes), per_row(sinks)

    saved = []
    xi = x0
    w_mlp1, w_mlp2 = [None, None], [None, None]
    for i in range(2):
        sh1, sc1, g1, sh2, sc2, g2 = [mod_mine[i, j] for j in range(N_MOD)]
        if i == 0:
            h, w_attn_in = _norm_mod_fwd(xi, norm_mix[i:i + 1], sc1, sh1, "norm_mix_fwd0", comm=gather((sb[0], None)))
            w_attn_in = _blocks_to_cols(w_attn_in)
            qh, kh, vh, w_attn_out = _matmul(h, w_attn_in, dims="nn", tm=1024, tn=512, tk=2048, name="attn_in_fwd",
                                             epilogue="heads", comm=gather((sb[1], 0)))
            o, w_mlp1[0] = _attn_fwd(qh, kh, vh, attn_q_gain, attn_k_gain, slopes, sinks, "attn_fwd",
                                     comm=gather((sb[4], 1)))
            mix = (qh, kh, vh)
            w_out = w_attn_out
        else:
            h = _norm_mod_fwd(xi, norm_mix[i:i + 1], sc1, sh1, "norm_mix_fwd1")
            proj, w_hgrn_out = _matmul(h, w_hgrn_in, dims="nn", tm=1024, tn=1024, tk=2048, name="hgrn_in_fwd",
                                       comm=gather((sb[3], 0)))
            o, states, w_mlp1[1] = _hgrn_fwd(proj, lb, og, "hgrn_fwd", comm=gather((sb[5], 1)))
            mix = (proj, states)
            w_out = w_hgrn_out
        y, x1 = _matmul(o, w_out, dims="nn", tm=1024, tn=1024, tk=2048, name=f"mix_out_fwd{i}", epilogue="resgate",
                        extras=(xi, g1), a_heads=i == 0)
        h2 = _norm_mod_fwd(x1, norm_mlp[i:i + 1], sc2, sh2, f"norm_mlp_fwd{i}")
        if i == 0:
            act, act2, w_mlp2[0] = _matmul(h2, w_mlp1[0], dims="nn", tm=1024, tn=1024, tk=2048, name="mlp1_fwd0",
                                           epilogue="relu2", comm=gather((sb[6], 0)))
            z, x2, w_hgrn_in = _matmul(act2, w_mlp2[0], dims="nn", tm=1024, tn=1024, tk=2048, name="mlp2_fwd0",
                                       epilogue="resgate", extras=(x1, g2), comm=gather((sb[2], 1)))
        else:
            act, act2, w_mlp2[1] = _matmul(h2, w_mlp1[1], dims="nn", tm=1024, tn=1024, tk=2048, name="mlp1_fwd1",
                                           epilogue="relu2", comm=gather((sb[7], 0)))
            z, x2 = _matmul(act2, w_mlp2[1], dims="nn", tm=1024, tn=1024, tk=2048, name="mlp2_fwd1", epilogue="resgate",
                            extras=(x1, g2))
        saved.append((xi, h, o, y, x1, h2, act, act2, z, mix))
        xi = x2

    dx, loss_tile, st_g2, dz = _loss_head(xi, target, (saved[1][8], mod_mine[1, 5]), "loss_head")
    loss = lax.psum(loss_tile[0, 0], ("x", "y", "c"))

    scatter = lambda *items: _Exchange([("scatter", a, axis) for a, axis in items])
    shares, dmods, dnorm_mix, dnorm_mlp = {}, [None, None], [None, None], [None, None]
    wgrad = lambda a, b, name, tn=1024: _matmul(a, b, dims="tn", tm=2048, tn=tn, tk=1024, name=name, out_dtype=BF16)
    for i in (1, 0):
        sh1, sc1, g1, sh2, sc2, g2 = [mod_mine[i, j] for j in range(N_MOD)]
        xin, h, o, y, x1, h2, act, act2, z, mix = saved[i]
        if i == 1:
            dpre = _matmul(dz, w_mlp2[1], dims="nt", tm=1024, tn=1024, tk=2048, name="mlp2_bwd1", epilogue="mul2a",
                           extras=(act,))
        else:
            dpre, shares["hgrn_w_in"] = _matmul(dz, w_mlp2[0], dims="nt", tm=1024, tn=1024, tk=2048, name="mlp2_bwd0",
                                                epilogue="mul2a", extras=(act,), comm=scatter((g_hgrn_in, 1)))
        g_mlp2 = wgrad(act2, dz, f"mlp2_wgrad{i}")
        dh2, shares[f"mlp_w2_{i}"] = _matmul(dpre, w_mlp1[i], dims="nt", tm=1024, tn=1024, tk=2048, name=f"mlp1_bwd{i}",
                                             comm=scatter((g_mlp2, 0)))
        g_mlp1 = wgrad(h2, dpre, f"mlp1_wgrad{i}")
        dx1, st_mlp, dy = _norm_mod_bwd(x1, dh2, dx, norm_mlp[i:i + 1], sc2, f"norm_mlp_bwd{i}", below=(y, g1))
        w_out = w_attn_out if i == 0 else w_hgrn_out
        if i == 0:
            qh, kh, vh = mix
            doh = _matmul(dy, w_out, dims="nt", tm=1024, tn=512, tk=2048, name="mix_out_bwd0", epilogue="heads")
            g_out = _matmul(o, dy, dims="tn", tm=1024, tn=1024, tk=1024, name="mix_out_wgrad0", out_dtype=BF16,
                            a_heads=True)
            dqh, dkh, dvh, dqg, dkg, dsk, shares["mlp_w1_0"], shares["attn_w_out"] = _attn_bwd(
                qh, kh, vh, doh, attn_q_gain, attn_k_gain, slope_col, sink_col, "attn_bwd",
                comm=scatter((g_mlp1, 1), (g_out, 0)))
            dproj = jnp.concatenate([dqh, dkh.astype(BF16), dvh.astype(BF16)], axis=0)
            d_q_gain, d_k_gain, d_sinks = dqg[0:1], dkg[0:1], dsk[:, 0].reshape(1, ATTN_HEADS)
            g_attn_in = _matmul(h, dproj, dims="tn", tm=1024, tn=640, tk=1024, name="mix_in_wgrad0", out_dtype=BF16,
                                b_heads=True)
            g_attn_in = _cols_to_blocks(g_attn_in, attn_w_in.shape[2])
            dh, shares["attn_w_in"] = _matmul(dproj, w_attn_in, dims="nt", tm=1024, tn=1024, tk=1280,
                                              name="mix_in_bwd0", comm=scatter((g_attn_in, None)), a_heads=True)
        else:
            do = _matmul(dy, w_out, dims="nt", tm=1024, tn=1024, tk=2048, name="mix_out_bwd1")
            g_out = wgrad(o, dy, "mix_out_wgrad1")
            proj, states = mix
            dproj, dlb, d_o_gain, shares["mlp_w1_1"], shares["hgrn_w_out"] = _hgrn_bwd(
                proj, states, do, lb, og, "hgrn_bwd", comm=scatter((g_mlp1, 1), (g_out, 0)))
            d_lb_logits = _lb_bwd(hgrn_lb_logits, dlb.reshape(1, D_MODEL), "lb_bwd")
            dh = _matmul(dproj, w_hgrn_in, dims="nt", tm=1024, tn=1024, tk=2048, name="mix_in_bwd1")
            g_hgrn_in = wgrad(h, dproj, "mix_in_wgrad1")
        d_gate2 = st_g2[0:1] if i == 1 else st_mix_above[3:4]
        if i == 1:
            dx, st_mix, dz = _norm_mod_bwd(xin, dh, dx1, norm_mix[i:i + 1], sc1, "norm_mix_bwd1",
                                           below=(saved[0][8], mod_mine[0, 5]))
            st_mix_above = st_mix
        else:
            dx, st_mix = _norm_mod_bwd(xin, dh, dx1, norm_mix[i:i + 1], sc1, "norm_mix_bwd0")
        dmods[i] = jnp.concatenate([st_mix[0:1], st_mix[1:2], st_mlp[3:4], st_mlp[0:1], st_mlp[1:2], d_gate2], axis=1)
        dnorm_mix[i], dnorm_mlp[i] = st_mix[2:3], st_mlp[2:3]

    names = ["attn_w_in", "attn_w_out", "hgrn_w_in", "hgrn_w_out", "mlp_w1_0", "mlp_w1_1", "mlp_w2_0", "mlp_w2_1"]
    shares = [shares[nm] for nm in names]
    moments = [(m_attn_w_in[0], v_attn_w_in[0]), (m_attn_w_out[0], v_attn_w_out[0]), (m_hgrn_w_in[0], v_hgrn_w_in[0]),
               (m_hgrn_w_out[0], v_hgrn_w_out[0]), (m_mlp_w1[0], v_mlp_w1[0]), (m_mlp_w1[1], v_mlp_w1[1]),
               (m_mlp_w2[0], v_mlp_w2[0]), (m_mlp_w2[1], v_mlp_w2[1])]
    big = {nm: _sum_adamw(sh, w, m, v, f"adamw_{nm}") for nm, sh, w, (m, v) in zip(names, shares, shards, moments)}

    small_w = [mod_b, norm_mix, norm_mlp, attn_q_gain, attn_k_gain, attn_sinks, hgrn_o_gain, hgrn_lb_logits]
    small_m = [m_mod_b, m_norm_mix, m_norm_mlp, m_attn_q_gain, m_attn_k_gain, m_attn_sinks, m_hgrn_o_gain, m_hgrn_lb_logits]
    small_v = [v_mod_b, v_norm_mix, v_norm_mlp, v_attn_q_gain, v_attn_k_gain, v_attn_sinks, v_hgrn_o_gain, v_hgrn_lb_logits]
    small_g = [jnp.concatenate(dmods, axis=0), jnp.concatenate(dnorm_mix, axis=0), jnp.concatenate(dnorm_mlp, axis=0),
               d_q_gain, d_k_gain, d_sinks, d_o_gain.reshape(hgrn_o_gain.shape), d_lb_logits]
    packed_g = _pack(small_g)
    pad_rows = (-packed_g.shape[0]) % 8
    pad8 = lambda a: jnp.pad(a, ((0, pad_rows), (0, 0)))
    all_small = _exchange([("gather", pad8(packed_g), None)], vmem=True, name="gather_small_grads")[0]
    sg, sd, sm, sv = _sum_adamw(all_small, pad8(_pack(small_w)), pad8(_pack(small_m)), pad8(_pack(small_v)),
                                "adamw_small")
    small = [_unpack(t, small_w) for t in (sg, sd, sm, sv)]

    n_modb = N_MOD * D_MODEL
    dmod_all = all_small[:, :2 * n_modb // 128, :].reshape(N_DEV, 2, n_modb)
    dmod_cols = lax.dynamic_slice_in_dim(dmod_all, me * n_mod, n_mod, axis=2).transpose(1, 0, 2)
    modw = _mod_w_update(c_all.T, dmod_cols, mod_w, m_mod_w, v_mod_w, "adamw_mod_w")

    def leaf(k):
        stack = lambda a, b: jnp.stack([big[a][k], big[b][k]])
        one = lambda a: big[a][k][None]
        s = small[k]
        return [modw[k], s[0], s[1], s[2], one("attn_w_in"), one("attn_w_out"), s[3], s[4], s[5], one("hgrn_w_in"),
                one("hgrn_w_out"), s[6], s[7], stack("mlp_w1_0", "mlp_w1_1"), stack("mlp_w2_0", "mlp_w2_1")]

    return (loss, dx[None], *leaf(0), *leaf(1), *leaf(2), *leaf(3))
```

```python
import functools
import math

import jax
import jax.numpy as jnp
from jax import lax
from jax.experimental import pallas as pl
from jax.experimental.pallas import tpu as pltpu

F32, BF16 = jnp.float32, jnp.bfloat16
N_DEV = 8
D_MODEL = 2048
N_MOD = 6
EPS = 1e-6
ATTN_HD, ATTN_HEADS, ATTN_KV, ATTN_GROUP, ATTN_BLOCK = 64, 32, 4, 8, 128
ATTN_SCALE = 1.0 / math.sqrt(ATTN_HD)
HGRN_HEADS, HGRN_DK, HGRN_CHUNK = 16, 128, 64
HGRN_SCALE = 1.0 / math.sqrt(HGRN_DK)
HGRN_CHUNK_SHIFT = HGRN_CHUNK.bit_length() - 1
assert 1 << HGRN_CHUNK_SHIFT == HGRN_CHUNK
D_FF = 4 * D_MODEL
ADAM_LR, ADAM_B1, ADAM_B2, ADAM_EPS, ADAM_WD, ADAM_STEP = 0.001, 0.9, 0.999, 1e-08, 0.01, 10
NEG_BIG = -1e30
VMEM_LIMIT = 56 * 1024 * 1024
MESH_ID = pl.DeviceIdType.MESH


def _params(*sem):
    return pltpu.CompilerParams(dimension_semantics=sem, vmem_limit_bytes=VMEM_LIMIT)


class _Exchange:
    def __init__(self, items):
        self.items = items
        self.n = len(items)
        self.out_shape = []
        for kind, a, axis in items:
            assert kind in ("gather", "gather_by_chip", "scatter")
            if kind != "scatter":
                shape = (N_DEV,) + a.shape if axis is None else tuple(
                    d * N_DEV if i == axis else d for i, d in enumerate(a.shape))
            else:
                shape = a.shape if axis is None else (N_DEV,) + tuple(
                    d // N_DEV if i == axis else d for i, d in enumerate(a.shape))
            self.out_shape.append(jax.ShapeDtypeStruct(shape, a.dtype))
        self.scratch = [pltpu.SemaphoreType.DMA((7 * self.n,)), pltpu.SemaphoreType.DMA((7 * self.n,)),
                        pltpu.SemaphoreType.DMA((self.n,))]
        self.arrays = [a for _, a, _ in items]

    @staticmethod
    def _block(ref, b, axis, size):
        if axis is None:
            return ref.at[b]
        sl = pl.ds(pl.multiple_of(b * size, size), size)
        return ref.at[sl, :] if axis == 0 else ref.at[:, sl]

    def _plan(self, ins, outs, sems):
        send_sems, recv_sems, loc_sems = sems
        x, y, c = lax.axis_index("x"), lax.axis_index("y"), lax.axis_index("c")
        me = 4 * x + 2 * y + c
        begin, middle, end = [], [], []

        def peer(d):
            px = 1 - x if d & 4 else x
            py = 1 - y if d & 2 else y
            pc = 1 - c if d & 1 else c
            return (px, py, pc), 4 * px + 2 * py + pc

        for a, (kind, arr, axis) in enumerate(self.items):
            gather = kind != "scatter"
            size = None if axis is None else (arr.shape[axis] if gather else arr.shape[axis] // N_DEV)

            def src(b):
                return ins[a] if gather else self._block(ins[a], b, axis, size)

            def dst(b):
                return self._block(outs[a], b, axis, size) if gather else outs[a].at[b]

            def remote(src_ref, dst_ref, slot, dev):
                return pltpu.make_async_remote_copy(
                    src_ref=src_ref, dst_ref=dst_ref, send_sem=send_sems.at[a * 7 + slot],
                    recv_sem=recv_sems.at[a * 7 + slot], device_id=dev, device_id_type=MESH_ID)

            local = pltpu.make_async_copy(src(me), dst(me), loc_sems.at[a])
            begin.append(local)
            end.append(local.wait)
            if kind == "gather_by_chip":
                sib_dev, sib_id = peer(1)
                to_sib = remote(ins[a], dst(me), 0, sib_dev)
                begin.append(to_sib)
                end += [to_sib.wait_send, remote(ins[a], dst(sib_id), 0, sib_dev).wait_recv]
                for j, d in enumerate((2, 4, 6)):
                    dev, pid = peer(d)
                    over_ici = remote(ins[a], dst(me), 1 + j, dev)
                    begin.append(over_ici)
                    passed_on = remote(dst(pid), dst(pid), 4 + j, sib_dev)
                    middle.append((remote(ins[a], dst(pid), 1 + j, dev), passed_on))
                    end += [over_ici.wait_send, passed_on.wait_send,
                            remote(ins[a], dst(peer(d ^ 1)[1]), 4 + j, sib_dev).wait_recv]
            else:
                for d in range(1, N_DEV):
                    dev, pid = peer(d)
                    begin.append(remote(src(pid), dst(me), d - 1, dev))
                    end.append(remote(src(pid), dst(pid), d - 1, dev).wait)
        return begin, middle, end

    def start(self, ins, outs, sems):
        for cp in self._plan(ins, outs, sems)[0]:
            cp.start()

    def pass_on(self, ins, outs, sems):
        for arrived, onward in self._plan(ins, outs, sems)[1]:
            arrived.wait_recv()
            onward.start()

    def wait(self, ins, outs, sems):
        for wait in self._plan(ins, outs, sems)[2]:
            wait()


def _call(body, *, name, grid, in_specs, out_specs, out_shape, args, scratch_shapes=(), sem=None, comm=None):
    n_in, n_out, n_scr = len(in_specs), len(out_specs), len(scratch_shapes)
    if comm is None:
        return pl.pallas_call(
            body, name=name, grid=grid, out_shape=list(out_shape), in_specs=list(in_specs), out_specs=list(out_specs),
            scratch_shapes=list(scratch_shapes), compiler_params=_params(*sem))(*args)
    hbm = pl.BlockSpec(memory_space=pltpu.HBM)

    def carrier(*refs):
        bounds = [0, n_in, n_in + comm.n, n_in + comm.n + n_out, n_in + 2 * comm.n + n_out, len(refs) - 3, len(refs)]
        ins, cin, outs, cout, scr, sems = [refs[lo:hi] for lo, hi in zip(bounds[:-1], bounds[1:])]
        assert len(scr) == n_scr
        step = functools.reduce(lambda lin, ax: lin * grid[ax] + pl.program_id(ax), range(len(grid)), 0)
        steps = math.prod(grid)

        @pl.when(step == 0)
        def _():
            comm.start(cin, cout, sems)

        body(*ins, *outs, *scr)

        @pl.when(step == (2 * steps) // 3)
        def _():
            comm.pass_on(cin, cout, sems)

        @pl.when(step == steps - 1)
        def _():
            comm.wait(cin, cout, sems)

    return pl.pallas_call(
        carrier, name=name, grid=grid, out_shape=list(out_shape) + comm.out_shape,
        in_specs=list(in_specs) + [hbm] * comm.n, out_specs=list(out_specs) + [hbm] * comm.n,
        scratch_shapes=list(scratch_shapes) + comm.scratch,
        compiler_params=_params(*["arbitrary"] * len(grid)))(*args, *comm.arrays)


def _exchange(items, *, name, vmem=False):
    comm = _Exchange(items)
    space = pl.BlockSpec(memory_space=pltpu.VMEM if vmem else pltpu.HBM)

    def body(*refs):
        ins, outs, sems = refs[:comm.n], refs[comm.n:2 * comm.n], refs[2 * comm.n:]
        comm.start(ins, outs, sems)
        comm.pass_on(ins, outs, sems)
        comm.wait(ins, outs, sems)

    return pl.pallas_call(
        body, name=name, out_shape=comm.out_shape, in_specs=[space] * comm.n, out_specs=[space] * comm.n,
        scratch_shapes=comm.scratch, compiler_params=pltpu.CompilerParams(vmem_limit_bytes=VMEM_LIMIT))(*comm.arrays)


_DIMS = {"nn": (((1,), (0,)), ((), ())), "nt": (((1,), (1,)), ((), ())), "tn": (((0,), (0,)), ((), ()))}


def _matmul(a, b, *, dims, tm, tn, tk, name, epilogue="plain", out_dtype=F32, extras=(), comm=None,
            a_heads=False, b_heads=False):
    a_parts = a.shape[0] if a.ndim == 3 else 0
    b_parts = b.shape[0] if b.ndim == 3 else 0
    assert not (a_parts and dims != "nt" and not a_heads) and not (b_parts and dims != "tn")
    a2 = (a.shape[1], a.shape[2] * a_parts) if a_parts else a.shape
    b2 = (b.shape[1], b.shape[2] * b_parts) if b_parts else b.shape
    if dims == "tn":
        (K, M), N = a2, b2[1]
    else:
        (M, K), N = a2, (b2[1] if dims == "nn" else b2[0])
    tm, tn, tk = min(tm, M), min(tn, N), min(tk, K)
    assert M % tm == 0 and N % tn == 0 and K % tk == 0, (name, M, N, K, tm, tn, tk)
    if a_heads and dims == "tn":
        a_spec = pl.BlockSpec((tm // ATTN_HD, tk, ATTN_HD), lambda i, j, k: (i, k, 0))
    elif a_heads:
        a_spec = pl.BlockSpec((tk // ATTN_HD, tm, ATTN_HD), lambda i, j, k: (k, i, 0))
    elif a_parts:
        per = K // a_parts // tk
        a_spec = pl.BlockSpec((None, tm, tk), lambda i, j, k: (k // per, i, k % per))
    elif dims == "tn":
        a_spec = pl.BlockSpec((tk, tm), lambda i, j, k: (k, i))
    else:
        a_spec = pl.BlockSpec((tm, tk), lambda i, j, k: (i, k))
    if b_heads:
        b_spec = pl.BlockSpec((tn // ATTN_HD, tk, ATTN_HD), lambda i, j, k: (j, k, 0))
    elif b_parts:
        per_n = N // b_parts // tn
        b_spec = pl.BlockSpec((None, tk, tn), lambda i, j, k: (j // per_n, k, j % per_n))
    elif dims == "nt":
        b_spec = pl.BlockSpec((tn, tk), lambda i, j, k: (j, k))
    else:
        b_spec = pl.BlockSpec((tk, tn), lambda i, j, k: (k, j))
    side_by_side = lambda ref: jnp.concatenate([ref[g] for g in range(ref.shape[0])], axis=1)
    nk = K // tk
    tile = pl.BlockSpec((tm, tn), lambda i, j, k: (i, j))
    row = pl.BlockSpec((1, tn), lambda i, j, k: (0, j))
    if epilogue == "plain":
        extra_specs, out_shape, out_specs = [], [jax.ShapeDtypeStruct((M, N), out_dtype)], [tile]
    elif epilogue == "relu2":
        extra_specs, out_shape, out_specs = [], [jax.ShapeDtypeStruct((M, N), BF16)] * 2, [tile, tile]
    elif epilogue == "resgate":
        extra_specs, out_specs = [tile, row], [tile, tile]
        out_shape = [jax.ShapeDtypeStruct((M, N), BF16), jax.ShapeDtypeStruct((M, N), F32)]
    elif epilogue == "mul2a":
        extra_specs, out_shape, out_specs = [tile], [jax.ShapeDtypeStruct((M, N), BF16)], [tile]
    elif epilogue == "heads":
        per_tile = tn // ATTN_HD
        q_tiles = ATTN_HEADS // per_tile
        assert tn == 2 * ATTN_KV * ATTN_HD and N // tn in (q_tiles, q_tiles + 1)
        extra_specs = []
        out_shape = [jax.ShapeDtypeStruct((ATTN_HEADS, M, ATTN_HD), out_dtype)]
        out_specs = [pl.BlockSpec((per_tile, tm, ATTN_HD), lambda i, j, k: (jnp.minimum(j, q_tiles - 1), i, 0))]
        if N // tn > q_tiles:
            out_shape += [jax.ShapeDtypeStruct((ATTN_KV, M, ATTN_HD), out_dtype)] * 2
            out_specs += [pl.BlockSpec((ATTN_KV, tm, ATTN_HD), lambda i, j, k: (0, i, 0))] * 2
    else:
        raise ValueError(epilogue)
    n_extra = len(extra_specs)

    def body(a_ref, b_ref, *rest):
        ex, outs, acc_ref = rest[:n_extra], rest[n_extra:-1], rest[-1]
        k = pl.program_id(2)

        @pl.when(k == 0)
        def _():
            acc_ref[...] = jnp.zeros_like(acc_ref)

        a_tile = side_by_side(a_ref) if a_heads else a_ref[...]
        b_tile = side_by_side(b_ref) if b_heads else b_ref[...]
        acc_ref[...] += lax.dot_general(a_tile, b_tile, _DIMS[dims], preferred_element_type=F32)

        if epilogue == "heads":
            j = pl.program_id(1)

            @pl.when((k == nk - 1) & (j < q_tiles))
            def _():
                for g in range(per_tile):
                    outs[0][g] = acc_ref[:, g * ATTN_HD:(g + 1) * ATTN_HD].astype(out_dtype)

            if len(outs) > 1:
                @pl.when((k == nk - 1) & (j == q_tiles))
                def _():
                    for g in range(ATTN_KV):
                        outs[1][g] = acc_ref[:, g * ATTN_HD:(g + 1) * ATTN_HD].astype(out_dtype)
                        outs[2][g] = acc_ref[:, (ATTN_KV + g) * ATTN_HD:(ATTN_KV + g + 1) * ATTN_HD].astype(out_dtype)
            return

        @pl.when(k == nk - 1)
        def _():
            acc = acc_ref[...]
            if epilogue == "plain":
                outs[0][...] = acc.astype(out_dtype)
            elif epilogue == "relu2":
                act = jnp.maximum(acc, 0.0)
                outs[0][...] = act.astype(BF16)
                outs[1][...] = (act * act).astype(BF16)
            elif epilogue == "resgate":
                outs[0][...] = acc.astype(BF16)
                outs[1][...] = ex[0][...] + ex[1][...] * acc
            else:
                outs[0][...] = (acc * (2.0 * ex[0][...].astype(F32))).astype(BF16)

    res = _call(body, name=name, grid=(M // tm, N // tn, nk), out_shape=out_shape,
                in_specs=[a_spec, b_spec] + extra_specs, out_specs=out_specs,
                scratch_shapes=[pltpu.VMEM((tm, tn), F32)],
                sem=("parallel", "arbitrary" if epilogue == "heads" else "parallel", "arbitrary"),
                args=(a, b, *extras), comm=comm)
    return res[0] if len(res) == 1 else res


def _row_tile(T):
    return min(T, 256)


def _norm_mod_fwd(x, gain, sc, sh, name, comm=None):
    T, D = x.shape
    tr = _row_tile(T)

    def body(x_ref, g_ref, sc_ref, sh_ref, h_ref):
        xv = x_ref[...]
        r = lax.rsqrt(jnp.mean(xv * xv, axis=-1, keepdims=True) + EPS)
        hn = (xv * r) * g_ref[...]
        h_ref[...] = (hn * (1.0 + sc_ref[...]) + sh_ref[...]).astype(BF16)

    vec = pl.BlockSpec((1, D), lambda i: (0, 0))
    res = _call(body, name=name, grid=(T // tr,), out_shape=[jax.ShapeDtypeStruct((T, D), BF16)],
                in_specs=[pl.BlockSpec((tr, D), lambda i: (i, 0)), vec, vec, vec],
                out_specs=[pl.BlockSpec((tr, D), lambda i: (i, 0))], sem=("parallel",), args=(x, gain, sc, sh),
                comm=comm)
    return res[0] if comm is None else res


def _through_gate(dx, branch_ref, gate_ref, dbranch_ref, st_ref, row):
    dbranch_ref[...] = (dx * gate_ref[...]).astype(BF16)
    st_ref[row:row + 1, :] += jnp.sum(dx * branch_ref[...].astype(F32), axis=0, keepdims=True)


def _norm_mod_bwd(x, dh, dres, gain, sc, name, below=None):
    T, D = x.shape
    tr = _row_tile(T)

    def body(x_ref, dh_ref, dres_ref, g_ref, sc_ref, *rest):
        dx_ref, st_ref = rest[-3:-1] if below else rest[-2:]
        xv, dh_v, gain_v = x_ref[...], dh_ref[...].astype(F32), g_ref[...]
        r = lax.rsqrt(jnp.mean(xv * xv, axis=-1, keepdims=True) + EPS)
        xn = xv * r
        hn = xn * gain_v
        dhn = dh_v * (1.0 + sc_ref[...])
        dxn = dhn * gain_v
        dx = dres_ref[...] + r * (dxn - xn * jnp.mean(dxn * xn, axis=-1, keepdims=True))
        dx_ref[...] = dx

        @pl.when(pl.program_id(0) == 0)
        def _():
            st_ref[...] = jnp.zeros_like(st_ref)

        st_ref[0:1, :] += jnp.sum(dh_v, axis=0, keepdims=True)
        st_ref[1:2, :] += jnp.sum(dh_v * hn, axis=0, keepdims=True)
        st_ref[2:3, :] += jnp.sum(dhn * xn, axis=0, keepdims=True)
        if below:
            _through_gate(dx, rest[0], rest[1], rest[-1], st_ref, 3)

    vec = pl.BlockSpec((1, D), lambda i: (0, 0))
    blk = pl.BlockSpec((tr, D), lambda i: (i, 0))
    return pl.pallas_call(
        body, name=name, grid=(T // tr,),
        out_shape=[jax.ShapeDtypeStruct((T, D), F32), jax.ShapeDtypeStruct((8, D), F32)]
        + ([jax.ShapeDtypeStruct((T, D), BF16)] if below else []),
        in_specs=[blk, blk, blk, vec, vec] + ([blk, vec] if below else []),
        out_specs=[blk, pl.BlockSpec((8, D), lambda i: (0, 0))] + ([blk] if below else []),
        compiler_params=_params("arbitrary"),
    )(x, dh, dres, gain, sc, *(below or ()))


def _loss_head(y, target, below, name):
    T, D = y.shape
    tr = _row_tile(T)

    def body(y_ref, t_ref, b_ref, g_ref, dy_ref, l_ref, st_ref, db_ref):
        err = y_ref[...] - t_ref[...]
        dy = err * (1.0 / D)
        dy_ref[...] = dy

        @pl.when(pl.program_id(0) == 0)
        def _():
            l_ref[...] = jnp.zeros_like(l_ref)
            st_ref[...] = jnp.zeros_like(st_ref)

        part = jnp.sum(jnp.mean(err * err, axis=-1, keepdims=True), axis=0, keepdims=True)
        l_ref[...] += jnp.broadcast_to(0.5 * part, l_ref.shape)
        _through_gate(dy, b_ref, g_ref, db_ref, st_ref, 0)

    blk = pl.BlockSpec((tr, D), lambda i: (i, 0))
    return pl.pallas_call(
        body, name=name, grid=(T // tr,),
        out_shape=[jax.ShapeDtypeStruct((T, D), F32), jax.ShapeDtypeStruct((8, 128), F32),
                   jax.ShapeDtypeStruct((8, D), F32), jax.ShapeDtypeStruct((T, D), BF16)],
        in_specs=[blk, blk, blk, pl.BlockSpec((1, D), lambda i: (0, 0))],
        out_specs=[blk, pl.BlockSpec((8, 128), lambda i: (0, 0)), pl.BlockSpec((8, D), lambda i: (0, 0)), blk],
        compiler_params=_params("arbitrary"),
    )(y, target, *below)


def _silu(v):
    return v * jax.nn.sigmoid(v)


def _mod_fwd(c_all, mod_w, mod_b_cols, name):
    L, D, n = mod_w.shape
    tn = 512

    def body(c_ref, w_ref, b_ref, o_ref):
        cond = _silu(c_ref[...]).astype(BF16)
        o_ref[0] = jnp.dot(cond, w_ref[0].astype(BF16), preferred_element_type=F32) + b_ref[0]

    return pl.pallas_call(
        body, name=name, grid=(L, n // tn), out_shape=jax.ShapeDtypeStruct((L, N_DEV, n), F32),
        in_specs=[pl.BlockSpec((N_DEV, D), lambda l, j: (0, 0)), pl.BlockSpec((1, D, tn), lambda l, j: (l, 0, j)),
                  pl.BlockSpec((1, 1, tn), lambda l, j: (l, 0, j))],
        out_specs=pl.BlockSpec((1, N_DEV, tn), lambda l, j: (l, 0, j)),
        compiler_params=_params("parallel", "parallel"),
    )(c_all, mod_w, mod_b_cols)


def _adamw(g, w, m, v):
    m = ADAM_B1 * m + (1.0 - ADAM_B1) * g
    v = ADAM_B2 * v + (1.0 - ADAM_B2) * (g * g)
    m_hat = m / (1.0 - ADAM_B1 ** ADAM_STEP)
    v_hat = v / (1.0 - ADAM_B2 ** ADAM_STEP)
    delta = -ADAM_LR * (m_hat / (jnp.sqrt(v_hat) + ADAM_EPS) + ADAM_WD * w)
    return delta, m, v


def _mod_w_update(c_t, dmod_cols, w, m, v, name):
    L, D, n = w.shape
    tr = 256

    def body(c_ref, dm_ref, w_ref, m_ref, v_ref, g_ref, d_ref, nm_ref, nv_ref):
        cond = _silu(c_ref[...])
        dm = dm_ref[0]
        g = cond[:, 0:1] * dm[0:1, :]
        for b in range(1, N_DEV):
            g = g + cond[:, b:b + 1] * dm[b:b + 1, :]
        delta, nm, nv = _adamw(g, w_ref[0], m_ref[0], v_ref[0])
        g_ref[0], d_ref[0], nm_ref[0], nv_ref[0] = g, delta, nm, nv

    blk = pl.BlockSpec((1, tr, n), lambda l, i: (l, i, 0))
    return pl.pallas_call(
        body, name=name, grid=(L, D // tr), out_shape=[jax.ShapeDtypeStruct(w.shape, F32)] * 4,
        in_specs=[pl.BlockSpec((tr, N_DEV), lambda l, i: (i, 0)), pl.BlockSpec((1, N_DEV, n), lambda l, i: (l, 0, 0)),
                  blk, blk, blk],
        out_specs=[blk] * 4, compiler_params=_params("parallel", "parallel"),
    )(c_t, dmod_cols, w, m, v)


def _sum_adamw(parts, w, m, v, name):
    R, C = w.shape
    tr = min(R, 256 if C >= 1024 else 1024)
    assert R % tr == 0

    def body(p_ref, w_ref, m_ref, v_ref, g_ref, d_ref, nm_ref, nv_ref):
        g = p_ref[0].astype(F32)
        for s in range(1, N_DEV):
            g = g + p_ref[s].astype(F32)
        delta, nm, nv = _adamw(g, w_ref[...], m_ref[...], v_ref[...])
        g_ref[...], d_ref[...], nm_ref[...], nv_ref[...] = g, delta, nm, nv

    blk = pl.BlockSpec((tr, C), lambda i: (i, 0))
    return pl.pallas_call(
        body, name=name, grid=(R // tr,), out_shape=[jax.ShapeDtypeStruct((R, C), F32)] * 4,
        in_specs=[pl.BlockSpec((N_DEV, tr, C), lambda i: (0, i, 0)), blk, blk, blk], out_specs=[blk] * 4,
        compiler_params=_params("parallel"),
    )(parts, w, m, v)


def _lower_bound_row1(l0, l1):
    mx = lax.stop_gradient(jnp.maximum(l0, l1))
    e0, e1 = jnp.exp(l0 - mx), jnp.exp(l1 - mx)
    p0, p1 = e0 / (e0 + e1), e1 / (e0 + e1)
    return (p0 + p1) - p0


def _lb_fwd(logits, name):
    def body(l_ref, o_ref):
        o_ref[...] = _lower_bound_row1(l_ref[0:1, :], l_ref[1:2, :])

    return pl.pallas_call(body, name=name, out_shape=jax.ShapeDtypeStruct((1, logits.shape[1]), F32))(logits)


def _lb_bwd(logits, dlb, name):
    def body(l_ref, d_ref, o_ref):
        _, vjp = jax.vjp(_lower_bound_row1, l_ref[0:1, :], l_ref[1:2, :])
        d0, d1 = vjp(d_ref[...])
        o_ref[0:1, :] = d0
        o_ref[1:2, :] = d1

    return pl.pallas_call(body, name=name, out_shape=jax.ShapeDtypeStruct(logits.shape, F32))(logits, dlb)


def _bdot(a, b, dims):
    return lax.dot_general(a.astype(BF16), b.astype(BF16), _DIMS[dims], preferred_element_type=F32)


def _rms(x, gain):
    r = lax.rsqrt(jnp.mean(x * x, axis=-1, keepdims=True) + EPS)
    xhat = x * r
    return xhat, r, xhat * gain


def _attn_band(n):
    qi = lax.broadcasted_iota(jnp.int32, (ATTN_BLOCK, 2 * ATTN_BLOCK), 0)
    ki = lax.broadcasted_iota(jnp.int32, (ATTN_BLOCK, 2 * ATTN_BLOCK), 1)
    dist = qi + ATTN_BLOCK - ki
    first_key = jnp.where(n > 0, 0, ATTN_BLOCK)
    valid = (dist >= 0) & (dist < ATTN_BLOCK) & (ki >= first_key)
    return valid, jnp.abs(dist).astype(F32)


def _attn_head_probs(qn, kn_b, valid, absdist, slope, sink):
    s = lax.dot_general(qn.astype(BF16), kn_b, _DIMS["nt"], preferred_element_type=F32) * ATTN_SCALE
    s = jnp.where(valid, s - slope * absdist, NEG_BIG)
    mx = jnp.maximum(jnp.max(s, axis=-1, keepdims=True), sink)
    e = jnp.exp(s - mx)
    es = jnp.exp(sink - mx)
    inv = 1.0 / (jnp.sum(e, axis=-1, keepdims=True) + es)
    return e * inv, es * inv


def _attn_specs(T):
    nb = T // ATTN_BLOCK
    qspec = pl.BlockSpec((ATTN_GROUP, ATTN_BLOCK, ATTN_HD), lambda h, n: (h, n, 0))
    prev = pl.BlockSpec((1, ATTN_BLOCK, ATTN_HD), lambda h, n: (h, jnp.maximum(n - 1, 0), 0))
    cur = pl.BlockSpec((1, ATTN_BLOCK, ATTN_HD), lambda h, n: (h, n, 0))
    gain = pl.BlockSpec((1, ATTN_HD), lambda h, n: (0, 0))
    scalars = pl.BlockSpec(memory_space=pltpu.SMEM)
    return nb, qspec, prev, cur, gain, scalars


def _attn_fwd(qh, kh, vh, qg, kg, slopes, sinks, name, comm=None):
    T = qh.shape[1]
    nb, qspec, prev, cur, gain, scalars = _attn_specs(T)

    def body(q_ref, kp_ref, kc_ref, vp_ref, vc_ref, qg_ref, kg_ref, sl_ref, sk_ref, o_ref):
        h, n = pl.program_id(0), pl.program_id(1)
        valid, absdist = _attn_band(n)
        _, _, kn = _rms(jnp.concatenate([kp_ref[0], kc_ref[0]], axis=0), kg_ref[...])
        kn_b = kn.astype(BF16)
        v_b = jnp.concatenate([vp_ref[0], vc_ref[0]], axis=0).astype(BF16)
        for g in range(ATTN_GROUP):
            head = h * ATTN_GROUP + g
            _, _, qn = _rms(q_ref[g], qg_ref[...])
            p, _ = _attn_head_probs(qn, kn_b, valid, absdist, sl_ref[head], sk_ref[head])
            o_ref[g] = jnp.dot(p.astype(BF16), v_b, preferred_element_type=F32).astype(BF16)

    return _call(body, name=name, grid=(ATTN_KV, nb), out_shape=[jax.ShapeDtypeStruct(qh.shape, BF16)],
                 in_specs=[qspec, prev, cur, prev, cur, gain, gain, scalars, scalars], out_specs=[qspec],
                 sem=("parallel", "parallel"), args=(qh, kh, kh, vh, vh, qg, kg, slopes, sinks), comm=comm)


def _rms_bwd(dy, xhat, r, gain):
    dxh = dy * gain
    dx = r * (dxh - xhat * jnp.mean(dxh * xhat, axis=-1, keepdims=True))
    return dx, jnp.sum(dy * xhat, axis=0, keepdims=True)


def _attn_bwd(qh, kh, vh, doh, qg, kg, slope_col, sink_col, name, comm=None):
    T = qh.shape[1]
    nb = T // ATTN_BLOCK
    rows = ATTN_GROUP * ATTN_BLOCK

    def body(q_ref, kp_ref, kc_ref, vp_ref, vc_ref, do_ref, qg_ref, kg_ref, sl_ref, sk_ref,
             dq_ref, dk_ref, dv_ref, dqg_ref, dkg_ref, dsk_ref, carry_ref, sk_acc):
        h, step = pl.program_id(0), pl.program_id(1)
        n = nb - 1 - step
        qg_v, kg_v = qg_ref[...], kg_ref[...]
        qhat, rq, qn = _rms(q_ref[...].reshape(rows, ATTN_HD), qg_v)
        khat, rk, kn = _rms(jnp.concatenate([kp_ref[0], kc_ref[0]], axis=0), kg_v)
        s = _bdot(qn, kn, "nt") * ATTN_SCALE
        qi = lax.broadcasted_iota(jnp.int32, s.shape, 0) & (ATTN_BLOCK - 1)
        ki = lax.broadcasted_iota(jnp.int32, s.shape, 1)
        dist = qi + ATTN_BLOCK - ki
        first_key = jnp.where(n > 0, 0, ATTN_BLOCK)
        valid = (dist >= 0) & (dist < ATTN_BLOCK) & (ki >= first_key)
        s = jnp.where(valid, s - sl_ref[0] * jnp.abs(dist).astype(F32), NEG_BIG)
        sink = sk_ref[0]
        mx = jnp.maximum(jnp.max(s, axis=-1, keepdims=True), sink)
        e = jnp.exp(s - mx)
        es = jnp.exp(sink - mx)
        den = jnp.sum(e, axis=-1, keepdims=True) + es
        p, ps = e / den, es / den
        v = jnp.concatenate([vp_ref[0], vc_ref[0]], axis=0)
        do = do_ref[...].reshape(rows, ATTN_HD)
        dp = _bdot(do, v, "nt")
        delta = jnp.sum(p * dp, axis=-1, keepdims=True)
        ds = p * (dp - delta)
        dqn = _bdot(ds, kn, "nn") * ATTN_SCALE
        dkn = _bdot(ds, qn, "tn") * ATTN_SCALE
        dv = _bdot(p, do, "tn")
        dq, dqg = _rms_bwd(dqn, qhat, rq, qg_v)
        dk, dkg = _rms_bwd(dkn, khat, rk, kg_v)
        dq_ref[...] = dq.reshape(ATTN_GROUP, ATTN_BLOCK, ATTN_HD).astype(BF16)

        @pl.when((h == 0) & (step == 0))
        def _():
            dqg_ref[...] = jnp.zeros_like(dqg_ref)
            dkg_ref[...] = jnp.zeros_like(dkg_ref)

        dqg_ref[0:1, :] += dqg
        dkg_ref[0:1, :] += dkg

        @pl.when(step == 0)
        def _():
            carry_ref[...] = jnp.zeros_like(carry_ref)
            sk_acc[...] = jnp.zeros_like(sk_acc)

        dk_ref[0] = dk[ATTN_BLOCK:, :] + carry_ref[0]
        dv_ref[0] = dv[ATTN_BLOCK:, :] + carry_ref[1]
        carry_ref[0] = dk[:ATTN_BLOCK, :]
        carry_ref[1] = dv[:ATTN_BLOCK, :]
        sk_acc[...] += -ps * delta

        @pl.when(step == nb - 1)
        def _():
            for g in range(ATTN_GROUP):
                tot = jnp.sum(sk_acc[g * ATTN_BLOCK:(g + 1) * ATTN_BLOCK, :], axis=0, keepdims=True)
                dsk_ref[g:g + 1, :] = jnp.broadcast_to(tot, (1, 128))

    qspec = pl.BlockSpec((ATTN_GROUP, ATTN_BLOCK, ATTN_HD), lambda h, s: (h, nb - 1 - s, 0))
    prev = pl.BlockSpec((1, ATTN_BLOCK, ATTN_HD), lambda h, s: (h, jnp.maximum(nb - 2 - s, 0), 0))
    cur = pl.BlockSpec((1, ATTN_BLOCK, ATTN_HD), lambda h, s: (h, nb - 1 - s, 0))
    gain = pl.BlockSpec((1, ATTN_HD), lambda h, s: (0, 0))
    col = pl.BlockSpec((1, rows, 1), lambda h, s: (h, 0, 0))
    acc = pl.BlockSpec((8, ATTN_HD), lambda h, s: (0, 0))
    return _call(
        body, name=name, grid=(ATTN_KV, nb),
        out_shape=[jax.ShapeDtypeStruct(qh.shape, BF16), jax.ShapeDtypeStruct(kh.shape, F32),
                   jax.ShapeDtypeStruct(kh.shape, F32), jax.ShapeDtypeStruct((8, ATTN_HD), F32),
                   jax.ShapeDtypeStruct((8, ATTN_HD), F32), jax.ShapeDtypeStruct((ATTN_HEADS, 128), F32)],
        in_specs=[qspec, prev, cur, prev, cur, qspec, gain, gain, col, col],
        out_specs=[qspec, cur, cur, acc, acc, pl.BlockSpec((ATTN_GROUP, 128), lambda h, s: (h, 0))],
        scratch_shapes=[pltpu.VMEM((2, ATTN_BLOCK, ATTN_HD), F32), pltpu.VMEM((rows, 1), F32)],
        sem=("arbitrary", "arbitrary"), args=(qh, kh, kh, vh, vh, doh, qg, kg, slope_col, sink_col), comm=comm)


@functools.partial(jax.custom_vjp, nondiff_argnums=(2,))
def _mm(a, b, dims):
    return _bdot(a, b, dims)


def _mm_fwd(a, b, dims):
    return _bdot(a, b, dims), (a, b)


def _mm_bwd(dims, res, ct):
    a, b = res
    if dims == "nn":
        return _bdot(ct, b, "nt"), _bdot(a, ct, "tn")
    if dims == "nt":
        return _bdot(ct, b, "nn"), _bdot(ct, a, "tn")
    return _bdot(b, ct, "nt"), _bdot(a, ct, "nn")


_mm.defvjp(_mm_fwd, _mm_bwd)


def _same_chunk_mask(rows, upper):
    ri = lax.broadcasted_iota(jnp.int32, (rows, rows), 0)
    ci = lax.broadcasted_iota(jnp.int32, (rows, rows), 1)
    same = (ri >> HGRN_CHUNK_SHIFT) == (ci >> HGRN_CHUNK_SHIFT)
    return same & ((ri <= ci) if upper else (ri >= ci))


@functools.partial(jax.custom_vjp, nondiff_argnums=(1,))
def _chunk_cumsum(x, reverse):
    tri = _same_chunk_mask(x.shape[0], reverse).astype(BF16)
    hi = x.astype(BF16)
    rest = x - hi.astype(F32)
    mid = rest.astype(BF16)
    lo = (rest - mid.astype(F32)).astype(BF16)
    out = jnp.dot(tri, jnp.concatenate([hi, mid, lo], axis=1), preferred_element_type=F32)
    w = x.shape[1]
    return out[:, :w] + out[:, w:2 * w] + out[:, 2 * w:]


def _chunk_cumsum_fwd(x, reverse):
    return _chunk_cumsum(x, reverse), None


def _chunk_cumsum_bwd(reverse, _, ct):
    return (_chunk_cumsum(ct, not reverse),)


_chunk_cumsum.defvjp(_chunk_cumsum_fwd, _chunk_cumsum_bwd)


def _hgrn_block(st, qr, fr, v, gr, lb, og):
    rows = qr.shape[0]
    nc = rows // HGRN_CHUNK
    chunk_of_row = lax.broadcasted_iota(jnp.int32, qr.shape, 0) >> HGRN_CHUNK_SHIFT
    row_in_chunk = lax.broadcasted_iota(jnp.int32, (nc, HGRN_CHUNK, HGRN_DK), 1)
    per_chunk = lambda m: m.reshape(nc, HGRN_CHUNK, HGRN_DK)
    flat = lambda m: m.reshape(rows, HGRN_DK)
    by_chunk = lambda m: jnp.concatenate([jnp.where(chunk_of_row == c, m, 0.0) for c in range(nc)], axis=1)

    forget = lb + (1.0 - lb) * jax.nn.sigmoid(fr)
    k = 1.0 - forget
    b = _chunk_cumsum(jnp.log(forget), False)
    b3 = per_chunk(b)
    piv = jnp.sum(jnp.where(row_in_chunk == HGRN_CHUNK // 2 - 1, b3, 0.0), axis=1, keepdims=True)
    b_last = jnp.sum(jnp.where(row_in_chunk == HGRN_CHUNK - 1, b3, 0.0), axis=1, keepdims=True)
    q = _silu(qr) * HGRN_SCALE
    a = _mm(q * flat(jnp.exp(b3 - piv)), k * flat(jnp.exp(piv - b3)), "nt")
    o = _mm(jnp.where(_same_chunk_mask(rows, False), a, 0.0), v, "nn")
    updates = _mm(v, by_chunk(k * flat(jnp.exp(b_last - b3))), "tn")
    decay = jnp.exp(b_last)
    before = []
    for c in range(nc):
        before.append(st)
        st = st * decay[c] + updates[:, c * HGRN_DK:(c + 1) * HGRN_DK]
    o = o + _mm(by_chunk(q * jnp.exp(b)), jnp.concatenate(before, axis=1), "nt")
    y = (o * lax.rsqrt(jnp.mean(o * o, axis=-1, keepdims=True) + EPS)) * og * _silu(gr)
    return y, st


def _hgrn_tile(T):
    return min(T, 256)


HGRN_HEADS_PER_STEP = 4
HGRN_GROUPS = HGRN_HEADS // HGRN_HEADS_PER_STEP


def _hgrn_fwd(proj, lb, og, name, comm=None):
    T = proj.shape[0]
    tb = _hgrn_tile(T)
    hp, wide = HGRN_HEADS_PER_STEP, HGRN_HEADS_PER_STEP * HGRN_DK

    def body(q_ref, f_ref, v_ref, g_ref, lb_ref, og_ref, o_ref, s_ref, st_ref):
        @pl.when(pl.program_id(1) == 0)
        def _():
            st_ref[...] = jnp.zeros_like(st_ref)

        for j in range(hp):
            ln = slice(j * HGRN_DK, (j + 1) * HGRN_DK)
            st = st_ref[j]
            s_ref[j, 0] = st
            y, st_ref[j] = _hgrn_block(st, q_ref[:, ln], f_ref[:, ln], v_ref[:, ln], g_ref[:, ln], lb_ref[j], og_ref[j])
            o_ref[:, ln] = y.astype(BF16)

    part = lambda p: pl.BlockSpec((tb, wide), lambda h, t: (t, p * HGRN_GROUPS + h))
    vec = pl.BlockSpec((hp, 1, HGRN_DK), lambda h, t: (h, 0, 0))
    return _call(
        body, name=name, grid=(HGRN_GROUPS, T // tb),
        out_shape=[jax.ShapeDtypeStruct((T, D_MODEL), BF16),
                   jax.ShapeDtypeStruct((HGRN_HEADS, T // tb, HGRN_DK, HGRN_DK), F32)],
        in_specs=[part(0), part(1), part(2), part(3), vec, vec],
        out_specs=[pl.BlockSpec((tb, wide), lambda h, t: (t, h)),
                   pl.BlockSpec((hp, 1, HGRN_DK, HGRN_DK), lambda h, t: (h, t, 0, 0))],
        scratch_shapes=[pltpu.VMEM((hp, HGRN_DK, HGRN_DK), F32)], sem=("parallel", "arbitrary"),
        args=(proj, proj, proj, proj, lb, og), comm=comm)


def _hgrn_bwd(proj, states, do, lb, og, name, comm=None):
    T = proj.shape[0]
    tb = _hgrn_tile(T)
    nt, hp, wide = T // tb, HGRN_HEADS_PER_STEP, HGRN_HEADS_PER_STEP * HGRN_DK

    def body(q_ref, f_ref, v_ref, g_ref, s_ref, do_ref, lb_ref, og_ref, dp_ref, dlb_ref, dog_ref, dst_ref):
        @pl.when(pl.program_id(1) == 0)
        def _():
            dst_ref[...] = jnp.zeros_like(dst_ref)
            dlb_ref[...] = jnp.zeros_like(dlb_ref)
            dog_ref[...] = jnp.zeros_like(dog_ref)

        for j in range(hp):
            ln = slice(j * HGRN_DK, (j + 1) * HGRN_DK)
            _, vjp = jax.vjp(_hgrn_block, s_ref[j, 0], q_ref[:, ln], f_ref[:, ln], v_ref[:, ln], g_ref[:, ln],
                             lb_ref[j], og_ref[j])
            dst_ref[j], dq, df, dv, dg, dlb, dog = vjp((do_ref[:, ln], dst_ref[j]))
            for p, part_grad in enumerate((dq, df, dv, dg)):
                dp_ref[p, :, ln] = part_grad.astype(BF16)
            dlb_ref[j] += dlb
            dog_ref[j] += dog

    part = lambda p: pl.BlockSpec((tb, wide), lambda h, t: (nt - 1 - t, p * HGRN_GROUPS + h))
    vec = pl.BlockSpec((hp, 1, HGRN_DK), lambda h, t: (h, 0, 0))
    head = pl.BlockSpec((tb, wide), lambda h, t: (nt - 1 - t, h))
    return _call(
        body, name=name, grid=(HGRN_GROUPS, nt),
        out_shape=[jax.ShapeDtypeStruct((4, T, D_MODEL), BF16)] + [jax.ShapeDtypeStruct((HGRN_HEADS, 1, HGRN_DK), F32)] * 2,
        in_specs=[part(0), part(1), part(2), part(3),
                  pl.BlockSpec((hp, 1, HGRN_DK, HGRN_DK), lambda h, t: (h, nt - 1 - t, 0, 0)), head, vec, vec],
        out_specs=[pl.BlockSpec((4, tb, wide), lambda h, t: (0, nt - 1 - t, h)), vec, vec],
        scratch_shapes=[pltpu.VMEM((hp, HGRN_DK, HGRN_DK), F32)], sem=("parallel", "arbitrary"),
        args=(proj, proj, proj, proj, states, do, lb, og), comm=comm)


def _cols_to_blocks(g, n8):
    K = g.shape[0]
    return g.reshape(K, N_DEV, n8).transpose(1, 0, 2)


def _blocks_to_cols(wg):
    _, K, n8 = wg.shape
    return wg.transpose(1, 0, 2).reshape(K, N_DEV * n8)


def _pack(parts):
    flat = []
    for p in parts:
        v = p.reshape(-1)
        flat.append(jnp.pad(v, (0, (-v.shape[0]) % 1024)))
    return jnp.concatenate(flat).reshape(-1, 128)


def _unpack(packed, like):
    flat, out, off = packed.reshape(-1), [], 0
    for p in like:
        size = math.prod(p.shape)
        out.append(flat[off:off + size].reshape(p.shape))
        off += size + (-size) % 1024
    return out


def _heads_major(a, heads):
    T = a.shape[0]
    return a.reshape(T, heads, ATTN_HD).transpose(1, 0, 2)


def _heads_minor(a):
    heads, T, _ = a.shape
    return a.transpose(1, 0, 2).reshape(T, heads * ATTN_HD)


def kernel(x, c, mod_w, mod_b, norm_mix, norm_mlp, attn_w_in, attn_w_out, attn_q_gain, attn_k_gain, attn_sinks, hgrn_w_in, hgrn_w_out, hgrn_o_gain, hgrn_lb_logits, mlp_w1, mlp_w2, loss_target, m_mod_w, m_mod_b, m_norm_mix, m_norm_mlp, m_attn_w_in, m_attn_w_out, m_attn_q_gain, m_attn_k_gain, m_attn_sinks, m_hgrn_w_in, m_hgrn_w_out, m_hgrn_o_gain, m_hgrn_lb_logits, m_mlp_w1, m_mlp_w2, v_mod_w, v_mod_b, v_norm_mix, v_norm_mlp, v_attn_w_in, v_attn_w_out, v_attn_q_gain, v_attn_k_gain, v_attn_sinks, v_hgrn_w_in, v_hgrn_w_out, v_hgrn_o_gain, v_hgrn_lb_logits, v_mlp_w1, v_mlp_w2):
    T = x.shape[1]
    me = 4 * lax.axis_index("x") + 2 * lax.axis_index("y") + lax.axis_index("c")
    x0, target = x[0], loss_target[0]
    n_mod = mod_w.shape[2]

    shards = [attn_w_in[0], attn_w_out[0], hgrn_w_in[0], hgrn_w_out[0], mlp_w1[0], mlp_w1[1], mlp_w2[0], mlp_w2[1]]
    sb = [s.astype(BF16) for s in shards]
    gather = lambda *items: _Exchange([("gather_by_chip", a, axis) for a, axis in items])

    c_all = _exchange([("gather", c.reshape(16, 128), None)], vmem=True, name="gather_c")[0].reshape(N_DEV, D_MODEL)
    mod_b_cols = lax.dynamic_slice_in_dim(mod_b, me * n_mod, n_mod, axis=1).reshape(2, 1, n_mod)
    mod_cols = _mod_fwd(c_all, mod_w, mod_b_cols, "mod_fwd")
    mod_all = _exchange([("gather", mod_cols.reshape(-1, 128), None)], vmem=True, name="gather_mod")[0]
    mod_all = mod_all.reshape(N_DEV, 2, N_DEV, n_mod)
    mod_mine = lax.dynamic_index_in_dim(mod_all, me, axis=2, keepdims=False)
    mod_mine = mod_mine.transpose(1, 0, 2).reshape(2, N_MOD, 1, D_MODEL)

    lb = _lb_fwd(hgrn_lb_logits, "lb_fwd").reshape(HGRN_HEADS, 1, HGRN_DK)
    og = hgrn_o_gain.reshape(HGRN_HEADS, 1, HGRN_DK)
    slopes = jnp.exp2(-8.0 * jnp.arange(1, ATTN_HEADS + 1, dtype=F32) / ATTN_HEADS)
    sinks = attn_sinks[0]
    per_row = lambda vals: jnp.repeat(vals.reshape(ATTN_KV, ATTN_GROUP), ATTN_BLOCK, axis=1).reshape(
        ATTN_KV, ATTN_GROUP * ATTN_BLOCK, 1)
    slope_col, sink_col = per_row(slopes), per_row(sinks)

    saved = []
    xi = x0
    w_mlp1, w_mlp2 = [None, None], [None, None]
    for i in range(2):
        sh1, sc1, g1, sh2, sc2, g2 = [mod_mine[i, j] for j in range(N_MOD)]
        if i == 0:
            h, w_attn_in = _norm_mod_fwd(xi, norm_mix[i:i + 1], sc1, sh1, "norm_mix_fwd0", comm=gather((sb[0], None)))
            w_attn_in = _blocks_to_cols(w_attn_in)
            qh, kh, vh, w_attn_out = _matmul(h, w_attn_in, dims="nn", tm=1024, tn=512, tk=2048, name="attn_in_fwd",
                                             epilogue="heads", comm=gather((sb[1], 0)))
            o, w_mlp1[0] = _attn_fwd(qh, kh, vh, attn_q_gain, attn_k_gain, slopes, sinks, "attn_fwd",
                                     comm=gather((sb[4], 1)))
            mix = (qh, kh, vh)
            w_out = w_attn_out
        else:
            h = _norm_mod_fwd(xi, norm_mix[i:i + 1], sc1, sh1, "norm_mix_fwd1")
            proj, w_hgrn_out = _matmul(h, w_hgrn_in, dims="nn", tm=1024, tn=1024, tk=2048, name="hgrn_in_fwd",
                                       comm=gather((sb[3], 0)))
            o, states, w_mlp1[1] = _hgrn_fwd(proj, lb, og, "hgrn_fwd", comm=gather((sb[5], 1)))
            mix = (proj, states)
            w_out = w_hgrn_out
        y, x1 = _matmul(o, w_out, dims="nn", tm=1024, tn=1024, tk=2048, name=f"mix_out_fwd{i}", epilogue="resgate",
                        extras=(xi, g1), a_heads=i == 0)
        h2 = _norm_mod_fwd(x1, norm_mlp[i:i + 1], sc2, sh2, f"norm_mlp_fwd{i}")
        if i == 0:
            act, act2, w_mlp2[0] = _matmul(h2, w_mlp1[0], dims="nn", tm=1024, tn=1024, tk=2048, name="mlp1_fwd0",
                                           epilogue="relu2", comm=gather((sb[6], 0)))
            z, x2, w_hgrn_in = _matmul(act2, w_mlp2[0], dims="nn", tm=1024, tn=1024, tk=2048, name="mlp2_fwd0",
                                       epilogue="resgate", extras=(x1, g2), comm=gather((sb[2], 1)))
        else:
            act, act2, w_mlp2[1] = _matmul(h2, w_mlp1[1], dims="nn", tm=1024, tn=1024, tk=2048, name="mlp1_fwd1",
                                           epilogue="relu2", comm=gather((sb[7], 0)))
            z, x2 = _matmul(act2, w_mlp2[1], dims="nn", tm=1024, tn=1024, tk=2048, name="mlp2_fwd1", epilogue="resgate",
                            extras=(x1, g2))
        saved.append((xi, h, o, y, x1, h2, act, act2, z, mix))
        xi = x2

    dx, loss_tile, st_g2, dz = _loss_head(xi, target, (saved[1][8], mod_mine[1, 5]), "loss_head")
    loss = lax.psum(loss_tile[0, 0], ("x", "y", "c"))

    scatter = lambda *items: _Exchange([("scatter", a, axis) for a, axis in items])
    shares, dmods, dnorm_mix, dnorm_mlp = {}, [None, None], [None, None], [None, None]
    wgrad = lambda a, b, name, tn=1024: _matmul(a, b, dims="tn", tm=2048, tn=tn, tk=1024, name=name, out_dtype=BF16)
    for i in (1, 0):
        sh1, sc1, g1, sh2, sc2, g2 = [mod_mine[i, j] for j in range(N_MOD)]
        xin, h, o, y, x1, h2, act, act2, z, mix = saved[i]
        if i == 1:
            dpre = _matmul(dz, w_mlp2[1], dims="nt", tm=1024, tn=1024, tk=2048, name="mlp2_bwd1", epilogue="mul2a",
                           extras=(act,))
        else:
            dpre, shares["hgrn_w_in"] = _matmul(dz, w_mlp2[0], dims="nt", tm=1024, tn=1024, tk=2048, name="mlp2_bwd0",
                                                epilogue="mul2a", extras=(act,), comm=scatter((g_hgrn_in, 1)))
        g_mlp2 = wgrad(act2, dz, f"mlp2_wgrad{i}")
        dh2 = _matmul(dpre, w_mlp1[i], dims="nt", tm=1024, tn=1024, tk=2048, name=f"mlp1_bwd{i}", out_dtype=BF16)
        g_mlp1 = wgrad(h2, dpre, f"mlp1_wgrad{i}")
        dx1, st_mlp, dy = _norm_mod_bwd(x1, dh2, dx, norm_mlp[i:i + 1], sc2, f"norm_mlp_bwd{i}", below=(y, g1))
        w_out = w_attn_out if i == 0 else w_hgrn_out
        if i == 0:
            qh, kh, vh = mix
            doh = _matmul(dy, w_out, dims="nt", tm=1024, tn=512, tk=2048, name="mix_out_bwd0", epilogue="heads",
                          out_dtype=BF16)
            g_out = _matmul(o, dy, dims="tn", tm=1024, tn=1024, tk=1024, name="mix_out_wgrad0", out_dtype=BF16,
                            a_heads=True)
            dqh, dkh, dvh, dqg, dkg, dsk, shares["mlp_w2_0"], shares["mlp_w1_0"], shares["attn_w_out"] = _attn_bwd(
                qh, kh, vh, doh, attn_q_gain, attn_k_gain, slope_col, sink_col, "attn_bwd",
                comm=scatter((g_mlp2, 0), (g_mlp1, 1), (g_out, 0)))
            dproj = jnp.concatenate([dqh, dkh.astype(BF16), dvh.astype(BF16)], axis=0)
            d_q_gain, d_k_gain, d_sinks = dqg[0:1], dkg[0:1], dsk[:, 0].reshape(1, ATTN_HEADS)
            g_attn_in = _matmul(h, dproj, dims="tn", tm=1024, tn=640, tk=1024, name="mix_in_wgrad0", out_dtype=BF16,
                                b_heads=True)
            g_attn_in = _cols_to_blocks(g_attn_in, attn_w_in.shape[2])
            dh, shares["attn_w_in"] = _matmul(dproj, w_attn_in, dims="nt", tm=1024, tn=1024, tk=1280, out_dtype=BF16,
                                              name="mix_in_bwd0", comm=scatter((g_attn_in, None)), a_heads=True)
        else:
            do = _matmul(dy, w_out, dims="nt", tm=1024, tn=1024, tk=2048, name="mix_out_bwd1")
            g_out = wgrad(o, dy, "mix_out_wgrad1")
            proj, states = mix
            dproj, dlb, d_o_gain, shares["mlp_w2_1"], shares["mlp_w1_1"], shares["hgrn_w_out"] = _hgrn_bwd(
                proj, states, do, lb, og, "hgrn_bwd", comm=scatter((g_mlp2, 0), (g_mlp1, 1), (g_out, 0)))
            d_lb_logits = _lb_bwd(hgrn_lb_logits, dlb.reshape(1, D_MODEL), "lb_bwd")
            dh = _matmul(dproj, w_hgrn_in, dims="nt", tm=1024, tn=1024, tk=2048, name="mix_in_bwd1", out_dtype=BF16)
            g_hgrn_in = wgrad(h, dproj, "mix_in_wgrad1")
        d_gate2 = st_g2[0:1] if i == 1 else st_mix_above[3:4]
        if i == 1:
            dx, st_mix, dz = _norm_mod_bwd(xin, dh, dx1, norm_mix[i:i + 1], sc1, "norm_mix_bwd1",
                                           below=(saved[0][8], mod_mine[0, 5]))
            st_mix_above = st_mix
        else:
            dx, st_mix = _norm_mod_bwd(xin, dh, dx1, norm_mix[i:i + 1], sc1, "norm_mix_bwd0")
        dmods[i] = jnp.concatenate([st_mix[0:1], st_mix[1:2], st_mlp[3:4], st_mlp[0:1], st_mlp[1:2], d_gate2], axis=1)
        dnorm_mix[i], dnorm_mlp[i] = st_mix[2:3], st_mlp[2:3]

    names = ["attn_w_in", "attn_w_out", "hgrn_w_in", "hgrn_w_out", "mlp_w1_0", "mlp_w1_1", "mlp_w2_0", "mlp_w2_1"]
    shares = [shares[nm] for nm in names]
    moments = [(m_attn_w_in[0], v_attn_w_in[0]), (m_attn_w_out[0], v_attn_w_out[0]), (m_hgrn_w_in[0], v_hgrn_w_in[0]),
               (m_hgrn_w_out[0], v_hgrn_w_out[0]), (m_mlp_w1[0], v_mlp_w1[0]), (m_mlp_w1[1], v_mlp_w1[1]),
               (m_mlp_w2[0], v_mlp_w2[0]), (m_mlp_w2[1], v_mlp_w2[1])]
    big = {nm: _sum_adamw(sh, w, m, v, f"adamw_{nm}") for nm, sh, w, (m, v) in zip(names, shares, shards, moments)}

    small_w = [mod_b, norm_mix, norm_mlp, attn_q_gain, attn_k_gain, attn_sinks, hgrn_o_gain, hgrn_lb_logits]
    small_m = [m_mod_b, m_norm_mix, m_norm_mlp, m_attn_q_gain, m_attn_k_gain, m_attn_sinks, m_hgrn_o_gain, m_hgrn_lb_logits]
    small_v = [v_mod_b, v_norm_mix, v_norm_mlp, v_attn_q_gain, v_attn_k_gain, v_attn_sinks, v_hgrn_o_gain, v_hgrn_lb_logits]
    small_g = [jnp.concatenate(dmods, axis=0), jnp.concatenate(dnorm_mix, axis=0), jnp.concatenate(dnorm_mlp, axis=0),
               d_q_gain, d_k_gain, d_sinks, d_o_gain.reshape(hgrn_o_gain.shape), d_lb_logits]
    packed_g = _pack(small_g)
    pad_rows = (-packed_g.shape[0]) % 8
    pad8 = lambda a: jnp.pad(a, ((0, pad_rows), (0, 0)))
    all_small = _exchange([("gather", pad8(packed_g), None)], vmem=True, name="gather_small_grads")[0]
    sg, sd, sm, sv = _sum_adamw(all_small, pad8(_pack(small_w)), pad8(_pack(small_m)), pad8(_pack(small_v)),
                                "adamw_small")
    small = [_unpack(t, small_w) for t in (sg, sd, sm, sv)]

    n_modb = N_MOD * D_MODEL
    dmod_all = all_small[:, :2 * n_modb // 128, :].reshape(N_DEV, 2, n_modb)
    dmod_cols = lax.dynamic_slice_in_dim(dmod_all, me * n_mod, n_mod, axis=2).transpose(1, 0, 2)
    modw = _mod_w_update(c_all.T, dmod_cols, mod_w, m_mod_w, v_mod_w, "adamw_mod_w")

    def leaf(k):
        stack = lambda a, b: jnp.stack([big[a][k], big[b][k]])
        one = lambda a: big[a][k][None]
        s = small[k]
        return [modw[k], s[0], s[1], s[2], one("attn_w_in"), one("attn_w_out"), s[3], s[4], s[5], one("hgrn_w_in"),
                one("hgrn_w_out"), s[6], s[7], stack("mlp_w1_0", "mlp_w1_1"), stack("mlp_w2_0", "mlp_w2_1")]

    return (loss, dx[None], *leaf(0), *leaf(1), *leaf(2), *leaf(3))
```

```python
import functools
import math

import jax
import jax.numpy as jnp
from jax import lax
from jax.experimental import pallas as pl
from jax.experimental.pallas import tpu as pltpu

F32, BF16 = jnp.float32, jnp.bfloat16
N_DEV = 8
D_MODEL = 2048
N_MOD = 6
EPS = 1e-6
ATTN_HD, ATTN_HEADS, ATTN_KV, ATTN_GROUP, ATTN_BLOCK = 64, 32, 4, 8, 128
ATTN_SCALE = 1.0 / math.sqrt(ATTN_HD)
HGRN_HEADS, HGRN_DK, HGRN_CHUNK = 16, 128, 64
HGRN_SCALE = 1.0 / math.sqrt(HGRN_DK)
HGRN_CHUNK_SHIFT = HGRN_CHUNK.bit_length() - 1
assert 1 << HGRN_CHUNK_SHIFT == HGRN_CHUNK
D_FF = 4 * D_MODEL
ADAM_LR, ADAM_B1, ADAM_B2, ADAM_EPS, ADAM_WD, ADAM_STEP = 0.001, 0.9, 0.999, 1e-08, 0.01, 10
NEG_BIG = -1e30
VMEM_LIMIT = 56 * 1024 * 1024
MESH_ID = pl.DeviceIdType.MESH


def _params(*sem):
    return pltpu.CompilerParams(dimension_semantics=sem, vmem_limit_bytes=VMEM_LIMIT)


class _Exchange:
    def __init__(self, items):
        self.items = items
        self.n = len(items)
        self.out_shape = []
        for kind, a, axis in items:
            assert kind in ("gather", "gather_by_chip", "scatter")
            if kind != "scatter":
                shape = (N_DEV,) + a.shape if axis is None else tuple(
                    d * N_DEV if i == axis else d for i, d in enumerate(a.shape))
            else:
                shape = a.shape if axis is None else (N_DEV,) + tuple(
                    d // N_DEV if i == axis else d for i, d in enumerate(a.shape))
            self.out_shape.append(jax.ShapeDtypeStruct(shape, a.dtype))
        self.scratch = [pltpu.SemaphoreType.DMA((7 * self.n,)), pltpu.SemaphoreType.DMA((7 * self.n,)),
                        pltpu.SemaphoreType.DMA((self.n,))]
        self.arrays = [a for _, a, _ in items]

    @staticmethod
    def _block(ref, b, axis, size):
        if axis is None:
            return ref.at[b]
        sl = pl.ds(pl.multiple_of(b * size, size), size)
        return ref.at[sl, :] if axis == 0 else ref.at[:, sl]

    def _plan(self, ins, outs, sems):
        send_sems, recv_sems, loc_sems = sems
        x, y, c = lax.axis_index("x"), lax.axis_index("y"), lax.axis_index("c")
        me = 4 * x + 2 * y + c
        begin, middle, end = [], [], []

        def peer(d):
            px = 1 - x if d & 4 else x
            py = 1 - y if d & 2 else y
            pc = 1 - c if d & 1 else c
            return (px, py, pc), 4 * px + 2 * py + pc

        for a, (kind, arr, axis) in enumerate(self.items):
            gather = kind != "scatter"
            size = None if axis is None else (arr.shape[axis] if gather else arr.shape[axis] // N_DEV)

            def src(b):
                return ins[a] if gather else self._block(ins[a], b, axis, size)

            def dst(b):
                return self._block(outs[a], b, axis, size) if gather else outs[a].at[b]

            def remote(src_ref, dst_ref, slot, dev):
                return pltpu.make_async_remote_copy(
                    src_ref=src_ref, dst_ref=dst_ref, send_sem=send_sems.at[a * 7 + slot],
                    recv_sem=recv_sems.at[a * 7 + slot], device_id=dev, device_id_type=MESH_ID)

            local = pltpu.make_async_copy(src(me), dst(me), loc_sems.at[a])
            begin.append(local)
            end.append(local.wait)
            if kind == "gather_by_chip":
                sib_dev, sib_id = peer(1)
                to_sib = remote(ins[a], dst(me), 0, sib_dev)
                begin.append(to_sib)
                end += [to_sib.wait_send, remote(ins[a], dst(sib_id), 0, sib_dev).wait_recv]
                for j, d in enumerate((2, 4, 6)):
                    dev, pid = peer(d)
                    over_ici = remote(ins[a], dst(me), 1 + j, dev)
                    begin.append(over_ici)
                    passed_on = remote(dst(pid), dst(pid), 4 + j, sib_dev)
                    middle.append((remote(ins[a], dst(pid), 1 + j, dev), passed_on))
                    end += [over_ici.wait_send, passed_on.wait_send,
                            remote(ins[a], dst(peer(d ^ 1)[1]), 4 + j, sib_dev).wait_recv]
            else:
                for d in range(1, N_DEV):
                    dev, pid = peer(d)
                    begin.append(remote(src(pid), dst(me), d - 1, dev))
                    end.append(remote(src(pid), dst(pid), d - 1, dev).wait)
        return begin, middle, end

    def start(self, ins, outs, sems):
        for cp in self._plan(ins, outs, sems)[0]:
            cp.start()

    def pass_on(self, ins, outs, sems):
        for arrived, onward in self._plan(ins, outs, sems)[1]:
            arrived.wait_recv()
            onward.start()

    def wait(self, ins, outs, sems):
        for wait in self._plan(ins, outs, sems)[2]:
            wait()


def _call(body, *, name, grid, in_specs, out_specs, out_shape, args, scratch_shapes=(), sem=None, comm=None):
    n_in, n_out, n_scr = len(in_specs), len(out_specs), len(scratch_shapes)
    if comm is None:
        return pl.pallas_call(
            body, name=name, grid=grid, out_shape=list(out_shape), in_specs=list(in_specs), out_specs=list(out_specs),
            scratch_shapes=list(scratch_shapes), compiler_params=_params(*sem))(*args)
    hbm = pl.BlockSpec(memory_space=pltpu.HBM)

    def carrier(*refs):
        bounds = [0, n_in, n_in + comm.n, n_in + comm.n + n_out, n_in + 2 * comm.n + n_out, len(refs) - 3, len(refs)]
        ins, cin, outs, cout, scr, sems = [refs[lo:hi] for lo, hi in zip(bounds[:-1], bounds[1:])]
        assert len(scr) == n_scr
        step = functools.reduce(lambda lin, ax: lin * grid[ax] + pl.program_id(ax), range(len(grid)), 0)
        steps = math.prod(grid)

        @pl.when(step == 0)
        def _():
            comm.start(cin, cout, sems)

        body(*ins, *outs, *scr)

        @pl.when(step == (2 * steps) // 3)
        def _():
            comm.pass_on(cin, cout, sems)

        @pl.when(step == steps - 1)
        def _():
            comm.wait(cin, cout, sems)

    return pl.pallas_call(
        carrier, name=name, grid=grid, out_shape=list(out_shape) + comm.out_shape,
        in_specs=list(in_specs) + [hbm] * comm.n, out_specs=list(out_specs) + [hbm] * comm.n,
        scratch_shapes=list(scratch_shapes) + comm.scratch,
        compiler_params=_params(*["arbitrary"] * len(grid)))(*args, *comm.arrays)


def _exchange(items, *, name, vmem=False):
    comm = _Exchange(items)
    space = pl.BlockSpec(memory_space=pltpu.VMEM if vmem else pltpu.HBM)

    def body(*refs):
        ins, outs, sems = refs[:comm.n], refs[comm.n:2 * comm.n], refs[2 * comm.n:]
        comm.start(ins, outs, sems)
        comm.pass_on(ins, outs, sems)
        comm.wait(ins, outs, sems)

    return pl.pallas_call(
        body, name=name, out_shape=comm.out_shape, in_specs=[space] * comm.n, out_specs=[space] * comm.n,
        scratch_shapes=comm.scratch, compiler_params=pltpu.CompilerParams(vmem_limit_bytes=VMEM_LIMIT))(*comm.arrays)


_DIMS = {"nn": (((1,), (0,)), ((), ())), "nt": (((1,), (1,)), ((), ())), "tn": (((0,), (0,)), ((), ()))}


def _matmul(a, b, *, dims, tm, tn, tk, name, epilogue="plain", out_dtype=F32, extras=(), comm=None,
            a_heads=False, b_heads=False):
    a_parts = a.shape[0] if a.ndim == 3 else 0
    b_parts = b.shape[0] if b.ndim == 3 else 0
    assert not (a_parts and dims != "nt" and not a_heads) and not (b_parts and dims != "tn")
    a2 = (a.shape[1], a.shape[2] * a_parts) if a_parts else a.shape
    b2 = (b.shape[1], b.shape[2] * b_parts) if b_parts else b.shape
    if dims == "tn":
        (K, M), N = a2, b2[1]
    else:
        (M, K), N = a2, (b2[1] if dims == "nn" else b2[0])
    tm, tn, tk = min(tm, M), min(tn, N), min(tk, K)
    assert M % tm == 0 and N % tn == 0 and K % tk == 0, (name, M, N, K, tm, tn, tk)
    if a_heads and dims == "tn":
        a_spec = pl.BlockSpec((tm // ATTN_HD, tk, ATTN_HD), lambda i, j, k: (i, k, 0))
    elif a_heads:
        a_spec = pl.BlockSpec((tk // ATTN_HD, tm, ATTN_HD), lambda i, j, k: (k, i, 0))
    elif a_parts:
        per = K // a_parts // tk
        a_spec = pl.BlockSpec((None, tm, tk), lambda i, j, k: (k // per, i, k % per))
    elif dims == "tn":
        a_spec = pl.BlockSpec((tk, tm), lambda i, j, k: (k, i))
    else:
        a_spec = pl.BlockSpec((tm, tk), lambda i, j, k: (i, k))
    if b_heads:
        b_spec = pl.BlockSpec((tn // ATTN_HD, tk, ATTN_HD), lambda i, j, k: (j, k, 0))
    elif b_parts:
        per_n = N // b_parts // tn
        b_spec = pl.BlockSpec((None, tk, tn), lambda i, j, k: (j // per_n, k, j % per_n))
    elif dims == "nt":
        b_spec = pl.BlockSpec((tn, tk), lambda i, j, k: (j, k))
    else:
        b_spec = pl.BlockSpec((tk, tn), lambda i, j, k: (k, j))
    side_by_side = lambda ref: jnp.concatenate([ref[g] for g in range(ref.shape[0])], axis=1)
    nk = K // tk
    tile = pl.BlockSpec((tm, tn), lambda i, j, k: (i, j))
    row = pl.BlockSpec((1, tn), lambda i, j, k: (0, j))
    if epilogue == "plain":
        extra_specs, out_shape, out_specs = [], [jax.ShapeDtypeStruct((M, N), out_dtype)], [tile]
    elif epilogue == "relu2":
        extra_specs, out_shape, out_specs = [], [jax.ShapeDtypeStruct((M, N), BF16)] * 2, [tile, tile]
    elif epilogue == "resgate":
        extra_specs, out_specs = [tile, row], [tile, tile]
        out_shape = [jax.ShapeDtypeStruct((M, N), BF16), jax.ShapeDtypeStruct((M, N), F32)]
    elif epilogue == "mul2a":
        extra_specs, out_shape, out_specs = [tile], [jax.ShapeDtypeStruct((M, N), BF16)], [tile]
    elif epilogue == "heads":
        per_tile = tn // ATTN_HD
        q_tiles = ATTN_HEADS // per_tile
        assert tn == 2 * ATTN_KV * ATTN_HD and N // tn in (q_tiles, q_tiles + 1)
        extra_specs = []
        out_shape = [jax.ShapeDtypeStruct((ATTN_HEADS, M, ATTN_HD), out_dtype)]
        out_specs = [pl.BlockSpec((per_tile, tm, ATTN_HD), lambda i, j, k: (jnp.minimum(j, q_tiles - 1), i, 0))]
        if N // tn > q_tiles:
            out_shape += [jax.ShapeDtypeStruct((ATTN_KV, M, ATTN_HD), out_dtype)] * 2
            out_specs += [pl.BlockSpec((ATTN_KV, tm, ATTN_HD), lambda i, j, k: (0, i, 0))] * 2
    else:
        raise ValueError(epilogue)
    n_extra = len(extra_specs)

    def body(a_ref, b_ref, *rest):
        ex, outs, acc_ref = rest[:n_extra], rest[n_extra:-1], rest[-1]
        k = pl.program_id(2)

        @pl.when(k == 0)
        def _():
            acc_ref[...] = jnp.zeros_like(acc_ref)

        a_tile = side_by_side(a_ref) if a_heads else a_ref[...]
        b_tile = side_by_side(b_ref) if b_heads else b_ref[...]
        acc_ref[...] += lax.dot_general(a_tile, b_tile, _DIMS[dims], preferred_element_type=F32)

        if epilogue == "heads":
            j = pl.program_id(1)

            @pl.when((k == nk - 1) & (j < q_tiles))
            def _():
                for g in range(per_tile):
                    outs[0][g] = acc_ref[:, g * ATTN_HD:(g + 1) * ATTN_HD].astype(out_dtype)

            if len(outs) > 1:
                @pl.when((k == nk - 1) & (j == q_tiles))
                def _():
                    for g in range(ATTN_KV):
                        outs[1][g] = acc_ref[:, g * ATTN_HD:(g + 1) * ATTN_HD].astype(out_dtype)
                        outs[2][g] = acc_ref[:, (ATTN_KV + g) * ATTN_HD:(ATTN_KV + g + 1) * ATTN_HD].astype(out_dtype)
            return

        @pl.when(k == nk - 1)
        def _():
            acc = acc_ref[...]
            if epilogue == "plain":
                outs[0][...] = acc.astype(out_dtype)
            elif epilogue == "relu2":
                act = jnp.maximum(acc, 0.0)
                outs[0][...] = act.astype(BF16)
                outs[1][...] = (act * act).astype(BF16)
            elif epilogue == "resgate":
                outs[0][...] = acc.astype(BF16)
                outs[1][...] = ex[0][...] + ex[1][...] * acc
            else:
                outs[0][...] = (acc * (2.0 * ex[0][...].astype(F32))).astype(BF16)

    res = _call(body, name=name, grid=(M // tm, N // tn, nk), out_shape=out_shape,
                in_specs=[a_spec, b_spec] + extra_specs, out_specs=out_specs,
                scratch_shapes=[pltpu.VMEM((tm, tn), F32)],
                sem=("parallel", "arbitrary" if epilogue == "heads" else "parallel", "arbitrary"),
                args=(a, b, *extras), comm=comm)
    return res[0] if len(res) == 1 else res


def _row_tile(T):
    return min(T, 256)


def _norm_mod_fwd(x, gain, sc, sh, name, comm=None):
    T, D = x.shape
    tr = _row_tile(T)

    def body(x_ref, g_ref, sc_ref, sh_ref, h_ref):
        xv = x_ref[...]
        r = lax.rsqrt(jnp.mean(xv * xv, axis=-1, keepdims=True) + EPS)
        hn = (xv * r) * g_ref[...]
        h_ref[...] = (hn * (1.0 + sc_ref[...]) + sh_ref[...]).astype(BF16)

    vec = pl.BlockSpec((1, D), lambda i: (0, 0))
    res = _call(body, name=name, grid=(T // tr,), out_shape=[jax.ShapeDtypeStruct((T, D), BF16)],
                in_specs=[pl.BlockSpec((tr, D), lambda i: (i, 0)), vec, vec, vec],
                out_specs=[pl.BlockSpec((tr, D), lambda i: (i, 0))], sem=("parallel",), args=(x, gain, sc, sh),
                comm=comm)
    return res[0] if comm is None else res


def _through_gate(dx, branch_ref, gate_ref, dbranch_ref, st_ref, row):
    dbranch_ref[...] = (dx * gate_ref[...]).astype(BF16)
    st_ref[row:row + 1, :] += jnp.sum(dx * branch_ref[...].astype(F32), axis=0, keepdims=True)


def _norm_mod_bwd(x, dh, dres, gain, sc, name, below=None):
    T, D = x.shape
    tr = _row_tile(T)

    def body(x_ref, dh_ref, dres_ref, g_ref, sc_ref, *rest):
        dx_ref, st_ref = rest[-3:-1] if below else rest[-2:]
        xv, dh_v, gain_v = x_ref[...], dh_ref[...].astype(F32), g_ref[...]
        r = lax.rsqrt(jnp.mean(xv * xv, axis=-1, keepdims=True) + EPS)
        xn = xv * r
        hn = xn * gain_v
        dhn = dh_v * (1.0 + sc_ref[...])
        dxn = dhn * gain_v
        dx = dres_ref[...] + r * (dxn - xn * jnp.mean(dxn * xn, axis=-1, keepdims=True))
        dx_ref[...] = dx

        @pl.when(pl.program_id(0) == 0)
        def _():
            st_ref[...] = jnp.zeros_like(st_ref)

        st_ref[0:1, :] += jnp.sum(dh_v, axis=0, keepdims=True)
        st_ref[1:2, :] += jnp.sum(dh_v * hn, axis=0, keepdims=True)
        st_ref[2:3, :] += jnp.sum(dhn * xn, axis=0, keepdims=True)
        if below:
            _through_gate(dx, rest[0], rest[1], rest[-1], st_ref, 3)

    vec = pl.BlockSpec((1, D), lambda i: (0, 0))
    blk = pl.BlockSpec((tr, D), lambda i: (i, 0))
    return pl.pallas_call(
        body, name=name, grid=(T // tr,),
        out_shape=[jax.ShapeDtypeStruct((T, D), F32), jax.ShapeDtypeStruct((8, D), F32)]
        + ([jax.ShapeDtypeStruct((T, D), BF16)] if below else []),
        in_specs=[blk, blk, blk, vec, vec] + ([blk, vec] if below else []),
        out_specs=[blk, pl.BlockSpec((8, D), lambda i: (0, 0))] + ([blk] if below else []),
        compiler_params=_params("arbitrary"),
    )(x, dh, dres, gain, sc, *(below or ()))


def _loss_head(y, target, below, name):
    T, D = y.shape
    tr = _row_tile(T)

    def body(y_ref, t_ref, b_ref, g_ref, dy_ref, l_ref, st_ref, db_ref):
        err = y_ref[...] - t_ref[...]
        dy = err * (1.0 / D)
        dy_ref[...] = dy

        @pl.when(pl.program_id(0) == 0)
        def _():
            l_ref[...] = jnp.zeros_like(l_ref)
            st_ref[...] = jnp.zeros_like(st_ref)

        part = jnp.sum(jnp.mean(err * err, axis=-1, keepdims=True), axis=0, keepdims=True)
        l_ref[...] += jnp.broadcast_to(0.5 * part, l_ref.shape)
        _through_gate(dy, b_ref, g_ref, db_ref, st_ref, 0)

    blk = pl.BlockSpec((tr, D), lambda i: (i, 0))
    return pl.pallas_call(
        body, name=name, grid=(T // tr,),
        out_shape=[jax.ShapeDtypeStruct((T, D), F32), jax.ShapeDtypeStruct((8, 128), F32),
                   jax.ShapeDtypeStruct((8, D), F32), jax.ShapeDtypeStruct((T, D), BF16)],
        in_specs=[blk, blk, blk, pl.BlockSpec((1, D), lambda i: (0, 0))],
        out_specs=[blk, pl.BlockSpec((8, 128), lambda i: (0, 0)), pl.BlockSpec((8, D), lambda i: (0, 0)), blk],
        compiler_params=_params("arbitrary"),
    )(y, target, *below)


def _silu(v):
    return v * jax.nn.sigmoid(v)


def _mod_fwd(c_all, mod_w, mod_b_cols, name):
    L, D, n = mod_w.shape
    tn = 512

    def body(c_ref, w_ref, b_ref, o_ref):
        cond = _silu(c_ref[...]).astype(BF16)
        o_ref[0] = jnp.dot(cond, w_ref[0].astype(BF16), preferred_element_type=F32) + b_ref[0]

    return pl.pallas_call(
        body, name=name, grid=(L, n // tn), out_shape=jax.ShapeDtypeStruct((L, N_DEV, n), F32),
        in_specs=[pl.BlockSpec((N_DEV, D), lambda l, j: (0, 0)), pl.BlockSpec((1, D, tn), lambda l, j: (l, 0, j)),
                  pl.BlockSpec((1, 1, tn), lambda l, j: (l, 0, j))],
        out_specs=pl.BlockSpec((1, N_DEV, tn), lambda l, j: (l, 0, j)),
        compiler_params=_params("parallel", "parallel"),
    )(c_all, mod_w, mod_b_cols)


def _adamw(g, w, m, v):
    m = ADAM_B1 * m + (1.0 - ADAM_B1) * g
    v = ADAM_B2 * v + (1.0 - ADAM_B2) * (g * g)
    m_hat = m / (1.0 - ADAM_B1 ** ADAM_STEP)
    v_hat = v / (1.0 - ADAM_B2 ** ADAM_STEP)
    delta = -ADAM_LR * (m_hat / (jnp.sqrt(v_hat) + ADAM_EPS) + ADAM_WD * w)
    return delta, m, v


def _mod_w_update(c_t, dmod_cols, w, m, v, name):
    L, D, n = w.shape
    tr = 256

    def body(c_ref, dm_ref, w_ref, m_ref, v_ref, g_ref, d_ref, nm_ref, nv_ref):
        cond = _silu(c_ref[...])
        dm = dm_ref[0]
        g = cond[:, 0:1] * dm[0:1, :]
        for b in range(1, N_DEV):
            g = g + cond[:, b:b + 1] * dm[b:b + 1, :]
        delta, nm, nv = _adamw(g, w_ref[0], m_ref[0], v_ref[0])
        g_ref[0], d_ref[0], nm_ref[0], nv_ref[0] = g, delta, nm, nv

    blk = pl.BlockSpec((1, tr, n), lambda l, i: (l, i, 0))
    return pl.pallas_call(
        body, name=name, grid=(L, D // tr), out_shape=[jax.ShapeDtypeStruct(w.shape, F32)] * 4,
        in_specs=[pl.BlockSpec((tr, N_DEV), lambda l, i: (i, 0)), pl.BlockSpec((1, N_DEV, n), lambda l, i: (l, 0, 0)),
                  blk, blk, blk],
        out_specs=[blk] * 4, compiler_params=_params("parallel", "parallel"),
    )(c_t, dmod_cols, w, m, v)


def _sum_adamw(parts, w, m, v, name):
    R, C = w.shape
    tr = min(R, 256 if C >= 1024 else 1024)
    assert R % tr == 0

    def body(p_ref, w_ref, m_ref, v_ref, g_ref, d_ref, nm_ref, nv_ref):
        g = p_ref[0].astype(F32)
        for s in range(1, N_DEV):
            g = g + p_ref[s].astype(F32)
        delta, nm, nv = _adamw(g, w_ref[...], m_ref[...], v_ref[...])
        g_ref[...], d_ref[...], nm_ref[...], nv_ref[...] = g, delta, nm, nv

    blk = pl.BlockSpec((tr, C), lambda i: (i, 0))
    return pl.pallas_call(
        body, name=name, grid=(R // tr,), out_shape=[jax.ShapeDtypeStruct((R, C), F32)] * 4,
        in_specs=[pl.BlockSpec((N_DEV, tr, C), lambda i: (0, i, 0)), blk, blk, blk], out_specs=[blk] * 4,
        compiler_params=_params("parallel"),
    )(parts, w, m, v)


def _sum_adamw_layers(parts, w, m, v, name):
    L, R, C = w.shape
    tr = min(R, 256 * 1024 // C)
    assert R % tr == 0 and len(parts) == L
    ni = R // tr

    def body(*refs):
        p_refs, (w_ref, m_ref, v_ref), outs = refs[:L], refs[L:L + 3], refs[L + 3:]
        for layer in range(L):
            @pl.when(pl.program_id(0) == layer)
            def _(p_ref=p_refs[layer]):
                g = p_ref[0].astype(F32)
                for s in range(1, N_DEV):
                    g = g + p_ref[s].astype(F32)
                delta, nm, nv = _adamw(g, w_ref[...], m_ref[...], v_ref[...])
                for o_ref, val in zip(outs, (g, delta, nm, nv)):
                    o_ref[...] = val

    def shares(layer):
        park = 0 if layer else ni - 1
        return pl.BlockSpec((N_DEV, tr, C), lambda l, i: (0, jnp.where(l == layer, i, park), 0))

    blk = pl.BlockSpec((None, tr, C), lambda l, i: (l, i, 0))
    return pl.pallas_call(
        body, name=name, grid=(L, ni), out_shape=[jax.ShapeDtypeStruct((L, R, C), F32)] * 4,
        in_specs=[shares(layer) for layer in range(L)] + [blk] * 3, out_specs=[blk] * 4,
        compiler_params=_params("arbitrary", "arbitrary"),
    )(*parts, w, m, v)


def _lower_bound_row1(l0, l1):
    mx = lax.stop_gradient(jnp.maximum(l0, l1))
    e0, e1 = jnp.exp(l0 - mx), jnp.exp(l1 - mx)
    p0, p1 = e0 / (e0 + e1), e1 / (e0 + e1)
    return (p0 + p1) - p0


def _lb_fwd(logits, name):
    def body(l_ref, o_ref):
        o_ref[...] = _lower_bound_row1(l_ref[0:1, :], l_ref[1:2, :])

    return pl.pallas_call(body, name=name, out_shape=jax.ShapeDtypeStruct((1, logits.shape[1]), F32))(logits)


def _lb_bwd(logits, dlb, name):
    def body(l_ref, d_ref, o_ref):
        _, vjp = jax.vjp(_lower_bound_row1, l_ref[0:1, :], l_ref[1:2, :])
        d0, d1 = vjp(d_ref[...])
        o_ref[0:1, :] = d0
        o_ref[1:2, :] = d1

    return pl.pallas_call(body, name=name, out_shape=jax.ShapeDtypeStruct(logits.shape, F32))(logits, dlb)


def _bdot(a, b, dims):
    return lax.dot_general(a.astype(BF16), b.astype(BF16), _DIMS[dims], preferred_element_type=F32)


def _rms(x, gain):
    r = lax.rsqrt(jnp.mean(x * x, axis=-1, keepdims=True) + EPS)
    xhat = x * r
    return xhat, r, xhat * gain


def _attn_band(n):
    qi = lax.broadcasted_iota(jnp.int32, (ATTN_BLOCK, 2 * ATTN_BLOCK), 0)
    ki = lax.broadcasted_iota(jnp.int32, (ATTN_BLOCK, 2 * ATTN_BLOCK), 1)
    dist = qi + ATTN_BLOCK - ki
    first_key = jnp.where(n > 0, 0, ATTN_BLOCK)
    valid = (dist >= 0) & (dist < ATTN_BLOCK) & (ki >= first_key)
    return valid, jnp.abs(dist).astype(F32)


def _attn_head_probs(qn, kn_b, valid, absdist, slope, sink):
    s = lax.dot_general(qn.astype(BF16), kn_b, _DIMS["nt"], preferred_element_type=F32) * ATTN_SCALE
    s = jnp.where(valid, s - slope * absdist, NEG_BIG)
    mx = jnp.maximum(jnp.max(s, axis=-1, keepdims=True), sink)
    e = jnp.exp(s - mx)
    es = jnp.exp(sink - mx)
    inv = 1.0 / (jnp.sum(e, axis=-1, keepdims=True) + es)
    return e * inv, es * inv


def _attn_specs(T):
    nb = T // ATTN_BLOCK
    qspec = pl.BlockSpec((ATTN_GROUP, ATTN_BLOCK, ATTN_HD), lambda h, n: (h, n, 0))
    prev = pl.BlockSpec((1, ATTN_BLOCK, ATTN_HD), lambda h, n: (h, jnp.maximum(n - 1, 0), 0))
    cur = pl.BlockSpec((1, ATTN_BLOCK, ATTN_HD), lambda h, n: (h, n, 0))
    gain = pl.BlockSpec((1, ATTN_HD), lambda h, n: (0, 0))
    scalars = pl.BlockSpec(memory_space=pltpu.SMEM)
    return nb, qspec, prev, cur, gain, scalars


def _attn_fwd(qh, kh, vh, qg, kg, slopes, sinks, name, comm=None):
    T = qh.shape[1]
    nb, qspec, prev, cur, gain, scalars = _attn_specs(T)

    def body(q_ref, kp_ref, kc_ref, vp_ref, vc_ref, qg_ref, kg_ref, sl_ref, sk_ref, o_ref):
        h, n = pl.program_id(0), pl.program_id(1)
        valid, absdist = _attn_band(n)
        _, _, kn = _rms(jnp.concatenate([kp_ref[0], kc_ref[0]], axis=0), kg_ref[...])
        kn_b = kn.astype(BF16)
        v_b = jnp.concatenate([vp_ref[0], vc_ref[0]], axis=0).astype(BF16)
        for g in range(ATTN_GROUP):
            head = h * ATTN_GROUP + g
            _, _, qn = _rms(q_ref[g], qg_ref[...])
            p, _ = _attn_head_probs(qn, kn_b, valid, absdist, sl_ref[head], sk_ref[head])
            o_ref[g] = jnp.dot(p.astype(BF16), v_b, preferred_element_type=F32).astype(BF16)

    return _call(body, name=name, grid=(ATTN_KV, nb), out_shape=[jax.ShapeDtypeStruct(qh.shape, BF16)],
                 in_specs=[qspec, prev, cur, prev, cur, gain, gain, scalars, scalars], out_specs=[qspec],
                 sem=("parallel", "parallel"), args=(qh, kh, kh, vh, vh, qg, kg, slopes, sinks), comm=comm)


def _rms_bwd(dy, xhat, r, gain):
    dxh = dy * gain
    dx = r * (dxh - xhat * jnp.mean(dxh * xhat, axis=-1, keepdims=True))
    return dx, jnp.sum(dy * xhat, axis=0, keepdims=True)


def _attn_bwd(qh, kh, vh, doh, qg, kg, slope_col, sink_col, name, comm=None):
    T = qh.shape[1]
    nb = T // ATTN_BLOCK
    rows = ATTN_GROUP * ATTN_BLOCK

    def body(q_ref, kp_ref, kc_ref, vp_ref, vc_ref, do_ref, qg_ref, kg_ref, sl_ref, sk_ref,
             dq_ref, dk_ref, dv_ref, dqg_ref, dkg_ref, dsk_ref, carry_ref, sk_acc):
        h, step = pl.program_id(0), pl.program_id(1)
        n = nb - 1 - step
        qg_v, kg_v = qg_ref[...], kg_ref[...]
        qhat, rq, qn = _rms(q_ref[...].reshape(rows, ATTN_HD), qg_v)
        khat, rk, kn = _rms(jnp.concatenate([kp_ref[0], kc_ref[0]], axis=0), kg_v)
        s = _bdot(qn, kn, "nt") * ATTN_SCALE
        qi = lax.broadcasted_iota(jnp.int32, s.shape, 0) & (ATTN_BLOCK - 1)
        ki = lax.broadcasted_iota(jnp.int32, s.shape, 1)
        dist = qi + ATTN_BLOCK - ki
        first_key = jnp.where(n > 0, 0, ATTN_BLOCK)
        valid = (dist >= 0) & (dist < ATTN_BLOCK) & (ki >= first_key)
        s = jnp.where(valid, s - sl_ref[0] * jnp.abs(dist).astype(F32), NEG_BIG)
        sink = sk_ref[0]
        mx = jnp.maximum(jnp.max(s, axis=-1, keepdims=True), sink)
        e = jnp.exp(s - mx)
        es = jnp.exp(sink - mx)
        den = jnp.sum(e, axis=-1, keepdims=True) + es
        p, ps = e / den, es / den
        v = jnp.concatenate([vp_ref[0], vc_ref[0]], axis=0)
        do = do_ref[...].reshape(rows, ATTN_HD)
        dp = _bdot(do, v, "nt")
        delta = jnp.sum(p * dp, axis=-1, keepdims=True)
        ds = p * (dp - delta)
        dqn = _bdot(ds, kn, "nn") * ATTN_SCALE
        dkn = _bdot(ds, qn, "tn") * ATTN_SCALE
        dv = _bdot(p, do, "tn")
        dq, dqg = _rms_bwd(dqn, qhat, rq, qg_v)
        dk, dkg = _rms_bwd(dkn, khat, rk, kg_v)
        dq_ref[...] = dq.reshape(ATTN_GROUP, ATTN_BLOCK, ATTN_HD).astype(BF16)

        @pl.when((h == 0) & (step == 0))
        def _():
            dqg_ref[...] = jnp.zeros_like(dqg_ref)
            dkg_ref[...] = jnp.zeros_like(dkg_ref)

        dqg_ref[0:1, :] += dqg
        dkg_ref[0:1, :] += dkg

        @pl.when(step == 0)
        def _():
            carry_ref[...] = jnp.zeros_like(carry_ref)
            sk_acc[...] = jnp.zeros_like(sk_acc)

        dk_ref[0] = (dk[ATTN_BLOCK:, :] + carry_ref[0]).astype(BF16)
        dv_ref[0] = (dv[ATTN_BLOCK:, :] + carry_ref[1]).astype(BF16)
        carry_ref[0] = dk[:ATTN_BLOCK, :]
        carry_ref[1] = dv[:ATTN_BLOCK, :]
        sk_acc[...] += -ps * delta

        @pl.when(step == nb - 1)
        def _():
            for g in range(ATTN_GROUP):
                tot = jnp.sum(sk_acc[g * ATTN_BLOCK:(g + 1) * ATTN_BLOCK, :], axis=0, keepdims=True)
                dsk_ref[g:g + 1, :] = jnp.broadcast_to(tot, (1, 128))

    qspec = pl.BlockSpec((ATTN_GROUP, ATTN_BLOCK, ATTN_HD), lambda h, s: (h, nb - 1 - s, 0))
    prev = pl.BlockSpec((1, ATTN_BLOCK, ATTN_HD), lambda h, s: (h, jnp.maximum(nb - 2 - s, 0), 0))
    cur = pl.BlockSpec((1, ATTN_BLOCK, ATTN_HD), lambda h, s: (h, nb - 1 - s, 0))
    gain = pl.BlockSpec((1, ATTN_HD), lambda h, s: (0, 0))
    col = pl.BlockSpec((1, rows, 1), lambda h, s: (h, 0, 0))
    acc = pl.BlockSpec((8, ATTN_HD), lambda h, s: (0, 0))
    return _call(
        body, name=name, grid=(ATTN_KV, nb),
        out_shape=[jax.ShapeDtypeStruct(qh.shape, BF16), jax.ShapeDtypeStruct(kh.shape, BF16),
                   jax.ShapeDtypeStruct(kh.shape, BF16), jax.ShapeDtypeStruct((8, ATTN_HD), F32),
                   jax.ShapeDtypeStruct((8, ATTN_HD), F32), jax.ShapeDtypeStruct((ATTN_HEADS, 128), F32)],
        in_specs=[qspec, prev, cur, prev, cur, qspec, gain, gain, col, col],
        out_specs=[qspec, cur, cur, acc, acc, pl.BlockSpec((ATTN_GROUP, 128), lambda h, s: (h, 0))],
        scratch_shapes=[pltpu.VMEM((2, ATTN_BLOCK, ATTN_HD), F32), pltpu.VMEM((rows, 1), F32)],
        sem=("arbitrary", "arbitrary"), args=(qh, kh, kh, vh, vh, doh, qg, kg, slope_col, sink_col), comm=comm)


@functools.partial(jax.custom_vjp, nondiff_argnums=(2,))
def _mm(a, b, dims):
    return _bdot(a, b, dims)


def _mm_fwd(a, b, dims):
    return _bdot(a, b, dims), (a, b)


def _mm_bwd(dims, res, ct):
    a, b = res
    if dims == "nn":
        return _bdot(ct, b, "nt"), _bdot(a, ct, "tn")
    if dims == "nt":
        return _bdot(ct, b, "nn"), _bdot(ct, a, "tn")
    return _bdot(b, ct, "nt"), _bdot(a, ct, "nn")


_mm.defvjp(_mm_fwd, _mm_bwd)


def _same_chunk_mask(rows, upper):
    ri = lax.broadcasted_iota(jnp.int32, (rows, rows), 0)
    ci = lax.broadcasted_iota(jnp.int32, (rows, rows), 1)
    same = (ri >> HGRN_CHUNK_SHIFT) == (ci >> HGRN_CHUNK_SHIFT)
    return same & ((ri <= ci) if upper else (ri >= ci))


@jax.custom_vjp
def _chunk_cumsum(x, tri, tri_t):
    hi = x.astype(BF16)
    rest = x - hi.astype(F32)
    mid = rest.astype(BF16)
    lo = (rest - mid.astype(F32)).astype(BF16)
    out = jnp.dot(tri, jnp.concatenate([hi, mid, lo], axis=1), preferred_element_type=F32)
    w = x.shape[1]
    return out[:, :w] + out[:, w:2 * w] + out[:, 2 * w:]


def _chunk_cumsum_fwd(x, tri, tri_t):
    return _chunk_cumsum(x, tri, tri_t), (tri, tri_t)


def _chunk_cumsum_bwd(res, ct):
    tri, tri_t = res
    return _chunk_cumsum(ct, tri_t, tri), jnp.zeros_like(tri), jnp.zeros_like(tri_t)


_chunk_cumsum.defvjp(_chunk_cumsum_fwd, _chunk_cumsum_bwd)


def _hgrn_masks(rows):
    nc = rows // HGRN_CHUNK
    lower = _same_chunk_mask(rows, False)
    chunk_of_row = lax.broadcasted_iota(jnp.int32, (rows, HGRN_DK), 0) >> HGRN_CHUNK_SHIFT
    row_in_chunk = lax.broadcasted_iota(jnp.int32, (nc, HGRN_CHUNK, HGRN_DK), 1)
    return dict(lower=lower, tri=lower.astype(BF16), tri_t=_same_chunk_mask(rows, True).astype(BF16),
                in_chunk=[chunk_of_row == c for c in range(nc)],
                mid_row=row_in_chunk == HGRN_CHUNK // 2 - 1, last_row=row_in_chunk == HGRN_CHUNK - 1)


def _hgrn_block(masks, st, qr, fr, v, gr, lb, og):
    rows = qr.shape[0]
    nc = rows // HGRN_CHUNK
    per_chunk = lambda m: m.reshape(nc, HGRN_CHUNK, HGRN_DK)
    flat = lambda m: m.reshape(rows, HGRN_DK)
    by_chunk = lambda m: jnp.concatenate([jnp.where(masks["in_chunk"][c], m, 0.0) for c in range(nc)], axis=1)

    forget = lb + (1.0 - lb) * jax.nn.sigmoid(fr)
    k = 1.0 - forget
    b = _chunk_cumsum(jnp.log(forget), masks["tri"], masks["tri_t"])
    b3 = per_chunk(b)
    piv = jnp.sum(jnp.where(masks["mid_row"], b3, 0.0), axis=1, keepdims=True)
    b_last = jnp.sum(jnp.where(masks["last_row"], b3, 0.0), axis=1, keepdims=True)
    q = _silu(qr) * HGRN_SCALE
    a = _mm(q * flat(jnp.exp(b3 - piv)), k * flat(jnp.exp(piv - b3)), "nt")
    o = _mm(jnp.where(masks["lower"], a, 0.0), v, "nn")
    updates = _mm(v, by_chunk(k * flat(jnp.exp(b_last - b3))), "tn")
    decay = jnp.exp(b_last)
    before = []
    for c in range(nc):
        before.append(st)
        st = st * decay[c] + updates[:, c * HGRN_DK:(c + 1) * HGRN_DK]
    o = o + _mm(by_chunk(q * jnp.exp(b)), jnp.concatenate(before, axis=1), "nt")
    y = (o * lax.rsqrt(jnp.mean(o * o, axis=-1, keepdims=True) + EPS)) * og * _silu(gr)
    return y, st


def _hgrn_tile(T):
    return min(T, 256)


HGRN_HEADS_PER_STEP = 4
HGRN_GROUPS = HGRN_HEADS // HGRN_HEADS_PER_STEP


def _hgrn_fwd(proj, lb, og, name, comm=None):
    T = proj.shape[0]
    tb = _hgrn_tile(T)
    hp, wide = HGRN_HEADS_PER_STEP, HGRN_HEADS_PER_STEP * HGRN_DK

    def body(q_ref, f_ref, v_ref, g_ref, lb_ref, og_ref, o_ref, s_ref, st_ref):
        @pl.when(pl.program_id(1) == 0)
        def _():
            st_ref[...] = jnp.zeros_like(st_ref)

        masks = _hgrn_masks(tb)
        for j in range(hp):
            ln = slice(j * HGRN_DK, (j + 1) * HGRN_DK)
            st = st_ref[j]
            s_ref[j, 0] = st
            y, st_ref[j] = _hgrn_block(masks, st, q_ref[:, ln], f_ref[:, ln], v_ref[:, ln], g_ref[:, ln], lb_ref[j],
                                       og_ref[j])
            o_ref[:, ln] = y.astype(BF16)

    part = lambda p: pl.BlockSpec((tb, wide), lambda h, t: (t, p * HGRN_GROUPS + h))
    vec = pl.BlockSpec((hp, 1, HGRN_DK), lambda h, t: (h, 0, 0))
    return _call(
        body, name=name, grid=(HGRN_GROUPS, T // tb),
        out_shape=[jax.ShapeDtypeStruct((T, D_MODEL), BF16),
                   jax.ShapeDtypeStruct((HGRN_HEADS, T // tb, HGRN_DK, HGRN_DK), F32)],
        in_specs=[part(0), part(1), part(2), part(3), vec, vec],
        out_specs=[pl.BlockSpec((tb, wide), lambda h, t: (t, h)),
                   pl.BlockSpec((hp, 1, HGRN_DK, HGRN_DK), lambda h, t: (h, t, 0, 0))],
        scratch_shapes=[pltpu.VMEM((hp, HGRN_DK, HGRN_DK), F32)], sem=("parallel", "arbitrary"),
        args=(proj, proj, proj, proj, lb, og), comm=comm)


def _hgrn_bwd(proj, states, do, lb, og, name, comm=None):
    T = proj.shape[0]
    tb = _hgrn_tile(T)
    nt, hp, wide = T // tb, HGRN_HEADS_PER_STEP, HGRN_HEADS_PER_STEP * HGRN_DK

    def body(q_ref, f_ref, v_ref, g_ref, s_ref, do_ref, lb_ref, og_ref, dp_ref, dlb_ref, dog_ref, dst_ref):
        @pl.when(pl.program_id(1) == 0)
        def _():
            dst_ref[...] = jnp.zeros_like(dst_ref)
            dlb_ref[...] = jnp.zeros_like(dlb_ref)
            dog_ref[...] = jnp.zeros_like(dog_ref)

        block = functools.partial(_hgrn_block, _hgrn_masks(tb))
        for j in range(hp):
            ln = slice(j * HGRN_DK, (j + 1) * HGRN_DK)
            _, vjp = jax.vjp(block, s_ref[j, 0], q_ref[:, ln], f_ref[:, ln], v_ref[:, ln], g_ref[:, ln],
                             lb_ref[j], og_ref[j])
            dst_ref[j], dq, df, dv, dg, dlb, dog = vjp((do_ref[:, ln], dst_ref[j]))
            for p, part_grad in enumerate((dq, df, dv, dg)):
                dp_ref[p, :, ln] = part_grad.astype(BF16)
            dlb_ref[j] += dlb
            dog_ref[j] += dog

    part = lambda p: pl.BlockSpec((tb, wide), lambda h, t: (nt - 1 - t, p * HGRN_GROUPS + h))
    vec = pl.BlockSpec((hp, 1, HGRN_DK), lambda h, t: (h, 0, 0))
    head = pl.BlockSpec((tb, wide), lambda h, t: (nt - 1 - t, h))
    return _call(
        body, name=name, grid=(HGRN_GROUPS, nt),
        out_shape=[jax.ShapeDtypeStruct((4, T, D_MODEL), BF16)] + [jax.ShapeDtypeStruct((HGRN_HEADS, 1, HGRN_DK), F32)] * 2,
        in_specs=[part(0), part(1), part(2), part(3),
                  pl.BlockSpec((hp, 1, HGRN_DK, HGRN_DK), lambda h, t: (h, nt - 1 - t, 0, 0)), head, vec, vec],
        out_specs=[pl.BlockSpec((4, tb, wide), lambda h, t: (0, nt - 1 - t, h)), vec, vec],
        scratch_shapes=[pltpu.VMEM((hp, HGRN_DK, HGRN_DK), F32)], sem=("parallel", "arbitrary"),
        args=(proj, proj, proj, proj, states, do, lb, og), comm=comm)


def _cols_to_blocks(g, n8):
    K = g.shape[0]
    return g.reshape(K, N_DEV, n8).transpose(1, 0, 2)


def _blocks_to_cols(wg):
    _, K, n8 = wg.shape
    return wg.transpose(1, 0, 2).reshape(K, N_DEV * n8)


def _pack(parts):
    flat = []
    for p in parts:
        v = p.reshape(-1)
        flat.append(jnp.pad(v, (0, (-v.shape[0]) % 1024)))
    return jnp.concatenate(flat).reshape(-1, 128)


def _unpack(packed, like):
    flat, out, off = packed.reshape(-1), [], 0
    for p in like:
        size = math.prod(p.shape)
        out.append(flat[off:off + size].reshape(p.shape))
        off += size + (-size) % 1024
    return out


def _heads_major(a, heads):
    T = a.shape[0]
    return a.reshape(T, heads, ATTN_HD).transpose(1, 0, 2)


def _heads_minor(a):
    heads, T, _ = a.shape
    return a.transpose(1, 0, 2).reshape(T, heads * ATTN_HD)


def kernel(x, c, mod_w, mod_b, norm_mix, norm_mlp, attn_w_in, attn_w_out, attn_q_gain, attn_k_gain, attn_sinks, hgrn_w_in, hgrn_w_out, hgrn_o_gain, hgrn_lb_logits, mlp_w1, mlp_w2, loss_target, m_mod_w, m_mod_b, m_norm_mix, m_norm_mlp, m_attn_w_in, m_attn_w_out, m_attn_q_gain, m_attn_k_gain, m_attn_sinks, m_hgrn_w_in, m_hgrn_w_out, m_hgrn_o_gain, m_hgrn_lb_logits, m_mlp_w1, m_mlp_w2, v_mod_w, v_mod_b, v_norm_mix, v_norm_mlp, v_attn_w_in, v_attn_w_out, v_attn_q_gain, v_attn_k_gain, v_attn_sinks, v_hgrn_w_in, v_hgrn_w_out, v_hgrn_o_gain, v_hgrn_lb_logits, v_mlp_w1, v_mlp_w2):
    T = x.shape[1]
    me = 4 * lax.axis_index("x") + 2 * lax.axis_index("y") + lax.axis_index("c")
    x0, target = x[0], loss_target[0]
    n_mod = mod_w.shape[2]

    shards = [attn_w_in[0], attn_w_out[0], hgrn_w_in[0], hgrn_w_out[0], mlp_w1[0], mlp_w1[1], mlp_w2[0], mlp_w2[1]]
    sb = [s.astype(BF16) for s in shards]
    gather = lambda *items: _Exchange([("gather_by_chip", a, axis) for a, axis in items])

    c_all = _exchange([("gather", c.reshape(16, 128), None)], vmem=True, name="gather_c")[0].reshape(N_DEV, D_MODEL)
    mod_b_cols = lax.dynamic_slice_in_dim(mod_b, me * n_mod, n_mod, axis=1).reshape(2, 1, n_mod)
    mod_cols = _mod_fwd(c_all, mod_w, mod_b_cols, "mod_fwd")
    mod_all = _exchange([("gather", mod_cols.reshape(-1, 128), None)], vmem=True, name="gather_mod")[0]
    mod_all = mod_all.reshape(N_DEV, 2, N_DEV, n_mod)
    mod_mine = lax.dynamic_index_in_dim(mod_all, me, axis=2, keepdims=False)
    mod_mine = mod_mine.transpose(1, 0, 2).reshape(2, N_MOD, 1, D_MODEL)

    lb = _lb_fwd(hgrn_lb_logits, "lb_fwd").reshape(HGRN_HEADS, 1, HGRN_DK)
    og = hgrn_o_gain.reshape(HGRN_HEADS, 1, HGRN_DK)
    slopes = jnp.exp2(-8.0 * jnp.arange(1, ATTN_HEADS + 1, dtype=F32) / ATTN_HEADS)
    sinks = attn_sinks[0]
    per_row = lambda vals: jnp.repeat(vals.reshape(ATTN_KV, ATTN_GROUP), ATTN_BLOCK, axis=1).reshape(
        ATTN_KV, ATTN_GROUP * ATTN_BLOCK, 1)
    slope_col, sink_col = per_row(slopes), per_row(sinks)

    saved = []
    xi = x0
    w_mlp1, w_mlp2 = [None, None], [None, None]
    for i in range(2):
        sh1, sc1, g1, sh2, sc2, g2 = [mod_mine[i, j] for j in range(N_MOD)]
        if i == 0:
            h, w_attn_in = _norm_mod_fwd(xi, norm_mix[i:i + 1], sc1, sh1, "norm_mix_fwd0", comm=gather((sb[0], None)))
            w_attn_in = _blocks_to_cols(w_attn_in)
            qh, kh, vh, w_attn_out = _matmul(h, w_attn_in, dims="nn", tm=1024, tn=512, tk=2048, name="attn_in_fwd",
                                             epilogue="heads", comm=gather((sb[1], 0)))
            o, w_mlp1[0] = _attn_fwd(qh, kh, vh, attn_q_gain, attn_k_gain, slopes, sinks, "attn_fwd",
                                     comm=gather((sb[4], 1)))
            mix = (qh, kh, vh)
            w_out = w_attn_out
        else:
            h = _norm_mod_fwd(xi, norm_mix[i:i + 1], sc1, sh1, "norm_mix_fwd1")
            proj, w_hgrn_out = _matmul(h, w_hgrn_in, dims="nn", tm=1024, tn=1024, tk=2048, name="hgrn_in_fwd",
                                       comm=gather((sb[3], 0)))
            o, states, w_mlp1[1] = _hgrn_fwd(proj, lb, og, "hgrn_fwd", comm=gather((sb[5], 1)))
            mix = (proj, states)
            w_out = w_hgrn_out
        y, x1 = _matmul(o, w_out, dims="nn", tm=1024, tn=1024, tk=2048, name=f"mix_out_fwd{i}", epilogue="resgate",
                        extras=(xi, g1), a_heads=i == 0)
        h2 = _norm_mod_fwd(x1, norm_mlp[i:i + 1], sc2, sh2, f"norm_mlp_fwd{i}")
        if i == 0:
            act, act2, w_mlp2[0] = _matmul(h2, w_mlp1[0], dims="nn", tm=1024, tn=1024, tk=2048, name="mlp1_fwd0",
                                           epilogue="relu2", comm=gather((sb[6], 0)))
            z, x2, w_hgrn_in = _matmul(act2, w_mlp2[0], dims="nn", tm=1024, tn=1024, tk=2048, name="mlp2_fwd0",
                                       epilogue="resgate", extras=(x1, g2), comm=gather((sb[2], 1)))
        else:
            act, act2, w_mlp2[1] = _matmul(h2, w_mlp1[1], dims="nn", tm=1024, tn=1024, tk=2048, name="mlp1_fwd1",
                                           epilogue="relu2", comm=gather((sb[7], 0)))
            z, x2 = _matmul(act2, w_mlp2[1], dims="nn", tm=1024, tn=1024, tk=2048, name="mlp2_fwd1", epilogue="resgate",
                            extras=(x1, g2))
        saved.append((xi, h, o, y, x1, h2, act, act2, z, mix))
        xi = x2

    dx, loss_tile, st_g2, dz = _loss_head(xi, target, (saved[1][8], mod_mine[1, 5]), "loss_head")
    loss = lax.psum(loss_tile[0, 0], ("x", "y", "c"))

    scatter = lambda *items: _Exchange([("scatter", a, axis) for a, axis in items])
    shares, dmods, dnorm_mix, dnorm_mlp = {}, [None, None], [None, None], [None, None]
    wgrad = lambda a, b, name, tn=1024: _matmul(a, b, dims="tn", tm=2048, tn=tn, tk=1024, name=name, out_dtype=BF16)
    for i in (1, 0):
        sh1, sc1, g1, sh2, sc2, g2 = [mod_mine[i, j] for j in range(N_MOD)]
        xin, h, o, y, x1, h2, act, act2, z, mix = saved[i]
        if i == 1:
            dpre = _matmul(dz, w_mlp2[1], dims="nt", tm=1024, tn=1024, tk=2048, name="mlp2_bwd1", epilogue="mul2a",
                           extras=(act,))
        else:
            dpre, shares["hgrn_w_in"] = _matmul(dz, w_mlp2[0], dims="nt", tm=1024, tn=1024, tk=2048, name="mlp2_bwd0",
                                                epilogue="mul2a", extras=(act,), comm=scatter((g_hgrn_in, 1)))
        g_mlp2 = wgrad(act2, dz, f"mlp2_wgrad{i}")
        dh2 = _matmul(dpre, w_mlp1[i], dims="nt", tm=1024, tn=1024, tk=2048, name=f"mlp1_bwd{i}", out_dtype=BF16)
        g_mlp1 = wgrad(h2, dpre, f"mlp1_wgrad{i}")
        dx1, st_mlp, dy = _norm_mod_bwd(x1, dh2, dx, norm_mlp[i:i + 1], sc2, f"norm_mlp_bwd{i}", below=(y, g1))
        w_out = w_attn_out if i == 0 else w_hgrn_out
        if i == 0:
            qh, kh, vh = mix
            doh = _matmul(dy, w_out, dims="nt", tm=1024, tn=512, tk=2048, name="mix_out_bwd0", epilogue="heads",
                          out_dtype=BF16)
            g_out = _matmul(o, dy, dims="tn", tm=1024, tn=1024, tk=1024, name="mix_out_wgrad0", out_dtype=BF16,
                            a_heads=True)
            dqh, dkh, dvh, dqg, dkg, dsk, shares["mlp_w2_0"], shares["mlp_w1_0"], shares["attn_w_out"] = _attn_bwd(
                qh, kh, vh, doh, attn_q_gain, attn_k_gain, slope_col, sink_col, "attn_bwd",
                comm=scatter((g_mlp2, 0), (g_mlp1, 1), (g_out, 0)))
            dproj = jnp.concatenate([dqh, dkh, dvh], axis=0)
            d_q_gain, d_k_gain, d_sinks = dqg[0:1], dkg[0:1], dsk[:, 0].reshape(1, ATTN_HEADS)
            g_attn_in = _matmul(h, dproj, dims="tn", tm=1024, tn=640, tk=1024, name="mix_in_wgrad0", out_dtype=BF16,
                                b_heads=True)
            g_attn_in = _cols_to_blocks(g_attn_in, attn_w_in.shape[2])
            dh, shares["attn_w_in"] = _matmul(dproj, w_attn_in, dims="nt", tm=1024, tn=1024, tk=1280, out_dtype=BF16,
                                              name="mix_in_bwd0", comm=scatter((g_attn_in, None)), a_heads=True)
        else:
            do = _matmul(dy, w_out, dims="nt", tm=1024, tn=1024, tk=2048, name="mix_out_bwd1")
            g_out = wgrad(o, dy, "mix_out_wgrad1")
            proj, states = mix
            dproj, dlb, d_o_gain, shares["mlp_w2_1"], shares["mlp_w1_1"], shares["hgrn_w_out"] = _hgrn_bwd(
                proj, states, do, lb, og, "hgrn_bwd", comm=scatter((g_mlp2, 0), (g_mlp1, 1), (g_out, 0)))
            d_lb_logits = _lb_bwd(hgrn_lb_logits, dlb.reshape(1, D_MODEL), "lb_bwd")
            dh = _matmul(dproj, w_hgrn_in, dims="nt", tm=1024, tn=1024, tk=2048, name="mix_in_bwd1", out_dtype=BF16)
            g_hgrn_in = wgrad(h, dproj, "mix_in_wgrad1")
        d_gate2 = st_g2[0:1] if i == 1 else st_mix_above[3:4]
        if i == 1:
            dx, st_mix, dz = _norm_mod_bwd(xin, dh, dx1, norm_mix[i:i + 1], sc1, "norm_mix_bwd1",
                                           below=(saved[0][8], mod_mine[0, 5]))
            st_mix_above = st_mix
        else:
            dx, st_mix = _norm_mod_bwd(xin, dh, dx1, norm_mix[i:i + 1], sc1, "norm_mix_bwd0")
        dmods[i] = jnp.concatenate([st_mix[0:1], st_mix[1:2], st_mlp[3:4], st_mlp[0:1], st_mlp[1:2], d_gate2], axis=1)
        dnorm_mix[i], dnorm_mlp[i] = st_mix[2:3], st_mlp[2:3]

    single = {"attn_w_in": (attn_w_in, m_attn_w_in, v_attn_w_in), "attn_w_out": (attn_w_out, m_attn_w_out, v_attn_w_out),
              "hgrn_w_in": (hgrn_w_in, m_hgrn_w_in, v_hgrn_w_in), "hgrn_w_out": (hgrn_w_out, m_hgrn_w_out, v_hgrn_w_out)}
    big = {nm: _sum_adamw(shares[nm], w[0], m[0], v[0], f"adamw_{nm}") for nm, (w, m, v) in single.items()}
    big["mlp_w1"] = _sum_adamw_layers([shares["mlp_w1_0"], shares["mlp_w1_1"]], mlp_w1, m_mlp_w1, v_mlp_w1, "adamw_mlp_w1")
    big["mlp_w2"] = _sum_adamw_layers([shares["mlp_w2_0"], shares["mlp_w2_1"]], mlp_w2, m_mlp_w2, v_mlp_w2, "adamw_mlp_w2")

    small_w = [mod_b, norm_mix, norm_mlp, attn_q_gain, attn_k_gain, attn_sinks, hgrn_o_gain, hgrn_lb_logits]
    small_m = [m_mod_b, m_norm_mix, m_norm_mlp, m_attn_q_gain, m_attn_k_gain, m_attn_sinks, m_hgrn_o_gain, m_hgrn_lb_logits]
    small_v = [v_mod_b, v_norm_mix, v_norm_mlp, v_attn_q_gain, v_attn_k_gain, v_attn_sinks, v_hgrn_o_gain, v_hgrn_lb_logits]
    small_g = [jnp.concatenate(dmods, axis=0), jnp.concatenate(dnorm_mix, axis=0), jnp.concatenate(dnorm_mlp, axis=0),
               d_q_gain, d_k_gain, d_sinks, d_o_gain.reshape(hgrn_o_gain.shape), d_lb_logits]
    packed_g = _pack(small_g)
    pad_rows = (-packed_g.shape[0]) % 8
    pad8 = lambda a: jnp.pad(a, ((0, pad_rows), (0, 0)))
    all_small = _exchange([("gather", pad8(packed_g), None)], vmem=True, name="gather_small_grads")[0]
    sg, sd, sm, sv = _sum_adamw(all_small, pad8(_pack(small_w)), pad8(_pack(small_m)), pad8(_pack(small_v)),
                                "adamw_small")
    small = [_unpack(t, small_w) for t in (sg, sd, sm, sv)]

    n_modb = N_MOD * D_MODEL
    dmod_all = all_small[:, :2 * n_modb // 128, :].reshape(N_DEV, 2, n_modb)
    dmod_cols = lax.dynamic_slice_in_dim(dmod_all, me * n_mod, n_mod, axis=2).transpose(1, 0, 2)
    modw = _mod_w_update(c_all.T, dmod_cols, mod_w, m_mod_w, v_mod_w, "adamw_mod_w")

    def leaf(k):
        one = lambda a: big[a][k][None]
        s = small[k]
        return [modw[k], s[0], s[1], s[2], one("attn_w_in"), one("attn_w_out"), s[3], s[4], s[5], one("hgrn_w_in"),
                one("hgrn_w_out"), s[6], s[7], big["mlp_w1"][k], big["mlp_w2"][k]]

    return (loss, dx[None], *leaf(0), *leaf(1), *leaf(2), *leaf(3))
```

```python
import functools
import math

import jax
import jax.numpy as jnp
from jax import lax
from jax.experimental import pallas as pl
from jax.experimental.pallas import tpu as pltpu

F32, BF16 = jnp.float32, jnp.bfloat16
N_DEV = 8
D_MODEL = 2048
N_MOD = 6
EPS = 1e-6
ATTN_HD, ATTN_HEADS, ATTN_KV, ATTN_GROUP, ATTN_BLOCK = 64, 32, 4, 8, 128
ATTN_SCALE = 1.0 / math.sqrt(ATTN_HD)
HGRN_HEADS, HGRN_DK, HGRN_CHUNK = 16, 128, 64
HGRN_SCALE = 1.0 / math.sqrt(HGRN_DK)
HGRN_CHUNK_SHIFT = HGRN_CHUNK.bit_length() - 1
assert 1 << HGRN_CHUNK_SHIFT == HGRN_CHUNK
D_FF = 4 * D_MODEL
ADAM_LR, ADAM_B1, ADAM_B2, ADAM_EPS, ADAM_WD, ADAM_STEP = 0.001, 0.9, 0.999, 1e-08, 0.01, 10
NEG_BIG = -1e30
VMEM_LIMIT = 56 * 1024 * 1024
MESH_ID = pl.DeviceIdType.MESH


def _params(*sem):
    return pltpu.CompilerParams(dimension_semantics=sem, vmem_limit_bytes=VMEM_LIMIT)


class _Exchange:
    def __init__(self, items):
        self.items = items
        self.n = len(items)
        self.out_shape = []
        for kind, a, axis in items:
            assert kind in ("gather", "gather_by_chip", "scatter")
            if kind != "scatter":
                shape = (N_DEV,) + a.shape if axis is None else tuple(
                    d * N_DEV if i == axis else d for i, d in enumerate(a.shape))
            else:
                shape = a.shape if axis is None else (N_DEV,) + tuple(
                    d // N_DEV if i == axis else d for i, d in enumerate(a.shape))
            self.out_shape.append(jax.ShapeDtypeStruct(shape, a.dtype))
        self.scratch = [pltpu.SemaphoreType.DMA((7 * self.n,)), pltpu.SemaphoreType.DMA((7 * self.n,)),
                        pltpu.SemaphoreType.DMA((self.n,))]
        self.arrays = [a for _, a, _ in items]

    @staticmethod
    def _block(ref, b, axis, size):
        if axis is None:
            return ref.at[b]
        sl = pl.ds(pl.multiple_of(b * size, size), size)
        return ref.at[sl, :] if axis == 0 else ref.at[:, sl]

    def _plan(self, ins, outs, sems):
        send_sems, recv_sems, loc_sems = sems
        x, y, c = lax.axis_index("x"), lax.axis_index("y"), lax.axis_index("c")
        me = 4 * x + 2 * y + c
        begin, middle, end = [], [], []

        def peer(d):
            px = 1 - x if d & 4 else x
            py = 1 - y if d & 2 else y
            pc = 1 - c if d & 1 else c
            return (px, py, pc), 4 * px + 2 * py + pc

        for a, (kind, arr, axis) in enumerate(self.items):
            gather = kind != "scatter"
            size = None if axis is None else (arr.shape[axis] if gather else arr.shape[axis] // N_DEV)

            def src(b):
                return ins[a] if gather else self._block(ins[a], b, axis, size)

            def dst(b):
                return self._block(outs[a], b, axis, size) if gather else outs[a].at[b]

            def remote(src_ref, dst_ref, slot, dev):
                return pltpu.make_async_remote_copy(
                    src_ref=src_ref, dst_ref=dst_ref, send_sem=send_sems.at[a * 7 + slot],
                    recv_sem=recv_sems.at[a * 7 + slot], device_id=dev, device_id_type=MESH_ID)

            local = pltpu.make_async_copy(src(me), dst(me), loc_sems.at[a])
            begin.append(local)
            end.append(local.wait)
            if kind == "gather_by_chip":
                sib_dev, sib_id = peer(1)
                to_sib = remote(ins[a], dst(me), 0, sib_dev)
                begin.append(to_sib)
                end += [to_sib.wait_send, remote(ins[a], dst(sib_id), 0, sib_dev).wait_recv]
                for j, d in enumerate((2, 4, 6)):
                    dev, pid = peer(d)
                    over_ici = remote(ins[a], dst(me), 1 + j, dev)
                    begin.append(over_ici)
                    passed_on = remote(dst(pid), dst(pid), 4 + j, sib_dev)
                    middle.append((remote(ins[a], dst(pid), 1 + j, dev), passed_on))
                    end += [over_ici.wait_send, passed_on.wait_send,
                            remote(ins[a], dst(peer(d ^ 1)[1]), 4 + j, sib_dev).wait_recv]
            else:
                for d in range(1, N_DEV):
                    dev, pid = peer(d)
                    begin.append(remote(src(pid), dst(me), d - 1, dev))
                    end.append(remote(src(pid), dst(pid), d - 1, dev).wait)
        return begin, middle, end

    def start(self, ins, outs, sems):
        for cp in self._plan(ins, outs, sems)[0]:
            cp.start()

    def pass_on(self, ins, outs, sems):
        for arrived, onward in self._plan(ins, outs, sems)[1]:
            arrived.wait_recv()
            onward.start()

    def wait(self, ins, outs, sems):
        for wait in self._plan(ins, outs, sems)[2]:
            wait()


def _call(body, *, name, grid, in_specs, out_specs, out_shape, args, scratch_shapes=(), sem=None, comm=None):
    n_in, n_out, n_scr = len(in_specs), len(out_specs), len(scratch_shapes)
    if comm is None:
        return pl.pallas_call(
            body, name=name, grid=grid, out_shape=list(out_shape), in_specs=list(in_specs), out_specs=list(out_specs),
            scratch_shapes=list(scratch_shapes), compiler_params=_params(*sem))(*args)
    hbm = pl.BlockSpec(memory_space=pltpu.HBM)

    def carrier(*refs):
        bounds = [0, n_in, n_in + comm.n, n_in + comm.n + n_out, n_in + 2 * comm.n + n_out, len(refs) - 3, len(refs)]
        ins, cin, outs, cout, scr, sems = [refs[lo:hi] for lo, hi in zip(bounds[:-1], bounds[1:])]
        assert len(scr) == n_scr
        step = functools.reduce(lambda lin, ax: lin * grid[ax] + pl.program_id(ax), range(len(grid)), 0)
        steps = math.prod(grid)

        @pl.when(step == 0)
        def _():
            comm.start(cin, cout, sems)

        body(*ins, *outs, *scr)

        @pl.when(step == (2 * steps) // 3)
        def _():
            comm.pass_on(cin, cout, sems)

        @pl.when(step == steps - 1)
        def _():
            comm.wait(cin, cout, sems)

    return pl.pallas_call(
        carrier, name=name, grid=grid, out_shape=list(out_shape) + comm.out_shape,
        in_specs=list(in_specs) + [hbm] * comm.n, out_specs=list(out_specs) + [hbm] * comm.n,
        scratch_shapes=list(scratch_shapes) + comm.scratch,
        compiler_params=_params(*["arbitrary"] * len(grid)))(*args, *comm.arrays)


def _exchange(items, *, name, vmem=False):
    comm = _Exchange(items)
    space = pl.BlockSpec(memory_space=pltpu.VMEM if vmem else pltpu.HBM)

    def body(*refs):
        ins, outs, sems = refs[:comm.n], refs[comm.n:2 * comm.n], refs[2 * comm.n:]
        comm.start(ins, outs, sems)
        comm.pass_on(ins, outs, sems)
        comm.wait(ins, outs, sems)

    return pl.pallas_call(
        body, name=name, out_shape=comm.out_shape, in_specs=[space] * comm.n, out_specs=[space] * comm.n,
        scratch_shapes=comm.scratch, compiler_params=pltpu.CompilerParams(vmem_limit_bytes=VMEM_LIMIT))(*comm.arrays)


_DIMS = {"nn": (((1,), (0,)), ((), ())), "nt": (((1,), (1,)), ((), ())), "tn": (((0,), (0,)), ((), ()))}


def _matmul(a, b, *, dims, tm, tn, tk, name, epilogue="plain", out_dtype=F32, extras=(), comm=None,
            a_heads=False, b_heads=False):
    a_parts = a.shape[0] if a.ndim == 3 else 0
    b_parts = b.shape[0] if b.ndim == 3 else 0
    assert not (a_parts and dims != "nt" and not a_heads) and not (b_parts and dims != "tn")
    a2 = (a.shape[1], a.shape[2] * a_parts) if a_parts else a.shape
    b2 = (b.shape[1], b.shape[2] * b_parts) if b_parts else b.shape
    if dims == "tn":
        (K, M), N = a2, b2[1]
    else:
        (M, K), N = a2, (b2[1] if dims == "nn" else b2[0])
    tm, tn, tk = min(tm, M), min(tn, N), min(tk, K)
    assert M % tm == 0 and N % tn == 0 and K % tk == 0, (name, M, N, K, tm, tn, tk)
    if a_heads and dims == "tn":
        a_spec = pl.BlockSpec((tm // ATTN_HD, tk, ATTN_HD), lambda i, j, k: (i, k, 0))
    elif a_heads:
        a_spec = pl.BlockSpec((tk // ATTN_HD, tm, ATTN_HD), lambda i, j, k: (k, i, 0))
    elif a_parts:
        per = K // a_parts // tk
        a_spec = pl.BlockSpec((None, tm, tk), lambda i, j, k: (k // per, i, k % per))
    elif dims == "tn":
        a_spec = pl.BlockSpec((tk, tm), lambda i, j, k: (k, i))
    else:
        a_spec = pl.BlockSpec((tm, tk), lambda i, j, k: (i, k))
    if b_heads:
        b_spec = pl.BlockSpec((tn // ATTN_HD, tk, ATTN_HD), lambda i, j, k: (j, k, 0))
    elif b_parts:
        per_n = N // b_parts // tn
        b_spec = pl.BlockSpec((None, tk, tn), lambda i, j, k: (j // per_n, k, j % per_n))
    elif dims == "nt":
        b_spec = pl.BlockSpec((tn, tk), lambda i, j, k: (j, k))
    else:
        b_spec = pl.BlockSpec((tk, tn), lambda i, j, k: (k, j))
    side_by_side = lambda ref: jnp.concatenate([ref[g] for g in range(ref.shape[0])], axis=1)
    nk = K // tk
    tile = pl.BlockSpec((tm, tn), lambda i, j, k: (i, j))
    row = pl.BlockSpec((1, tn), lambda i, j, k: (0, j))
    if epilogue == "plain":
        extra_specs, out_shape, out_specs = [], [jax.ShapeDtypeStruct((M, N), out_dtype)], [tile]
    elif epilogue == "relu2":
        extra_specs, out_shape, out_specs = [], [jax.ShapeDtypeStruct((M, N), BF16)] * 2, [tile, tile]
    elif epilogue == "resgate":
        extra_specs, out_specs = [tile, row], [tile, tile]
        out_shape = [jax.ShapeDtypeStruct((M, N), BF16), jax.ShapeDtypeStruct((M, N), F32)]
    elif epilogue == "mul2a":
        extra_specs, out_shape, out_specs = [tile], [jax.ShapeDtypeStruct((M, N), BF16)], [tile]
    elif epilogue == "heads":
        per_tile = tn // ATTN_HD
        q_tiles = ATTN_HEADS // per_tile
        assert tn == 2 * ATTN_KV * ATTN_HD and N // tn in (q_tiles, q_tiles + 1)
        extra_specs = []
        out_shape = [jax.ShapeDtypeStruct((ATTN_HEADS, M, ATTN_HD), out_dtype)]
        out_specs = [pl.BlockSpec((per_tile, tm, ATTN_HD), lambda i, j, k: (jnp.minimum(j, q_tiles - 1), i, 0))]
        if N // tn > q_tiles:
            out_shape += [jax.ShapeDtypeStruct((ATTN_KV, M, ATTN_HD), out_dtype)] * 2
            out_specs += [pl.BlockSpec((ATTN_KV, tm, ATTN_HD), lambda i, j, k: (0, i, 0))] * 2
    else:
        raise ValueError(epilogue)
    n_extra = len(extra_specs)

    def body(a_ref, b_ref, *rest):
        ex, outs, acc_ref = rest[:n_extra], rest[n_extra:-1], rest[-1]
        k = pl.program_id(2)

        @pl.when(k == 0)
        def _():
            acc_ref[...] = jnp.zeros_like(acc_ref)

        a_tile = side_by_side(a_ref) if a_heads else a_ref[...]
        b_tile = side_by_side(b_ref) if b_heads else b_ref[...]
        acc_ref[...] += lax.dot_general(a_tile, b_tile, _DIMS[dims], preferred_element_type=F32)

        if epilogue == "heads":
            j = pl.program_id(1)

            @pl.when((k == nk - 1) & (j < q_tiles))
            def _():
                for g in range(per_tile):
                    outs[0][g] = acc_ref[:, g * ATTN_HD:(g + 1) * ATTN_HD].astype(out_dtype)

            if len(outs) > 1:
                @pl.when((k == nk - 1) & (j == q_tiles))
                def _():
                    for g in range(ATTN_KV):
                        outs[1][g] = acc_ref[:, g * ATTN_HD:(g + 1) * ATTN_HD].astype(out_dtype)
                        outs[2][g] = acc_ref[:, (ATTN_KV + g) * ATTN_HD:(ATTN_KV + g + 1) * ATTN_HD].astype(out_dtype)
            return

        @pl.when(k == nk - 1)
        def _():
            acc = acc_ref[...]
            if epilogue == "plain":
                outs[0][...] = acc.astype(out_dtype)
            elif epilogue == "relu2":
                act = jnp.maximum(acc, 0.0)
                outs[0][...] = act.astype(BF16)
                outs[1][...] = (act * act).astype(BF16)
            elif epilogue == "resgate":
                outs[0][...] = acc.astype(BF16)
                outs[1][...] = ex[0][...] + ex[1][...] * acc
            else:
                outs[0][...] = (acc * (2.0 * ex[0][...].astype(F32))).astype(BF16)

    res = _call(body, name=name, grid=(M // tm, N // tn, nk), out_shape=out_shape,
                in_specs=[a_spec, b_spec] + extra_specs, out_specs=out_specs,
                scratch_shapes=[pltpu.VMEM((tm, tn), F32)],
                sem=("parallel", "arbitrary" if epilogue == "heads" else "parallel", "arbitrary"),
                args=(a, b, *extras), comm=comm)
    return res[0] if len(res) == 1 else res


def _row_tile(T):
    return min(T, 256)


def _norm_mod_fwd(x, gain, sc, sh, name, comm=None):
    T, D = x.shape
    tr = _row_tile(T)

    def body(x_ref, g_ref, sc_ref, sh_ref, h_ref):
        xv = x_ref[...]
        r = lax.rsqrt(jnp.mean(xv * xv, axis=-1, keepdims=True) + EPS)
        hn = (xv * r) * g_ref[...]
        h_ref[...] = (hn * (1.0 + sc_ref[...]) + sh_ref[...]).astype(BF16)

    vec = pl.BlockSpec((1, D), lambda i: (0, 0))
    res = _call(body, name=name, grid=(T // tr,), out_shape=[jax.ShapeDtypeStruct((T, D), BF16)],
                in_specs=[pl.BlockSpec((tr, D), lambda i: (i, 0)), vec, vec, vec],
                out_specs=[pl.BlockSpec((tr, D), lambda i: (i, 0))], sem=("parallel",), args=(x, gain, sc, sh),
                comm=comm)
    return res[0] if comm is None else res


def _through_gate(dx, branch_ref, gate_ref, dbranch_ref, st_ref, row):
    dbranch_ref[...] = (dx * gate_ref[...]).astype(BF16)
    st_ref[row:row + 1, :] += jnp.sum(dx * branch_ref[...].astype(F32), axis=0, keepdims=True)


def _norm_mod_bwd(x, dh, dres, gain, sc, name, below=None):
    T, D = x.shape
    tr = _row_tile(T)

    def body(x_ref, dh_ref, dres_ref, g_ref, sc_ref, *rest):
        dx_ref, st_ref = rest[-3:-1] if below else rest[-2:]
        xv, dh_v, gain_v = x_ref[...], dh_ref[...].astype(F32), g_ref[...]
        r = lax.rsqrt(jnp.mean(xv * xv, axis=-1, keepdims=True) + EPS)
        xn = xv * r
        hn = xn * gain_v
        dhn = dh_v * (1.0 + sc_ref[...])
        dxn = dhn * gain_v
        dx = dres_ref[...] + r * (dxn - xn * jnp.mean(dxn * xn, axis=-1, keepdims=True))
        dx_ref[...] = dx

        @pl.when(pl.program_id(0) == 0)
        def _():
            st_ref[...] = jnp.zeros_like(st_ref)

        st_ref[0:1, :] += jnp.sum(dh_v, axis=0, keepdims=True)
        st_ref[1:2, :] += jnp.sum(dh_v * hn, axis=0, keepdims=True)
        st_ref[2:3, :] += jnp.sum(dhn * xn, axis=0, keepdims=True)
        if below:
            _through_gate(dx, rest[0], rest[1], rest[-1], st_ref, 3)

    vec = pl.BlockSpec((1, D), lambda i: (0, 0))
    blk = pl.BlockSpec((tr, D), lambda i: (i, 0))
    return pl.pallas_call(
        body, name=name, grid=(T // tr,),
        out_shape=[jax.ShapeDtypeStruct((T, D), F32), jax.ShapeDtypeStruct((8, D), F32)]
        + ([jax.ShapeDtypeStruct((T, D), BF16)] if below else []),
        in_specs=[blk, blk, blk, vec, vec] + ([blk, vec] if below else []),
        out_specs=[blk, pl.BlockSpec((8, D), lambda i: (0, 0))] + ([blk] if below else []),
        compiler_params=_params("arbitrary"),
    )(x, dh, dres, gain, sc, *(below or ()))


def _loss_head(y, target, below, name):
    T, D = y.shape
    tr = _row_tile(T)

    def body(y_ref, t_ref, b_ref, g_ref, dy_ref, l_ref, st_ref, db_ref):
        err = y_ref[...] - t_ref[...]
        dy = err * (1.0 / D)
        dy_ref[...] = dy

        @pl.when(pl.program_id(0) == 0)
        def _():
            l_ref[...] = jnp.zeros_like(l_ref)
            st_ref[...] = jnp.zeros_like(st_ref)

        part = jnp.sum(jnp.mean(err * err, axis=-1, keepdims=True), axis=0, keepdims=True)
        l_ref[...] += jnp.broadcast_to(0.5 * part, l_ref.shape)
        _through_gate(dy, b_ref, g_ref, db_ref, st_ref, 0)

    blk = pl.BlockSpec((tr, D), lambda i: (i, 0))
    return pl.pallas_call(
        body, name=name, grid=(T // tr,),
        out_shape=[jax.ShapeDtypeStruct((T, D), F32), jax.ShapeDtypeStruct((8, 128), F32),
                   jax.ShapeDtypeStruct((8, D), F32), jax.ShapeDtypeStruct((T, D), BF16)],
        in_specs=[blk, blk, blk, pl.BlockSpec((1, D), lambda i: (0, 0))],
        out_specs=[blk, pl.BlockSpec((8, 128), lambda i: (0, 0)), pl.BlockSpec((8, D), lambda i: (0, 0)), blk],
        compiler_params=_params("arbitrary"),
    )(y, target, *below)


def _silu(v):
    return v * jax.nn.sigmoid(v)


def _mod_fwd(c_all, mod_w, mod_b_cols, name):
    L, D, n = mod_w.shape
    tn = 512

    def body(c_ref, w_ref, b_ref, o_ref):
        cond = _silu(c_ref[...]).astype(BF16)
        o_ref[0] = jnp.dot(cond, w_ref[0].astype(BF16), preferred_element_type=F32) + b_ref[0]

    return pl.pallas_call(
        body, name=name, grid=(L, n // tn), out_shape=jax.ShapeDtypeStruct((L, N_DEV, n), F32),
        in_specs=[pl.BlockSpec((N_DEV, D), lambda l, j: (0, 0)), pl.BlockSpec((1, D, tn), lambda l, j: (l, 0, j)),
                  pl.BlockSpec((1, 1, tn), lambda l, j: (l, 0, j))],
        out_specs=pl.BlockSpec((1, N_DEV, tn), lambda l, j: (l, 0, j)),
        compiler_params=_params("parallel", "parallel"),
    )(c_all, mod_w, mod_b_cols)


def _adamw(g, w, m, v):
    m = ADAM_B1 * m + (1.0 - ADAM_B1) * g
    v = ADAM_B2 * v + (1.0 - ADAM_B2) * (g * g)
    m_hat = m / (1.0 - ADAM_B1 ** ADAM_STEP)
    v_hat = v / (1.0 - ADAM_B2 ** ADAM_STEP)
    delta = -ADAM_LR * (m_hat / (jnp.sqrt(v_hat) + ADAM_EPS) + ADAM_WD * w)
    return delta, m, v


def _mod_w_update(c_t, dmod_cols, w, m, v, name):
    L, D, n = w.shape
    tr = 256

    def body(c_ref, dm_ref, w_ref, m_ref, v_ref, g_ref, d_ref, nm_ref, nv_ref):
        cond = _silu(c_ref[...])
        dm = dm_ref[0]
        g = cond[:, 0:1] * dm[0:1, :]
        for b in range(1, N_DEV):
            g = g + cond[:, b:b + 1] * dm[b:b + 1, :]
        delta, nm, nv = _adamw(g, w_ref[0], m_ref[0], v_ref[0])
        g_ref[0], d_ref[0], nm_ref[0], nv_ref[0] = g, delta, nm, nv

    blk = pl.BlockSpec((1, tr, n), lambda l, i: (l, i, 0))
    return pl.pallas_call(
        body, name=name, grid=(L, D // tr), out_shape=[jax.ShapeDtypeStruct(w.shape, F32)] * 4,
        in_specs=[pl.BlockSpec((tr, N_DEV), lambda l, i: (i, 0)), pl.BlockSpec((1, N_DEV, n), lambda l, i: (l, 0, 0)),
                  blk, blk, blk],
        out_specs=[blk] * 4, compiler_params=_params("parallel", "parallel"),
    )(c_t, dmod_cols, w, m, v)


def _sum_adamw(parts, w, m, v, name):
    R, C = w.shape
    tr = min(R, 256 if C >= 1024 else 1024)
    assert R % tr == 0

    def body(p_ref, w_ref, m_ref, v_ref, g_ref, d_ref, nm_ref, nv_ref):
        g = p_ref[0].astype(F32)
        for s in range(1, N_DEV):
            g = g + p_ref[s].astype(F32)
        delta, nm, nv = _adamw(g, w_ref[...], m_ref[...], v_ref[...])
        g_ref[...], d_ref[...], nm_ref[...], nv_ref[...] = g, delta, nm, nv

    blk = pl.BlockSpec((tr, C), lambda i: (i, 0))
    return pl.pallas_call(
        body, name=name, grid=(R // tr,), out_shape=[jax.ShapeDtypeStruct((R, C), F32)] * 4,
        in_specs=[pl.BlockSpec((N_DEV, tr, C), lambda i: (0, i, 0)), blk, blk, blk], out_specs=[blk] * 4,
        compiler_params=_params("parallel"),
    )(parts, w, m, v)


def _sum_adamw_layers(parts, w, m, v, name):
    L, R, C = w.shape
    tr = min(R, 256 * 1024 // C)
    assert R % tr == 0 and len(parts) == L
    ni = R // tr

    def body(*refs):
        p_refs, (w_ref, m_ref, v_ref), outs = refs[:L], refs[L:L + 3], refs[L + 3:]
        for layer in range(L):
            @pl.when(pl.program_id(0) == layer)
            def _(p_ref=p_refs[layer]):
                g = p_ref[0].astype(F32)
                for s in range(1, N_DEV):
                    g = g + p_ref[s].astype(F32)
                delta, nm, nv = _adamw(g, w_ref[...], m_ref[...], v_ref[...])
                for o_ref, val in zip(outs, (g, delta, nm, nv)):
                    o_ref[...] = val

    def shares(layer):
        park = 0 if layer else ni - 1
        return pl.BlockSpec((N_DEV, tr, C), lambda l, i: (0, jnp.where(l == layer, i, park), 0))

    blk = pl.BlockSpec((None, tr, C), lambda l, i: (l, i, 0))
    return pl.pallas_call(
        body, name=name, grid=(L, ni), out_shape=[jax.ShapeDtypeStruct((L, R, C), F32)] * 4,
        in_specs=[shares(layer) for layer in range(L)] + [blk] * 3, out_specs=[blk] * 4,
        compiler_params=_params("arbitrary", "arbitrary"),
    )(*parts, w, m, v)


def _lower_bound_row1(l0, l1):
    mx = lax.stop_gradient(jnp.maximum(l0, l1))
    e0, e1 = jnp.exp(l0 - mx), jnp.exp(l1 - mx)
    p0, p1 = e0 / (e0 + e1), e1 / (e0 + e1)
    return (p0 + p1) - p0


def _lb_fwd(logits, name):
    def body(l_ref, o_ref):
        o_ref[...] = _lower_bound_row1(l_ref[0:1, :], l_ref[1:2, :])

    return pl.pallas_call(body, name=name, out_shape=jax.ShapeDtypeStruct((1, logits.shape[1]), F32))(logits)


def _lb_bwd(logits, dlb, name):
    def body(l_ref, d_ref, o_ref):
        _, vjp = jax.vjp(_lower_bound_row1, l_ref[0:1, :], l_ref[1:2, :])
        d0, d1 = vjp(d_ref[...])
        o_ref[0:1, :] = d0
        o_ref[1:2, :] = d1

    return pl.pallas_call(body, name=name, out_shape=jax.ShapeDtypeStruct(logits.shape, F32))(logits, dlb)


def _bdot(a, b, dims):
    return lax.dot_general(a.astype(BF16), b.astype(BF16), _DIMS[dims], preferred_element_type=F32)


def _rms(x, gain):
    r = lax.rsqrt(jnp.mean(x * x, axis=-1, keepdims=True) + EPS)
    xhat = x * r
    return xhat, r, xhat * gain


def _attn_band(n):
    qi = lax.broadcasted_iota(jnp.int32, (ATTN_BLOCK, 2 * ATTN_BLOCK), 0)
    ki = lax.broadcasted_iota(jnp.int32, (ATTN_BLOCK, 2 * ATTN_BLOCK), 1)
    dist = qi + ATTN_BLOCK - ki
    first_key = jnp.where(n > 0, 0, ATTN_BLOCK)
    valid = (dist >= 0) & (dist < ATTN_BLOCK) & (ki >= first_key)
    return valid, jnp.abs(dist).astype(F32)


def _attn_head_probs(qn, kn_b, valid, absdist, slope, sink):
    s = lax.dot_general(qn.astype(BF16), kn_b, _DIMS["nt"], preferred_element_type=F32) * ATTN_SCALE
    s = jnp.where(valid, s - slope * absdist, NEG_BIG)
    mx = jnp.maximum(jnp.max(s, axis=-1, keepdims=True), sink)
    e = jnp.exp(s - mx)
    es = jnp.exp(sink - mx)
    inv = 1.0 / (jnp.sum(e, axis=-1, keepdims=True) + es)
    return e * inv, es * inv


def _attn_specs(T):
    nb = T // ATTN_BLOCK
    qspec = pl.BlockSpec((ATTN_GROUP, ATTN_BLOCK, ATTN_HD), lambda h, n: (h, n, 0))
    prev = pl.BlockSpec((1, ATTN_BLOCK, ATTN_HD), lambda h, n: (h, jnp.maximum(n - 1, 0), 0))
    cur = pl.BlockSpec((1, ATTN_BLOCK, ATTN_HD), lambda h, n: (h, n, 0))
    gain = pl.BlockSpec((1, ATTN_HD), lambda h, n: (0, 0))
    scalars = pl.BlockSpec(memory_space=pltpu.SMEM)
    return nb, qspec, prev, cur, gain, scalars


def _attn_fwd(qh, kh, vh, qg, kg, slopes, sinks, name, comm=None):
    T = qh.shape[1]
    nb, qspec, prev, cur, gain, scalars = _attn_specs(T)

    def body(q_ref, kp_ref, kc_ref, vp_ref, vc_ref, qg_ref, kg_ref, sl_ref, sk_ref, o_ref):
        h, n = pl.program_id(0), pl.program_id(1)
        valid, absdist = _attn_band(n)
        _, _, kn = _rms(jnp.concatenate([kp_ref[0], kc_ref[0]], axis=0), kg_ref[...])
        kn_b = kn.astype(BF16)
        v_b = jnp.concatenate([vp_ref[0], vc_ref[0]], axis=0).astype(BF16)
        for g in range(ATTN_GROUP):
            head = h * ATTN_GROUP + g
            _, _, qn = _rms(q_ref[g], qg_ref[...])
            p, _ = _attn_head_probs(qn, kn_b, valid, absdist, sl_ref[head], sk_ref[head])
            o_ref[g] = jnp.dot(p.astype(BF16), v_b, preferred_element_type=F32).astype(BF16)

    return _call(body, name=name, grid=(ATTN_KV, nb), out_shape=[jax.ShapeDtypeStruct(qh.shape, BF16)],
                 in_specs=[qspec, prev, cur, prev, cur, gain, gain, scalars, scalars], out_specs=[qspec],
                 sem=("parallel", "parallel"), args=(qh, kh, kh, vh, vh, qg, kg, slopes, sinks), comm=comm)


def _rms_bwd(dy, xhat, r, gain):
    dxh = dy * gain
    dx = r * (dxh - xhat * jnp.mean(dxh * xhat, axis=-1, keepdims=True))
    return dx, jnp.sum(dy * xhat, axis=0, keepdims=True)


def _attn_bwd(qh, kh, vh, doh, qg, kg, slope_col, sink_col, name, comm=None):
    T = qh.shape[1]
    nb = T // ATTN_BLOCK
    rows = ATTN_GROUP * ATTN_BLOCK

    def body(q_ref, kp_ref, kc_ref, vp_ref, vc_ref, do_ref, qg_ref, kg_ref, sl_ref, sk_ref,
             dq_ref, dk_ref, dv_ref, dqg_ref, dkg_ref, dsk_ref, carry_ref, sk_acc):
        h, step = pl.program_id(0), pl.program_id(1)
        n = nb - 1 - step
        qg_v, kg_v = qg_ref[...], kg_ref[...]
        qhat, rq, qn = _rms(q_ref[...].reshape(rows, ATTN_HD), qg_v)
        khat, rk, kn = _rms(jnp.concatenate([kp_ref[0], kc_ref[0]], axis=0), kg_v)
        s = _bdot(qn, kn, "nt") * ATTN_SCALE
        qi = lax.broadcasted_iota(jnp.int32, s.shape, 0) & (ATTN_BLOCK - 1)
        ki = lax.broadcasted_iota(jnp.int32, s.shape, 1)
        dist = qi + ATTN_BLOCK - ki
        first_key = jnp.where(n > 0, 0, ATTN_BLOCK)
        valid = (dist >= 0) & (dist < ATTN_BLOCK) & (ki >= first_key)
        s = jnp.where(valid, s - sl_ref[0] * jnp.abs(dist).astype(F32), NEG_BIG)
        sink = sk_ref[0]
        mx = jnp.maximum(jnp.max(s, axis=-1, keepdims=True), sink)
        e = jnp.exp(s - mx)
        es = jnp.exp(sink - mx)
        den = jnp.sum(e, axis=-1, keepdims=True) + es
        p, ps = e / den, es / den
        v = jnp.concatenate([vp_ref[0], vc_ref[0]], axis=0)
        do = do_ref[...].reshape(rows, ATTN_HD)
        dp = _bdot(do, v, "nt")
        delta = jnp.sum(p * dp, axis=-1, keepdims=True)
        ds = p * (dp - delta)
        dqn = _bdot(ds, kn, "nn") * ATTN_SCALE
        dkn = _bdot(ds, qn, "tn") * ATTN_SCALE
        dv = _bdot(p, do, "tn")
        dq, dqg = _rms_bwd(dqn, qhat, rq, qg_v)
        dk, dkg = _rms_bwd(dkn, khat, rk, kg_v)
        dq_ref[...] = dq.reshape(ATTN_GROUP, ATTN_BLOCK, ATTN_HD).astype(BF16)

        @pl.when((h == 0) & (step == 0))
        def _():
            dqg_ref[...] = jnp.zeros_like(dqg_ref)
            dkg_ref[...] = jnp.zeros_like(dkg_ref)

        dqg_ref[0:1, :] += dqg
        dkg_ref[0:1, :] += dkg

        @pl.when(step == 0)
        def _():
            carry_ref[...] = jnp.zeros_like(carry_ref)
            sk_acc[...] = jnp.zeros_like(sk_acc)

        dk_ref[0] = (dk[ATTN_BLOCK:, :] + carry_ref[0]).astype(BF16)
        dv_ref[0] = (dv[ATTN_BLOCK:, :] + carry_ref[1]).astype(BF16)
        carry_ref[0] = dk[:ATTN_BLOCK, :]
        carry_ref[1] = dv[:ATTN_BLOCK, :]
        sk_acc[...] += -ps * delta

        @pl.when(step == nb - 1)
        def _():
            for g in range(ATTN_GROUP):
                tot = jnp.sum(sk_acc[g * ATTN_BLOCK:(g + 1) * ATTN_BLOCK, :], axis=0, keepdims=True)
                dsk_ref[g:g + 1, :] = jnp.broadcast_to(tot, (1, 128))

    qspec = pl.BlockSpec((ATTN_GROUP, ATTN_BLOCK, ATTN_HD), lambda h, s: (h, nb - 1 - s, 0))
    prev = pl.BlockSpec((1, ATTN_BLOCK, ATTN_HD), lambda h, s: (h, jnp.maximum(nb - 2 - s, 0), 0))
    cur = pl.BlockSpec((1, ATTN_BLOCK, ATTN_HD), lambda h, s: (h, nb - 1 - s, 0))
    gain = pl.BlockSpec((1, ATTN_HD), lambda h, s: (0, 0))
    col = pl.BlockSpec((1, rows, 1), lambda h, s: (h, 0, 0))
    acc = pl.BlockSpec((8, ATTN_HD), lambda h, s: (0, 0))
    return _call(
        body, name=name, grid=(ATTN_KV, nb),
        out_shape=[jax.ShapeDtypeStruct(qh.shape, BF16), jax.ShapeDtypeStruct(kh.shape, BF16),
                   jax.ShapeDtypeStruct(kh.shape, BF16), jax.ShapeDtypeStruct((8, ATTN_HD), F32),
                   jax.ShapeDtypeStruct((8, ATTN_HD), F32), jax.ShapeDtypeStruct((ATTN_HEADS, 128), F32)],
        in_specs=[qspec, prev, cur, prev, cur, qspec, gain, gain, col, col],
        out_specs=[qspec, cur, cur, acc, acc, pl.BlockSpec((ATTN_GROUP, 128), lambda h, s: (h, 0))],
        scratch_shapes=[pltpu.VMEM((2, ATTN_BLOCK, ATTN_HD), F32), pltpu.VMEM((rows, 1), F32)],
        sem=("arbitrary", "arbitrary"), args=(qh, kh, kh, vh, vh, doh, qg, kg, slope_col, sink_col), comm=comm)


@functools.partial(jax.custom_vjp, nondiff_argnums=(2,))
def _mm(a, b, dims):
    return _bdot(a, b, dims)


def _mm_fwd(a, b, dims):
    return _bdot(a, b, dims), (a, b)


def _mm_bwd(dims, res, ct):
    a, b = res
    if dims == "nn":
        return _bdot(ct, b, "nt"), _bdot(a, ct, "tn")
    if dims == "nt":
        return _bdot(ct, b, "nn"), _bdot(ct, a, "tn")
    return _bdot(b, ct, "nt"), _bdot(a, ct, "nn")


_mm.defvjp(_mm_fwd, _mm_bwd)


def _same_chunk_mask(rows, upper):
    ri = lax.broadcasted_iota(jnp.int32, (rows, rows), 0)
    ci = lax.broadcasted_iota(jnp.int32, (rows, rows), 1)
    same = (ri >> HGRN_CHUNK_SHIFT) == (ci >> HGRN_CHUNK_SHIFT)
    return same & ((ri <= ci) if upper else (ri >= ci))


@jax.custom_vjp
def _chunk_cumsum(x, tri, tri_t):
    hi = x.astype(BF16)
    rest = x - hi.astype(F32)
    mid = rest.astype(BF16)
    lo = (rest - mid.astype(F32)).astype(BF16)
    out = jnp.dot(tri, jnp.concatenate([hi, mid, lo], axis=1), preferred_element_type=F32)
    w = x.shape[1]
    return out[:, :w] + out[:, w:2 * w] + out[:, 2 * w:]


def _chunk_cumsum_fwd(x, tri, tri_t):
    return _chunk_cumsum(x, tri, tri_t), (tri, tri_t)


def _chunk_cumsum_bwd(res, ct):
    tri, tri_t = res
    return _chunk_cumsum(ct, tri_t, tri), jnp.zeros_like(tri), jnp.zeros_like(tri_t)


_chunk_cumsum.defvjp(_chunk_cumsum_fwd, _chunk_cumsum_bwd)


def _hgrn_masks(rows):
    nc = rows // HGRN_CHUNK
    lower = _same_chunk_mask(rows, False)
    chunk_of_row = lax.broadcasted_iota(jnp.int32, (rows, HGRN_DK), 0) >> HGRN_CHUNK_SHIFT
    row_in_chunk = lax.broadcasted_iota(jnp.int32, (nc, HGRN_CHUNK, HGRN_DK), 1)
    return dict(lower=lower, tri=lower.astype(BF16), tri_t=_same_chunk_mask(rows, True).astype(BF16),
                in_chunk=[chunk_of_row == c for c in range(nc)],
                mid_row=row_in_chunk == HGRN_CHUNK // 2 - 1, last_row=row_in_chunk == HGRN_CHUNK - 1)


def _hgrn_block(masks, st, qr, fr, v, gr, lb, og):
    rows = qr.shape[0]
    nc = rows // HGRN_CHUNK
    per_chunk = lambda m: m.reshape(nc, HGRN_CHUNK, HGRN_DK)
    flat = lambda m: m.reshape(rows, HGRN_DK)
    by_chunk = lambda m: jnp.concatenate([jnp.where(masks["in_chunk"][c], m, 0.0) for c in range(nc)], axis=1)

    forget = lb + (1.0 - lb) * jax.nn.sigmoid(fr)
    k = 1.0 - forget
    b = _chunk_cumsum(jnp.log(forget), masks["tri"], masks["tri_t"])
    b3 = per_chunk(b)
    piv = jnp.sum(jnp.where(masks["mid_row"], b3, 0.0), axis=1, keepdims=True)
    b_last = jnp.sum(jnp.where(masks["last_row"], b3, 0.0), axis=1, keepdims=True)
    q = _silu(qr) * HGRN_SCALE
    a = _mm(q * flat(jnp.exp(b3 - piv)), k * flat(jnp.exp(piv - b3)), "nt")
    o = _mm(jnp.where(masks["lower"], a, 0.0), v, "nn")
    updates = _mm(v, by_chunk(k * flat(jnp.exp(b_last - b3))), "tn")
    decay = jnp.exp(b_last)
    before = []
    for c in range(nc):
        before.append(st)
        st = st * decay[c] + updates[:, c * HGRN_DK:(c + 1) * HGRN_DK]
    o = o + _mm(by_chunk(q * jnp.exp(b)), jnp.concatenate(before, axis=1), "nt")
    y = (o * lax.rsqrt(jnp.mean(o * o, axis=-1, keepdims=True) + EPS)) * og * _silu(gr)
    return y, st


def _hgrn_tile(T):
    return min(T, 256)


HGRN_HEADS_PER_STEP = 4
HGRN_GROUPS = HGRN_HEADS // HGRN_HEADS_PER_STEP


def _hgrn_fwd(proj, lb, og, name, comm=None):
    T = proj.shape[0]
    tb = _hgrn_tile(T)
    hp, wide = HGRN_HEADS_PER_STEP, HGRN_HEADS_PER_STEP * HGRN_DK

    def body(q_ref, f_ref, v_ref, g_ref, lb_ref, og_ref, o_ref, s_ref, st_ref):
        @pl.when(pl.program_id(1) == 0)
        def _():
            st_ref[...] = jnp.zeros_like(st_ref)

        masks = _hgrn_masks(tb)
        for j in range(hp):
            ln = slice(j * HGRN_DK, (j + 1) * HGRN_DK)
            st = st_ref[j]
            s_ref[j, 0] = st
            y, st_ref[j] = _hgrn_block(masks, st, q_ref[:, ln], f_ref[:, ln], v_ref[:, ln], g_ref[:, ln], lb_ref[j],
                                       og_ref[j])
            o_ref[:, ln] = y.astype(BF16)

    part = lambda p: pl.BlockSpec((tb, wide), lambda h, t: (t, p * HGRN_GROUPS + h))
    vec = pl.BlockSpec((hp, 1, HGRN_DK), lambda h, t: (h, 0, 0))
    return _call(
        body, name=name, grid=(HGRN_GROUPS, T // tb),
        out_shape=[jax.ShapeDtypeStruct((T, D_MODEL), BF16),
                   jax.ShapeDtypeStruct((HGRN_HEADS, T // tb, HGRN_DK, HGRN_DK), F32)],
        in_specs=[part(0), part(1), part(2), part(3), vec, vec],
        out_specs=[pl.BlockSpec((tb, wide), lambda h, t: (t, h)),
                   pl.BlockSpec((hp, 1, HGRN_DK, HGRN_DK), lambda h, t: (h, t, 0, 0))],
        scratch_shapes=[pltpu.VMEM((hp, HGRN_DK, HGRN_DK), F32)], sem=("parallel", "arbitrary"),
        args=(proj, proj, proj, proj, lb, og), comm=comm)


def _hgrn_bwd(proj, states, do, lb, og, name, comm=None):
    T = proj.shape[0]
    tb = _hgrn_tile(T)
    nt, hp, wide = T // tb, HGRN_HEADS_PER_STEP, HGRN_HEADS_PER_STEP * HGRN_DK

    def body(q_ref, f_ref, v_ref, g_ref, s_ref, do_ref, lb_ref, og_ref, dp_ref, dlb_ref, dog_ref, dst_ref):
        @pl.when(pl.program_id(1) == 0)
        def _():
            dst_ref[...] = jnp.zeros_like(dst_ref)
            dlb_ref[...] = jnp.zeros_like(dlb_ref)
            dog_ref[...] = jnp.zeros_like(dog_ref)

        block = functools.partial(_hgrn_block, _hgrn_masks(tb))
        for j in range(hp):
            ln = slice(j * HGRN_DK, (j + 1) * HGRN_DK)
            _, vjp = jax.vjp(block, s_ref[j, 0], q_ref[:, ln], f_ref[:, ln], v_ref[:, ln], g_ref[:, ln],
                             lb_ref[j], og_ref[j])
            dst_ref[j], dq, df, dv, dg, dlb, dog = vjp((do_ref[:, ln], dst_ref[j]))
            for p, part_grad in enumerate((dq, df, dv, dg)):
                dp_ref[p, :, ln] = part_grad.astype(BF16)
            dlb_ref[j] += dlb
            dog_ref[j] += dog

    part = lambda p: pl.BlockSpec((tb, wide), lambda h, t: (nt - 1 - t, p * HGRN_GROUPS + h))
    vec = pl.BlockSpec((hp, 1, HGRN_DK), lambda h, t: (h, 0, 0))
    head = pl.BlockSpec((tb, wide), lambda h, t: (nt - 1 - t, h))
    return _call(
        body, name=name, grid=(HGRN_GROUPS, nt),
        out_shape=[jax.ShapeDtypeStruct((4, T, D_MODEL), BF16)] + [jax.ShapeDtypeStruct((HGRN_HEADS, 1, HGRN_DK), F32)] * 2,
        in_specs=[part(0), part(1), part(2), part(3),
                  pl.BlockSpec((hp, 1, HGRN_DK, HGRN_DK), lambda h, t: (h, nt - 1 - t, 0, 0)), head, vec, vec],
        out_specs=[pl.BlockSpec((4, tb, wide), lambda h, t: (0, nt - 1 - t, h)), vec, vec],
        scratch_shapes=[pltpu.VMEM((hp, HGRN_DK, HGRN_DK), F32)], sem=("parallel", "arbitrary"),
        args=(proj, proj, proj, proj, states, do, lb, og), comm=comm)


def _cols_to_blocks(g, n8):
    K = g.shape[0]
    return g.reshape(K, N_DEV, n8).transpose(1, 0, 2)


def _blocks_to_cols(wg):
    _, K, n8 = wg.shape
    return wg.transpose(1, 0, 2).reshape(K, N_DEV * n8)


def _pack(parts):
    flat = []
    for p in parts:
        v = p.reshape(-1)
        flat.append(jnp.pad(v, (0, (-v.shape[0]) % 1024)))
    return jnp.concatenate(flat).reshape(-1, 128)


def _unpack(packed, like):
    flat, out, off = packed.reshape(-1), [], 0
    for p in like:
        size = math.prod(p.shape)
        out.append(flat[off:off + size].reshape(p.shape))
        off += size + (-size) % 1024
    return out


def _heads_major(a, heads):
    T = a.shape[0]
    return a.reshape(T, heads, ATTN_HD).transpose(1, 0, 2)


def _heads_minor(a):
    heads, T, _ = a.shape
    return a.transpose(1, 0, 2).reshape(T, heads * ATTN_HD)


def kernel(x, c, mod_w, mod_b, norm_mix, norm_mlp, attn_w_in, attn_w_out, attn_q_gain, attn_k_gain, attn_sinks, hgrn_w_in, hgrn_w_out, hgrn_o_gain, hgrn_lb_logits, mlp_w1, mlp_w2, loss_target, m_mod_w, m_mod_b, m_norm_mix, m_norm_mlp, m_attn_w_in, m_attn_w_out, m_attn_q_gain, m_attn_k_gain, m_attn_sinks, m_hgrn_w_in, m_hgrn_w_out, m_hgrn_o_gain, m_hgrn_lb_logits, m_mlp_w1, m_mlp_w2, v_mod_w, v_mod_b, v_norm_mix, v_norm_mlp, v_attn_w_in, v_attn_w_out, v_attn_q_gain, v_attn_k_gain, v_attn_sinks, v_hgrn_w_in, v_hgrn_w_out, v_hgrn_o_gain, v_hgrn_lb_logits, v_mlp_w1, v_mlp_w2):
    T = x.shape[1]
    me = 4 * lax.axis_index("x") + 2 * lax.axis_index("y") + lax.axis_index("c")
    x0, target = x[0], loss_target[0]
    n_mod = mod_w.shape[2]

    shards = [attn_w_in[0], attn_w_out[0], hgrn_w_in[0], hgrn_w_out[0], mlp_w1[0], mlp_w1[1], mlp_w2[0], mlp_w2[1]]
    sb = [s.astype(BF16) for s in shards]
    gather = lambda *items: _Exchange([("gather_by_chip", a, axis) for a, axis in items])

    c_all = _exchange([("gather", c.reshape(16, 128), None)], vmem=True, name="gather_c")[0].reshape(N_DEV, D_MODEL)
    mod_b_cols = lax.dynamic_slice_in_dim(mod_b, me * n_mod, n_mod, axis=1).reshape(2, 1, n_mod)
    mod_cols = _mod_fwd(c_all, mod_w, mod_b_cols, "mod_fwd")
    mod_all = _exchange([("gather", mod_cols.reshape(-1, 128), None)], vmem=True, name="gather_mod")[0]
    mod_all = mod_all.reshape(N_DEV, 2, N_DEV, n_mod)
    mod_mine = lax.dynamic_index_in_dim(mod_all, me, axis=2, keepdims=False)
    mod_mine = mod_mine.transpose(1, 0, 2).reshape(2, N_MOD, 1, D_MODEL)

    lb = _lb_fwd(hgrn_lb_logits, "lb_fwd").reshape(HGRN_HEADS, 1, HGRN_DK)
    og = hgrn_o_gain.reshape(HGRN_HEADS, 1, HGRN_DK)
    slopes = jnp.exp2(-8.0 * jnp.arange(1, ATTN_HEADS + 1, dtype=F32) / ATTN_HEADS)
    sinks = attn_sinks[0]
    per_row = lambda vals: jnp.repeat(vals.reshape(ATTN_KV, ATTN_GROUP), ATTN_BLOCK, axis=1).reshape(
        ATTN_KV, ATTN_GROUP * ATTN_BLOCK, 1)
    slope_col, sink_col = per_row(slopes), per_row(sinks)

    saved = []
    xi = x0
    w_mlp1, w_mlp2 = [None, None], [None, None]
    for i in range(2):
        sh1, sc1, g1, sh2, sc2, g2 = [mod_mine[i, j] for j in range(N_MOD)]
        if i == 0:
            h, w_attn_in = _norm_mod_fwd(xi, norm_mix[i:i + 1], sc1, sh1, "norm_mix_fwd0", comm=gather((sb[0], None)))
            w_attn_in = _blocks_to_cols(w_attn_in)
            qh, kh, vh, w_attn_out = _matmul(h, w_attn_in, dims="nn", tm=1024, tn=512, tk=2048, name="attn_in_fwd",
                                             epilogue="heads", comm=gather((sb[1], 0)))
            o, w_mlp1[0] = _attn_fwd(qh, kh, vh, attn_q_gain, attn_k_gain, slopes, sinks, "attn_fwd",
                                     comm=gather((sb[4], 1)))
            mix = (qh, kh, vh)
            w_out = w_attn_out
        else:
            h = _norm_mod_fwd(xi, norm_mix[i:i + 1], sc1, sh1, "norm_mix_fwd1")
            proj, w_hgrn_out = _matmul(h, w_hgrn_in, dims="nn", tm=1024, tn=1024, tk=2048, name="hgrn_in_fwd",
                                       comm=gather((sb[3], 0)))
            o, states, w_mlp1[1] = _hgrn_fwd(proj, lb, og, "hgrn_fwd", comm=gather((sb[5], 1)))
            mix = (proj, states)
            w_out = w_hgrn_out
        y, x1 = _matmul(o, w_out, dims="nn", tm=1024, tn=1024, tk=2048, name=f"mix_out_fwd{i}", epilogue="resgate",
                        extras=(xi, g1), a_heads=i == 0)
        h2 = _norm_mod_fwd(x1, norm_mlp[i:i + 1], sc2, sh2, f"norm_mlp_fwd{i}")
        if i == 0:
            act, act2, w_mlp2[0] = _matmul(h2, w_mlp1[0], dims="nn", tm=1024, tn=1024, tk=2048, name="mlp1_fwd0",
                                           epilogue="relu2", comm=gather((sb[6], 0)))
            z, x2, w_hgrn_in = _matmul(act2, w_mlp2[0], dims="nn", tm=1024, tn=1024, tk=2048, name="mlp2_fwd0",
                                       epilogue="resgate", extras=(x1, g2), comm=gather((sb[2], 1)))
        else:
            act, act2, w_mlp2[1] = _matmul(h2, w_mlp1[1], dims="nn", tm=1024, tn=1024, tk=2048, name="mlp1_fwd1",
                                           epilogue="relu2", comm=gather((sb[7], 0)))
            z, x2 = _matmul(act2, w_mlp2[1], dims="nn", tm=1024, tn=1024, tk=2048, name="mlp2_fwd1", epilogue="resgate",
                            extras=(x1, g2))
        saved.append((xi, h, o, y, x1, h2, act, act2, z, mix))
        xi = x2

    dx, loss_tile, st_g2, dz = _loss_head(xi, target, (saved[1][8], mod_mine[1, 5]), "loss_head")
    loss = lax.psum(loss_tile[0, 0], ("x", "y", "c"))

    scatter = lambda *items: _Exchange([("scatter", a, axis) for a, axis in items])
    shares, dmods, dnorm_mix, dnorm_mlp = {}, [None, None], [None, None], [None, None]
    wgrad = lambda a, b, name, tn=1024: _matmul(a, b, dims="tn", tm=2048, tn=tn, tk=2048, name=name, out_dtype=BF16)
    for i in (1, 0):
        sh1, sc1, g1, sh2, sc2, g2 = [mod_mine[i, j] for j in range(N_MOD)]
        xin, h, o, y, x1, h2, act, act2, z, mix = saved[i]
        if i == 1:
            dpre = _matmul(dz, w_mlp2[1], dims="nt", tm=1024, tn=1024, tk=2048, name="mlp2_bwd1", epilogue="mul2a",
                           extras=(act,))
        else:
            dpre, shares["hgrn_w_in"] = _matmul(dz, w_mlp2[0], dims="nt", tm=1024, tn=1024, tk=2048, name="mlp2_bwd0",
                                                epilogue="mul2a", extras=(act,), comm=scatter((g_hgrn_in, 1)))
        g_mlp2 = wgrad(act2, dz, f"mlp2_wgrad{i}")
        dh2 = _matmul(dpre, w_mlp1[i], dims="nt", tm=1024, tn=1024, tk=4096, name=f"mlp1_bwd{i}", out_dtype=BF16)
        g_mlp1 = wgrad(h2, dpre, f"mlp1_wgrad{i}")
        dx1, st_mlp, dy = _norm_mod_bwd(x1, dh2, dx, norm_mlp[i:i + 1], sc2, f"norm_mlp_bwd{i}", below=(y, g1))
        w_out = w_attn_out if i == 0 else w_hgrn_out
        if i == 0:
            qh, kh, vh = mix
            doh = _matmul(dy, w_out, dims="nt", tm=1024, tn=512, tk=2048, name="mix_out_bwd0", epilogue="heads",
                          out_dtype=BF16)
            g_out = _matmul(o, dy, dims="tn", tm=1024, tn=1024, tk=2048, name="mix_out_wgrad0", out_dtype=BF16,
                            a_heads=True)
            dqh, dkh, dvh, dqg, dkg, dsk, shares["mlp_w2_0"], shares["mlp_w1_0"], shares["attn_w_out"] = _attn_bwd(
                qh, kh, vh, doh, attn_q_gain, attn_k_gain, slope_col, sink_col, "attn_bwd",
                comm=scatter((g_mlp2, 0), (g_mlp1, 1), (g_out, 0)))
            dproj = jnp.concatenate([dqh, dkh, dvh], axis=0)
            d_q_gain, d_k_gain, d_sinks = dqg[0:1], dkg[0:1], dsk[:, 0].reshape(1, ATTN_HEADS)
            g_attn_in = _matmul(h, dproj, dims="tn", tm=2048, tn=640, tk=2048, name="mix_in_wgrad0", out_dtype=BF16,
                                b_heads=True)
            g_attn_in = _cols_to_blocks(g_attn_in, attn_w_in.shape[2])
            dh, shares["attn_w_in"] = _matmul(dproj, w_attn_in, dims="nt", tm=1024, tn=1024, tk=2560, out_dtype=BF16,
                                              name="mix_in_bwd0", comm=scatter((g_attn_in, None)), a_heads=True)
        else:
            do = _matmul(dy, w_out, dims="nt", tm=1024, tn=1024, tk=2048, name="mix_out_bwd1")
            g_out = wgrad(o, dy, "mix_out_wgrad1")
            proj, states = mix
            dproj, dlb, d_o_gain, shares["mlp_w2_1"], shares["mlp_w1_1"], shares["hgrn_w_out"] = _hgrn_bwd(
                proj, states, do, lb, og, "hgrn_bwd", comm=scatter((g_mlp2, 0), (g_mlp1, 1), (g_out, 0)))
            d_lb_logits = _lb_bwd(hgrn_lb_logits, dlb.reshape(1, D_MODEL), "lb_bwd")
            dh = _matmul(dproj, w_hgrn_in, dims="nt", tm=1024, tn=1024, tk=2048, name="mix_in_bwd1", out_dtype=BF16)
            g_hgrn_in = wgrad(h, dproj, "mix_in_wgrad1")
        d_gate2 = st_g2[0:1] if i == 1 else st_mix_above[3:4]
        if i == 1:
            dx, st_mix, dz = _norm_mod_bwd(xin, dh, dx1, norm_mix[i:i + 1], sc1, "norm_mix_bwd1",
                                           below=(saved[0][8], mod_mine[0, 5]))
            st_mix_above = st_mix
        else:
            dx, st_mix = _norm_mod_bwd(xin, dh, dx1, norm_mix[i:i + 1], sc1, "norm_mix_bwd0")
        dmods[i] = jnp.concatenate([st_mix[0:1], st_mix[1:2], st_mlp[3:4], st_mlp[0:1], st_mlp[1:2], d_gate2], axis=1)
        dnorm_mix[i], dnorm_mlp[i] = st_mix[2:3], st_mlp[2:3]

    single = {"attn_w_in": (attn_w_in, m_attn_w_in, v_attn_w_in), "attn_w_out": (attn_w_out, m_attn_w_out, v_attn_w_out),
              "hgrn_w_in": (hgrn_w_in, m_hgrn_w_in, v_hgrn_w_in), "hgrn_w_out": (hgrn_w_out, m_hgrn_w_out, v_hgrn_w_out)}
    big = {nm: _sum_adamw(shares[nm], w[0], m[0], v[0], f"adamw_{nm}") for nm, (w, m, v) in single.items()}
    big["mlp_w1"] = _sum_adamw_layers([shares["mlp_w1_0"], shares["mlp_w1_1"]], mlp_w1, m_mlp_w1, v_mlp_w1, "adamw_mlp_w1")
    big["mlp_w2"] = _sum_adamw_layers([shares["mlp_w2_0"], shares["mlp_w2_1"]], mlp_w2, m_mlp_w2, v_mlp_w2, "adamw_mlp_w2")

    small_w = [mod_b, norm_mix, norm_mlp, attn_q_gain, attn_k_gain, attn_sinks, hgrn_o_gain, hgrn_lb_logits]
    small_m = [m_mod_b, m_norm_mix, m_norm_mlp, m_attn_q_gain, m_attn_k_gain, m_attn_sinks, m_hgrn_o_gain, m_hgrn_lb_logits]
    small_v = [v_mod_b, v_norm_mix, v_norm_mlp, v_attn_q_gain, v_attn_k_gain, v_attn_sinks, v_hgrn_o_gain, v_hgrn_lb_logits]
    small_g = [jnp.concatenate(dmods, axis=0), jnp.concatenate(dnorm_mix, axis=0), jnp.concatenate(dnorm_mlp, axis=0),
               d_q_gain, d_k_gain, d_sinks, d_o_gain.reshape(hgrn_o_gain.shape), d_lb_logits]
    packed_g = _pack(small_g)
    pad_rows = (-packed_g.shape[0]) % 8
    pad8 = lambda a: jnp.pad(a, ((0, pad_rows), (0, 0)))
    all_small = _exchange([("gather", pad8(packed_g), None)], vmem=True, name="gather_small_grads")[0]
    sg, sd, sm, sv = _sum_adamw(all_small, pad8(_pack(small_w)), pad8(_pack(small_m)), pad8(_pack(small_v)),
                                "adamw_small")
    small = [_unpack(t, small_w) for t in (sg, sd, sm, sv)]

    n_modb = N_MOD * D_MODEL
    dmod_all = all_small[:, :2 * n_modb // 128, :].reshape(N_DEV, 2, n_modb)
    dmod_cols = lax.dynamic_slice_in_dim(dmod_all, me * n_mod, n_mod, axis=2).transpose(1, 0, 2)
    modw = _mod_w_update(c_all.T, dmod_cols, mod_w, m_mod_w, v_mod_w, "adamw_mod_w")

    def leaf(k):
        one = lambda a: big[a][k][None]
        s = small[k]
        return [modw[k], s[0], s[1], s[2], one("attn_w_in"), one("attn_w_out"), s[3], s[4], s[5], one("hgrn_w_in"),
                one("hgrn_w_out"), s[6], s[7], big["mlp_w1"][k], big["mlp_w2"][k]]

    return (loss, dx[None], *leaf(0), *leaf(1), *leaf(2), *leaf(3))
```

```python
import functools
import math

import jax
import jax.numpy as jnp
from jax import lax
from jax.experimental import pallas as pl
from jax.experimental.pallas import tpu as pltpu

F32, BF16 = jnp.float32, jnp.bfloat16
N_DEV = 8
D_MODEL = 2048
N_MOD = 6
EPS = 1e-6
ATTN_HD, ATTN_HEADS, ATTN_KV, ATTN_GROUP, ATTN_BLOCK = 64, 32, 4, 8, 128
ATTN_SCALE = 1.0 / math.sqrt(ATTN_HD)
HGRN_HEADS, HGRN_DK, HGRN_CHUNK = 16, 128, 64
HGRN_SCALE = 1.0 / math.sqrt(HGRN_DK)
HGRN_CHUNK_SHIFT = HGRN_CHUNK.bit_length() - 1
assert 1 << HGRN_CHUNK_SHIFT == HGRN_CHUNK
D_FF = 4 * D_MODEL
ADAM_LR, ADAM_B1, ADAM_B2, ADAM_EPS, ADAM_WD, ADAM_STEP = 0.001, 0.9, 0.999, 1e-08, 0.01, 10
NEG_BIG = -1e30
VMEM_LIMIT = 56 * 1024 * 1024
MESH_ID = pl.DeviceIdType.MESH


def _params(*sem):
    return pltpu.CompilerParams(dimension_semantics=sem, vmem_limit_bytes=VMEM_LIMIT)


class _Exchange:
    def __init__(self, items):
        self.items = items
        self.n = len(items)
        self.out_shape = []
        for kind, a, axis in items:
            assert kind in ("gather", "gather_by_chip", "scatter")
            if kind != "scatter":
                shape = (N_DEV,) + a.shape if axis is None else tuple(
                    d * N_DEV if i == axis else d for i, d in enumerate(a.shape))
            else:
                shape = a.shape if axis is None else (N_DEV,) + tuple(
                    d // N_DEV if i == axis else d for i, d in enumerate(a.shape))
            self.out_shape.append(jax.ShapeDtypeStruct(shape, a.dtype))
        self.scratch = [pltpu.SemaphoreType.DMA((7 * self.n,)), pltpu.SemaphoreType.DMA((7 * self.n,)),
                        pltpu.SemaphoreType.DMA((self.n,))]
        self.arrays = [a for _, a, _ in items]

    @staticmethod
    def _block(ref, b, axis, size):
        if axis is None:
            return ref.at[b]
        sl = pl.ds(pl.multiple_of(b * size, size), size)
        return ref.at[sl, :] if axis == 0 else ref.at[:, sl]

    def _plan(self, ins, outs, sems):
        send_sems, recv_sems, loc_sems = sems
        x, y, c = lax.axis_index("x"), lax.axis_index("y"), lax.axis_index("c")
        me = 4 * x + 2 * y + c
        begin, middle, end = [], [], []

        def peer(d):
            px = 1 - x if d & 4 else x
            py = 1 - y if d & 2 else y
            pc = 1 - c if d & 1 else c
            return (px, py, pc), 4 * px + 2 * py + pc

        for a, (kind, arr, axis) in enumerate(self.items):
            gather = kind != "scatter"
            size = None if axis is None else (arr.shape[axis] if gather else arr.shape[axis] // N_DEV)

            def src(b):
                return ins[a] if gather else self._block(ins[a], b, axis, size)

            def dst(b):
                return self._block(outs[a], b, axis, size) if gather else outs[a].at[b]

            def remote(src_ref, dst_ref, slot, dev):
                return pltpu.make_async_remote_copy(
                    src_ref=src_ref, dst_ref=dst_ref, send_sem=send_sems.at[a * 7 + slot],
                    recv_sem=recv_sems.at[a * 7 + slot], device_id=dev, device_id_type=MESH_ID)

            local = pltpu.make_async_copy(src(me), dst(me), loc_sems.at[a])
            begin.append(local)
            end.append(local.wait)
            if kind == "gather_by_chip":
                sib_dev, sib_id = peer(1)
                to_sib = remote(ins[a], dst(me), 0, sib_dev)
                begin.append(to_sib)
                end += [to_sib.wait_send, remote(ins[a], dst(sib_id), 0, sib_dev).wait_recv]
                for j, d in enumerate((2, 4, 6)):
                    dev, pid = peer(d)
                    over_ici = remote(ins[a], dst(me), 1 + j, dev)
                    begin.append(over_ici)
                    passed_on = remote(dst(pid), dst(pid), 4 + j, sib_dev)
                    middle.append((remote(ins[a], dst(pid), 1 + j, dev), passed_on))
                    end += [over_ici.wait_send, passed_on.wait_send,
                            remote(ins[a], dst(peer(d ^ 1)[1]), 4 + j, sib_dev).wait_recv]
            else:
                for d in range(1, N_DEV):
                    dev, pid = peer(d)
                    begin.append(remote(src(pid), dst(me), d - 1, dev))
                    end.append(remote(src(pid), dst(pid), d - 1, dev).wait)
        return begin, middle, end

    def start(self, ins, outs, sems):
        for cp in self._plan(ins, outs, sems)[0]:
            cp.start()

    def pass_on(self, ins, outs, sems):
        for arrived, onward in self._plan(ins, outs, sems)[1]:
            arrived.wait_recv()
            onward.start()

    def wait(self, ins, outs, sems):
        for wait in self._plan(ins, outs, sems)[2]:
            wait()


def _call(body, *, name, grid, in_specs, out_specs, out_shape, args, scratch_shapes=(), sem=None, comm=None):
    n_in, n_out, n_scr = len(in_specs), len(out_specs), len(scratch_shapes)
    if comm is None:
        return pl.pallas_call(
            body, name=name, grid=grid, out_shape=list(out_shape), in_specs=list(in_specs), out_specs=list(out_specs),
            scratch_shapes=list(scratch_shapes), compiler_params=_params(*sem))(*args)
    hbm = pl.BlockSpec(memory_space=pltpu.HBM)

    def carrier(*refs):
        bounds = [0, n_in, n_in + comm.n, n_in + comm.n + n_out, n_in + 2 * comm.n + n_out, len(refs) - 3, len(refs)]
        ins, cin, outs, cout, scr, sems = [refs[lo:hi] for lo, hi in zip(bounds[:-1], bounds[1:])]
        assert len(scr) == n_scr
        step = functools.reduce(lambda lin, ax: lin * grid[ax] + pl.program_id(ax), range(len(grid)), 0)
        steps = math.prod(grid)

        @pl.when(step == 0)
        def _():
            comm.start(cin, cout, sems)

        body(*ins, *outs, *scr)

        @pl.when(step == (2 * steps) // 3)
        def _():
            comm.pass_on(cin, cout, sems)

        @pl.when(step == steps - 1)
        def _():
            comm.wait(cin, cout, sems)

    return pl.pallas_call(
        carrier, name=name, grid=grid, out_shape=list(out_shape) + comm.out_shape,
        in_specs=list(in_specs) + [hbm] * comm.n, out_specs=list(out_specs) + [hbm] * comm.n,
        scratch_shapes=list(scratch_shapes) + comm.scratch,
        compiler_params=_params(*["arbitrary"] * len(grid)))(*args, *comm.arrays)


def _exchange(items, *, name, vmem=False):
    comm = _Exchange(items)
    space = pl.BlockSpec(memory_space=pltpu.VMEM if vmem else pltpu.HBM)

    def body(*refs):
        ins, outs, sems = refs[:comm.n], refs[comm.n:2 * comm.n], refs[2 * comm.n:]
        comm.start(ins, outs, sems)
        comm.pass_on(ins, outs, sems)
        comm.wait(ins, outs, sems)

    return pl.pallas_call(
        body, name=name, out_shape=comm.out_shape, in_specs=[space] * comm.n, out_specs=[space] * comm.n,
        scratch_shapes=comm.scratch, compiler_params=pltpu.CompilerParams(vmem_limit_bytes=VMEM_LIMIT))(*comm.arrays)


_DIMS = {"nn": (((1,), (0,)), ((), ())), "nt": (((1,), (1,)), ((), ())), "tn": (((0,), (0,)), ((), ()))}


def _matmul(a, b, *, dims, tm, tn, tk, name, epilogue="plain", out_dtype=F32, extras=(), comm=None,
            a_heads=False, b_heads=False):
    a_parts = a.shape[0] if a.ndim == 3 else 0
    b_parts = b.shape[0] if b.ndim == 3 else 0
    assert not (a_parts and dims != "nt" and not a_heads) and not (b_parts and dims != "tn")
    a2 = (a.shape[1], a.shape[2] * a_parts) if a_parts else a.shape
    b2 = (b.shape[1], b.shape[2] * b_parts) if b_parts else b.shape
    if dims == "tn":
        (K, M), N = a2, b2[1]
    else:
        (M, K), N = a2, (b2[1] if dims == "nn" else b2[0])
    tm, tn, tk = min(tm, M), min(tn, N), min(tk, K)
    assert M % tm == 0 and N % tn == 0 and K % tk == 0, (name, M, N, K, tm, tn, tk)
    if a_heads and dims == "tn":
        a_spec = pl.BlockSpec((tm // ATTN_HD, tk, ATTN_HD), lambda i, j, k: (i, k, 0))
    elif a_heads:
        a_spec = pl.BlockSpec((tk // ATTN_HD, tm, ATTN_HD), lambda i, j, k: (k, i, 0))
    elif a_parts:
        per = K // a_parts // tk
        a_spec = pl.BlockSpec((None, tm, tk), lambda i, j, k: (k // per, i, k % per))
    elif dims == "tn":
        a_spec = pl.BlockSpec((tk, tm), lambda i, j, k: (k, i))
    else:
        a_spec = pl.BlockSpec((tm, tk), lambda i, j, k: (i, k))
    if b_heads:
        b_spec = pl.BlockSpec((tn // ATTN_HD, tk, ATTN_HD), lambda i, j, k: (j, k, 0))
    elif b_parts:
        per_n = N // b_parts // tn
        b_spec = pl.BlockSpec((None, tk, tn), lambda i, j, k: (j // per_n, k, j % per_n))
    elif dims == "nt":
        b_spec = pl.BlockSpec((tn, tk), lambda i, j, k: (j, k))
    else:
        b_spec = pl.BlockSpec((tk, tn), lambda i, j, k: (k, j))
    side_by_side = lambda ref: jnp.concatenate([ref[g] for g in range(ref.shape[0])], axis=1)
    nk = K // tk
    tile = pl.BlockSpec((tm, tn), lambda i, j, k: (i, j))
    row = pl.BlockSpec((1, tn), lambda i, j, k: (0, j))
    if epilogue == "plain":
        extra_specs, out_shape, out_specs = [], [jax.ShapeDtypeStruct((M, N), out_dtype)], [tile]
    elif epilogue == "relu2":
        extra_specs, out_shape, out_specs = [], [jax.ShapeDtypeStruct((M, N), BF16)] * 2, [tile, tile]
    elif epilogue == "resgate":
        extra_specs, out_specs = [tile, row], [tile, tile]
        out_shape = [jax.ShapeDtypeStruct((M, N), BF16), jax.ShapeDtypeStruct((M, N), F32)]
    elif epilogue == "mul2a":
        extra_specs, out_shape, out_specs = [tile], [jax.ShapeDtypeStruct((M, N), BF16)], [tile]
    elif epilogue == "heads":
        per_tile = tn // ATTN_HD
        q_tiles = ATTN_HEADS // per_tile
        assert tn == 2 * ATTN_KV * ATTN_HD and N // tn in (q_tiles, q_tiles + 1)
        extra_specs = []
        out_shape = [jax.ShapeDtypeStruct((ATTN_HEADS, M, ATTN_HD), out_dtype)]
        out_specs = [pl.BlockSpec((per_tile, tm, ATTN_HD), lambda i, j, k: (jnp.minimum(j, q_tiles - 1), i, 0))]
        if N // tn > q_tiles:
            out_shape += [jax.ShapeDtypeStruct((ATTN_KV, M, ATTN_HD), out_dtype)] * 2
            out_specs += [pl.BlockSpec((ATTN_KV, tm, ATTN_HD), lambda i, j, k: (0, i, 0))] * 2
    else:
        raise ValueError(epilogue)
    n_extra = len(extra_specs)

    def body(a_ref, b_ref, *rest):
        ex, outs, acc_ref = rest[:n_extra], rest[n_extra:-1], rest[-1]
        k = pl.program_id(2)

        @pl.when(k == 0)
        def _():
            acc_ref[...] = jnp.zeros_like(acc_ref)

        a_tile = side_by_side(a_ref) if a_heads else a_ref[...]
        b_tile = side_by_side(b_ref) if b_heads else b_ref[...]
        acc_ref[...] += lax.dot_general(a_tile, b_tile, _DIMS[dims], preferred_element_type=F32)

        if epilogue == "heads":
            j = pl.program_id(1)

            @pl.when((k == nk - 1) & (j < q_tiles))
            def _():
                for g in range(per_tile):
                    outs[0][g] = acc_ref[:, g * ATTN_HD:(g + 1) * ATTN_HD].astype(out_dtype)

            if len(outs) > 1:
                @pl.when((k == nk - 1) & (j == q_tiles))
                def _():
                    for g in range(ATTN_KV):
                        outs[1][g] = acc_ref[:, g * ATTN_HD:(g + 1) * ATTN_HD].astype(out_dtype)
                        outs[2][g] = acc_ref[:, (ATTN_KV + g) * ATTN_HD:(ATTN_KV + g + 1) * ATTN_HD].astype(out_dtype)
            return

        @pl.when(k == nk - 1)
        def _():
            acc = acc_ref[...]
            if epilogue == "plain":
                outs[0][...] = acc.astype(out_dtype)
            elif epilogue == "relu2":
                act = jnp.maximum(acc, 0.0)
                outs[0][...] = act.astype(BF16)
                outs[1][...] = (act * act).astype(BF16)
            elif epilogue == "resgate":
                outs[0][...] = acc.astype(BF16)
                outs[1][...] = ex[0][...] + ex[1][...] * acc
            else:
                outs[0][...] = (acc * (2.0 * ex[0][...].astype(F32))).astype(BF16)

    res = _call(body, name=name, grid=(M // tm, N // tn, nk), out_shape=out_shape,
                in_specs=[a_spec, b_spec] + extra_specs, out_specs=out_specs,
                scratch_shapes=[pltpu.VMEM((tm, tn), F32)],
                sem=("parallel", "arbitrary" if epilogue == "heads" else "parallel", "arbitrary"),
                args=(a, b, *extras), comm=comm)
    return res[0] if len(res) == 1 else res


def _row_tile(T):
    return min(T, 256)


def _norm_mod_fwd(x, gain, sc, sh, name, comm=None):
    T, D = x.shape
    tr = _row_tile(T)

    def body(x_ref, g_ref, sc_ref, sh_ref, h_ref):
        xv = x_ref[...]
        r = lax.rsqrt(jnp.mean(xv * xv, axis=-1, keepdims=True) + EPS)
        hn = (xv * r) * g_ref[...]
        h_ref[...] = (hn * (1.0 + sc_ref[...]) + sh_ref[...]).astype(BF16)

    vec = pl.BlockSpec((1, D), lambda i: (0, 0))
    res = _call(body, name=name, grid=(T // tr,), out_shape=[jax.ShapeDtypeStruct((T, D), BF16)],
                in_specs=[pl.BlockSpec((tr, D), lambda i: (i, 0)), vec, vec, vec],
                out_specs=[pl.BlockSpec((tr, D), lambda i: (i, 0))], sem=("parallel",), args=(x, gain, sc, sh),
                comm=comm)
    return res[0] if comm is None else res


def _through_gate(dx, branch_ref, gate_ref, dbranch_ref, st_ref, row):
    dbranch_ref[...] = (dx * gate_ref[...]).astype(BF16)
    st_ref[row:row + 1, :] += jnp.sum(dx * branch_ref[...].astype(F32), axis=0, keepdims=True)


def _norm_mod_bwd(x, dh, dres, gain, sc, name, below=None):
    T, D = x.shape
    tr = _row_tile(T)

    def body(x_ref, dh_ref, dres_ref, g_ref, sc_ref, *rest):
        dx_ref, st_ref = rest[-3:-1] if below else rest[-2:]
        xv, dh_v, gain_v = x_ref[...], dh_ref[...].astype(F32), g_ref[...]
        r = lax.rsqrt(jnp.mean(xv * xv, axis=-1, keepdims=True) + EPS)
        xn = xv * r
        hn = xn * gain_v
        dhn = dh_v * (1.0 + sc_ref[...])
        dxn = dhn * gain_v
        dx = dres_ref[...] + r * (dxn - xn * jnp.mean(dxn * xn, axis=-1, keepdims=True))
        dx_ref[...] = dx

        @pl.when(pl.program_id(0) == 0)
        def _():
            st_ref[...] = jnp.zeros_like(st_ref)

        st_ref[0:1, :] += jnp.sum(dh_v, axis=0, keepdims=True)
        st_ref[1:2, :] += jnp.sum(dh_v * hn, axis=0, keepdims=True)
        st_ref[2:3, :] += jnp.sum(dhn * xn, axis=0, keepdims=True)
        if below:
            _through_gate(dx, rest[0], rest[1], rest[-1], st_ref, 3)

    vec = pl.BlockSpec((1, D), lambda i: (0, 0))
    blk = pl.BlockSpec((tr, D), lambda i: (i, 0))
    return pl.pallas_call(
        body, name=name, grid=(T // tr,),
        out_shape=[jax.ShapeDtypeStruct((T, D), F32), jax.ShapeDtypeStruct((8, D), F32)]
        + ([jax.ShapeDtypeStruct((T, D), BF16)] if below else []),
        in_specs=[blk, blk, blk, vec, vec] + ([blk, vec] if below else []),
        out_specs=[blk, pl.BlockSpec((8, D), lambda i: (0, 0))] + ([blk] if below else []),
        compiler_params=_params("arbitrary"),
    )(x, dh, dres, gain, sc, *(below or ()))


def _loss_head(y, target, below, name):
    T, D = y.shape
    tr = _row_tile(T)

    def body(y_ref, t_ref, b_ref, g_ref, dy_ref, l_ref, st_ref, db_ref):
        err = y_ref[...] - t_ref[...]
        dy = err * (1.0 / D)
        dy_ref[...] = dy

        @pl.when(pl.program_id(0) == 0)
        def _():
            l_ref[...] = jnp.zeros_like(l_ref)
            st_ref[...] = jnp.zeros_like(st_ref)

        part = jnp.sum(jnp.mean(err * err, axis=-1, keepdims=True), axis=0, keepdims=True)
        l_ref[...] += jnp.broadcast_to(0.5 * part, l_ref.shape)
        _through_gate(dy, b_ref, g_ref, db_ref, st_ref, 0)

    blk = pl.BlockSpec((tr, D), lambda i: (i, 0))
    return pl.pallas_call(
        body, name=name, grid=(T // tr,),
        out_shape=[jax.ShapeDtypeStruct((T, D), F32), jax.ShapeDtypeStruct((8, 128), F32),
                   jax.ShapeDtypeStruct((8, D), F32), jax.ShapeDtypeStruct((T, D), BF16)],
        in_specs=[blk, blk, blk, pl.BlockSpec((1, D), lambda i: (0, 0))],
        out_specs=[blk, pl.BlockSpec((8, 128), lambda i: (0, 0)), pl.BlockSpec((8, D), lambda i: (0, 0)), blk],
        compiler_params=_params("arbitrary"),
    )(y, target, *below)


def _silu(v):
    return v * jax.nn.sigmoid(v)


def _mod_fwd(c_all, mod_w, mod_b_cols, name):
    L, D, n = mod_w.shape
    tn = 512

    def body(c_ref, w_ref, b_ref, o_ref):
        cond = _silu(c_ref[...]).astype(BF16)
        o_ref[0] = jnp.dot(cond, w_ref[0].astype(BF16), preferred_element_type=F32) + b_ref[0]

    return pl.pallas_call(
        body, name=name, grid=(L, n // tn), out_shape=jax.ShapeDtypeStruct((L, N_DEV, n), F32),
        in_specs=[pl.BlockSpec((N_DEV, D), lambda l, j: (0, 0)), pl.BlockSpec((1, D, tn), lambda l, j: (l, 0, j)),
                  pl.BlockSpec((1, 1, tn), lambda l, j: (l, 0, j))],
        out_specs=pl.BlockSpec((1, N_DEV, tn), lambda l, j: (l, 0, j)),
        compiler_params=_params("parallel", "parallel"),
    )(c_all, mod_w, mod_b_cols)


def _adamw(g, w, m, v):
    m = ADAM_B1 * m + (1.0 - ADAM_B1) * g
    v = ADAM_B2 * v + (1.0 - ADAM_B2) * (g * g)
    m_hat = m / (1.0 - ADAM_B1 ** ADAM_STEP)
    v_hat = v / (1.0 - ADAM_B2 ** ADAM_STEP)
    delta = -ADAM_LR * (m_hat / (jnp.sqrt(v_hat) + ADAM_EPS) + ADAM_WD * w)
    return delta, m, v


def _mod_w_update(c_t, dmod_cols, w, m, v, name):
    L, D, n = w.shape
    tr = 256

    def body(c_ref, dm_ref, w_ref, m_ref, v_ref, g_ref, d_ref, nm_ref, nv_ref):
        cond = _silu(c_ref[...])
        dm = dm_ref[0]
        g = cond[:, 0:1] * dm[0:1, :]
        for b in range(1, N_DEV):
            g = g + cond[:, b:b + 1] * dm[b:b + 1, :]
        delta, nm, nv = _adamw(g, w_ref[0], m_ref[0], v_ref[0])
        g_ref[0], d_ref[0], nm_ref[0], nv_ref[0] = g, delta, nm, nv

    blk = pl.BlockSpec((1, tr, n), lambda l, i: (l, i, 0))
    return pl.pallas_call(
        body, name=name, grid=(L, D // tr), out_shape=[jax.ShapeDtypeStruct(w.shape, F32)] * 4,
        in_specs=[pl.BlockSpec((tr, N_DEV), lambda l, i: (i, 0)), pl.BlockSpec((1, N_DEV, n), lambda l, i: (l, 0, 0)),
                  blk, blk, blk],
        out_specs=[blk] * 4, compiler_params=_params("parallel", "parallel"),
    )(c_t, dmod_cols, w, m, v)


def _sum_adamw(parts, w, m, v, name):
    R, C = w.shape
    tr = min(R, 256 if C >= 1024 else 1024)
    assert R % tr == 0

    def body(p_ref, w_ref, m_ref, v_ref, g_ref, d_ref, nm_ref, nv_ref):
        g = p_ref[0].astype(F32)
        for s in range(1, N_DEV):
            g = g + p_ref[s].astype(F32)
        delta, nm, nv = _adamw(g, w_ref[...], m_ref[...], v_ref[...])
        g_ref[...], d_ref[...], nm_ref[...], nv_ref[...] = g, delta, nm, nv

    blk = pl.BlockSpec((tr, C), lambda i: (i, 0))
    return pl.pallas_call(
        body, name=name, grid=(R // tr,), out_shape=[jax.ShapeDtypeStruct((R, C), F32)] * 4,
        in_specs=[pl.BlockSpec((N_DEV, tr, C), lambda i: (0, i, 0)), blk, blk, blk], out_specs=[blk] * 4,
        compiler_params=_params("parallel"),
    )(parts, w, m, v)


def _sum_adamw_layers(parts, w, m, v, name):
    L, R, C = w.shape
    tr = min(R, 256 * 1024 // C)
    assert R % tr == 0 and len(parts) == L
    ni = R // tr

    def body(*refs):
        p_refs, (w_ref, m_ref, v_ref), outs = refs[:L], refs[L:L + 3], refs[L + 3:]
        for layer in range(L):
            @pl.when(pl.program_id(0) == layer)
            def _(p_ref=p_refs[layer]):
                g = p_ref[0].astype(F32)
                for s in range(1, N_DEV):
                    g = g + p_ref[s].astype(F32)
                delta, nm, nv = _adamw(g, w_ref[...], m_ref[...], v_ref[...])
                for o_ref, val in zip(outs, (g, delta, nm, nv)):
                    o_ref[...] = val

    def shares(layer):
        park = 0 if layer else ni - 1
        return pl.BlockSpec((N_DEV, tr, C), lambda l, i: (0, jnp.where(l == layer, i, park), 0))

    blk = pl.BlockSpec((None, tr, C), lambda l, i: (l, i, 0))
    return pl.pallas_call(
        body, name=name, grid=(L, ni), out_shape=[jax.ShapeDtypeStruct((L, R, C), F32)] * 4,
        in_specs=[shares(layer) for layer in range(L)] + [blk] * 3, out_specs=[blk] * 4,
        compiler_params=_params("arbitrary", "arbitrary"),
    )(*parts, w, m, v)


def _lower_bound_row1(l0, l1):
    mx = lax.stop_gradient(jnp.maximum(l0, l1))
    e0, e1 = jnp.exp(l0 - mx), jnp.exp(l1 - mx)
    p0, p1 = e0 / (e0 + e1), e1 / (e0 + e1)
    return (p0 + p1) - p0


def _lb_fwd(logits, name):
    def body(l_ref, o_ref):
        o_ref[...] = _lower_bound_row1(l_ref[0:1, :], l_ref[1:2, :])

    return pl.pallas_call(body, name=name, out_shape=jax.ShapeDtypeStruct((1, logits.shape[1]), F32))(logits)


def _lb_bwd(logits, dlb, name):
    def body(l_ref, d_ref, o_ref):
        _, vjp = jax.vjp(_lower_bound_row1, l_ref[0:1, :], l_ref[1:2, :])
        d0, d1 = vjp(d_ref[...])
        o_ref[0:1, :] = d0
        o_ref[1:2, :] = d1

    return pl.pallas_call(body, name=name, out_shape=jax.ShapeDtypeStruct(logits.shape, F32))(logits, dlb)


def _bdot(a, b, dims):
    return lax.dot_general(a.astype(BF16), b.astype(BF16), _DIMS[dims], preferred_element_type=F32)


def _rms(x, gain):
    r = lax.rsqrt(jnp.mean(x * x, axis=-1, keepdims=True) + EPS)
    xhat = x * r
    return xhat, r, xhat * gain


def _attn_band(n):
    qi = lax.broadcasted_iota(jnp.int32, (ATTN_BLOCK, 2 * ATTN_BLOCK), 0)
    ki = lax.broadcasted_iota(jnp.int32, (ATTN_BLOCK, 2 * ATTN_BLOCK), 1)
    dist = qi + ATTN_BLOCK - ki
    first_key = jnp.where(n > 0, 0, ATTN_BLOCK)
    valid = (dist >= 0) & (dist < ATTN_BLOCK) & (ki >= first_key)
    return valid, jnp.abs(dist).astype(F32)


def _attn_head_probs(qn, kn_b, valid, absdist, slope, sink):
    s = lax.dot_general(qn.astype(BF16), kn_b, _DIMS["nt"], preferred_element_type=F32) * ATTN_SCALE
    s = jnp.where(valid, s - slope * absdist, NEG_BIG)
    mx = jnp.maximum(jnp.max(s, axis=-1, keepdims=True), sink)
    e = jnp.exp(s - mx)
    es = jnp.exp(sink - mx)
    inv = 1.0 / (jnp.sum(e, axis=-1, keepdims=True) + es)
    return e * inv, es * inv


def _attn_specs(T):
    nb = T // ATTN_BLOCK
    qspec = pl.BlockSpec((ATTN_GROUP, ATTN_BLOCK, ATTN_HD), lambda h, n: (h, n, 0))
    prev = pl.BlockSpec((1, ATTN_BLOCK, ATTN_HD), lambda h, n: (h, jnp.maximum(n - 1, 0), 0))
    cur = pl.BlockSpec((1, ATTN_BLOCK, ATTN_HD), lambda h, n: (h, n, 0))
    gain = pl.BlockSpec((1, ATTN_HD), lambda h, n: (0, 0))
    scalars = pl.BlockSpec(memory_space=pltpu.SMEM)
    return nb, qspec, prev, cur, gain, scalars


def _attn_fwd(qh, kh, vh, qg, kg, slopes, sinks, name, comm=None):
    T = qh.shape[1]
    nb, qspec, prev, cur, gain, scalars = _attn_specs(T)

    def body(q_ref, kp_ref, kc_ref, vp_ref, vc_ref, qg_ref, kg_ref, sl_ref, sk_ref, o_ref):
        h, n = pl.program_id(0), pl.program_id(1)
        valid, absdist = _attn_band(n)
        _, _, kn = _rms(jnp.concatenate([kp_ref[0], kc_ref[0]], axis=0), kg_ref[...])
        kn_b = kn.astype(BF16)
        v_b = jnp.concatenate([vp_ref[0], vc_ref[0]], axis=0).astype(BF16)
        for g in range(ATTN_GROUP):
            head = h * ATTN_GROUP + g
            _, _, qn = _rms(q_ref[g], qg_ref[...])
            p, _ = _attn_head_probs(qn, kn_b, valid, absdist, sl_ref[head], sk_ref[head])
            o_ref[g] = jnp.dot(p.astype(BF16), v_b, preferred_element_type=F32).astype(BF16)

    return _call(body, name=name, grid=(ATTN_KV, nb), out_shape=[jax.ShapeDtypeStruct(qh.shape, BF16)],
                 in_specs=[qspec, prev, cur, prev, cur, gain, gain, scalars, scalars], out_specs=[qspec],
                 sem=("parallel", "parallel"), args=(qh, kh, kh, vh, vh, qg, kg, slopes, sinks), comm=comm)


def _rms_bwd(dy, xhat, r, gain):
    dxh = dy * gain
    dx = r * (dxh - xhat * jnp.mean(dxh * xhat, axis=-1, keepdims=True))
    return dx, jnp.sum(dy * xhat, axis=0, keepdims=True)


def _attn_bwd(qh, kh, vh, doh, qg, kg, slope_col, sink_col, name, comm=None):
    T = qh.shape[1]
    nb = T // ATTN_BLOCK
    rows = ATTN_GROUP * ATTN_BLOCK

    def body(q_ref, kp_ref, kc_ref, vp_ref, vc_ref, do_ref, qg_ref, kg_ref, sl_ref, sk_ref,
             dq_ref, dk_ref, dv_ref, dqg_ref, dkg_ref, dsk_ref, carry_ref, sk_acc):
        h, step = pl.program_id(0), pl.program_id(1)
        n = nb - 1 - step
        qg_v, kg_v = qg_ref[...], kg_ref[...]
        qhat, rq, qn = _rms(q_ref[...].reshape(rows, ATTN_HD), qg_v)
        khat, rk, kn = _rms(jnp.concatenate([kp_ref[0], kc_ref[0]], axis=0), kg_v)
        s = _bdot(qn, kn, "nt") * ATTN_SCALE
        qi = lax.broadcasted_iota(jnp.int32, s.shape, 0) & (ATTN_BLOCK - 1)
        ki = lax.broadcasted_iota(jnp.int32, s.shape, 1)
        dist = qi + ATTN_BLOCK - ki
        first_key = jnp.where(n > 0, 0, ATTN_BLOCK)
        valid = (dist >= 0) & (dist < ATTN_BLOCK) & (ki >= first_key)
        s = jnp.where(valid, s - sl_ref[0] * jnp.abs(dist).astype(F32), NEG_BIG)
        sink = sk_ref[0]
        mx = jnp.maximum(jnp.max(s, axis=-1, keepdims=True), sink)
        e = jnp.exp(s - mx)
        es = jnp.exp(sink - mx)
        den = jnp.sum(e, axis=-1, keepdims=True) + es
        p, ps = e / den, es / den
        v = jnp.concatenate([vp_ref[0], vc_ref[0]], axis=0)
        do = do_ref[...].reshape(rows, ATTN_HD)
        dp = _bdot(do, v, "nt")
        delta = jnp.sum(p * dp, axis=-1, keepdims=True)
        ds = p * (dp - delta)
        dqn = _bdot(ds, kn, "nn") * ATTN_SCALE
        dkn = _bdot(ds, qn, "tn") * ATTN_SCALE
        dv = _bdot(p, do, "tn")
        dq, dqg = _rms_bwd(dqn, qhat, rq, qg_v)
        dk, dkg = _rms_bwd(dkn, khat, rk, kg_v)
        dq_ref[...] = dq.reshape(ATTN_GROUP, ATTN_BLOCK, ATTN_HD).astype(BF16)

        @pl.when((h == 0) & (step == 0))
        def _():
            dqg_ref[...] = jnp.zeros_like(dqg_ref)
            dkg_ref[...] = jnp.zeros_like(dkg_ref)

        dqg_ref[0:1, :] += dqg
        dkg_ref[0:1, :] += dkg

        @pl.when(step == 0)
        def _():
            carry_ref[...] = jnp.zeros_like(carry_ref)
            sk_acc[...] = jnp.zeros_like(sk_acc)

        dk_ref[0] = (dk[ATTN_BLOCK:, :] + carry_ref[0]).astype(BF16)
        dv_ref[0] = (dv[ATTN_BLOCK:, :] + carry_ref[1]).astype(BF16)
        carry_ref[0] = dk[:ATTN_BLOCK, :]
        carry_ref[1] = dv[:ATTN_BLOCK, :]
        sk_acc[...] += -ps * delta

        @pl.when(step == nb - 1)
        def _():
            for g in range(ATTN_GROUP):
                tot = jnp.sum(sk_acc[g * ATTN_BLOCK:(g + 1) * ATTN_BLOCK, :], axis=0, keepdims=True)
                dsk_ref[g:g + 1, :] = jnp.broadcast_to(tot, (1, 128))

    qspec = pl.BlockSpec((ATTN_GROUP, ATTN_BLOCK, ATTN_HD), lambda h, s: (h, nb - 1 - s, 0))
    prev = pl.BlockSpec((1, ATTN_BLOCK, ATTN_HD), lambda h, s: (h, jnp.maximum(nb - 2 - s, 0), 0))
    cur = pl.BlockSpec((1, ATTN_BLOCK, ATTN_HD), lambda h, s: (h, nb - 1 - s, 0))
    gain = pl.BlockSpec((1, ATTN_HD), lambda h, s: (0, 0))
    col = pl.BlockSpec((1, rows, 1), lambda h, s: (h, 0, 0))
    acc = pl.BlockSpec((8, ATTN_HD), lambda h, s: (0, 0))
    return _call(
        body, name=name, grid=(ATTN_KV, nb),
        out_shape=[jax.ShapeDtypeStruct(qh.shape, BF16), jax.ShapeDtypeStruct(kh.shape, BF16),
                   jax.ShapeDtypeStruct(kh.shape, BF16), jax.ShapeDtypeStruct((8, ATTN_HD), F32),
                   jax.ShapeDtypeStruct((8, ATTN_HD), F32), jax.ShapeDtypeStruct((ATTN_HEADS, 128), F32)],
        in_specs=[qspec, prev, cur, prev, cur, qspec, gain, gain, col, col],
        out_specs=[qspec, cur, cur, acc, acc, pl.BlockSpec((ATTN_GROUP, 128), lambda h, s: (h, 0))],
        scratch_shapes=[pltpu.VMEM((2, ATTN_BLOCK, ATTN_HD), F32), pltpu.VMEM((rows, 1), F32)],
        sem=("arbitrary", "arbitrary"), args=(qh, kh, kh, vh, vh, doh, qg, kg, slope_col, sink_col), comm=comm)


@functools.partial(jax.custom_vjp, nondiff_argnums=(2,))
def _mm(a, b, dims):
    return _bdot(a, b, dims)


def _mm_fwd(a, b, dims):
    return _bdot(a, b, dims), (a, b)


def _mm_bwd(dims, res, ct):
    a, b = res
    if dims == "nn":
        return _bdot(ct, b, "nt"), _bdot(a, ct, "tn")
    if dims == "nt":
        return _bdot(ct, b, "nn"), _bdot(ct, a, "tn")
    return _bdot(b, ct, "nt"), _bdot(a, ct, "nn")


_mm.defvjp(_mm_fwd, _mm_bwd)


def _same_chunk_mask(rows, upper):
    ri = lax.broadcasted_iota(jnp.int32, (rows, rows), 0)
    ci = lax.broadcasted_iota(jnp.int32, (rows, rows), 1)
    same = (ri >> HGRN_CHUNK_SHIFT) == (ci >> HGRN_CHUNK_SHIFT)
    return same & ((ri <= ci) if upper else (ri >= ci))


@jax.custom_vjp
def _chunk_cumsum(x, tri, tri_t):
    hi = x.astype(BF16)
    rest = x - hi.astype(F32)
    mid = rest.astype(BF16)
    lo = (rest - mid.astype(F32)).astype(BF16)
    out = jnp.dot(tri, jnp.concatenate([hi, mid, lo], axis=1), preferred_element_type=F32)
    w = x.shape[1]
    return out[:, :w] + out[:, w:2 * w] + out[:, 2 * w:]


def _chunk_cumsum_fwd(x, tri, tri_t):
    return _chunk_cumsum(x, tri, tri_t), (tri, tri_t)


def _chunk_cumsum_bwd(res, ct):
    tri, tri_t = res
    return _chunk_cumsum(ct, tri_t, tri), jnp.zeros_like(tri), jnp.zeros_like(tri_t)


_chunk_cumsum.defvjp(_chunk_cumsum_fwd, _chunk_cumsum_bwd)


def _hgrn_masks(rows):
    nc = rows // HGRN_CHUNK
    lower = _same_chunk_mask(rows, False)
    chunk_of_row = lax.broadcasted_iota(jnp.int32, (rows, HGRN_DK), 0) >> HGRN_CHUNK_SHIFT
    row_in_chunk = lax.broadcasted_iota(jnp.int32, (nc, HGRN_CHUNK, HGRN_DK), 1)
    return dict(lower=lower, tri=lower.astype(BF16), tri_t=_same_chunk_mask(rows, True).astype(BF16),
                in_chunk=[chunk_of_row == c for c in range(nc)],
                mid_row=row_in_chunk == HGRN_CHUNK // 2 - 1, last_row=row_in_chunk == HGRN_CHUNK - 1)


def _hgrn_block(masks, st, qr, fr, v, gr, lb, og):
    rows = qr.shape[0]
    nc = rows // HGRN_CHUNK
    per_chunk = lambda m: m.reshape(nc, HGRN_CHUNK, HGRN_DK)
    flat = lambda m: m.reshape(rows, HGRN_DK)
    by_chunk = lambda m: jnp.concatenate([jnp.where(masks["in_chunk"][c], m, 0.0) for c in range(nc)], axis=1)

    forget = lb + (1.0 - lb) * jax.nn.sigmoid(fr)
    k = 1.0 - forget
    b = _chunk_cumsum(jnp.log(forget), masks["tri"], masks["tri_t"])
    b3 = per_chunk(b)
    piv = jnp.sum(jnp.where(masks["mid_row"], b3, 0.0), axis=1, keepdims=True)
    b_last = jnp.sum(jnp.where(masks["last_row"], b3, 0.0), axis=1, keepdims=True)
    q = _silu(qr) * HGRN_SCALE
    a = _mm(q * flat(jnp.exp(b3 - piv)), k * flat(jnp.exp(piv - b3)), "nt")
    o = _mm(jnp.where(masks["lower"], a, 0.0), v, "nn")
    updates = _mm(v, by_chunk(k * flat(jnp.exp(b_last - b3))), "tn")
    decay = jnp.exp(b_last)
    before = []
    for c in range(nc):
        before.append(st)
        st = st * decay[c] + updates[:, c * HGRN_DK:(c + 1) * HGRN_DK]
    o = o + _mm(by_chunk(q * jnp.exp(b)), jnp.concatenate(before, axis=1), "nt")
    y = (o * lax.rsqrt(jnp.mean(o * o, axis=-1, keepdims=True) + EPS)) * og * _silu(gr)
    return y, st


def _hgrn_tile(T):
    return min(T, 256)


HGRN_HEADS_PER_STEP = 16
HGRN_GROUPS = HGRN_HEADS // HGRN_HEADS_PER_STEP


def _hgrn_fwd(proj, lb, og, name, comm=None):
    T = proj.shape[0]
    tb = _hgrn_tile(T)
    hp, wide = HGRN_HEADS_PER_STEP, HGRN_HEADS_PER_STEP * HGRN_DK

    def body(q_ref, f_ref, v_ref, g_ref, lb_ref, og_ref, o_ref, s_ref, st_ref):
        @pl.when(pl.program_id(1) == 0)
        def _():
            st_ref[...] = jnp.zeros_like(st_ref)

        masks = _hgrn_masks(tb)
        for j in range(hp):
            ln = slice(j * HGRN_DK, (j + 1) * HGRN_DK)
            st = st_ref[j]
            s_ref[j, 0] = st
            y, st_ref[j] = _hgrn_block(masks, st, q_ref[:, ln], f_ref[:, ln], v_ref[:, ln], g_ref[:, ln], lb_ref[j],
                                       og_ref[j])
            o_ref[:, ln] = y.astype(BF16)

    part = lambda p: pl.BlockSpec((tb, wide), lambda h, t: (t, p * HGRN_GROUPS + h))
    vec = pl.BlockSpec((hp, 1, HGRN_DK), lambda h, t: (h, 0, 0))
    return _call(
        body, name=name, grid=(HGRN_GROUPS, T // tb),
        out_shape=[jax.ShapeDtypeStruct((T, D_MODEL), BF16),
                   jax.ShapeDtypeStruct((HGRN_HEADS, T // tb, HGRN_DK, HGRN_DK), F32)],
        in_specs=[part(0), part(1), part(2), part(3), vec, vec],
        out_specs=[pl.BlockSpec((tb, wide), lambda h, t: (t, h)),
                   pl.BlockSpec((hp, 1, HGRN_DK, HGRN_DK), lambda h, t: (h, t, 0, 0))],
        scratch_shapes=[pltpu.VMEM((hp, HGRN_DK, HGRN_DK), F32)], sem=("parallel", "arbitrary"),
        args=(proj, proj, proj, proj, lb, og), comm=comm)


def _hgrn_bwd(proj, states, do, lb, og, name, comm=None):
    T = proj.shape[0]
    tb = _hgrn_tile(T)
    nt, hp, wide = T // tb, HGRN_HEADS_PER_STEP, HGRN_HEADS_PER_STEP * HGRN_DK

    def body(q_ref, f_ref, v_ref, g_ref, s_ref, do_ref, lb_ref, og_ref, dp_ref, dlb_ref, dog_ref, dst_ref):
        @pl.when(pl.program_id(1) == 0)
        def _():
            dst_ref[...] = jnp.zeros_like(dst_ref)
            dlb_ref[...] = jnp.zeros_like(dlb_ref)
            dog_ref[...] = jnp.zeros_like(dog_ref)

        block = functools.partial(_hgrn_block, _hgrn_masks(tb))
        for j in range(hp):
            ln = slice(j * HGRN_DK, (j + 1) * HGRN_DK)
            _, vjp = jax.vjp(block, s_ref[j, 0], q_ref[:, ln], f_ref[:, ln], v_ref[:, ln], g_ref[:, ln],
                             lb_ref[j], og_ref[j])
            dst_ref[j], dq, df, dv, dg, dlb, dog = vjp((do_ref[:, ln], dst_ref[j]))
            for p, part_grad in enumerate((dq, df, dv, dg)):
                dp_ref[p, :, ln] = part_grad.astype(BF16)
            dlb_ref[j] += dlb
            dog_ref[j] += dog

    part = lambda p: pl.BlockSpec((tb, wide), lambda h, t: (nt - 1 - t, p * HGRN_GROUPS + h))
    vec = pl.BlockSpec((hp, 1, HGRN_DK), lambda h, t: (h, 0, 0))
    head = pl.BlockSpec((tb, wide), lambda h, t: (nt - 1 - t, h))
    return _call(
        body, name=name, grid=(HGRN_GROUPS, nt),
        out_shape=[jax.ShapeDtypeStruct((4, T, D_MODEL), BF16)] + [jax.ShapeDtypeStruct((HGRN_HEADS, 1, HGRN_DK), F32)] * 2,
        in_specs=[part(0), part(1), part(2), part(3),
                  pl.BlockSpec((hp, 1, HGRN_DK, HGRN_DK), lambda h, t: (h, nt - 1 - t, 0, 0)), head, vec, vec],
        out_specs=[pl.BlockSpec((4, tb, wide), lambda h, t: (0, nt - 1 - t, h)), vec, vec],
        scratch_shapes=[pltpu.VMEM((hp, HGRN_DK, HGRN_DK), F32)], sem=("parallel", "arbitrary"),
        args=(proj, proj, proj, proj, states, do, lb, og), comm=comm)


def _cols_to_blocks(g, n8):
    K = g.shape[0]
    return g.reshape(K, N_DEV, n8).transpose(1, 0, 2)


def _blocks_to_cols(wg):
    _, K, n8 = wg.shape
    return wg.transpose(1, 0, 2).reshape(K, N_DEV * n8)


def _pack(parts):
    flat = []
    for p in parts:
        v = p.reshape(-1)
        flat.append(jnp.pad(v, (0, (-v.shape[0]) % 1024)))
    return jnp.concatenate(flat).reshape(-1, 128)


def _unpack(packed, like):
    flat, out, off = packed.reshape(-1), [], 0
    for p in like:
        size = math.prod(p.shape)
        out.append(flat[off:off + size].reshape(p.shape))
        off += size + (-size) % 1024
    return out


def _heads_major(a, heads):
    T = a.shape[0]
    return a.reshape(T, heads, ATTN_HD).transpose(1, 0, 2)


def _heads_minor(a):
    heads, T, _ = a.shape
    return a.transpose(1, 0, 2).reshape(T, heads * ATTN_HD)


def kernel(x, c, mod_w, mod_b, norm_mix, norm_mlp, attn_w_in, attn_w_out, attn_q_gain, attn_k_gain, attn_sinks, hgrn_w_in, hgrn_w_out, hgrn_o_gain, hgrn_lb_logits, mlp_w1, mlp_w2, loss_target, m_mod_w, m_mod_b, m_norm_mix, m_norm_mlp, m_attn_w_in, m_attn_w_out, m_attn_q_gain, m_attn_k_gain, m_attn_sinks, m_hgrn_w_in, m_hgrn_w_out, m_hgrn_o_gain, m_hgrn_lb_logits, m_mlp_w1, m_mlp_w2, v_mod_w, v_mod_b, v_norm_mix, v_norm_mlp, v_attn_w_in, v_attn_w_out, v_attn_q_gain, v_attn_k_gain, v_attn_sinks, v_hgrn_w_in, v_hgrn_w_out, v_hgrn_o_gain, v_hgrn_lb_logits, v_mlp_w1, v_mlp_w2):
    T = x.shape[1]
    me = 4 * lax.axis_index("x") + 2 * lax.axis_index("y") + lax.axis_index("c")
    x0, target = x[0], loss_target[0]
    n_mod = mod_w.shape[2]

    shards = [attn_w_in[0], attn_w_out[0], hgrn_w_in[0], hgrn_w_out[0], mlp_w1[0], mlp_w1[1], mlp_w2[0], mlp_w2[1]]
    sb = [s.astype(BF16) for s in shards]
    gather = lambda *items: _Exchange([("gather_by_chip", a, axis) for a, axis in items])

    c_all = _exchange([("gather", c.reshape(16, 128), None)], vmem=True, name="gather_c")[0].reshape(N_DEV, D_MODEL)
    mod_b_cols = lax.dynamic_slice_in_dim(mod_b, me * n_mod, n_mod, axis=1).reshape(2, 1, n_mod)
    mod_cols = _mod_fwd(c_all, mod_w, mod_b_cols, "mod_fwd")
    mod_all = _exchange([("gather", mod_cols.reshape(-1, 128), None)], vmem=True, name="gather_mod")[0]
    mod_all = mod_all.reshape(N_DEV, 2, N_DEV, n_mod)
    mod_mine = lax.dynamic_index_in_dim(mod_all, me, axis=2, keepdims=False)
    mod_mine = mod_mine.transpose(1, 0, 2).reshape(2, N_MOD, 1, D_MODEL)

    lb = _lb_fwd(hgrn_lb_logits, "lb_fwd").reshape(HGRN_HEADS, 1, HGRN_DK)
    og = hgrn_o_gain.reshape(HGRN_HEADS, 1, HGRN_DK)
    slopes = jnp.exp2(-8.0 * jnp.arange(1, ATTN_HEADS + 1, dtype=F32) / ATTN_HEADS)
    sinks = attn_sinks[0]
    per_row = lambda vals: jnp.repeat(vals.reshape(ATTN_KV, ATTN_GROUP), ATTN_BLOCK, axis=1).reshape(
        ATTN_KV, ATTN_GROUP * ATTN_BLOCK, 1)
    slope_col, sink_col = per_row(slopes), per_row(sinks)

    saved = []
    xi = x0
    w_mlp1, w_mlp2 = [None, None], [None, None]
    for i in range(2):
        sh1, sc1, g1, sh2, sc2, g2 = [mod_mine[i, j] for j in range(N_MOD)]
        if i == 0:
            h, w_attn_in = _norm_mod_fwd(xi, norm_mix[i:i + 1], sc1, sh1, "norm_mix_fwd0", comm=gather((sb[0], None)))
            w_attn_in = _blocks_to_cols(w_attn_in)
            qh, kh, vh, w_attn_out = _matmul(h, w_attn_in, dims="nn", tm=1024, tn=512, tk=2048, name="attn_in_fwd",
                                             epilogue="heads", comm=gather((sb[1], 0)))
            o, w_mlp1[0] = _attn_fwd(qh, kh, vh, attn_q_gain, attn_k_gain, slopes, sinks, "attn_fwd",
                                     comm=gather((sb[4], 1)))
            mix = (qh, kh, vh)
            w_out = w_attn_out
        else:
            h = _norm_mod_fwd(xi, norm_mix[i:i + 1], sc1, sh1, "norm_mix_fwd1")
            proj, w_hgrn_out = _matmul(h, w_hgrn_in, dims="nn", tm=1024, tn=1024, tk=2048, name="hgrn_in_fwd",
                                       comm=gather((sb[3], 0)))
            o, states, w_mlp1[1] = _hgrn_fwd(proj, lb, og, "hgrn_fwd", comm=gather((sb[5], 1)))
            mix = (proj, states)
            w_out = w_hgrn_out
        y, x1 = _matmul(o, w_out, dims="nn", tm=1024, tn=1024, tk=2048, name=f"mix_out_fwd{i}", epilogue="resgate",
                        extras=(xi, g1), a_heads=i == 0)
        h2 = _norm_mod_fwd(x1, norm_mlp[i:i + 1], sc2, sh2, f"norm_mlp_fwd{i}")
        if i == 0:
            act, act2, w_mlp2[0] = _matmul(h2, w_mlp1[0], dims="nn", tm=1024, tn=1024, tk=2048, name="mlp1_fwd0",
                                           epilogue="relu2", comm=gather((sb[6], 0)))
            z, x2, w_hgrn_in = _matmul(act2, w_mlp2[0], dims="nn", tm=1024, tn=1024, tk=2048, name="mlp2_fwd0",
                                       epilogue="resgate", extras=(x1, g2), comm=gather((sb[2], 1)))
        else:
            act, act2, w_mlp2[1] = _matmul(h2, w_mlp1[1], dims="nn", tm=1024, tn=1024, tk=2048, name="mlp1_fwd1",
                                           epilogue="relu2", comm=gather((sb[7], 0)))
            z, x2 = _matmul(act2, w_mlp2[1], dims="nn", tm=1024, tn=1024, tk=2048, name="mlp2_fwd1", epilogue="resgate",
                            extras=(x1, g2))
        saved.append((xi, h, o, y, x1, h2, act, act2, z, mix))
        xi = x2

    dx, loss_tile, st_g2, dz = _loss_head(xi, target, (saved[1][8], mod_mine[1, 5]), "loss_head")
    loss = lax.psum(loss_tile[0, 0], ("x", "y", "c"))

    scatter = lambda *items: _Exchange([("scatter", a, axis) for a, axis in items])
    shares, dmods, dnorm_mix, dnorm_mlp = {}, [None, None], [None, None], [None, None]
    wgrad = lambda a, b, name, tn=1024: _matmul(a, b, dims="tn", tm=2048, tn=tn, tk=2048, name=name, out_dtype=BF16)
    for i in (1, 0):
        sh1, sc1, g1, sh2, sc2, g2 = [mod_mine[i, j] for j in range(N_MOD)]
        xin, h, o, y, x1, h2, act, act2, z, mix = saved[i]
        if i == 1:
            dpre = _matmul(dz, w_mlp2[1], dims="nt", tm=1024, tn=1024, tk=2048, name="mlp2_bwd1", epilogue="mul2a",
                           extras=(act,))
        else:
            dpre, shares["hgrn_w_in"] = _matmul(dz, w_mlp2[0], dims="nt", tm=1024, tn=1024, tk=2048, name="mlp2_bwd0",
                                                epilogue="mul2a", extras=(act,), comm=scatter((g_hgrn_in, 1)))
        g_mlp2 = wgrad(act2, dz, f"mlp2_wgrad{i}")
        dh2 = _matmul(dpre, w_mlp1[i], dims="nt", tm=1024, tn=1024, tk=4096, name=f"mlp1_bwd{i}", out_dtype=BF16)
        g_mlp1 = wgrad(h2, dpre, f"mlp1_wgrad{i}")
        dx1, st_mlp, dy = _norm_mod_bwd(x1, dh2, dx, norm_mlp[i:i + 1], sc2, f"norm_mlp_bwd{i}", below=(y, g1))
        w_out = w_attn_out if i == 0 else w_hgrn_out
        if i == 0:
            qh, kh, vh = mix
            doh = _matmul(dy, w_out, dims="nt", tm=1024, tn=512, tk=2048, name="mix_out_bwd0", epilogue="heads",
                          out_dtype=BF16)
            g_out = _matmul(o, dy, dims="tn", tm=1024, tn=1024, tk=2048, name="mix_out_wgrad0", out_dtype=BF16,
                            a_heads=True)
            dqh, dkh, dvh, dqg, dkg, dsk, shares["mlp_w2_0"], shares["mlp_w1_0"] = _attn_bwd(
                qh, kh, vh, doh, attn_q_gain, attn_k_gain, slope_col, sink_col, "attn_bwd",
                comm=scatter((g_mlp2, 0), (g_mlp1, 1)))
            dproj = jnp.concatenate([dqh, dkh, dvh], axis=0)
            d_q_gain, d_k_gain, d_sinks = dqg[0:1], dkg[0:1], dsk[:, 0].reshape(1, ATTN_HEADS)
            g_attn_in, shares["attn_w_out"] = _matmul(h, dproj, dims="tn", tm=2048, tn=640, tk=2048, name="mix_in_wgrad0",
                                                      out_dtype=BF16, b_heads=True, comm=scatter((g_out, 0)))
            g_attn_in = _cols_to_blocks(g_attn_in, attn_w_in.shape[2])
            dh, shares["attn_w_in"] = _matmul(dproj, w_attn_in, dims="nt", tm=1024, tn=1024, tk=2560, out_dtype=BF16,
                                              name="mix_in_bwd0", comm=scatter((g_attn_in, None)), a_heads=True)
        else:
            do = _matmul(dy, w_out, dims="nt", tm=1024, tn=1024, tk=2048, name="mix_out_bwd1")
            g_out = wgrad(o, dy, "mix_out_wgrad1")
            proj, states = mix
            dproj, dlb, d_o_gain, shares["mlp_w2_1"], shares["mlp_w1_1"], shares["hgrn_w_out"] = _hgrn_bwd(
                proj, states, do, lb, og, "hgrn_bwd", comm=scatter((g_mlp2, 0), (g_mlp1, 1), (g_out, 0)))
            d_lb_logits = _lb_bwd(hgrn_lb_logits, dlb.reshape(1, D_MODEL), "lb_bwd")
            dh = _matmul(dproj, w_hgrn_in, dims="nt", tm=1024, tn=1024, tk=2048, name="mix_in_bwd1", out_dtype=BF16)
            g_hgrn_in = wgrad(h, dproj, "mix_in_wgrad1")
        d_gate2 = st_g2[0:1] if i == 1 else st_mix_above[3:4]
        if i == 1:
            dx, st_mix, dz = _norm_mod_bwd(xin, dh, dx1, norm_mix[i:i + 1], sc1, "norm_mix_bwd1",
                                           below=(saved[0][8], mod_mine[0, 5]))
            st_mix_above = st_mix
        else:
            dx, st_mix = _norm_mod_bwd(xin, dh, dx1, norm_mix[i:i + 1], sc1, "norm_mix_bwd0")
        dmods[i] = jnp.concatenate([st_mix[0:1], st_mix[1:2], st_mlp[3:4], st_mlp[0:1], st_mlp[1:2], d_gate2], axis=1)
        dnorm_mix[i], dnorm_mlp[i] = st_mix[2:3], st_mlp[2:3]

    single = {"attn_w_in": (attn_w_in, m_attn_w_in, v_attn_w_in), "attn_w_out": (attn_w_out, m_attn_w_out, v_attn_w_out),
              "hgrn_w_in": (hgrn_w_in, m_hgrn_w_in, v_hgrn_w_in), "hgrn_w_out": (hgrn_w_out, m_hgrn_w_out, v_hgrn_w_out)}
    big = {nm: _sum_adamw(shares[nm], w[0], m[0], v[0], f"adamw_{nm}") for nm, (w, m, v) in single.items()}
    big["mlp_w1"] = _sum_adamw_layers([shares["mlp_w1_0"], shares["mlp_w1_1"]], mlp_w1, m_mlp_w1, v_mlp_w1, "adamw_mlp_w1")
    big["mlp_w2"] = _sum_adamw_layers([shares["mlp_w2_0"], shares["mlp_w2_1"]], mlp_w2, m_mlp_w2, v_mlp_w2, "adamw_mlp_w2")

    small_w = [mod_b, norm_mix, norm_mlp, attn_q_gain, attn_k_gain, attn_sinks, hgrn_o_gain, hgrn_lb_logits]
    small_m = [m_mod_b, m_norm_mix, m_norm_mlp, m_attn_q_gain, m_attn_k_gain, m_attn_sinks, m_hgrn_o_gain, m_hgrn_lb_logits]
    small_v = [v_mod_b, v_norm_mix, v_norm_mlp, v_attn_q_gain, v_attn_k_gain, v_attn_sinks, v_hgrn_o_gain, v_hgrn_lb_logits]
    small_g = [jnp.concatenate(dmods, axis=0), jnp.concatenate(dnorm_mix, axis=0), jnp.concatenate(dnorm_mlp, axis=0),
               d_q_gain, d_k_gain, d_sinks, d_o_gain.reshape(hgrn_o_gain.shape), d_lb_logits]
    packed_g = _pack(small_g)
    pad_rows = (-packed_g.shape[0]) % 8
    pad8 = lambda a: jnp.pad(a, ((0, pad_rows), (0, 0)))
    all_small = _exchange([("gather", pad8(packed_g), None)], vmem=True, name="gather_small_grads")[0]
    sg, sd, sm, sv = _sum_adamw(all_small, pad8(_pack(small_w)), pad8(_pack(small_m)), pad8(_pack(small_v)),
                                "adamw_small")
    small = [_unpack(t, small_w) for t in (sg, sd, sm, sv)]

    n_modb = N_MOD * D_MODEL
    dmod_all = all_small[:, :2 * n_modb // 128, :].reshape(N_DEV, 2, n_modb)
    dmod_cols = lax.dynamic_slice_in_dim(dmod_all, me * n_mod, n_mod, axis=2).transpose(1, 0, 2)
    modw = _mod_w_update(c_all.T, dmod_cols, mod_w, m_mod_w, v_mod_w, "adamw_mod_w")

    def leaf(k):
        one = lambda a: big[a][k][None]
        s = small[k]
        return [modw[k], s[0], s[1], s[2], one("attn_w_in"), one("attn_w_out"), s[3], s[4], s[5], one("hgrn_w_in"),
                one("hgrn_w_out"), s[6], s[7], big["mlp_w1"][k], big["mlp_w2"][k]]

    return (loss, dx[None], *leaf(0), *leaf(1), *leaf(2), *leaf(3))
```

```python
import functools
import math

import jax
import jax.numpy as jnp
from jax import lax
from jax.experimental import pallas as pl
from jax.experimental.pallas import tpu as pltpu

F32, BF16 = jnp.float32, jnp.bfloat16
N_DEV = 8
D_MODEL = 2048
N_MOD = 6
EPS = 1e-6
ATTN_HD, ATTN_HEADS, ATTN_KV, ATTN_GROUP, ATTN_BLOCK = 64, 32, 4, 8, 128
ATTN_SCALE = 1.0 / math.sqrt(ATTN_HD)
HGRN_HEADS, HGRN_DK, HGRN_CHUNK = 16, 128, 64
HGRN_SCALE = 1.0 / math.sqrt(HGRN_DK)
HGRN_CHUNK_SHIFT = HGRN_CHUNK.bit_length() - 1
assert 1 << HGRN_CHUNK_SHIFT == HGRN_CHUNK
D_FF = 4 * D_MODEL
ADAM_LR, ADAM_B1, ADAM_B2, ADAM_EPS, ADAM_WD, ADAM_STEP = 0.001, 0.9, 0.999, 1e-08, 0.01, 10
NEG_BIG = -1e30
VMEM_LIMIT = 56 * 1024 * 1024
MESH_ID = pl.DeviceIdType.MESH


def _params(*sem):
    return pltpu.CompilerParams(dimension_semantics=sem, vmem_limit_bytes=VMEM_LIMIT)


class _Exchange:
    def __init__(self, items):
        self.items = items
        self.n = len(items)
        self.out_shape = []
        for kind, a, axis in items:
            assert kind in ("gather", "gather_by_chip", "scatter")
            if kind != "scatter":
                shape = (N_DEV,) + a.shape if axis is None else tuple(
                    d * N_DEV if i == axis else d for i, d in enumerate(a.shape))
            else:
                shape = a.shape if axis is None else (N_DEV,) + tuple(
                    d // N_DEV if i == axis else d for i, d in enumerate(a.shape))
            self.out_shape.append(jax.ShapeDtypeStruct(shape, a.dtype))
        self.scratch = [pltpu.SemaphoreType.DMA((7 * self.n,)), pltpu.SemaphoreType.DMA((7 * self.n,)),
                        pltpu.SemaphoreType.DMA((self.n,))]
        self.arrays = [a for _, a, _ in items]

    @staticmethod
    def _block(ref, b, axis, size):
        if axis is None:
            return ref.at[b]
        sl = pl.ds(pl.multiple_of(b * size, size), size)
        return ref.at[sl, :] if axis == 0 else ref.at[:, sl]

    def _plan(self, ins, outs, sems):
        send_sems, recv_sems, loc_sems = sems
        x, y, c = lax.axis_index("x"), lax.axis_index("y"), lax.axis_index("c")
        me = 4 * x + 2 * y + c
        begin, middle, end = [], [], []

        def peer(d):
            px = 1 - x if d & 4 else x
            py = 1 - y if d & 2 else y
            pc = 1 - c if d & 1 else c
            return (px, py, pc), 4 * px + 2 * py + pc

        for a, (kind, arr, axis) in enumerate(self.items):
            gather = kind != "scatter"
            size = None if axis is None else (arr.shape[axis] if gather else arr.shape[axis] // N_DEV)

            def src(b):
                return ins[a] if gather else self._block(ins[a], b, axis, size)

            def dst(b):
                return self._block(outs[a], b, axis, size) if gather else outs[a].at[b]

            def remote(src_ref, dst_ref, slot, dev):
                return pltpu.make_async_remote_copy(
                    src_ref=src_ref, dst_ref=dst_ref, send_sem=send_sems.at[a * 7 + slot],
                    recv_sem=recv_sems.at[a * 7 + slot], device_id=dev, device_id_type=MESH_ID)

            local = pltpu.make_async_copy(src(me), dst(me), loc_sems.at[a])
            begin.append(local)
            end.append(local.wait)
            if kind == "gather_by_chip":
                sib_dev, sib_id = peer(1)
                to_sib = remote(ins[a], dst(me), 0, sib_dev)
                begin.append(to_sib)
                end += [to_sib.wait_send, remote(ins[a], dst(sib_id), 0, sib_dev).wait_recv]
                for j, d in enumerate((2, 4, 6)):
                    dev, pid = peer(d)
                    over_ici = remote(ins[a], dst(me), 1 + j, dev)
                    begin.append(over_ici)
                    passed_on = remote(dst(pid), dst(pid), 4 + j, sib_dev)
                    middle.append((remote(ins[a], dst(pid), 1 + j, dev), passed_on))
                    end += [over_ici.wait_send, passed_on.wait_send,
                            remote(ins[a], dst(peer(d ^ 1)[1]), 4 + j, sib_dev).wait_recv]
            else:
                for d in range(1, N_DEV):
                    dev, pid = peer(d)
                    begin.append(remote(src(pid), dst(me), d - 1, dev))
                    end.append(remote(src(pid), dst(pid), d - 1, dev).wait)
        return begin, middle, end

    def start(self, ins, outs, sems):
        for cp in self._plan(ins, outs, sems)[0]:
            cp.start()

    def pass_on(self, ins, outs, sems):
        for arrived, onward in self._plan(ins, outs, sems)[1]:
            arrived.wait_recv()
            onward.start()

    def wait(self, ins, outs, sems):
        for wait in self._plan(ins, outs, sems)[2]:
            wait()


def _call(body, *, name, grid, in_specs, out_specs, out_shape, args, scratch_shapes=(), sem=None, comm=None):
    n_in, n_out, n_scr = len(in_specs), len(out_specs), len(scratch_shapes)
    if comm is None:
        return pl.pallas_call(
            body, name=name, grid=grid, out_shape=list(out_shape), in_specs=list(in_specs), out_specs=list(out_specs),
            scratch_shapes=list(scratch_shapes), compiler_params=_params(*sem))(*args)
    hbm = pl.BlockSpec(memory_space=pltpu.HBM)

    def carrier(*refs):
        bounds = [0, n_in, n_in + comm.n, n_in + comm.n + n_out, n_in + 2 * comm.n + n_out, len(refs) - 3, len(refs)]
        ins, cin, outs, cout, scr, sems = [refs[lo:hi] for lo, hi in zip(bounds[:-1], bounds[1:])]
        assert len(scr) == n_scr
        step = functools.reduce(lambda lin, ax: lin * grid[ax] + pl.program_id(ax), range(len(grid)), 0)
        steps = math.prod(grid)

        @pl.when(step == 0)
        def _():
            comm.start(cin, cout, sems)

        body(*ins, *outs, *scr)

        @pl.when(step == (2 * steps) // 3)
        def _():
            comm.pass_on(cin, cout, sems)

        @pl.when(step == steps - 1)
        def _():
            comm.wait(cin, cout, sems)

    return pl.pallas_call(
        carrier, name=name, grid=grid, out_shape=list(out_shape) + comm.out_shape,
        in_specs=list(in_specs) + [hbm] * comm.n, out_specs=list(out_specs) + [hbm] * comm.n,
        scratch_shapes=list(scratch_shapes) + comm.scratch,
        compiler_params=_params(*["arbitrary"] * len(grid)))(*args, *comm.arrays)


def _exchange(items, *, name, vmem=False):
    comm = _Exchange(items)
    space = pl.BlockSpec(memory_space=pltpu.VMEM if vmem else pltpu.HBM)

    def body(*refs):
        ins, outs, sems = refs[:comm.n], refs[comm.n:2 * comm.n], refs[2 * comm.n:]
        comm.start(ins, outs, sems)
        comm.pass_on(ins, outs, sems)
        comm.wait(ins, outs, sems)

    return pl.pallas_call(
        body, name=name, out_shape=comm.out_shape, in_specs=[space] * comm.n, out_specs=[space] * comm.n,
        scratch_shapes=comm.scratch, compiler_params=pltpu.CompilerParams(vmem_limit_bytes=VMEM_LIMIT))(*comm.arrays)


_DIMS = {"nn": (((1,), (0,)), ((), ())), "nt": (((1,), (1,)), ((), ())), "tn": (((0,), (0,)), ((), ()))}


def _matmul(a, b, *, dims, tm, tn, tk, name, epilogue="plain", out_dtype=F32, extras=(), comm=None,
            a_heads=False, b_heads=False):
    a_parts = a.shape[0] if a.ndim == 3 else 0
    b_parts = b.shape[0] if b.ndim == 3 else 0
    assert not (a_parts and dims != "nt" and not a_heads) and not (b_parts and dims != "tn")
    a2 = (a.shape[1], a.shape[2] * a_parts) if a_parts else a.shape
    b2 = (b.shape[1], b.shape[2] * b_parts) if b_parts else b.shape
    if dims == "tn":
        (K, M), N = a2, b2[1]
    else:
        (M, K), N = a2, (b2[1] if dims == "nn" else b2[0])
    tm, tn, tk = min(tm, M), min(tn, N), min(tk, K)
    assert M % tm == 0 and N % tn == 0 and K % tk == 0, (name, M, N, K, tm, tn, tk)
    if a_heads and dims == "tn":
        a_spec = pl.BlockSpec((tm // ATTN_HD, tk, ATTN_HD), lambda i, j, k: (i, k, 0))
    elif a_heads:
        a_spec = pl.BlockSpec((tk // ATTN_HD, tm, ATTN_HD), lambda i, j, k: (k, i, 0))
    elif a_parts:
        per = K // a_parts // tk
        a_spec = pl.BlockSpec((None, tm, tk), lambda i, j, k: (k // per, i, k % per))
    elif dims == "tn":
        a_spec = pl.BlockSpec((tk, tm), lambda i, j, k: (k, i))
    else:
        a_spec = pl.BlockSpec((tm, tk), lambda i, j, k: (i, k))
    if b_heads:
        b_spec = pl.BlockSpec((tn // ATTN_HD, tk, ATTN_HD), lambda i, j, k: (j, k, 0))
    elif b_parts:
        per_n = N // b_parts // tn
        b_spec = pl.BlockSpec((None, tk, tn), lambda i, j, k: (j // per_n, k, j % per_n))
    elif dims == "nt":
        b_spec = pl.BlockSpec((tn, tk), lambda i, j, k: (j, k))
    else:
        b_spec = pl.BlockSpec((tk, tn), lambda i, j, k: (k, j))
    side_by_side = lambda ref: jnp.concatenate([ref[g] for g in range(ref.shape[0])], axis=1)
    nk = K // tk
    tile = pl.BlockSpec((tm, tn), lambda i, j, k: (i, j))
    row = pl.BlockSpec((1, tn), lambda i, j, k: (0, j))
    if epilogue == "plain":
        extra_specs, out_shape, out_specs = [], [jax.ShapeDtypeStruct((M, N), out_dtype)], [tile]
    elif epilogue == "relu2":
        extra_specs, out_shape, out_specs = [], [jax.ShapeDtypeStruct((M, N), BF16)] * 2, [tile, tile]
    elif epilogue == "resgate":
        extra_specs, out_specs = [tile, row], [tile, tile]
        out_shape = [jax.ShapeDtypeStruct((M, N), BF16), jax.ShapeDtypeStruct((M, N), F32)]
    elif epilogue == "mul2a":
        extra_specs, out_shape, out_specs = [tile], [jax.ShapeDtypeStruct((M, N), BF16)], [tile]
    elif epilogue == "heads":
        per_tile = tn // ATTN_HD
        q_tiles = ATTN_HEADS // per_tile
        assert tn == 2 * ATTN_KV * ATTN_HD and N // tn in (q_tiles, q_tiles + 1)
        extra_specs = []
        out_shape = [jax.ShapeDtypeStruct((ATTN_HEADS, M, ATTN_HD), out_dtype)]
        out_specs = [pl.BlockSpec((per_tile, tm, ATTN_HD), lambda i, j, k: (jnp.minimum(j, q_tiles - 1), i, 0))]
        if N // tn > q_tiles:
            out_shape += [jax.ShapeDtypeStruct((ATTN_KV, M, ATTN_HD), out_dtype)] * 2
            out_specs += [pl.BlockSpec((ATTN_KV, tm, ATTN_HD), lambda i, j, k: (0, i, 0))] * 2
    else:
        raise ValueError(epilogue)
    n_extra = len(extra_specs)

    def body(a_ref, b_ref, *rest):
        ex, outs, acc_ref = rest[:n_extra], rest[n_extra:-1], rest[-1]
        k = pl.program_id(2)

        @pl.when(k == 0)
        def _():
            acc_ref[...] = jnp.zeros_like(acc_ref)

        a_tile = side_by_side(a_ref) if a_heads else a_ref[...]
        b_tile = side_by_side(b_ref) if b_heads else b_ref[...]
        acc_ref[...] += lax.dot_general(a_tile, b_tile, _DIMS[dims], preferred_element_type=F32)

        if epilogue == "heads":
            j = pl.program_id(1)

            @pl.when((k == nk - 1) & (j < q_tiles))
            def _():
                for g in range(per_tile):
                    outs[0][g] = acc_ref[:, g * ATTN_HD:(g + 1) * ATTN_HD].astype(out_dtype)

            if len(outs) > 1:
                @pl.when((k == nk - 1) & (j == q_tiles))
                def _():
                    for g in range(ATTN_KV):
                        outs[1][g] = acc_ref[:, g * ATTN_HD:(g + 1) * ATTN_HD].astype(out_dtype)
                        outs[2][g] = acc_ref[:, (ATTN_KV + g) * ATTN_HD:(ATTN_KV + g + 1) * ATTN_HD].astype(out_dtype)
            return

        @pl.when(k == nk - 1)
        def _():
            acc = acc_ref[...]
            if epilogue == "plain":
                outs[0][...] = acc.astype(out_dtype)
            elif epilogue == "relu2":
                act = jnp.maximum(acc, 0.0)
                outs[0][...] = act.astype(BF16)
                outs[1][...] = (act * act).astype(BF16)
            elif epilogue == "resgate":
                outs[0][...] = acc.astype(BF16)
                outs[1][...] = ex[0][...] + ex[1][...] * acc
            else:
                outs[0][...] = (acc * (2.0 * ex[0][...].astype(F32))).astype(BF16)

    res = _call(body, name=name, grid=(M // tm, N // tn, nk), out_shape=out_shape,
                in_specs=[a_spec, b_spec] + extra_specs, out_specs=out_specs,
                scratch_shapes=[pltpu.VMEM((tm, tn), F32)],
                sem=("parallel", "arbitrary" if epilogue == "heads" else "parallel", "arbitrary"),
                args=(a, b, *extras), comm=comm)
    return res[0] if len(res) == 1 else res


def _row_tile(T):
    return min(T, 256)


def _norm_mod_fwd(x, gain, sc, sh, name, comm=None):
    T, D = x.shape
    tr = min(T, 2 * _row_tile(T))

    def body(x_ref, g_ref, sc_ref, sh_ref, h_ref):
        xv = x_ref[...]
        r = lax.rsqrt(jnp.mean(xv * xv, axis=-1, keepdims=True) + EPS)
        hn = (xv * r) * g_ref[...]
        h_ref[...] = (hn * (1.0 + sc_ref[...]) + sh_ref[...]).astype(BF16)

    vec = pl.BlockSpec((1, D), lambda i: (0, 0))
    res = _call(body, name=name, grid=(T // tr,), out_shape=[jax.ShapeDtypeStruct((T, D), BF16)],
                in_specs=[pl.BlockSpec((tr, D), lambda i: (i, 0)), vec, vec, vec],
                out_specs=[pl.BlockSpec((tr, D), lambda i: (i, 0))], sem=("parallel",), args=(x, gain, sc, sh),
                comm=comm)
    return res[0] if comm is None else res


def _through_gate(dx, branch_ref, gate_ref, dbranch_ref, st_ref, row):
    dbranch_ref[...] = (dx * gate_ref[...]).astype(BF16)
    st_ref[row:row + 1, :] += jnp.sum(dx * branch_ref[...].astype(F32), axis=0, keepdims=True)


def _norm_mod_bwd(x, dh, dres, gain, sc, name, below=None):
    T, D = x.shape
    tr = _row_tile(T)

    def body(x_ref, dh_ref, dres_ref, g_ref, sc_ref, *rest):
        dx_ref, st_ref = rest[-3:-1] if below else rest[-2:]
        xv, dh_v, gain_v = x_ref[...], dh_ref[...].astype(F32), g_ref[...]
        r = lax.rsqrt(jnp.mean(xv * xv, axis=-1, keepdims=True) + EPS)
        xn = xv * r
        hn = xn * gain_v
        dhn = dh_v * (1.0 + sc_ref[...])
        dxn = dhn * gain_v
        dx = dres_ref[...] + r * (dxn - xn * jnp.mean(dxn * xn, axis=-1, keepdims=True))
        dx_ref[...] = dx

        @pl.when(pl.program_id(0) == 0)
        def _():
            st_ref[...] = jnp.zeros_like(st_ref)

        st_ref[0:1, :] += jnp.sum(dh_v, axis=0, keepdims=True)
        st_ref[1:2, :] += jnp.sum(dh_v * hn, axis=0, keepdims=True)
        st_ref[2:3, :] += jnp.sum(dhn * xn, axis=0, keepdims=True)
        if below:
            _through_gate(dx, rest[0], rest[1], rest[-1], st_ref, 3)

    vec = pl.BlockSpec((1, D), lambda i: (0, 0))
    blk = pl.BlockSpec((tr, D), lambda i: (i, 0))
    return pl.pallas_call(
        body, name=name, grid=(T // tr,),
        out_shape=[jax.ShapeDtypeStruct((T, D), F32), jax.ShapeDtypeStruct((8, D), F32)]
        + ([jax.ShapeDtypeStruct((T, D), BF16)] if below else []),
        in_specs=[blk, blk, blk, vec, vec] + ([blk, vec] if below else []),
        out_specs=[blk, pl.BlockSpec((8, D), lambda i: (0, 0))] + ([blk] if below else []),
        compiler_params=_params("arbitrary"),
    )(x, dh, dres, gain, sc, *(below or ()))


def _loss_head(y, target, below, name):
    T, D = y.shape
    tr = _row_tile(T)

    def body(y_ref, t_ref, b_ref, g_ref, dy_ref, l_ref, st_ref, db_ref):
        err = y_ref[...] - t_ref[...]
        dy = err * (1.0 / D)
        dy_ref[...] = dy

        @pl.when(pl.program_id(0) == 0)
        def _():
            l_ref[...] = jnp.zeros_like(l_ref)
            st_ref[...] = jnp.zeros_like(st_ref)

        part = jnp.sum(jnp.mean(err * err, axis=-1, keepdims=True), axis=0, keepdims=True)
        l_ref[...] += jnp.broadcast_to(0.5 * part, l_ref.shape)
        _through_gate(dy, b_ref, g_ref, db_ref, st_ref, 0)

    blk = pl.BlockSpec((tr, D), lambda i: (i, 0))
    return pl.pallas_call(
        body, name=name, grid=(T // tr,),
        out_shape=[jax.ShapeDtypeStruct((T, D), F32), jax.ShapeDtypeStruct((8, 128), F32),
                   jax.ShapeDtypeStruct((8, D), F32), jax.ShapeDtypeStruct((T, D), BF16)],
        in_specs=[blk, blk, blk, pl.BlockSpec((1, D), lambda i: (0, 0))],
        out_specs=[blk, pl.BlockSpec((8, 128), lambda i: (0, 0)), pl.BlockSpec((8, D), lambda i: (0, 0)), blk],
        compiler_params=_params("arbitrary"),
    )(y, target, *below)


def _silu(v):
    return v * jax.nn.sigmoid(v)


def _mod_fwd(c_all, mod_w, mod_b_cols, name):
    L, D, n = mod_w.shape
    tn = 512

    def body(c_ref, w_ref, b_ref, o_ref):
        cond = _silu(c_ref[...]).astype(BF16)
        o_ref[0] = jnp.dot(cond, w_ref[0].astype(BF16), preferred_element_type=F32) + b_ref[0]

    return pl.pallas_call(
        body, name=name, grid=(L, n // tn), out_shape=jax.ShapeDtypeStruct((L, N_DEV, n), F32),
        in_specs=[pl.BlockSpec((N_DEV, D), lambda l, j: (0, 0)), pl.BlockSpec((1, D, tn), lambda l, j: (l, 0, j)),
                  pl.BlockSpec((1, 1, tn), lambda l, j: (l, 0, j))],
        out_specs=pl.BlockSpec((1, N_DEV, tn), lambda l, j: (l, 0, j)),
        compiler_params=_params("parallel", "parallel"),
    )(c_all, mod_w, mod_b_cols)


def _adamw(g, w, m, v):
    m = ADAM_B1 * m + (1.0 - ADAM_B1) * g
    v = ADAM_B2 * v + (1.0 - ADAM_B2) * (g * g)
    m_hat = m / (1.0 - ADAM_B1 ** ADAM_STEP)
    v_hat = v / (1.0 - ADAM_B2 ** ADAM_STEP)
    delta = -ADAM_LR * (m_hat / (jnp.sqrt(v_hat) + ADAM_EPS) + ADAM_WD * w)
    return delta, m, v


def _mod_w_update(c_t, dmod_cols, w, m, v, name):
    L, D, n = w.shape
    tr = 256

    def body(c_ref, dm_ref, w_ref, m_ref, v_ref, g_ref, d_ref, nm_ref, nv_ref):
        cond = _silu(c_ref[...])
        dm = dm_ref[0]
        g = cond[:, 0:1] * dm[0:1, :]
        for b in range(1, N_DEV):
            g = g + cond[:, b:b + 1] * dm[b:b + 1, :]
        delta, nm, nv = _adamw(g, w_ref[0], m_ref[0], v_ref[0])
        g_ref[0], d_ref[0], nm_ref[0], nv_ref[0] = g, delta, nm, nv

    blk = pl.BlockSpec((1, tr, n), lambda l, i: (l, i, 0))
    return pl.pallas_call(
        body, name=name, grid=(L, D // tr), out_shape=[jax.ShapeDtypeStruct(w.shape, F32)] * 4,
        in_specs=[pl.BlockSpec((tr, N_DEV), lambda l, i: (i, 0)), pl.BlockSpec((1, N_DEV, n), lambda l, i: (l, 0, 0)),
                  blk, blk, blk],
        out_specs=[blk] * 4, compiler_params=_params("parallel", "parallel"),
    )(c_t, dmod_cols, w, m, v)


def _sum_adamw(parts, w, m, v, name):
    R, C = w.shape
    tr = min(R, 256 if C >= 1024 else 1024)
    assert R % tr == 0

    def body(p_ref, w_ref, m_ref, v_ref, g_ref, d_ref, nm_ref, nv_ref):
        g = p_ref[0].astype(F32)
        for s in range(1, N_DEV):
            g = g + p_ref[s].astype(F32)
        delta, nm, nv = _adamw(g, w_ref[...], m_ref[...], v_ref[...])
        g_ref[...], d_ref[...], nm_ref[...], nv_ref[...] = g, delta, nm, nv

    blk = pl.BlockSpec((tr, C), lambda i: (i, 0))
    return pl.pallas_call(
        body, name=name, grid=(R // tr,), out_shape=[jax.ShapeDtypeStruct((R, C), F32)] * 4,
        in_specs=[pl.BlockSpec((N_DEV, tr, C), lambda i: (0, i, 0)), blk, blk, blk], out_specs=[blk] * 4,
        compiler_params=_params("parallel"),
    )(parts, w, m, v)


def _sum_adamw_layers(parts, w, m, v, name):
    L, R, C = w.shape
    tr = min(R, 256 * 1024 // C)
    assert R % tr == 0 and len(parts) == L
    ni = R // tr

    def body(*refs):
        p_refs, (w_ref, m_ref, v_ref), outs = refs[:L], refs[L:L + 3], refs[L + 3:]
        for layer in range(L):
            @pl.when(pl.program_id(0) == layer)
            def _(p_ref=p_refs[layer]):
                g = p_ref[0].astype(F32)
                for s in range(1, N_DEV):
                    g = g + p_ref[s].astype(F32)
                delta, nm, nv = _adamw(g, w_ref[...], m_ref[...], v_ref[...])
                for o_ref, val in zip(outs, (g, delta, nm, nv)):
                    o_ref[...] = val

    def shares(layer):
        park = 0 if layer else ni - 1
        return pl.BlockSpec((N_DEV, tr, C), lambda l, i: (0, jnp.where(l == layer, i, park), 0))

    blk = pl.BlockSpec((None, tr, C), lambda l, i: (l, i, 0))
    return pl.pallas_call(
        body, name=name, grid=(L, ni), out_shape=[jax.ShapeDtypeStruct((L, R, C), F32)] * 4,
        in_specs=[shares(layer) for layer in range(L)] + [blk] * 3, out_specs=[blk] * 4,
        compiler_params=_params("arbitrary", "arbitrary"),
    )(*parts, w, m, v)


def _lower_bound_row1(l0, l1):
    mx = lax.stop_gradient(jnp.maximum(l0, l1))
    e0, e1 = jnp.exp(l0 - mx), jnp.exp(l1 - mx)
    p0, p1 = e0 / (e0 + e1), e1 / (e0 + e1)
    return (p0 + p1) - p0


def _lb_fwd(logits, name):
    def body(l_ref, o_ref):
        o_ref[...] = _lower_bound_row1(l_ref[0:1, :], l_ref[1:2, :])

    return pl.pallas_call(body, name=name, out_shape=jax.ShapeDtypeStruct((1, logits.shape[1]), F32))(logits)


def _lb_bwd(logits, dlb, name):
    def body(l_ref, d_ref, o_ref):
        _, vjp = jax.vjp(_lower_bound_row1, l_ref[0:1, :], l_ref[1:2, :])
        d0, d1 = vjp(d_ref[...])
        o_ref[0:1, :] = d0
        o_ref[1:2, :] = d1

    return pl.pallas_call(body, name=name, out_shape=jax.ShapeDtypeStruct(logits.shape, F32))(logits, dlb)


def _bdot(a, b, dims):
    return lax.dot_general(a.astype(BF16), b.astype(BF16), _DIMS[dims], preferred_element_type=F32)


def _rms(x, gain):
    r = lax.rsqrt(jnp.mean(x * x, axis=-1, keepdims=True) + EPS)
    xhat = x * r
    return xhat, r, xhat * gain


def _attn_band(n):
    qi = lax.broadcasted_iota(jnp.int32, (ATTN_BLOCK, 2 * ATTN_BLOCK), 0)
    ki = lax.broadcasted_iota(jnp.int32, (ATTN_BLOCK, 2 * ATTN_BLOCK), 1)
    dist = qi + ATTN_BLOCK - ki
    first_key = jnp.where(n > 0, 0, ATTN_BLOCK)
    valid = (dist >= 0) & (dist < ATTN_BLOCK) & (ki >= first_key)
    return valid, jnp.abs(dist).astype(F32)


def _attn_head_probs(qn, kn_b, valid, absdist, slope, sink):
    s = lax.dot_general(qn.astype(BF16), kn_b, _DIMS["nt"], preferred_element_type=F32) * ATTN_SCALE
    s = jnp.where(valid, s - slope * absdist, NEG_BIG)
    mx = jnp.maximum(jnp.max(s, axis=-1, keepdims=True), sink)
    e = jnp.exp(s - mx)
    es = jnp.exp(sink - mx)
    inv = 1.0 / (jnp.sum(e, axis=-1, keepdims=True) + es)
    return e * inv, es * inv


def _attn_fwd(qh, kh, vh, qg, kg, slopes, sinks, name, comm=None):
    T = qh.shape[1]
    nb = T // ATTN_BLOCK

    def body(q_ref, kp_ref, kc_ref, vp_ref, vc_ref, qg_ref, kg_ref, sl_ref, sk_ref, o_ref):
        n = pl.program_id(0)
        valid, absdist = _attn_band(n)
        for kv in range(ATTN_KV):
            _, _, kn = _rms(jnp.concatenate([kp_ref[kv], kc_ref[kv]], axis=0), kg_ref[...])
            kn_b = kn.astype(BF16)
            v_b = jnp.concatenate([vp_ref[kv], vc_ref[kv]], axis=0).astype(BF16)
            for g in range(ATTN_GROUP):
                head = kv * ATTN_GROUP + g
                _, _, qn = _rms(q_ref[head], qg_ref[...])
                p, _ = _attn_head_probs(qn, kn_b, valid, absdist, sl_ref[head], sk_ref[head])
                o_ref[head] = jnp.dot(p.astype(BF16), v_b, preferred_element_type=F32).astype(BF16)

    qspec = pl.BlockSpec((ATTN_HEADS, ATTN_BLOCK, ATTN_HD), lambda n: (0, n, 0))
    prev = pl.BlockSpec((ATTN_KV, ATTN_BLOCK, ATTN_HD), lambda n: (0, jnp.maximum(n - 1, 0), 0))
    cur = pl.BlockSpec((ATTN_KV, ATTN_BLOCK, ATTN_HD), lambda n: (0, n, 0))
    gain = pl.BlockSpec((1, ATTN_HD), lambda n: (0, 0))
    scalars = pl.BlockSpec(memory_space=pltpu.SMEM)
    return _call(body, name=name, grid=(nb,), out_shape=[jax.ShapeDtypeStruct(qh.shape, BF16)],
                 in_specs=[qspec, prev, cur, prev, cur, gain, gain, scalars, scalars], out_specs=[qspec],
                 sem=("parallel",), args=(qh, kh, kh, vh, vh, qg, kg, slopes, sinks), comm=comm)


def _rms_bwd(dy, xhat, r, gain):
    dxh = dy * gain
    dx = r * (dxh - xhat * jnp.mean(dxh * xhat, axis=-1, keepdims=True))
    return dx, jnp.sum(dy * xhat, axis=0, keepdims=True)


def _attn_bwd(qh, kh, vh, doh, qg, kg, slope_col, sink_col, name, comm=None):
    T = qh.shape[1]
    nb = T // ATTN_BLOCK
    rows = ATTN_GROUP * ATTN_BLOCK

    def body(q_ref, kp_ref, kc_ref, vp_ref, vc_ref, do_ref, qg_ref, kg_ref, sl_ref, sk_ref,
             dq_ref, dk_ref, dv_ref, dqg_ref, dkg_ref, dsk_ref, carry_ref, sk_acc):
        step = pl.program_id(0)
        n = nb - 1 - step
        qg_v, kg_v = qg_ref[...], kg_ref[...]

        @pl.when(step == 0)
        def _():
            dqg_ref[...] = jnp.zeros_like(dqg_ref)
            dkg_ref[...] = jnp.zeros_like(dkg_ref)
            carry_ref[...] = jnp.zeros_like(carry_ref)
            sk_acc[...] = jnp.zeros_like(sk_acc)

        for kv in range(ATTN_KV):
            heads = slice(kv * ATTN_GROUP, (kv + 1) * ATTN_GROUP)
            qhat, rq, qn = _rms(q_ref[heads].reshape(rows, ATTN_HD), qg_v)
            khat, rk, kn = _rms(jnp.concatenate([kp_ref[kv], kc_ref[kv]], axis=0), kg_v)
            s = _bdot(qn, kn, "nt") * ATTN_SCALE
            qi = lax.broadcasted_iota(jnp.int32, s.shape, 0) & (ATTN_BLOCK - 1)
            ki = lax.broadcasted_iota(jnp.int32, s.shape, 1)
            dist = qi + ATTN_BLOCK - ki
            first_key = jnp.where(n > 0, 0, ATTN_BLOCK)
            valid = (dist >= 0) & (dist < ATTN_BLOCK) & (ki >= first_key)
            s = jnp.where(valid, s - sl_ref[kv] * jnp.abs(dist).astype(F32), NEG_BIG)
            sink = sk_ref[kv]
            mx = jnp.maximum(jnp.max(s, axis=-1, keepdims=True), sink)
            e = jnp.exp(s - mx)
            es = jnp.exp(sink - mx)
            den = jnp.sum(e, axis=-1, keepdims=True) + es
            p, ps = e / den, es / den
            v = jnp.concatenate([vp_ref[kv], vc_ref[kv]], axis=0)
            do = do_ref[heads].reshape(rows, ATTN_HD)
            dp = _bdot(do, v, "nt")
            delta = jnp.sum(p * dp, axis=-1, keepdims=True)
            ds = p * (dp - delta)
            dqn = _bdot(ds, kn, "nn") * ATTN_SCALE
            dkn = _bdot(ds, qn, "tn") * ATTN_SCALE
            dv = _bdot(p, do, "tn")
            dq, dqg = _rms_bwd(dqn, qhat, rq, qg_v)
            dk, dkg = _rms_bwd(dkn, khat, rk, kg_v)
            dq_ref[heads] = dq.reshape(ATTN_GROUP, ATTN_BLOCK, ATTN_HD).astype(BF16)
            dqg_ref[0:1, :] += dqg
            dkg_ref[0:1, :] += dkg
            dk_ref[kv] = (dk[ATTN_BLOCK:, :] + carry_ref[kv, 0]).astype(BF16)
            dv_ref[kv] = (dv[ATTN_BLOCK:, :] + carry_ref[kv, 1]).astype(BF16)
            carry_ref[kv, 0] = dk[:ATTN_BLOCK, :]
            carry_ref[kv, 1] = dv[:ATTN_BLOCK, :]
            sk_acc[kv] += -ps * delta

        @pl.when(step == nb - 1)
        def _():
            for head in range(ATTN_HEADS):
                kv, g = divmod(head, ATTN_GROUP)
                tot = jnp.sum(sk_acc[kv, g * ATTN_BLOCK:(g + 1) * ATTN_BLOCK, :], axis=0, keepdims=True)
                dsk_ref[head:head + 1, :] = jnp.broadcast_to(tot, (1, 128))

    qspec = pl.BlockSpec((ATTN_HEADS, ATTN_BLOCK, ATTN_HD), lambda s: (0, nb - 1 - s, 0))
    prev = pl.BlockSpec((ATTN_KV, ATTN_BLOCK, ATTN_HD), lambda s: (0, jnp.maximum(nb - 2 - s, 0), 0))
    cur = pl.BlockSpec((ATTN_KV, ATTN_BLOCK, ATTN_HD), lambda s: (0, nb - 1 - s, 0))
    gain = pl.BlockSpec((1, ATTN_HD), lambda s: (0, 0))
    col = pl.BlockSpec((ATTN_KV, rows, 1), lambda s: (0, 0, 0))
    acc = pl.BlockSpec((8, ATTN_HD), lambda s: (0, 0))
    return _call(
        body, name=name, grid=(nb,),
        out_shape=[jax.ShapeDtypeStruct(qh.shape, BF16), jax.ShapeDtypeStruct(kh.shape, BF16),
                   jax.ShapeDtypeStruct(kh.shape, BF16), jax.ShapeDtypeStruct((8, ATTN_HD), F32),
                   jax.ShapeDtypeStruct((8, ATTN_HD), F32), jax.ShapeDtypeStruct((ATTN_HEADS, 128), F32)],
        in_specs=[qspec, prev, cur, prev, cur, qspec, gain, gain, col, col],
        out_specs=[qspec, cur, cur, acc, acc, pl.BlockSpec((ATTN_HEADS, 128), lambda s: (0, 0))],
        scratch_shapes=[pltpu.VMEM((ATTN_KV, 2, ATTN_BLOCK, ATTN_HD), F32), pltpu.VMEM((ATTN_KV, rows, 1), F32)],
        sem=("arbitrary",), args=(qh, kh, kh, vh, vh, doh, qg, kg, slope_col, sink_col), comm=comm)


@functools.partial(jax.custom_vjp, nondiff_argnums=(2,))
def _mm(a, b, dims):
    return _bdot(a, b, dims)


def _mm_fwd(a, b, dims):
    return _bdot(a, b, dims), (a, b)


def _mm_bwd(dims, res, ct):
    a, b = res
    if dims == "nn":
        return _bdot(ct, b, "nt"), _bdot(a, ct, "tn")
    if dims == "nt":
        return _bdot(ct, b, "nn"), _bdot(ct, a, "tn")
    return _bdot(b, ct, "nt"), _bdot(a, ct, "nn")


_mm.defvjp(_mm_fwd, _mm_bwd)


def _same_chunk_mask(rows, upper):
    ri = lax.broadcasted_iota(jnp.int32, (rows, rows), 0)
    ci = lax.broadcasted_iota(jnp.int32, (rows, rows), 1)
    same = (ri >> HGRN_CHUNK_SHIFT) == (ci >> HGRN_CHUNK_SHIFT)
    return same & ((ri <= ci) if upper else (ri >= ci))


@jax.custom_vjp
def _chunk_cumsum(x, tri, tri_t):
    hi = x.astype(BF16)
    rest = x - hi.astype(F32)
    mid = rest.astype(BF16)
    lo = (rest - mid.astype(F32)).astype(BF16)
    out = jnp.dot(tri, jnp.concatenate([hi, mid, lo], axis=1), preferred_element_type=F32)
    w = x.shape[1]
    return out[:, :w] + out[:, w:2 * w] + out[:, 2 * w:]


def _chunk_cumsum_fwd(x, tri, tri_t):
    return _chunk_cumsum(x, tri, tri_t), (tri, tri_t)


def _chunk_cumsum_bwd(res, ct):
    tri, tri_t = res
    return _chunk_cumsum(ct, tri_t, tri), jnp.zeros_like(tri), jnp.zeros_like(tri_t)


_chunk_cumsum.defvjp(_chunk_cumsum_fwd, _chunk_cumsum_bwd)


def _hgrn_masks(rows):
    nc = rows // HGRN_CHUNK
    lower = _same_chunk_mask(rows, False)
    chunk_of_row = lax.broadcasted_iota(jnp.int32, (rows, HGRN_DK), 0) >> HGRN_CHUNK_SHIFT
    row_in_chunk = lax.broadcasted_iota(jnp.int32, (nc, HGRN_CHUNK, HGRN_DK), 1)
    return dict(lower=lower, tri=lower.astype(BF16), tri_t=_same_chunk_mask(rows, True).astype(BF16),
                in_chunk=[chunk_of_row == c for c in range(nc)],
                mid_row=row_in_chunk == HGRN_CHUNK // 2 - 1, last_row=row_in_chunk == HGRN_CHUNK - 1)


def _hgrn_block(masks, st, qr, fr, v, gr, lb, og):
    rows = qr.shape[0]
    nc = rows // HGRN_CHUNK
    per_chunk = lambda m: m.reshape(nc, HGRN_CHUNK, HGRN_DK)
    flat = lambda m: m.reshape(rows, HGRN_DK)
    by_chunk = lambda m: jnp.concatenate([jnp.where(masks["in_chunk"][c], m, 0.0) for c in range(nc)], axis=1)

    forget = lb + (1.0 - lb) * jax.nn.sigmoid(fr)
    k = 1.0 - forget
    b = _chunk_cumsum(jnp.log(forget), masks["tri"], masks["tri_t"])
    b3 = per_chunk(b)
    piv = jnp.sum(jnp.where(masks["mid_row"], b3, 0.0), axis=1, keepdims=True)
    b_last = jnp.sum(jnp.where(masks["last_row"], b3, 0.0), axis=1, keepdims=True)
    q = _silu(qr) * HGRN_SCALE
    a = _mm(q * flat(jnp.exp(b3 - piv)), k * flat(jnp.exp(piv - b3)), "nt")
    o = _mm(jnp.where(masks["lower"], a, 0.0), v, "nn")
    updates = _mm(v, by_chunk(k * flat(jnp.exp(b_last - b3))), "tn")
    decay = jnp.exp(b_last)
    before = []
    for c in range(nc):
        before.append(st)
        st = st * decay[c] + updates[:, c * HGRN_DK:(c + 1) * HGRN_DK]
    o = o + _mm(by_chunk(q * jnp.exp(b)), jnp.concatenate(before, axis=1), "nt")
    y = (o * lax.rsqrt(jnp.mean(o * o, axis=-1, keepdims=True) + EPS)) * og * _silu(gr)
    return y, st


def _hgrn_tile(T):
    return min(T, 256)


HGRN_HEADS_PER_STEP = 16
HGRN_GROUPS = HGRN_HEADS // HGRN_HEADS_PER_STEP


def _hgrn_fwd(proj, lb, og, name, comm=None):
    T = proj.shape[0]
    tb = _hgrn_tile(T)
    hp, wide = HGRN_HEADS_PER_STEP, HGRN_HEADS_PER_STEP * HGRN_DK

    def body(q_ref, f_ref, v_ref, g_ref, lb_ref, og_ref, o_ref, s_ref, st_ref):
        @pl.when(pl.program_id(1) == 0)
        def _():
            st_ref[...] = jnp.zeros_like(st_ref)

        masks = _hgrn_masks(tb)
        for j in range(hp):
            ln = slice(j * HGRN_DK, (j + 1) * HGRN_DK)
            st = st_ref[j]
            s_ref[j, 0] = st
            y, st_ref[j] = _hgrn_block(masks, st, q_ref[:, ln], f_ref[:, ln], v_ref[:, ln], g_ref[:, ln], lb_ref[j],
                                       og_ref[j])
            o_ref[:, ln] = y.astype(BF16)

    part = lambda p: pl.BlockSpec((tb, wide), lambda h, t: (t, p * HGRN_GROUPS + h))
    vec = pl.BlockSpec((hp, 1, HGRN_DK), lambda h, t: (h, 0, 0))
    return _call(
        body, name=name, grid=(HGRN_GROUPS, T // tb),
        out_shape=[jax.ShapeDtypeStruct((T, D_MODEL), BF16),
                   jax.ShapeDtypeStruct((HGRN_HEADS, T // tb, HGRN_DK, HGRN_DK), F32)],
        in_specs=[part(0), part(1), part(2), part(3), vec, vec],
        out_specs=[pl.BlockSpec((tb, wide), lambda h, t: (t, h)),
                   pl.BlockSpec((hp, 1, HGRN_DK, HGRN_DK), lambda h, t: (h, t, 0, 0))],
        scratch_shapes=[pltpu.VMEM((hp, HGRN_DK, HGRN_DK), F32)], sem=("parallel", "arbitrary"),
        args=(proj, proj, proj, proj, lb, og), comm=comm)


def _hgrn_bwd(proj, states, do, lb, og, name, comm=None):
    T = proj.shape[0]
    tb = _hgrn_tile(T)
    nt, hp, wide = T // tb, HGRN_HEADS_PER_STEP, HGRN_HEADS_PER_STEP * HGRN_DK

    def body(q_ref, f_ref, v_ref, g_ref, s_ref, do_ref, lb_ref, og_ref, dp_ref, dlb_ref, dog_ref, dst_ref):
        @pl.when(pl.program_id(1) == 0)
        def _():
            dst_ref[...] = jnp.zeros_like(dst_ref)
            dlb_ref[...] = jnp.zeros_like(dlb_ref)
            dog_ref[...] = jnp.zeros_like(dog_ref)

        block = functools.partial(_hgrn_block, _hgrn_masks(tb))
        for j in range(hp):
            ln = slice(j * HGRN_DK, (j + 1) * HGRN_DK)
            _, vjp = jax.vjp(block, s_ref[j, 0], q_ref[:, ln], f_ref[:, ln], v_ref[:, ln], g_ref[:, ln],
                             lb_ref[j], og_ref[j])
            dst_ref[j], dq, df, dv, dg, dlb, dog = vjp((do_ref[:, ln], dst_ref[j]))
            for p, part_grad in enumerate((dq, df, dv, dg)):
                dp_ref[p, :, ln] = part_grad.astype(BF16)
            dlb_ref[j] += dlb
            dog_ref[j] += dog

    part = lambda p: pl.BlockSpec((tb, wide), lambda h, t: (nt - 1 - t, p * HGRN_GROUPS + h))
    vec = pl.BlockSpec((hp, 1, HGRN_DK), lambda h, t: (h, 0, 0))
    head = pl.BlockSpec((tb, wide), lambda h, t: (nt - 1 - t, h))
    return _call(
        body, name=name, grid=(HGRN_GROUPS, nt),
        out_shape=[jax.ShapeDtypeStruct((4, T, D_MODEL), BF16)] + [jax.ShapeDtypeStruct((HGRN_HEADS, 1, HGRN_DK), F32)] * 2,
        in_specs=[part(0), part(1), part(2), part(3),
                  pl.BlockSpec((hp, 1, HGRN_DK, HGRN_DK), lambda h, t: (h, nt - 1 - t, 0, 0)), head, vec, vec],
        out_specs=[pl.BlockSpec((4, tb, wide), lambda h, t: (0, nt - 1 - t, h)), vec, vec],
        scratch_shapes=[pltpu.VMEM((hp, HGRN_DK, HGRN_DK), F32)], sem=("parallel", "arbitrary"),
        args=(proj, proj, proj, proj, states, do, lb, og), comm=comm)


def _cols_to_blocks(g, n8):
    K = g.shape[0]
    return g.reshape(K, N_DEV, n8).transpose(1, 0, 2)


def _blocks_to_cols(wg):
    _, K, n8 = wg.shape
    return wg.transpose(1, 0, 2).reshape(K, N_DEV * n8)


def _pack(parts):
    flat = []
    for p in parts:
        v = p.reshape(-1)
        flat.append(jnp.pad(v, (0, (-v.shape[0]) % 1024)))
    return jnp.concatenate(flat).reshape(-1, 128)


def _unpack(packed, like):
    flat, out, off = packed.reshape(-1), [], 0
    for p in like:
        size = math.prod(p.shape)
        out.append(flat[off:off + size].reshape(p.shape))
        off += size + (-size) % 1024
    return out


def _heads_major(a, heads):
    T = a.shape[0]
    return a.reshape(T, heads, ATTN_HD).transpose(1, 0, 2)


def _heads_minor(a):
    heads, T, _ = a.shape
    return a.transpose(1, 0, 2).reshape(T, heads * ATTN_HD)


def kernel(x, c, mod_w, mod_b, norm_mix, norm_mlp, attn_w_in, attn_w_out, attn_q_gain, attn_k_gain, attn_sinks, hgrn_w_in, hgrn_w_out, hgrn_o_gain, hgrn_lb_logits, mlp_w1, mlp_w2, loss_target, m_mod_w, m_mod_b, m_norm_mix, m_norm_mlp, m_attn_w_in, m_attn_w_out, m_attn_q_gain, m_attn_k_gain, m_attn_sinks, m_hgrn_w_in, m_hgrn_w_out, m_hgrn_o_gain, m_hgrn_lb_logits, m_mlp_w1, m_mlp_w2, v_mod_w, v_mod_b, v_norm_mix, v_norm_mlp, v_attn_w_in, v_attn_w_out, v_attn_q_gain, v_attn_k_gain, v_attn_sinks, v_hgrn_w_in, v_hgrn_w_out, v_hgrn_o_gain, v_hgrn_lb_logits, v_mlp_w1, v_mlp_w2):
    T = x.shape[1]
    me = 4 * lax.axis_index("x") + 2 * lax.axis_index("y") + lax.axis_index("c")
    x0, target = x[0], loss_target[0]
    n_mod = mod_w.shape[2]

    shards = [attn_w_in[0], attn_w_out[0], hgrn_w_in[0], hgrn_w_out[0], mlp_w1[0], mlp_w1[1], mlp_w2[0], mlp_w2[1]]
    sb = [s.astype(BF16) for s in shards]
    gather = lambda *items: _Exchange([("gather_by_chip", a, axis) for a, axis in items])

    c_all = _exchange([("gather", c.reshape(16, 128), None)], vmem=True, name="gather_c")[0].reshape(N_DEV, D_MODEL)
    mod_b_cols = lax.dynamic_slice_in_dim(mod_b, me * n_mod, n_mod, axis=1).reshape(2, 1, n_mod)
    mod_cols = _mod_fwd(c_all, mod_w, mod_b_cols, "mod_fwd")
    mod_all = _exchange([("gather", mod_cols.reshape(-1, 128), None)], vmem=True, name="gather_mod")[0]
    mod_all = mod_all.reshape(N_DEV, 2, N_DEV, n_mod)
    mod_mine = lax.dynamic_index_in_dim(mod_all, me, axis=2, keepdims=False)
    mod_mine = mod_mine.transpose(1, 0, 2).reshape(2, N_MOD, 1, D_MODEL)

    lb = _lb_fwd(hgrn_lb_logits, "lb_fwd").reshape(HGRN_HEADS, 1, HGRN_DK)
    og = hgrn_o_gain.reshape(HGRN_HEADS, 1, HGRN_DK)
    slopes = jnp.exp2(-8.0 * jnp.arange(1, ATTN_HEADS + 1, dtype=F32) / ATTN_HEADS)
    sinks = attn_sinks[0]
    per_row = lambda vals: jnp.repeat(vals.reshape(ATTN_KV, ATTN_GROUP), ATTN_BLOCK, axis=1).reshape(
        ATTN_KV, ATTN_GROUP * ATTN_BLOCK, 1)
    slope_col, sink_col = per_row(slopes), per_row(sinks)

    saved = []
    xi = x0
    w_mlp1, w_mlp2 = [None, None], [None, None]
    for i in range(2):
        sh1, sc1, g1, sh2, sc2, g2 = [mod_mine[i, j] for j in range(N_MOD)]
        if i == 0:
            h, w_attn_in = _norm_mod_fwd(xi, norm_mix[i:i + 1], sc1, sh1, "norm_mix_fwd0", comm=gather((sb[0], None)))
            w_attn_in = _blocks_to_cols(w_attn_in)
            qh, kh, vh, w_attn_out = _matmul(h, w_attn_in, dims="nn", tm=1024, tn=512, tk=2048, name="attn_in_fwd",
                                             epilogue="heads", comm=gather((sb[1], 0)))
            o, w_mlp1[0] = _attn_fwd(qh, kh, vh, attn_q_gain, attn_k_gain, slopes, sinks, "attn_fwd",
                                     comm=gather((sb[4], 1)))
            mix = (qh, kh, vh)
            w_out = w_attn_out
        else:
            h = _norm_mod_fwd(xi, norm_mix[i:i + 1], sc1, sh1, "norm_mix_fwd1")
            proj, w_hgrn_out = _matmul(h, w_hgrn_in, dims="nn", tm=1024, tn=1024, tk=2048, name="hgrn_in_fwd",
                                       comm=gather((sb[3], 0)))
            o, states, w_mlp1[1] = _hgrn_fwd(proj, lb, og, "hgrn_fwd", comm=gather((sb[5], 1)))
            mix = (proj, states)
            w_out = w_hgrn_out
        y, x1 = _matmul(o, w_out, dims="nn", tm=1024, tn=1024, tk=2048, name=f"mix_out_fwd{i}", epilogue="resgate",
                        extras=(xi, g1), a_heads=i == 0)
        h2 = _norm_mod_fwd(x1, norm_mlp[i:i + 1], sc2, sh2, f"norm_mlp_fwd{i}")
        if i == 0:
            act, act2, w_mlp2[0] = _matmul(h2, w_mlp1[0], dims="nn", tm=1024, tn=1024, tk=2048, name="mlp1_fwd0",
                                           epilogue="relu2", comm=gather((sb[6], 0)))
            z, x2, w_hgrn_in = _matmul(act2, w_mlp2[0], dims="nn", tm=1024, tn=1024, tk=2048, name="mlp2_fwd0",
                                       epilogue="resgate", extras=(x1, g2), comm=gather((sb[2], 1)))
        else:
            act, act2, w_mlp2[1] = _matmul(h2, w_mlp1[1], dims="nn", tm=1024, tn=1024, tk=2048, name="mlp1_fwd1",
                                           epilogue="relu2", comm=gather((sb[7], 0)))
            z, x2 = _matmul(act2, w_mlp2[1], dims="nn", tm=1024, tn=1024, tk=2048, name="mlp2_fwd1", epilogue="resgate",
                            extras=(x1, g2))
        saved.append((xi, h, o, y, x1, h2, act, act2, z, mix))
        xi = x2

    dx, loss_tile, st_g2, dz = _loss_head(xi, target, (saved[1][8], mod_mine[1, 5]), "loss_head")
    loss = lax.psum(loss_tile[0, 0], ("x", "y", "c"))

    scatter = lambda *items: _Exchange([("scatter", a, axis) for a, axis in items])
    shares, dmods, dnorm_mix, dnorm_mlp = {}, [None, None], [None, None], [None, None]
    wgrad = lambda a, b, name, tn=1024: _matmul(a, b, dims="tn", tm=2048, tn=tn, tk=2048, name=name, out_dtype=BF16)
    for i in (1, 0):
        sh1, sc1, g1, sh2, sc2, g2 = [mod_mine[i, j] for j in range(N_MOD)]
        xin, h, o, y, x1, h2, act, act2, z, mix = saved[i]
        if i == 1:
            dpre = _matmul(dz, w_mlp2[1], dims="nt", tm=1024, tn=1024, tk=2048, name="mlp2_bwd1", epilogue="mul2a",
                           extras=(act,))
        else:
            dpre, shares["hgrn_w_in"] = _matmul(dz, w_mlp2[0], dims="nt", tm=1024, tn=1024, tk=2048, name="mlp2_bwd0",
                                                epilogue="mul2a", extras=(act,), comm=scatter((g_hgrn_in, 1)))
        g_mlp2 = wgrad(act2, dz, f"mlp2_wgrad{i}")
        if i == 1:
            dh2 = _matmul(dpre, w_mlp1[1], dims="nt", tm=1024, tn=1024, tk=4096, name="mlp1_bwd1", out_dtype=BF16)
        else:
            dh2, shares["mlp_w2_0"] = _matmul(dpre, w_mlp1[0], dims="nt", tm=1024, tn=1024, tk=4096, name="mlp1_bwd0",
                                              out_dtype=BF16, comm=scatter((g_mlp2, 0)))
        g_mlp1 = wgrad(h2, dpre, f"mlp1_wgrad{i}")
        dx1, st_mlp, dy = _norm_mod_bwd(x1, dh2, dx, norm_mlp[i:i + 1], sc2, f"norm_mlp_bwd{i}", below=(y, g1))
        w_out = w_attn_out if i == 0 else w_hgrn_out
        if i == 0:
            qh, kh, vh = mix
            doh = _matmul(dy, w_out, dims="nt", tm=1024, tn=512, tk=2048, name="mix_out_bwd0", epilogue="heads",
                          out_dtype=BF16)
            g_out = _matmul(o, dy, dims="tn", tm=1024, tn=1024, tk=2048, name="mix_out_wgrad0", out_dtype=BF16,
                            a_heads=True)
            dqh, dkh, dvh, dqg, dkg, dsk, shares["mlp_w1_0"] = _attn_bwd(
                qh, kh, vh, doh, attn_q_gain, attn_k_gain, slope_col, sink_col, "attn_bwd",
                comm=scatter((g_mlp1, 1)))
            dproj = jnp.concatenate([dqh, dkh, dvh], axis=0)
            d_q_gain, d_k_gain, d_sinks = dqg[0:1], dkg[0:1], dsk[:, 0].reshape(1, ATTN_HEADS)
            g_attn_in, shares["attn_w_out"] = _matmul(h, dproj, dims="tn", tm=2048, tn=640, tk=2048, name="mix_in_wgrad0",
                                                      out_dtype=BF16, b_heads=True, comm=scatter((g_out, 0)))
            g_attn_in = _cols_to_blocks(g_attn_in, attn_w_in.shape[2])
            dh, shares["attn_w_in"] = _matmul(dproj, w_attn_in, dims="nt", tm=1024, tn=1024, tk=2560, out_dtype=BF16,
                                              name="mix_in_bwd0", comm=scatter((g_attn_in, None)), a_heads=True)
        else:
            do = _matmul(dy, w_out, dims="nt", tm=1024, tn=1024, tk=2048, name="mix_out_bwd1")
            g_out = wgrad(o, dy, "mix_out_wgrad1")
            proj, states = mix
            dproj, dlb, d_o_gain, shares["mlp_w2_1"], shares["mlp_w1_1"], shares["hgrn_w_out"] = _hgrn_bwd(
                proj, states, do, lb, og, "hgrn_bwd", comm=scatter((g_mlp2, 0), (g_mlp1, 1), (g_out, 0)))
            d_lb_logits = _lb_bwd(hgrn_lb_logits, dlb.reshape(1, D_MODEL), "lb_bwd")
            dh = _matmul(dproj, w_hgrn_in, dims="nt", tm=1024, tn=1024, tk=2048, name="mix_in_bwd1", out_dtype=BF16)
            g_hgrn_in = wgrad(h, dproj, "mix_in_wgrad1")
        d_gate2 = st_g2[0:1] if i == 1 else st_mix_above[3:4]
        if i == 1:
            dx, st_mix, dz = _norm_mod_bwd(xin, dh, dx1, norm_mix[i:i + 1], sc1, "norm_mix_bwd1",
                                           below=(saved[0][8], mod_mine[0, 5]))
            st_mix_above = st_mix
        else:
            dx, st_mix = _norm_mod_bwd(xin, dh, dx1, norm_mix[i:i + 1], sc1, "norm_mix_bwd0")
        dmods[i] = jnp.concatenate([st_mix[0:1], st_mix[1:2], st_mlp[3:4], st_mlp[0:1], st_mlp[1:2], d_gate2], axis=1)
        dnorm_mix[i], dnorm_mlp[i] = st_mix[2:3], st_mlp[2:3]

    single = {"attn_w_in": (attn_w_in, m_attn_w_in, v_attn_w_in), "attn_w_out": (attn_w_out, m_attn_w_out, v_attn_w_out),
              "hgrn_w_in": (hgrn_w_in, m_hgrn_w_in, v_hgrn_w_in), "hgrn_w_out": (hgrn_w_out, m_hgrn_w_out, v_hgrn_w_out)}
    big = {nm: _sum_adamw(shares[nm], w[0], m[0], v[0], f"adamw_{nm}") for nm, (w, m, v) in single.items()}
    big["mlp_w1"] = _sum_adamw_layers([shares["mlp_w1_0"], shares["mlp_w1_1"]], mlp_w1, m_mlp_w1, v_mlp_w1, "adamw_mlp_w1")
    big["mlp_w2"] = _sum_adamw_layers([shares["mlp_w2_0"], shares["mlp_w2_1"]], mlp_w2, m_mlp_w2, v_mlp_w2, "adamw_mlp_w2")

    small_w = [mod_b, norm_mix, norm_mlp, attn_q_gain, attn_k_gain, attn_sinks, hgrn_o_gain, hgrn_lb_logits]
    small_m = [m_mod_b, m_norm_mix, m_norm_mlp, m_attn_q_gain, m_attn_k_gain, m_attn_sinks, m_hgrn_o_gain, m_hgrn_lb_logits]
    small_v = [v_mod_b, v_norm_mix, v_norm_mlp, v_attn_q_gain, v_attn_k_gain, v_attn_sinks, v_hgrn_o_gain, v_hgrn_lb_logits]
    small_g = [jnp.concatenate(dmods, axis=0), jnp.concatenate(dnorm_mix, axis=0), jnp.concatenate(dnorm_mlp, axis=0),
               d_q_gain, d_k_gain, d_sinks, d_o_gain.reshape(hgrn_o_gain.shape), d_lb_logits]
    packed_g = _pack(small_g)
    pad_rows = (-packed_g.shape[0]) % 8
    pad8 = lambda a: jnp.pad(a, ((0, pad_rows), (0, 0)))
    all_small = _exchange([("gather", pad8(packed_g), None)], vmem=True, name="gather_small_grads")[0]
    sg, sd, sm, sv = _sum_adamw(all_small, pad8(_pack(small_w)), pad8(_pack(small_m)), pad8(_pack(small_v)),
                                "adamw_small")
    small = [_unpack(t, small_w) for t in (sg, sd, sm, sv)]

    n_modb = N_MOD * D_MODEL
    dmod_all = all_small[:, :2 * n_modb // 128, :].reshape(N_DEV, 2, n_modb)
    dmod_cols = lax.dynamic_slice_in_dim(dmod_all, me * n_mod, n_mod, axis=2).transpose(1, 0, 2)
    modw = _mod_w_update(c_all.T, dmod_cols, mod_w, m_mod_w, v_mod_w, "adamw_mod_w")

    def leaf(k):
        one = lambda a: big[a][k][None]
        s = small[k]
        return [modw[k], s[0], s[1], s[2], one("attn_w_in"), one("attn_w_out"), s[3], s[4], s[5], one("hgrn_w_in"),
                one("hgrn_w_out"), s[6], s[7], big["mlp_w1"][k], big["mlp_w2"][k]]

    return (loss, dx[None], *leaf(0), *leaf(1), *leaf(2), *leaf(3))
```

```python
import functools
import math

import jax
import jax.numpy as jnp
from jax import lax
from jax.experimental import pallas as pl
from jax.experimental.pallas import tpu as pltpu

F32, BF16 = jnp.float32, jnp.bfloat16
N_DEV = 8
D_MODEL = 2048
N_MOD = 6
EPS = 1e-6
ATTN_HD, ATTN_HEADS, ATTN_KV, ATTN_GROUP, ATTN_BLOCK = 64, 32, 4, 8, 128
ATTN_SCALE = 1.0 / math.sqrt(ATTN_HD)
HGRN_HEADS, HGRN_DK, HGRN_CHUNK = 16, 128, 64
HGRN_SCALE = 1.0 / math.sqrt(HGRN_DK)
HGRN_CHUNK_SHIFT = HGRN_CHUNK.bit_length() - 1
assert 1 << HGRN_CHUNK_SHIFT == HGRN_CHUNK
D_FF = 4 * D_MODEL
ADAM_LR, ADAM_B1, ADAM_B2, ADAM_EPS, ADAM_WD, ADAM_STEP = 0.001, 0.9, 0.999, 1e-08, 0.01, 10
NEG_BIG = -1e30
VMEM_LIMIT = 56 * 1024 * 1024
MESH_ID = pl.DeviceIdType.MESH


def _params(*sem):
    return pltpu.CompilerParams(dimension_semantics=sem, vmem_limit_bytes=VMEM_LIMIT)


class _Exchange:
    def __init__(self, items):
        self.items = items
        self.n = len(items)
        self.out_shape = []
        for kind, a, axis in items:
            assert kind in ("gather", "gather_by_chip", "scatter")
            if kind != "scatter":
                shape = (N_DEV,) + a.shape if axis is None else tuple(
                    d * N_DEV if i == axis else d for i, d in enumerate(a.shape))
            else:
                shape = a.shape if axis is None else (N_DEV,) + tuple(
                    d // N_DEV if i == axis else d for i, d in enumerate(a.shape))
            self.out_shape.append(jax.ShapeDtypeStruct(shape, a.dtype))
        self.scratch = [pltpu.SemaphoreType.DMA((7 * self.n,)), pltpu.SemaphoreType.DMA((7 * self.n,)),
                        pltpu.SemaphoreType.DMA((self.n,))]
        self.arrays = [a for _, a, _ in items]

    @staticmethod
    def _block(ref, b, axis, size):
        if axis is None:
            return ref.at[b]
        sl = pl.ds(pl.multiple_of(b * size, size), size)
        return ref.at[sl, :] if axis == 0 else ref.at[:, sl]

    def _plan(self, ins, outs, sems):
        send_sems, recv_sems, loc_sems = sems
        x, y, c = lax.axis_index("x"), lax.axis_index("y"), lax.axis_index("c")
        me = 4 * x + 2 * y + c
        begin, middle, end = [], [], []

        def peer(d):
            px = 1 - x if d & 4 else x
            py = 1 - y if d & 2 else y
            pc = 1 - c if d & 1 else c
            return (px, py, pc), 4 * px + 2 * py + pc

        for a, (kind, arr, axis) in enumerate(self.items):
            gather = kind != "scatter"
            size = None if axis is None else (arr.shape[axis] if gather else arr.shape[axis] // N_DEV)

            def src(b):
                return ins[a] if gather else self._block(ins[a], b, axis, size)

            def dst(b):
                return self._block(outs[a], b, axis, size) if gather else outs[a].at[b]

            def remote(src_ref, dst_ref, slot, dev):
                return pltpu.make_async_remote_copy(
                    src_ref=src_ref, dst_ref=dst_ref, send_sem=send_sems.at[a * 7 + slot],
                    recv_sem=recv_sems.at[a * 7 + slot], device_id=dev, device_id_type=MESH_ID)

            local = pltpu.make_async_copy(src(me), dst(me), loc_sems.at[a])
            begin.append(local)
            end.append(local.wait)
            if kind == "gather_by_chip":
                sib_dev, sib_id = peer(1)
                to_sib = remote(ins[a], dst(me), 0, sib_dev)
                begin.append(to_sib)
                end += [to_sib.wait_send, remote(ins[a], dst(sib_id), 0, sib_dev).wait_recv]
                for j, d in enumerate((2, 4, 6)):
                    dev, pid = peer(d)
                    over_ici = remote(ins[a], dst(me), 1 + j, dev)
                    begin.append(over_ici)
                    passed_on = remote(dst(pid), dst(pid), 4 + j, sib_dev)
                    middle.append((remote(ins[a], dst(pid), 1 + j, dev), passed_on))
                    end += [over_ici.wait_send, passed_on.wait_send,
                            remote(ins[a], dst(peer(d ^ 1)[1]), 4 + j, sib_dev).wait_recv]
            else:
                for d in range(1, N_DEV):
                    dev, pid = peer(d)
                    begin.append(remote(src(pid), dst(me), d - 1, dev))
                    end.append(remote(src(pid), dst(pid), d - 1, dev).wait)
        return begin, middle, end

    def start(self, ins, outs, sems):
        for cp in self._plan(ins, outs, sems)[0]:
            cp.start()

    def pass_on(self, ins, outs, sems):
        for arrived, onward in self._plan(ins, outs, sems)[1]:
            arrived.wait_recv()
            onward.start()

    def wait(self, ins, outs, sems):
        for wait in self._plan(ins, outs, sems)[2]:
            wait()


def _call(body, *, name, grid, in_specs, out_specs, out_shape, args, scratch_shapes=(), sem=None, comm=None):
    n_in, n_out, n_scr = len(in_specs), len(out_specs), len(scratch_shapes)
    if comm is None:
        return pl.pallas_call(
            body, name=name, grid=grid, out_shape=list(out_shape), in_specs=list(in_specs), out_specs=list(out_specs),
            scratch_shapes=list(scratch_shapes), compiler_params=_params(*sem))(*args)
    hbm = pl.BlockSpec(memory_space=pltpu.HBM)

    def carrier(*refs):
        bounds = [0, n_in, n_in + comm.n, n_in + comm.n + n_out, n_in + 2 * comm.n + n_out, len(refs) - 3, len(refs)]
        ins, cin, outs, cout, scr, sems = [refs[lo:hi] for lo, hi in zip(bounds[:-1], bounds[1:])]
        assert len(scr) == n_scr
        step = functools.reduce(lambda lin, ax: lin * grid[ax] + pl.program_id(ax), range(len(grid)), 0)
        steps = math.prod(grid)

        @pl.when(step == 0)
        def _():
            comm.start(cin, cout, sems)

        body(*ins, *outs, *scr)

        @pl.when(step == (2 * steps) // 3)
        def _():
            comm.pass_on(cin, cout, sems)

        @pl.when(step == steps - 1)
        def _():
            comm.wait(cin, cout, sems)

    return pl.pallas_call(
        carrier, name=name, grid=grid, out_shape=list(out_shape) + comm.out_shape,
        in_specs=list(in_specs) + [hbm] * comm.n, out_specs=list(out_specs) + [hbm] * comm.n,
        scratch_shapes=list(scratch_shapes) + comm.scratch,
        compiler_params=_params(*["arbitrary"] * len(grid)))(*args, *comm.arrays)


def _exchange(items, *, name, vmem=False):
    comm = _Exchange(items)
    space = pl.BlockSpec(memory_space=pltpu.VMEM if vmem else pltpu.HBM)

    def body(*refs):
        ins, outs, sems = refs[:comm.n], refs[comm.n:2 * comm.n], refs[2 * comm.n:]
        comm.start(ins, outs, sems)
        comm.pass_on(ins, outs, sems)
        comm.wait(ins, outs, sems)

    return pl.pallas_call(
        body, name=name, out_shape=comm.out_shape, in_specs=[space] * comm.n, out_specs=[space] * comm.n,
        scratch_shapes=comm.scratch, compiler_params=pltpu.CompilerParams(vmem_limit_bytes=VMEM_LIMIT))(*comm.arrays)


_DIMS = {"nn": (((1,), (0,)), ((), ())), "nt": (((1,), (1,)), ((), ())), "tn": (((0,), (0,)), ((), ()))}


def _matmul(a, b, *, dims, tm, tn, tk, name, epilogue="plain", out_dtype=F32, extras=(), comm=None,
            a_heads=False, b_heads=False):
    a_parts = a.shape[0] if a.ndim == 3 else 0
    b_parts = b.shape[0] if b.ndim == 3 else 0
    assert not (a_parts and dims != "nt" and not a_heads) and not (b_parts and dims != "tn")
    a2 = (a.shape[1], a.shape[2] * a_parts) if a_parts else a.shape
    b2 = (b.shape[1], b.shape[2] * b_parts) if b_parts else b.shape
    if dims == "tn":
        (K, M), N = a2, b2[1]
    else:
        (M, K), N = a2, (b2[1] if dims == "nn" else b2[0])
    tm, tn, tk = min(tm, M), min(tn, N), min(tk, K)
    assert M % tm == 0 and N % tn == 0 and K % tk == 0, (name, M, N, K, tm, tn, tk)
    if a_heads and dims == "tn":
        a_spec = pl.BlockSpec((tm // ATTN_HD, tk, ATTN_HD), lambda i, j, k: (i, k, 0))
    elif a_heads:
        a_spec = pl.BlockSpec((tk // ATTN_HD, tm, ATTN_HD), lambda i, j, k: (k, i, 0))
    elif a_parts:
        per = K // a_parts // tk
        a_spec = pl.BlockSpec((None, tm, tk), lambda i, j, k: (k // per, i, k % per))
    elif dims == "tn":
        a_spec = pl.BlockSpec((tk, tm), lambda i, j, k: (k, i))
    else:
        a_spec = pl.BlockSpec((tm, tk), lambda i, j, k: (i, k))
    if b_heads:
        b_spec = pl.BlockSpec((tn // ATTN_HD, tk, ATTN_HD), lambda i, j, k: (j, k, 0))
    elif b_parts:
        per_n = N // b_parts // tn
        b_spec = pl.BlockSpec((None, tk, tn), lambda i, j, k: (j // per_n, k, j % per_n))
    elif dims == "nt":
        b_spec = pl.BlockSpec((tn, tk), lambda i, j, k: (j, k))
    else:
        b_spec = pl.BlockSpec((tk, tn), lambda i, j, k: (k, j))
    side_by_side = lambda ref: jnp.concatenate([ref[g] for g in range(ref.shape[0])], axis=1)
    nk = K // tk
    tile = pl.BlockSpec((tm, tn), lambda i, j, k: (i, j))
    row = pl.BlockSpec((1, tn), lambda i, j, k: (0, j))
    if epilogue == "plain":
        extra_specs, out_shape, out_specs = [], [jax.ShapeDtypeStruct((M, N), out_dtype)], [tile]
    elif epilogue == "relu2":
        extra_specs, out_shape, out_specs = [], [jax.ShapeDtypeStruct((M, N), BF16)] * 2, [tile, tile]
    elif epilogue == "resgate":
        extra_specs, out_specs = [tile, row], [tile, tile]
        out_shape = [jax.ShapeDtypeStruct((M, N), BF16), jax.ShapeDtypeStruct((M, N), F32)]
    elif epilogue == "mul2a":
        extra_specs, out_shape, out_specs = [tile], [jax.ShapeDtypeStruct((M, N), BF16)], [tile]
    elif epilogue == "heads":
        per_tile = tn // ATTN_HD
        q_tiles = ATTN_HEADS // per_tile
        assert tn == 2 * ATTN_KV * ATTN_HD and N // tn in (q_tiles, q_tiles + 1)
        extra_specs = []
        out_shape = [jax.ShapeDtypeStruct((ATTN_HEADS, M, ATTN_HD), out_dtype)]
        out_specs = [pl.BlockSpec((per_tile, tm, ATTN_HD), lambda i, j, k: (jnp.minimum(j, q_tiles - 1), i, 0))]
        if N // tn > q_tiles:
            out_shape += [jax.ShapeDtypeStruct((ATTN_KV, M, ATTN_HD), out_dtype)] * 2
            out_specs += [pl.BlockSpec((ATTN_KV, tm, ATTN_HD), lambda i, j, k: (0, i, 0))] * 2
    else:
        raise ValueError(epilogue)
    n_extra = len(extra_specs)

    def body(a_ref, b_ref, *rest):
        ex, outs, acc_ref = rest[:n_extra], rest[n_extra:-1], rest[-1]
        k = pl.program_id(2)

        @pl.when(k == 0)
        def _():
            acc_ref[...] = jnp.zeros_like(acc_ref)

        a_tile = side_by_side(a_ref) if a_heads else a_ref[...]
        b_tile = side_by_side(b_ref) if b_heads else b_ref[...]
        acc_ref[...] += lax.dot_general(a_tile, b_tile, _DIMS[dims], preferred_element_type=F32)

        if epilogue == "heads":
            j = pl.program_id(1)

            @pl.when((k == nk - 1) & (j < q_tiles))
            def _():
                for g in range(per_tile):
                    outs[0][g] = acc_ref[:, g * ATTN_HD:(g + 1) * ATTN_HD].astype(out_dtype)

            if len(outs) > 1:
                @pl.when((k == nk - 1) & (j == q_tiles))
                def _():
                    for g in range(ATTN_KV):
                        outs[1][g] = acc_ref[:, g * ATTN_HD:(g + 1) * ATTN_HD].astype(out_dtype)
                        outs[2][g] = acc_ref[:, (ATTN_KV + g) * ATTN_HD:(ATTN_KV + g + 1) * ATTN_HD].astype(out_dtype)
            return

        @pl.when(k == nk - 1)
        def _():
            acc = acc_ref[...]
            if epilogue == "plain":
                outs[0][...] = acc.astype(out_dtype)
            elif epilogue == "relu2":
                act = jnp.maximum(acc, 0.0)
                outs[0][...] = act.astype(BF16)
                outs[1][...] = (act * act).astype(BF16)
            elif epilogue == "resgate":
                outs[0][...] = acc.astype(BF16)
                outs[1][...] = ex[0][...] + ex[1][...] * acc
            else:
                outs[0][...] = (acc * (2.0 * ex[0][...].astype(F32))).astype(BF16)

    res = _call(body, name=name, grid=(M // tm, N // tn, nk), out_shape=out_shape,
                in_specs=[a_spec, b_spec] + extra_specs, out_specs=out_specs,
                scratch_shapes=[pltpu.VMEM((tm, tn), F32)],
                sem=("parallel", "arbitrary" if epilogue == "heads" else "parallel", "arbitrary"),
                args=(a, b, *extras), comm=comm)
    return res[0] if len(res) == 1 else res


def _row_tile(T):
    return min(T, 256)


def _norm_mod_fwd(x, gain, sc, sh, name, comm=None):
    T, D = x.shape
    tr = min(T, 2 * _row_tile(T))

    def body(x_ref, g_ref, sc_ref, sh_ref, h_ref):
        xv = x_ref[...]
        r = lax.rsqrt(jnp.mean(xv * xv, axis=-1, keepdims=True) + EPS)
        hn = (xv * r) * g_ref[...]
        h_ref[...] = (hn * (1.0 + sc_ref[...]) + sh_ref[...]).astype(BF16)

    vec = pl.BlockSpec((1, D), lambda i: (0, 0))
    res = _call(body, name=name, grid=(T // tr,), out_shape=[jax.ShapeDtypeStruct((T, D), BF16)],
                in_specs=[pl.BlockSpec((tr, D), lambda i: (i, 0)), vec, vec, vec],
                out_specs=[pl.BlockSpec((tr, D), lambda i: (i, 0))], sem=("parallel",), args=(x, gain, sc, sh),
                comm=comm)
    return res[0] if comm is None else res


def _through_gate(dx, branch_ref, gate_ref, dbranch_ref, st_ref, row):
    dbranch_ref[...] = (dx * gate_ref[...]).astype(BF16)
    st_ref[row:row + 1, :] += jnp.sum(dx * branch_ref[...].astype(F32), axis=0, keepdims=True)


def _norm_mod_bwd(x, dh, dres, gain, sc, name, below=None):
    T, D = x.shape
    tr = _row_tile(T)

    def body(x_ref, dh_ref, dres_ref, g_ref, sc_ref, *rest):
        dx_ref, st_ref = rest[-3:-1] if below else rest[-2:]
        xv, dh_v, gain_v = x_ref[...], dh_ref[...].astype(F32), g_ref[...]
        r = lax.rsqrt(jnp.mean(xv * xv, axis=-1, keepdims=True) + EPS)
        xn = xv * r
        hn = xn * gain_v
        dhn = dh_v * (1.0 + sc_ref[...])
        dxn = dhn * gain_v
        dx = dres_ref[...] + r * (dxn - xn * jnp.mean(dxn * xn, axis=-1, keepdims=True))
        dx_ref[...] = dx

        @pl.when(pl.program_id(0) == 0)
        def _():
            st_ref[...] = jnp.zeros_like(st_ref)

        st_ref[0:1, :] += jnp.sum(dh_v, axis=0, keepdims=True)
        st_ref[1:2, :] += jnp.sum(dh_v * hn, axis=0, keepdims=True)
        st_ref[2:3, :] += jnp.sum(dhn * xn, axis=0, keepdims=True)
        if below:
            _through_gate(dx, rest[0], rest[1], rest[-1], st_ref, 3)

    vec = pl.BlockSpec((1, D), lambda i: (0, 0))
    blk = pl.BlockSpec((tr, D), lambda i: (i, 0))
    return pl.pallas_call(
        body, name=name, grid=(T // tr,),
        out_shape=[jax.ShapeDtypeStruct((T, D), F32), jax.ShapeDtypeStruct((8, D), F32)]
        + ([jax.ShapeDtypeStruct((T, D), BF16)] if below else []),
        in_specs=[blk, blk, blk, vec, vec] + ([blk, vec] if below else []),
        out_specs=[blk, pl.BlockSpec((8, D), lambda i: (0, 0))] + ([blk] if below else []),
        compiler_params=_params("arbitrary"),
    )(x, dh, dres, gain, sc, *(below or ()))


def _loss_head(y, target, below, name):
    T, D = y.shape
    tr = _row_tile(T)

    def body(y_ref, t_ref, b_ref, g_ref, dy_ref, l_ref, st_ref, db_ref):
        err = y_ref[...] - t_ref[...]
        dy = err * (1.0 / D)
        dy_ref[...] = dy

        @pl.when(pl.program_id(0) == 0)
        def _():
            l_ref[...] = jnp.zeros_like(l_ref)
            st_ref[...] = jnp.zeros_like(st_ref)

        part = jnp.sum(jnp.mean(err * err, axis=-1, keepdims=True), axis=0, keepdims=True)
        l_ref[...] += jnp.broadcast_to(0.5 * part, l_ref.shape)
        _through_gate(dy, b_ref, g_ref, db_ref, st_ref, 0)

    blk = pl.BlockSpec((tr, D), lambda i: (i, 0))
    return pl.pallas_call(
        body, name=name, grid=(T // tr,),
        out_shape=[jax.ShapeDtypeStruct((T, D), F32), jax.ShapeDtypeStruct((8, 128), F32),
                   jax.ShapeDtypeStruct((8, D), F32), jax.ShapeDtypeStruct((T, D), BF16)],
        in_specs=[blk, blk, blk, pl.BlockSpec((1, D), lambda i: (0, 0))],
        out_specs=[blk, pl.BlockSpec((8, 128), lambda i: (0, 0)), pl.BlockSpec((8, D), lambda i: (0, 0)), blk],
        compiler_params=_params("arbitrary"),
    )(y, target, *below)


def _silu(v):
    return v * jax.nn.sigmoid(v)


def _mod_fwd(c_all, mod_w, mod_b_cols, name):
    L, D, n = mod_w.shape
    tn = 512

    def body(c_ref, w_ref, b_ref, o_ref):
        cond = _silu(c_ref[...]).astype(BF16)
        o_ref[0] = jnp.dot(cond, w_ref[0].astype(BF16), preferred_element_type=F32) + b_ref[0]

    return pl.pallas_call(
        body, name=name, grid=(L, n // tn), out_shape=jax.ShapeDtypeStruct((L, N_DEV, n), F32),
        in_specs=[pl.BlockSpec((N_DEV, D), lambda l, j: (0, 0)), pl.BlockSpec((1, D, tn), lambda l, j: (l, 0, j)),
                  pl.BlockSpec((1, 1, tn), lambda l, j: (l, 0, j))],
        out_specs=pl.BlockSpec((1, N_DEV, tn), lambda l, j: (l, 0, j)),
        compiler_params=_params("parallel", "parallel"),
    )(c_all, mod_w, mod_b_cols)


def _adamw(g, w, m, v):
    m = ADAM_B1 * m + (1.0 - ADAM_B1) * g
    v = ADAM_B2 * v + (1.0 - ADAM_B2) * (g * g)
    m_hat = m / (1.0 - ADAM_B1 ** ADAM_STEP)
    v_hat = v / (1.0 - ADAM_B2 ** ADAM_STEP)
    delta = -ADAM_LR * (m_hat / (jnp.sqrt(v_hat) + ADAM_EPS) + ADAM_WD * w)
    return delta, m, v


def _mod_w_update(c_t, dmod_cols, w, m, v, name):
    L, D, n = w.shape
    tr = 256

    def body(c_ref, dm_ref, w_ref, m_ref, v_ref, g_ref, d_ref, nm_ref, nv_ref):
        cond = _silu(c_ref[...])
        dm = dm_ref[0]
        g = cond[:, 0:1] * dm[0:1, :]
        for b in range(1, N_DEV):
            g = g + cond[:, b:b + 1] * dm[b:b + 1, :]
        delta, nm, nv = _adamw(g, w_ref[0], m_ref[0], v_ref[0])
        g_ref[0], d_ref[0], nm_ref[0], nv_ref[0] = g, delta, nm, nv

    blk = pl.BlockSpec((1, tr, n), lambda l, i: (l, i, 0))
    return pl.pallas_call(
        body, name=name, grid=(L, D // tr), out_shape=[jax.ShapeDtypeStruct(w.shape, F32)] * 4,
        in_specs=[pl.BlockSpec((tr, N_DEV), lambda l, i: (i, 0)), pl.BlockSpec((1, N_DEV, n), lambda l, i: (l, 0, 0)),
                  blk, blk, blk],
        out_specs=[blk] * 4, compiler_params=_params("parallel", "parallel"),
    )(c_t, dmod_cols, w, m, v)


def _sum_adamw(parts, w, m, v, name):
    R, C = w.shape
    tr = min(R, 256 if C >= 1024 else 1024)
    assert R % tr == 0

    def body(p_ref, w_ref, m_ref, v_ref, g_ref, d_ref, nm_ref, nv_ref):
        g = p_ref[0].astype(F32)
        for s in range(1, N_DEV):
            g = g + p_ref[s].astype(F32)
        delta, nm, nv = _adamw(g, w_ref[...], m_ref[...], v_ref[...])
        g_ref[...], d_ref[...], nm_ref[...], nv_ref[...] = g, delta, nm, nv

    blk = pl.BlockSpec((tr, C), lambda i: (i, 0))
    return pl.pallas_call(
        body, name=name, grid=(R // tr,), out_shape=[jax.ShapeDtypeStruct((R, C), F32)] * 4,
        in_specs=[pl.BlockSpec((N_DEV, tr, C), lambda i: (0, i, 0)), blk, blk, blk], out_specs=[blk] * 4,
        compiler_params=_params("parallel"),
    )(parts, w, m, v)


def _sum_adamw_layers(parts, w, m, v, name):
    L, R, C = w.shape
    tr = min(R, 256 * 1024 // C)
    assert R % tr == 0 and len(parts) == L
    ni = R // tr

    def body(*refs):
        p_refs, (w_ref, m_ref, v_ref), outs = refs[:L], refs[L:L + 3], refs[L + 3:]
        for layer in range(L):
            @pl.when(pl.program_id(0) == layer)
            def _(p_ref=p_refs[layer]):
                g = p_ref[0].astype(F32)
                for s in range(1, N_DEV):
                    g = g + p_ref[s].astype(F32)
                delta, nm, nv = _adamw(g, w_ref[...], m_ref[...], v_ref[...])
                for o_ref, val in zip(outs, (g, delta, nm, nv)):
                    o_ref[...] = val

    def shares(layer):
        park = 0 if layer else ni - 1
        return pl.BlockSpec((N_DEV, tr, C), lambda l, i: (0, jnp.where(l == layer, i, park), 0))

    blk = pl.BlockSpec((None, tr, C), lambda l, i: (l, i, 0))
    return pl.pallas_call(
        body, name=name, grid=(L, ni), out_shape=[jax.ShapeDtypeStruct((L, R, C), F32)] * 4,
        in_specs=[shares(layer) for layer in range(L)] + [blk] * 3, out_specs=[blk] * 4,
        compiler_params=_params("arbitrary", "arbitrary"),
    )(*parts, w, m, v)


def _lower_bound_row1(l0, l1):
    mx = lax.stop_gradient(jnp.maximum(l0, l1))
    e0, e1 = jnp.exp(l0 - mx), jnp.exp(l1 - mx)
    p0, p1 = e0 / (e0 + e1), e1 / (e0 + e1)
    return (p0 + p1) - p0


def _lb_fwd(logits, name):
    def body(l_ref, o_ref):
        o_ref[...] = _lower_bound_row1(l_ref[0:1, :], l_ref[1:2, :])

    return pl.pallas_call(body, name=name, out_shape=jax.ShapeDtypeStruct((1, logits.shape[1]), F32))(logits)


def _lb_bwd(logits, dlb, name):
    def body(l_ref, d_ref, o_ref):
        _, vjp = jax.vjp(_lower_bound_row1, l_ref[0:1, :], l_ref[1:2, :])
        d0, d1 = vjp(d_ref[...])
        o_ref[0:1, :] = d0
        o_ref[1:2, :] = d1

    return pl.pallas_call(body, name=name, out_shape=jax.ShapeDtypeStruct(logits.shape, F32))(logits, dlb)


def _bdot(a, b, dims):
    return lax.dot_general(a.astype(BF16), b.astype(BF16), _DIMS[dims], preferred_element_type=F32)


def _rms(x, gain):
    r = lax.rsqrt(jnp.mean(x * x, axis=-1, keepdims=True) + EPS)
    xhat = x * r
    return xhat, r, xhat * gain


def _attn_band(n):
    qi = lax.broadcasted_iota(jnp.int32, (ATTN_BLOCK, 2 * ATTN_BLOCK), 0)
    ki = lax.broadcasted_iota(jnp.int32, (ATTN_BLOCK, 2 * ATTN_BLOCK), 1)
    dist = qi + ATTN_BLOCK - ki
    first_key = jnp.where(n > 0, 0, ATTN_BLOCK)
    valid = (dist >= 0) & (dist < ATTN_BLOCK) & (ki >= first_key)
    return valid, jnp.abs(dist).astype(F32)


def _attn_head_probs(qn, kn_b, valid, absdist, slope, sink):
    s = lax.dot_general(qn.astype(BF16), kn_b, _DIMS["nt"], preferred_element_type=F32) * ATTN_SCALE
    s = jnp.where(valid, s - slope * absdist, NEG_BIG)
    mx = jnp.maximum(jnp.max(s, axis=-1, keepdims=True), sink)
    e = jnp.exp(s - mx)
    es = jnp.exp(sink - mx)
    inv = 1.0 / (jnp.sum(e, axis=-1, keepdims=True) + es)
    return e * inv, es * inv


def _attn_fwd(qh, kh, vh, qg, kg, slopes, sinks, name, comm=None):
    T = qh.shape[1]
    nb = T // ATTN_BLOCK

    def body(q_ref, kp_ref, kc_ref, vp_ref, vc_ref, qg_ref, kg_ref, sl_ref, sk_ref, o_ref):
        n = pl.program_id(0)
        valid, absdist = _attn_band(n)
        for kv in range(ATTN_KV):
            _, _, kn = _rms(jnp.concatenate([kp_ref[kv], kc_ref[kv]], axis=0), kg_ref[...])
            kn_b = kn.astype(BF16)
            v_b = jnp.concatenate([vp_ref[kv], vc_ref[kv]], axis=0).astype(BF16)
            for g in range(ATTN_GROUP):
                head = kv * ATTN_GROUP + g
                _, _, qn = _rms(q_ref[head], qg_ref[...])
                p, _ = _attn_head_probs(qn, kn_b, valid, absdist, sl_ref[head], sk_ref[head])
                o_ref[head] = jnp.dot(p.astype(BF16), v_b, preferred_element_type=F32).astype(BF16)

    qspec = pl.BlockSpec((ATTN_HEADS, ATTN_BLOCK, ATTN_HD), lambda n: (0, n, 0))
    prev = pl.BlockSpec((ATTN_KV, ATTN_BLOCK, ATTN_HD), lambda n: (0, jnp.maximum(n - 1, 0), 0))
    cur = pl.BlockSpec((ATTN_KV, ATTN_BLOCK, ATTN_HD), lambda n: (0, n, 0))
    gain = pl.BlockSpec((1, ATTN_HD), lambda n: (0, 0))
    scalars = pl.BlockSpec(memory_space=pltpu.SMEM)
    return _call(body, name=name, grid=(nb,), out_shape=[jax.ShapeDtypeStruct(qh.shape, BF16)],
                 in_specs=[qspec, prev, cur, prev, cur, gain, gain, scalars, scalars], out_specs=[qspec],
                 sem=("parallel",), args=(qh, kh, kh, vh, vh, qg, kg, slopes, sinks), comm=comm)


def _rms_bwd(dy, xhat, r, gain):
    dxh = dy * gain
    dx = r * (dxh - xhat * jnp.mean(dxh * xhat, axis=-1, keepdims=True))
    return dx, jnp.sum(dy * xhat, axis=0, keepdims=True)


def _attn_bwd(qh, kh, vh, doh, qg, kg, slope_col, sink_col, name, comm=None):
    T = qh.shape[1]
    nb = T // ATTN_BLOCK
    rows = ATTN_GROUP * ATTN_BLOCK

    def body(q_ref, kp_ref, kc_ref, vp_ref, vc_ref, do_ref, qg_ref, kg_ref, sl_ref, sk_ref,
             dq_ref, dk_ref, dv_ref, dqg_ref, dkg_ref, dsk_ref, carry_ref, sk_acc):
        step = pl.program_id(0)
        n = nb - 1 - step
        qg_v, kg_v = qg_ref[...], kg_ref[...]

        @pl.when(step == 0)
        def _():
            dqg_ref[...] = jnp.zeros_like(dqg_ref)
            dkg_ref[...] = jnp.zeros_like(dkg_ref)
            carry_ref[...] = jnp.zeros_like(carry_ref)
            sk_acc[...] = jnp.zeros_like(sk_acc)

        for kv in range(ATTN_KV):
            heads = slice(kv * ATTN_GROUP, (kv + 1) * ATTN_GROUP)
            qhat, rq, qn = _rms(q_ref[heads].reshape(rows, ATTN_HD), qg_v)
            khat, rk, kn = _rms(jnp.concatenate([kp_ref[kv], kc_ref[kv]], axis=0), kg_v)
            s = _bdot(qn, kn, "nt") * ATTN_SCALE
            qi = lax.broadcasted_iota(jnp.int32, s.shape, 0) & (ATTN_BLOCK - 1)
            ki = lax.broadcasted_iota(jnp.int32, s.shape, 1)
            dist = qi + ATTN_BLOCK - ki
            first_key = jnp.where(n > 0, 0, ATTN_BLOCK)
            valid = (dist >= 0) & (dist < ATTN_BLOCK) & (ki >= first_key)
            s = jnp.where(valid, s - sl_ref[kv] * jnp.abs(dist).astype(F32), NEG_BIG)
            sink = sk_ref[kv]
            mx = jnp.maximum(jnp.max(s, axis=-1, keepdims=True), sink)
            e = jnp.exp(s - mx)
            es = jnp.exp(sink - mx)
            den = jnp.sum(e, axis=-1, keepdims=True) + es
            p, ps = e / den, es / den
            v = jnp.concatenate([vp_ref[kv], vc_ref[kv]], axis=0)
            do = do_ref[heads].reshape(rows, ATTN_HD)
            dp = _bdot(do, v, "nt")
            delta = jnp.sum(p * dp, axis=-1, keepdims=True)
            ds = p * (dp - delta)
            dqn = _bdot(ds, kn, "nn") * ATTN_SCALE
            dkn = _bdot(ds, qn, "tn") * ATTN_SCALE
            dv = _bdot(p, do, "tn")
            dq, dqg = _rms_bwd(dqn, qhat, rq, qg_v)
            dk, dkg = _rms_bwd(dkn, khat, rk, kg_v)
            dq_ref[heads] = dq.reshape(ATTN_GROUP, ATTN_BLOCK, ATTN_HD).astype(BF16)
            dqg_ref[0:1, :] += dqg
            dkg_ref[0:1, :] += dkg
            dk_ref[kv] = (dk[ATTN_BLOCK:, :] + carry_ref[kv, 0]).astype(BF16)
            dv_ref[kv] = (dv[ATTN_BLOCK:, :] + carry_ref[kv, 1]).astype(BF16)
            carry_ref[kv, 0] = dk[:ATTN_BLOCK, :]
            carry_ref[kv, 1] = dv[:ATTN_BLOCK, :]
            sk_acc[kv] += -ps * delta

        @pl.when(step == nb - 1)
        def _():
            for head in range(ATTN_HEADS):
                kv, g = divmod(head, ATTN_GROUP)
                tot = jnp.sum(sk_acc[kv, g * ATTN_BLOCK:(g + 1) * ATTN_BLOCK, :], axis=0, keepdims=True)
                dsk_ref[head:head + 1, :] = jnp.broadcast_to(tot, (1, 128))

    qspec = pl.BlockSpec((ATTN_HEADS, ATTN_BLOCK, ATTN_HD), lambda s: (0, nb - 1 - s, 0))
    prev = pl.BlockSpec((ATTN_KV, ATTN_BLOCK, ATTN_HD), lambda s: (0, jnp.maximum(nb - 2 - s, 0), 0))
    cur = pl.BlockSpec((ATTN_KV, ATTN_BLOCK, ATTN_HD), lambda s: (0, nb - 1 - s, 0))
    gain = pl.BlockSpec((1, ATTN_HD), lambda s: (0, 0))
    col = pl.BlockSpec((ATTN_KV, rows, 1), lambda s: (0, 0, 0))
    acc = pl.BlockSpec((8, ATTN_HD), lambda s: (0, 0))
    return _call(
        body, name=name, grid=(nb,),
        out_shape=[jax.ShapeDtypeStruct(qh.shape, BF16), jax.ShapeDtypeStruct(kh.shape, BF16),
                   jax.ShapeDtypeStruct(kh.shape, BF16), jax.ShapeDtypeStruct((8, ATTN_HD), F32),
                   jax.ShapeDtypeStruct((8, ATTN_HD), F32), jax.ShapeDtypeStruct((ATTN_HEADS, 128), F32)],
        in_specs=[qspec, prev, cur, prev, cur, qspec, gain, gain, col, col],
        out_specs=[qspec, cur, cur, acc, acc, pl.BlockSpec((ATTN_HEADS, 128), lambda s: (0, 0))],
        scratch_shapes=[pltpu.VMEM((ATTN_KV, 2, ATTN_BLOCK, ATTN_HD), F32), pltpu.VMEM((ATTN_KV, rows, 1), F32)],
        sem=("arbitrary",), args=(qh, kh, kh, vh, vh, doh, qg, kg, slope_col, sink_col), comm=comm)


@functools.partial(jax.custom_vjp, nondiff_argnums=(2,))
def _mm(a, b, dims):
    return _bdot(a, b, dims)


def _mm_fwd(a, b, dims):
    return _bdot(a, b, dims), (a, b)


def _mm_bwd(dims, res, ct):
    a, b = res
    if dims == "nn":
        return _bdot(ct, b, "nt"), _bdot(a, ct, "tn")
    if dims == "nt":
        return _bdot(ct, b, "nn"), _bdot(ct, a, "tn")
    return _bdot(b, ct, "nt"), _bdot(a, ct, "nn")


_mm.defvjp(_mm_fwd, _mm_bwd)


def _same_chunk_mask(rows, upper):
    ri = lax.broadcasted_iota(jnp.int32, (rows, rows), 0)
    ci = lax.broadcasted_iota(jnp.int32, (rows, rows), 1)
    same = (ri >> HGRN_CHUNK_SHIFT) == (ci >> HGRN_CHUNK_SHIFT)
    return same & ((ri <= ci) if upper else (ri >= ci))


@jax.custom_vjp
def _chunk_cumsum(x, tri, tri_t):
    hi = x.astype(BF16)
    rest = x - hi.astype(F32)
    mid = rest.astype(BF16)
    lo = (rest - mid.astype(F32)).astype(BF16)
    out = jnp.dot(tri, jnp.concatenate([hi, mid, lo], axis=1), preferred_element_type=F32)
    w = x.shape[1]
    return out[:, :w] + out[:, w:2 * w] + out[:, 2 * w:]


def _chunk_cumsum_fwd(x, tri, tri_t):
    return _chunk_cumsum(x, tri, tri_t), (tri, tri_t)


def _chunk_cumsum_bwd(res, ct):
    tri, tri_t = res
    return _chunk_cumsum(ct, tri_t, tri), jnp.zeros_like(tri), jnp.zeros_like(tri_t)


_chunk_cumsum.defvjp(_chunk_cumsum_fwd, _chunk_cumsum_bwd)


def _hgrn_masks(rows):
    nc = rows // HGRN_CHUNK
    lower = _same_chunk_mask(rows, False)
    chunk_of_row = lax.broadcasted_iota(jnp.int32, (rows, HGRN_DK), 0) >> HGRN_CHUNK_SHIFT
    row_in_chunk = lax.broadcasted_iota(jnp.int32, (nc, HGRN_CHUNK, HGRN_DK), 1)
    return dict(lower=lower, tri=lower.astype(BF16), tri_t=_same_chunk_mask(rows, True).astype(BF16),
                in_chunk=[chunk_of_row == c for c in range(nc)],
                mid_row=row_in_chunk == HGRN_CHUNK // 2 - 1, last_row=row_in_chunk == HGRN_CHUNK - 1)


def _hgrn_block(masks, st, qr, fr, v, gr, lb, og):
    rows = qr.shape[0]
    nc = rows // HGRN_CHUNK
    per_chunk = lambda m: m.reshape(nc, HGRN_CHUNK, HGRN_DK)
    flat = lambda m: m.reshape(rows, HGRN_DK)
    by_chunk = lambda m: jnp.concatenate([jnp.where(masks["in_chunk"][c], m, 0.0) for c in range(nc)], axis=1)

    forget = lb + (1.0 - lb) * jax.nn.sigmoid(fr)
    k = 1.0 - forget
    b = _chunk_cumsum(jnp.log(forget), masks["tri"], masks["tri_t"])
    b3 = per_chunk(b)
    piv = jnp.sum(jnp.where(masks["mid_row"], b3, 0.0), axis=1, keepdims=True)
    b_last = jnp.sum(jnp.where(masks["last_row"], b3, 0.0), axis=1, keepdims=True)
    q = _silu(qr) * HGRN_SCALE
    a = _mm(q * flat(jnp.exp(b3 - piv)), k * flat(jnp.exp(piv - b3)), "nt")
    o = _mm(jnp.where(masks["lower"], a, 0.0), v, "nn")
    updates = _mm(v, by_chunk(k * flat(jnp.exp(b_last - b3))), "tn")
    decay = jnp.exp(b_last)
    before = []
    for c in range(nc):
        before.append(st)
        st = st * decay[c] + updates[:, c * HGRN_DK:(c + 1) * HGRN_DK]
    o = o + _mm(by_chunk(q * jnp.exp(b)), jnp.concatenate(before, axis=1), "nt")
    y = (o * lax.rsqrt(jnp.mean(o * o, axis=-1, keepdims=True) + EPS)) * og * _silu(gr)
    return y, st


def _hgrn_tile(T):
    return min(T, 256)


HGRN_HEADS_PER_STEP = 16
HGRN_GROUPS = HGRN_HEADS // HGRN_HEADS_PER_STEP


def _hgrn_fwd(proj, lb, og, name, comm=None):
    T = proj.shape[0]
    tb = _hgrn_tile(T)
    hp, wide = HGRN_HEADS_PER_STEP, HGRN_HEADS_PER_STEP * HGRN_DK

    def body(q_ref, f_ref, v_ref, g_ref, lb_ref, og_ref, o_ref, s_ref, st_ref):
        @pl.when(pl.program_id(1) == 0)
        def _():
            st_ref[...] = jnp.zeros_like(st_ref)

        masks = _hgrn_masks(tb)
        for j in range(hp):
            ln = slice(j * HGRN_DK, (j + 1) * HGRN_DK)
            st = st_ref[j]
            s_ref[j, 0] = st
            y, st_ref[j] = _hgrn_block(masks, st, q_ref[:, ln], f_ref[:, ln], v_ref[:, ln], g_ref[:, ln], lb_ref[j],
                                       og_ref[j])
            o_ref[:, ln] = y.astype(BF16)

    part = lambda p: pl.BlockSpec((tb, wide), lambda h, t: (t, p * HGRN_GROUPS + h))
    vec = pl.BlockSpec((hp, 1, HGRN_DK), lambda h, t: (h, 0, 0))
    return _call(
        body, name=name, grid=(HGRN_GROUPS, T // tb),
        out_shape=[jax.ShapeDtypeStruct((T, D_MODEL), BF16),
                   jax.ShapeDtypeStruct((HGRN_HEADS, T // tb, HGRN_DK, HGRN_DK), F32)],
        in_specs=[part(0), part(1), part(2), part(3), vec, vec],
        out_specs=[pl.BlockSpec((tb, wide), lambda h, t: (t, h)),
                   pl.BlockSpec((hp, 1, HGRN_DK, HGRN_DK), lambda h, t: (h, t, 0, 0))],
        scratch_shapes=[pltpu.VMEM((hp, HGRN_DK, HGRN_DK), F32)], sem=("parallel", "arbitrary"),
        args=(proj, proj, proj, proj, lb, og), comm=comm)


def _hgrn_bwd(proj, states, do, lb, og, name, comm=None):
    T = proj.shape[0]
    tb = _hgrn_tile(T)
    nt, hp, wide = T // tb, HGRN_HEADS_PER_STEP, HGRN_HEADS_PER_STEP * HGRN_DK

    def body(q_ref, f_ref, v_ref, g_ref, s_ref, do_ref, lb_ref, og_ref, dp_ref, dlb_ref, dog_ref, dst_ref):
        @pl.when(pl.program_id(1) == 0)
        def _():
            dst_ref[...] = jnp.zeros_like(dst_ref)
            dlb_ref[...] = jnp.zeros_like(dlb_ref)
            dog_ref[...] = jnp.zeros_like(dog_ref)

        block = functools.partial(_hgrn_block, _hgrn_masks(tb))
        for j in range(hp):
            ln = slice(j * HGRN_DK, (j + 1) * HGRN_DK)
            _, vjp = jax.vjp(block, s_ref[j, 0], q_ref[:, ln], f_ref[:, ln], v_ref[:, ln], g_ref[:, ln],
                             lb_ref[j], og_ref[j])
            dst_ref[j], dq, df, dv, dg, dlb, dog = vjp((do_ref[:, ln], dst_ref[j]))
            for p, part_grad in enumerate((dq, df, dv, dg)):
                dp_ref[p, :, ln] = part_grad.astype(BF16)
            dlb_ref[j] += dlb
            dog_ref[j] += dog

    part = lambda p: pl.BlockSpec((tb, wide), lambda h, t: (nt - 1 - t, p * HGRN_GROUPS + h))
    vec = pl.BlockSpec((hp, 1, HGRN_DK), lambda h, t: (h, 0, 0))
    head = pl.BlockSpec((tb, wide), lambda h, t: (nt - 1 - t, h))
    return _call(
        body, name=name, grid=(HGRN_GROUPS, nt),
        out_shape=[jax.ShapeDtypeStruct((4, T, D_MODEL), BF16)] + [jax.ShapeDtypeStruct((HGRN_HEADS, 1, HGRN_DK), F32)] * 2,
        in_specs=[part(0), part(1), part(2), part(3),
                  pl.BlockSpec((hp, 1, HGRN_DK, HGRN_DK), lambda h, t: (h, nt - 1 - t, 0, 0)), head, vec, vec],
        out_specs=[pl.BlockSpec((4, tb, wide), lambda h, t: (0, nt - 1 - t, h)), vec, vec],
        scratch_shapes=[pltpu.VMEM((hp, HGRN_DK, HGRN_DK), F32)], sem=("parallel", "arbitrary"),
        args=(proj, proj, proj, proj, states, do, lb, og), comm=comm)


def _cols_to_blocks(g, n8):
    K = g.shape[0]
    return g.reshape(K, N_DEV, n8).transpose(1, 0, 2)


def _blocks_to_cols(wg):
    _, K, n8 = wg.shape
    return wg.transpose(1, 0, 2).reshape(K, N_DEV * n8)


def _pack(parts):
    flat = []
    for p in parts:
        v = p.reshape(-1)
        flat.append(jnp.pad(v, (0, (-v.shape[0]) % 1024)))
    return jnp.concatenate(flat).reshape(-1, 128)


def _unpack(packed, like):
    flat, out, off = packed.reshape(-1), [], 0
    for p in like:
        size = math.prod(p.shape)
        out.append(flat[off:off + size].reshape(p.shape))
        off += size + (-size) % 1024
    return out


def _heads_major(a, heads):
    T = a.shape[0]
    return a.reshape(T, heads, ATTN_HD).transpose(1, 0, 2)


def _heads_minor(a):
    heads, T, _ = a.shape
    return a.transpose(1, 0, 2).reshape(T, heads * ATTN_HD)


def kernel(x, c, mod_w, mod_b, norm_mix, norm_mlp, attn_w_in, attn_w_out, attn_q_gain, attn_k_gain, attn_sinks, hgrn_w_in, hgrn_w_out, hgrn_o_gain, hgrn_lb_logits, mlp_w1, mlp_w2, loss_target, m_mod_w, m_mod_b, m_norm_mix, m_norm_mlp, m_attn_w_in, m_attn_w_out, m_attn_q_gain, m_attn_k_gain, m_attn_sinks, m_hgrn_w_in, m_hgrn_w_out, m_hgrn_o_gain, m_hgrn_lb_logits, m_mlp_w1, m_mlp_w2, v_mod_w, v_mod_b, v_norm_mix, v_norm_mlp, v_attn_w_in, v_attn_w_out, v_attn_q_gain, v_attn_k_gain, v_attn_sinks, v_hgrn_w_in, v_hgrn_w_out, v_hgrn_o_gain, v_hgrn_lb_logits, v_mlp_w1, v_mlp_w2):
    T = x.shape[1]
    me = 4 * lax.axis_index("x") + 2 * lax.axis_index("y") + lax.axis_index("c")
    x0, target = x[0], loss_target[0]
    n_mod = mod_w.shape[2]

    shards = [attn_w_in[0], attn_w_out[0], hgrn_w_in[0], hgrn_w_out[0], mlp_w1[0], mlp_w1[1], mlp_w2[0], mlp_w2[1]]
    sb = [s.astype(BF16) for s in shards]
    gather = lambda *items: _Exchange([("gather_by_chip", a, axis) for a, axis in items])

    c_all = _exchange([("gather", c.reshape(16, 128), None)], vmem=True, name="gather_c")[0].reshape(N_DEV, D_MODEL)
    mod_b_cols = lax.dynamic_slice_in_dim(mod_b, me * n_mod, n_mod, axis=1).reshape(2, 1, n_mod)
    mod_cols = _mod_fwd(c_all, mod_w, mod_b_cols, "mod_fwd")
    mod_all = _exchange([("gather", mod_cols.reshape(-1, 128), None)], vmem=True, name="gather_mod")[0]
    mod_all = mod_all.reshape(N_DEV, 2, N_DEV, n_mod)
    mod_mine = lax.dynamic_index_in_dim(mod_all, me, axis=2, keepdims=False)
    mod_mine = mod_mine.transpose(1, 0, 2).reshape(2, N_MOD, 1, D_MODEL)

    lb = _lb_fwd(hgrn_lb_logits, "lb_fwd").reshape(HGRN_HEADS, 1, HGRN_DK)
    og = hgrn_o_gain.reshape(HGRN_HEADS, 1, HGRN_DK)
    slopes = jnp.exp2(-8.0 * jnp.arange(1, ATTN_HEADS + 1, dtype=F32) / ATTN_HEADS)
    sinks = attn_sinks[0]
    per_row = lambda vals: jnp.repeat(vals.reshape(ATTN_KV, ATTN_GROUP), ATTN_BLOCK, axis=1).reshape(
        ATTN_KV, ATTN_GROUP * ATTN_BLOCK, 1)
    slope_col, sink_col = per_row(slopes), per_row(sinks)

    saved = []
    xi = x0
    w_mlp1, w_mlp2 = [None, None], [None, None]
    for i in range(2):
        sh1, sc1, g1, sh2, sc2, g2 = [mod_mine[i, j] for j in range(N_MOD)]
        if i == 0:
            h, w_attn_in = _norm_mod_fwd(xi, norm_mix[i:i + 1], sc1, sh1, "norm_mix_fwd0", comm=gather((sb[0], None)))
            w_attn_in = _blocks_to_cols(w_attn_in)
            qh, kh, vh, w_attn_out = _matmul(h, w_attn_in, dims="nn", tm=1024, tn=512, tk=2048, name="attn_in_fwd",
                                             epilogue="heads", comm=gather((sb[1], 0)))
            o, w_mlp1[0] = _attn_fwd(qh, kh, vh, attn_q_gain, attn_k_gain, slopes, sinks, "attn_fwd",
                                     comm=gather((sb[4], 1)))
            mix = (qh, kh, vh)
            w_out = w_attn_out
        else:
            h = _norm_mod_fwd(xi, norm_mix[i:i + 1], sc1, sh1, "norm_mix_fwd1")
            proj, w_hgrn_out = _matmul(h, w_hgrn_in, dims="nn", tm=1024, tn=2048, tk=2048, name="hgrn_in_fwd",
                                       comm=gather((sb[3], 0)))
            o, states, w_mlp1[1] = _hgrn_fwd(proj, lb, og, "hgrn_fwd", comm=gather((sb[5], 1)))
            mix = (proj, states)
            w_out = w_hgrn_out
        y, x1 = _matmul(o, w_out, dims="nn", tm=1024, tn=1024, tk=2048, name=f"mix_out_fwd{i}", epilogue="resgate",
                        extras=(xi, g1), a_heads=i == 0)
        h2 = _norm_mod_fwd(x1, norm_mlp[i:i + 1], sc2, sh2, f"norm_mlp_fwd{i}")
        if i == 0:
            act, act2, w_mlp2[0] = _matmul(h2, w_mlp1[0], dims="nn", tm=1024, tn=2048, tk=2048, name="mlp1_fwd0",
                                           epilogue="relu2", comm=gather((sb[6], 0)))
            z, x2, w_hgrn_in = _matmul(act2, w_mlp2[0], dims="nn", tm=1024, tn=1024, tk=2048, name="mlp2_fwd0",
                                       epilogue="resgate", extras=(x1, g2), comm=gather((sb[2], 1)))
        else:
            act, act2, w_mlp2[1] = _matmul(h2, w_mlp1[1], dims="nn", tm=1024, tn=2048, tk=2048, name="mlp1_fwd1",
                                           epilogue="relu2", comm=gather((sb[7], 0)))
            z, x2 = _matmul(act2, w_mlp2[1], dims="nn", tm=1024, tn=1024, tk=2048, name="mlp2_fwd1", epilogue="resgate",
                            extras=(x1, g2))
        saved.append((xi, h, o, y, x1, h2, act, act2, z, mix))
        xi = x2

    dx, loss_tile, st_g2, dz = _loss_head(xi, target, (saved[1][8], mod_mine[1, 5]), "loss_head")
    loss = lax.psum(loss_tile[0, 0], ("x", "y", "c"))

    scatter = lambda *items: _Exchange([("scatter", a, axis) for a, axis in items])
    shares, dmods, dnorm_mix, dnorm_mlp = {}, [None, None], [None, None], [None, None]
    wgrad = lambda a, b, name, tn=1024: _matmul(a, b, dims="tn", tm=2048, tn=tn, tk=2048, name=name, out_dtype=BF16)
    for i in (1, 0):
        sh1, sc1, g1, sh2, sc2, g2 = [mod_mine[i, j] for j in range(N_MOD)]
        xin, h, o, y, x1, h2, act, act2, z, mix = saved[i]
        if i == 1:
            dpre = _matmul(dz, w_mlp2[1], dims="nt", tm=1024, tn=1024, tk=2048, name="mlp2_bwd1", epilogue="mul2a",
                           extras=(act,))
        else:
            dpre, shares["hgrn_w_in"] = _matmul(dz, w_mlp2[0], dims="nt", tm=1024, tn=1024, tk=2048, name="mlp2_bwd0",
                                                epilogue="mul2a", extras=(act,), comm=scatter((g_hgrn_in, 1)))
        g_mlp2 = wgrad(act2, dz, f"mlp2_wgrad{i}")
        if i == 1:
            dh2 = _matmul(dpre, w_mlp1[1], dims="nt", tm=1024, tn=1024, tk=4096, name="mlp1_bwd1", out_dtype=BF16)
        else:
            dh2, shares["mlp_w2_0"] = _matmul(dpre, w_mlp1[0], dims="nt", tm=1024, tn=1024, tk=4096, name="mlp1_bwd0",
                                              out_dtype=BF16, comm=scatter((g_mlp2, 0)))
        g_mlp1 = wgrad(h2, dpre, f"mlp1_wgrad{i}")
        dx1, st_mlp, dy = _norm_mod_bwd(x1, dh2, dx, norm_mlp[i:i + 1], sc2, f"norm_mlp_bwd{i}", below=(y, g1))
        w_out = w_attn_out if i == 0 else w_hgrn_out
        if i == 0:
            qh, kh, vh = mix
            doh = _matmul(dy, w_out, dims="nt", tm=1024, tn=512, tk=2048, name="mix_out_bwd0", epilogue="heads",
                          out_dtype=BF16)
            g_out = _matmul(o, dy, dims="tn", tm=1024, tn=1024, tk=2048, name="mix_out_wgrad0", out_dtype=BF16,
                            a_heads=True)
            dqh, dkh, dvh, dqg, dkg, dsk, shares["mlp_w1_0"] = _attn_bwd(
                qh, kh, vh, doh, attn_q_gain, attn_k_gain, slope_col, sink_col, "attn_bwd",
                comm=scatter((g_mlp1, 1)))
            dproj = jnp.concatenate([dqh, dkh, dvh], axis=0)
            d_q_gain, d_k_gain, d_sinks = dqg[0:1], dkg[0:1], dsk[:, 0].reshape(1, ATTN_HEADS)
            g_attn_in, shares["attn_w_out"] = _matmul(h, dproj, dims="tn", tm=2048, tn=640, tk=2048, name="mix_in_wgrad0",
                                                      out_dtype=BF16, b_heads=True, comm=scatter((g_out, 0)))
            g_attn_in = _cols_to_blocks(g_attn_in, attn_w_in.shape[2])
            dh, shares["attn_w_in"] = _matmul(dproj, w_attn_in, dims="nt", tm=1024, tn=1024, tk=2560, out_dtype=BF16,
                                              name="mix_in_bwd0", comm=scatter((g_attn_in, None)), a_heads=True)
        else:
            do = _matmul(dy, w_out, dims="nt", tm=1024, tn=1024, tk=2048, name="mix_out_bwd1")
            g_out = wgrad(o, dy, "mix_out_wgrad1")
            proj, states = mix
            dproj, dlb, d_o_gain, shares["mlp_w2_1"], shares["mlp_w1_1"], shares["hgrn_w_out"] = _hgrn_bwd(
                proj, states, do, lb, og, "hgrn_bwd", comm=scatter((g_mlp2, 0), (g_mlp1, 1), (g_out, 0)))
            d_lb_logits = _lb_bwd(hgrn_lb_logits, dlb.reshape(1, D_MODEL), "lb_bwd")
            dh = _matmul(dproj, w_hgrn_in, dims="nt", tm=1024, tn=1024, tk=2048, name="mix_in_bwd1", out_dtype=BF16)
            g_hgrn_in = wgrad(h, dproj, "mix_in_wgrad1")
        d_gate2 = st_g2[0:1] if i == 1 else st_mix_above[3:4]
        if i == 1:
            dx, st_mix, dz = _norm_mod_bwd(xin, dh, dx1, norm_mix[i:i + 1], sc1, "norm_mix_bwd1",
                                           below=(saved[0][8], mod_mine[0, 5]))
            st_mix_above = st_mix
        else:
            dx, st_mix = _norm_mod_bwd(xin, dh, dx1, norm_mix[i:i + 1], sc1, "norm_mix_bwd0")
        dmods[i] = jnp.concatenate([st_mix[0:1], st_mix[1:2], st_mlp[3:4], st_mlp[0:1], st_mlp[1:2], d_gate2], axis=1)
        dnorm_mix[i], dnorm_mlp[i] = st_mix[2:3], st_mlp[2:3]

    single = {"attn_w_in": (attn_w_in, m_attn_w_in, v_attn_w_in), "attn_w_out": (attn_w_out, m_attn_w_out, v_attn_w_out),
              "hgrn_w_in": (hgrn_w_in, m_hgrn_w_in, v_hgrn_w_in), "hgrn_w_out": (hgrn_w_out, m_hgrn_w_out, v_hgrn_w_out)}
    big = {nm: _sum_adamw(shares[nm], w[0], m[0], v[0], f"adamw_{nm}") for nm, (w, m, v) in single.items()}
    big["mlp_w1"] = _sum_adamw_layers([shares["mlp_w1_0"], shares["mlp_w1_1"]], mlp_w1, m_mlp_w1, v_mlp_w1, "adamw_mlp_w1")
    big["mlp_w2"] = _sum_adamw_layers([shares["mlp_w2_0"], shares["mlp_w2_1"]], mlp_w2, m_mlp_w2, v_mlp_w2, "adamw_mlp_w2")

    small_w = [mod_b, norm_mix, norm_mlp, attn_q_gain, attn_k_gain, attn_sinks, hgrn_o_gain, hgrn_lb_logits]
    small_m = [m_mod_b, m_norm_mix, m_norm_mlp, m_attn_q_gain, m_attn_k_gain, m_attn_sinks, m_hgrn_o_gain, m_hgrn_lb_logits]
    small_v = [v_mod_b, v_norm_mix, v_norm_mlp, v_attn_q_gain, v_attn_k_gain, v_attn_sinks, v_hgrn_o_gain, v_hgrn_lb_logits]
    small_g = [jnp.concatenate(dmods, axis=0), jnp.concatenate(dnorm_mix, axis=0), jnp.concatenate(dnorm_mlp, axis=0),
               d_q_gain, d_k_gain, d_sinks, d_o_gain.reshape(hgrn_o_gain.shape), d_lb_logits]
    packed_g = _pack(small_g)
    pad_rows = (-packed_g.shape[0]) % 8
    pad8 = lambda a: jnp.pad(a, ((0, pad_rows), (0, 0)))
    all_small = _exchange([("gather", pad8(packed_g), None)], vmem=True, name="gather_small_grads")[0]
    sg, sd, sm, sv = _sum_adamw(all_small, pad8(_pack(small_w)), pad8(_pack(small_m)), pad8(_pack(small_v)),
                                "adamw_small")
    small = [_unpack(t, small_w) for t in (sg, sd, sm, sv)]

    n_modb = N_MOD * D_MODEL
    dmod_all = all_small[:, :2 * n_modb // 128, :].reshape(N_DEV, 2, n_modb)
    dmod_cols = lax.dynamic_slice_in_dim(dmod_all, me * n_mod, n_mod, axis=2).transpose(1, 0, 2)
    modw = _mod_w_update(c_all.T, dmod_cols, mod_w, m_mod_w, v_mod_w, "adamw_mod_w")

    def leaf(k):
        one = lambda a: big[a][k][None]
        s = small[k]
        return [modw[k], s[0], s[1], s[2], one("attn_w_in"), one("attn_w_out"), s[3], s[4], s[5], one("hgrn_w_in"),
                one("hgrn_w_out"), s[6], s[7], big["mlp_w1"][k], big["mlp_w2"][k]]

    return (loss, dx[None], *leaf(0), *leaf(1), *leaf(2), *leaf(3))
```

```python
import functools
import math

import jax
import jax.numpy as jnp
from jax import lax
from jax.experimental import pallas as pl
from jax.experimental.pallas import tpu as pltpu

F32, BF16 = jnp.float32, jnp.bfloat16
N_DEV = 8
D_MODEL = 2048
N_MOD = 6
EPS = 1e-6
ATTN_HD, ATTN_HEADS, ATTN_KV, ATTN_GROUP, ATTN_BLOCK = 64, 32, 4, 8, 128
ATTN_SCALE = 1.0 / math.sqrt(ATTN_HD)
HGRN_HEADS, HGRN_DK, HGRN_CHUNK = 16, 128, 64
HGRN_SCALE = 1.0 / math.sqrt(HGRN_DK)
HGRN_CHUNK_SHIFT = HGRN_CHUNK.bit_length() - 1
assert 1 << HGRN_CHUNK_SHIFT == HGRN_CHUNK
D_FF = 4 * D_MODEL
ADAM_LR, ADAM_B1, ADAM_B2, ADAM_EPS, ADAM_WD, ADAM_STEP = 0.001, 0.9, 0.999, 1e-08, 0.01, 10
NEG_BIG = -1e30
VMEM_LIMIT = 56 * 1024 * 1024
MESH_ID = pl.DeviceIdType.MESH


def _params(*sem):
    return pltpu.CompilerParams(dimension_semantics=sem, vmem_limit_bytes=VMEM_LIMIT)


class _Exchange:
    def __init__(self, items):
        self.items = items
        self.n = len(items)
        self.out_shape = []
        for kind, a, axis in items:
            assert kind in ("gather", "gather_by_chip", "scatter")
            if kind != "scatter":
                shape = (N_DEV,) + a.shape if axis is None else tuple(
                    d * N_DEV if i == axis else d for i, d in enumerate(a.shape))
            else:
                shape = a.shape if axis is None else (N_DEV,) + tuple(
                    d // N_DEV if i == axis else d for i, d in enumerate(a.shape))
            self.out_shape.append(jax.ShapeDtypeStruct(shape, a.dtype))
        self.scratch = [pltpu.SemaphoreType.DMA((7 * self.n,)), pltpu.SemaphoreType.DMA((7 * self.n,)),
                        pltpu.SemaphoreType.DMA((self.n,))]
        self.arrays = [a for _, a, _ in items]

    @staticmethod
    def _block(ref, b, axis, size):
        if axis is None:
            return ref.at[b]
        sl = pl.ds(pl.multiple_of(b * size, size), size)
        return ref.at[sl, :] if axis == 0 else ref.at[:, sl]

    def _plan(self, ins, outs, sems):
        send_sems, recv_sems, loc_sems = sems
        x, y, c = lax.axis_index("x"), lax.axis_index("y"), lax.axis_index("c")
        me = 4 * x + 2 * y + c
        begin, middle, end = [], [], []

        def peer(d):
            px = 1 - x if d & 4 else x
            py = 1 - y if d & 2 else y
            pc = 1 - c if d & 1 else c
            return (px, py, pc), 4 * px + 2 * py + pc

        for a, (kind, arr, axis) in enumerate(self.items):
            gather = kind != "scatter"
            size = None if axis is None else (arr.shape[axis] if gather else arr.shape[axis] // N_DEV)

            def src(b):
                return ins[a] if gather else self._block(ins[a], b, axis, size)

            def dst(b):
                return self._block(outs[a], b, axis, size) if gather else outs[a].at[b]

            def remote(src_ref, dst_ref, slot, dev):
                return pltpu.make_async_remote_copy(
                    src_ref=src_ref, dst_ref=dst_ref, send_sem=send_sems.at[a * 7 + slot],
                    recv_sem=recv_sems.at[a * 7 + slot], device_id=dev, device_id_type=MESH_ID)

            local = pltpu.make_async_copy(src(me), dst(me), loc_sems.at[a])
            begin.append(local)
            end.append(local.wait)
            if kind == "gather_by_chip":
                sib_dev, sib_id = peer(1)
                to_sib = remote(ins[a], dst(me), 0, sib_dev)
                begin.append(to_sib)
                end += [to_sib.wait_send, remote(ins[a], dst(sib_id), 0, sib_dev).wait_recv]
                for j, d in enumerate((2, 4, 6)):
                    dev, pid = peer(d)
                    over_ici = remote(ins[a], dst(me), 1 + j, dev)
                    begin.append(over_ici)
                    passed_on = remote(dst(pid), dst(pid), 4 + j, sib_dev)
                    middle.append((remote(ins[a], dst(pid), 1 + j, dev), passed_on))
                    end += [over_ici.wait_send, passed_on.wait_send,
                            remote(ins[a], dst(peer(d ^ 1)[1]), 4 + j, sib_dev).wait_recv]
            else:
                for d in range(1, N_DEV):
                    dev, pid = peer(d)
                    begin.append(remote(src(pid), dst(me), d - 1, dev))
                    end.append(remote(src(pid), dst(pid), d - 1, dev).wait)
        return begin, middle, end

    def start(self, ins, outs, sems):
        for cp in self._plan(ins, outs, sems)[0]:
            cp.start()

    def pass_on(self, ins, outs, sems):
        for arrived, onward in self._plan(ins, outs, sems)[1]:
            arrived.wait_recv()
            onward.start()

    def wait(self, ins, outs, sems):
        for wait in self._plan(ins, outs, sems)[2]:
            wait()


def _call(body, *, name, grid, in_specs, out_specs, out_shape, args, scratch_shapes=(), sem=None, comm=None):
    n_in, n_out, n_scr = len(in_specs), len(out_specs), len(scratch_shapes)
    if comm is None:
        return pl.pallas_call(
            body, name=name, grid=grid, out_shape=list(out_shape), in_specs=list(in_specs), out_specs=list(out_specs),
            scratch_shapes=list(scratch_shapes), compiler_params=_params(*sem))(*args)
    hbm = pl.BlockSpec(memory_space=pltpu.HBM)

    def carrier(*refs):
        bounds = [0, n_in, n_in + comm.n, n_in + comm.n + n_out, n_in + 2 * comm.n + n_out, len(refs) - 3, len(refs)]
        ins, cin, outs, cout, scr, sems = [refs[lo:hi] for lo, hi in zip(bounds[:-1], bounds[1:])]
        assert len(scr) == n_scr
        step = functools.reduce(lambda lin, ax: lin * grid[ax] + pl.program_id(ax), range(len(grid)), 0)
        steps = math.prod(grid)

        @pl.when(step == 0)
        def _():
            comm.start(cin, cout, sems)

        body(*ins, *outs, *scr)

        @pl.when(step == (2 * steps) // 3)
        def _():
            comm.pass_on(cin, cout, sems)

        @pl.when(step == steps - 1)
        def _():
            comm.wait(cin, cout, sems)

    return pl.pallas_call(
        carrier, name=name, grid=grid, out_shape=list(out_shape) + comm.out_shape,
        in_specs=list(in_specs) + [hbm] * comm.n, out_specs=list(out_specs) + [hbm] * comm.n,
        scratch_shapes=list(scratch_shapes) + comm.scratch,
        compiler_params=_params(*["arbitrary"] * len(grid)))(*args, *comm.arrays)


def _exchange(items, *, name, vmem=False):
    comm = _Exchange(items)
    space = pl.BlockSpec(memory_space=pltpu.VMEM if vmem else pltpu.HBM)

    def body(*refs):
        ins, outs, sems = refs[:comm.n], refs[comm.n:2 * comm.n], refs[2 * comm.n:]
        comm.start(ins, outs, sems)
        comm.pass_on(ins, outs, sems)
        comm.wait(ins, outs, sems)

    return pl.pallas_call(
        body, name=name, out_shape=comm.out_shape, in_specs=[space] * comm.n, out_specs=[space] * comm.n,
        scratch_shapes=comm.scratch, compiler_params=pltpu.CompilerParams(vmem_limit_bytes=VMEM_LIMIT))(*comm.arrays)


_DIMS = {"nn": (((1,), (0,)), ((), ())), "nt": (((1,), (1,)), ((), ())), "tn": (((0,), (0,)), ((), ()))}


def _matmul(a, b, *, dims, tm, tn, tk, name, epilogue="plain", out_dtype=F32, extras=(), comm=None,
            a_heads=False, b_heads=False):
    a_parts = a.shape[0] if a.ndim == 3 else 0
    b_parts = b.shape[0] if b.ndim == 3 else 0
    assert not (a_parts and dims != "nt" and not a_heads) and not (b_parts and dims != "tn")
    a2 = (a.shape[1], a.shape[2] * a_parts) if a_parts else a.shape
    b2 = (b.shape[1], b.shape[2] * b_parts) if b_parts else b.shape
    if dims == "tn":
        (K, M), N = a2, b2[1]
    else:
        (M, K), N = a2, (b2[1] if dims == "nn" else b2[0])
    tm, tn, tk = min(tm, M), min(tn, N), min(tk, K)
    assert M % tm == 0 and N % tn == 0 and K % tk == 0, (name, M, N, K, tm, tn, tk)
    if a_heads and dims == "tn":
        a_spec = pl.BlockSpec((tm // ATTN_HD, tk, ATTN_HD), lambda i, j, k: (i, k, 0))
    elif a_heads:
        a_spec = pl.BlockSpec((tk // ATTN_HD, tm, ATTN_HD), lambda i, j, k: (k, i, 0))
    elif a_parts:
        per = K // a_parts // tk
        a_spec = pl.BlockSpec((None, tm, tk), lambda i, j, k: (k // per, i, k % per))
    elif dims == "tn":
        a_spec = pl.BlockSpec((tk, tm), lambda i, j, k: (k, i))
    else:
        a_spec = pl.BlockSpec((tm, tk), lambda i, j, k: (i, k))
    if b_heads:
        b_spec = pl.BlockSpec((tn // ATTN_HD, tk, ATTN_HD), lambda i, j, k: (j, k, 0))
    elif b_parts:
        per_n = N // b_parts // tn
        b_spec = pl.BlockSpec((None, tk, tn), lambda i, j, k: (j // per_n, k, j % per_n))
    elif dims == "nt":
        b_spec = pl.BlockSpec((tn, tk), lambda i, j, k: (j, k))
    else:
        b_spec = pl.BlockSpec((tk, tn), lambda i, j, k: (k, j))
    side_by_side = lambda ref: jnp.concatenate([ref[g] for g in range(ref.shape[0])], axis=1)
    nk = K // tk
    tile = pl.BlockSpec((tm, tn), lambda i, j, k: (i, j))
    row = pl.BlockSpec((1, tn), lambda i, j, k: (0, j))
    if epilogue == "plain":
        extra_specs, out_shape, out_specs = [], [jax.ShapeDtypeStruct((M, N), out_dtype)], [tile]
    elif epilogue == "relu2":
        extra_specs, out_shape, out_specs = [], [jax.ShapeDtypeStruct((M, N), BF16)] * 2, [tile, tile]
    elif epilogue == "resgate":
        extra_specs, out_specs = [tile, row], [tile, tile]
        out_shape = [jax.ShapeDtypeStruct((M, N), BF16), jax.ShapeDtypeStruct((M, N), F32)]
    elif epilogue == "mul2a":
        extra_specs, out_shape, out_specs = [tile], [jax.ShapeDtypeStruct((M, N), BF16)], [tile]
    elif epilogue == "heads":
        per_tile = tn // ATTN_HD
        q_tiles = ATTN_HEADS // per_tile
        assert tn == 2 * ATTN_KV * ATTN_HD and N // tn in (q_tiles, q_tiles + 1)
        extra_specs = []
        out_shape = [jax.ShapeDtypeStruct((ATTN_HEADS, M, ATTN_HD), out_dtype)]
        out_specs = [pl.BlockSpec((per_tile, tm, ATTN_HD), lambda i, j, k: (jnp.minimum(j, q_tiles - 1), i, 0))]
        if N // tn > q_tiles:
            out_shape += [jax.ShapeDtypeStruct((ATTN_KV, M, ATTN_HD), out_dtype)] * 2
            out_specs += [pl.BlockSpec((ATTN_KV, tm, ATTN_HD), lambda i, j, k: (0, i, 0))] * 2
    else:
        raise ValueError(epilogue)
    n_extra = len(extra_specs)

    def body(a_ref, b_ref, *rest):
        ex, outs, acc_ref = rest[:n_extra], rest[n_extra:-1], rest[-1]
        k = pl.program_id(2)

        @pl.when(k == 0)
        def _():
            acc_ref[...] = jnp.zeros_like(acc_ref)

        a_tile = side_by_side(a_ref) if a_heads else a_ref[...]
        b_tile = side_by_side(b_ref) if b_heads else b_ref[...]
        acc_ref[...] += lax.dot_general(a_tile, b_tile, _DIMS[dims], preferred_element_type=F32)

        if epilogue == "heads":
            j = pl.program_id(1)

            @pl.when((k == nk - 1) & (j < q_tiles))
            def _():
                for g in range(per_tile):
                    outs[0][g] = acc_ref[:, g * ATTN_HD:(g + 1) * ATTN_HD].astype(out_dtype)

            if len(outs) > 1:
                @pl.when((k == nk - 1) & (j == q_tiles))
                def _():
                    for g in range(ATTN_KV):
                        outs[1][g] = acc_ref[:, g * ATTN_HD:(g + 1) * ATTN_HD].astype(out_dtype)
                        outs[2][g] = acc_ref[:, (ATTN_KV + g) * ATTN_HD:(ATTN_KV + g + 1) * ATTN_HD].astype(out_dtype)
            return

        @pl.when(k == nk - 1)
        def _():
            acc = acc_ref[...]
            if epilogue == "plain":
                outs[0][...] = acc.astype(out_dtype)
            elif epilogue == "relu2":
                act = jnp.maximum(acc, 0.0)
                outs[0][...] = act.astype(BF16)
                outs[1][...] = (act * act).astype(BF16)
            elif epilogue == "resgate":
                outs[0][...] = acc.astype(BF16)
                outs[1][...] = ex[0][...] + ex[1][...] * acc
            else:
                outs[0][...] = (acc * (2.0 * ex[0][...].astype(F32))).astype(BF16)

    res = _call(body, name=name, grid=(M // tm, N // tn, nk), out_shape=out_shape,
                in_specs=[a_spec, b_spec] + extra_specs, out_specs=out_specs,
                scratch_shapes=[pltpu.VMEM((tm, tn), F32)],
                sem=("parallel", "arbitrary" if epilogue == "heads" else "parallel", "arbitrary"),
                args=(a, b, *extras), comm=comm)
    return res[0] if len(res) == 1 else res


def _row_tile(T):
    return min(T, 256)


def _norm_mod_fwd(x, gain, sc, sh, name, comm=None):
    T, D = x.shape
    tr = min(T, 2 * _row_tile(T))

    def body(x_ref, g_ref, sc_ref, sh_ref, h_ref):
        xv = x_ref[...]
        r = lax.rsqrt(jnp.mean(xv * xv, axis=-1, keepdims=True) + EPS)
        hn = (xv * r) * g_ref[...]
        h_ref[...] = (hn * (1.0 + sc_ref[...]) + sh_ref[...]).astype(BF16)

    vec = pl.BlockSpec((1, D), lambda i: (0, 0))
    res = _call(body, name=name, grid=(T // tr,), out_shape=[jax.ShapeDtypeStruct((T, D), BF16)],
                in_specs=[pl.BlockSpec((tr, D), lambda i: (i, 0)), vec, vec, vec],
                out_specs=[pl.BlockSpec((tr, D), lambda i: (i, 0))], sem=("parallel",), args=(x, gain, sc, sh),
                comm=comm)
    return res[0] if comm is None else res


def _through_gate(dx, branch_ref, gate_ref, dbranch_ref, st_ref, row):
    dbranch_ref[...] = (dx * gate_ref[...]).astype(BF16)
    st_ref[row:row + 1, :] += jnp.sum(dx * branch_ref[...].astype(F32), axis=0, keepdims=True)


def _norm_mod_bwd(x, dh, dres, gain, sc, name, below=None):
    T, D = x.shape
    tr = _row_tile(T)

    def body(x_ref, dh_ref, dres_ref, g_ref, sc_ref, *rest):
        dx_ref, st_ref = rest[-3:-1] if below else rest[-2:]
        xv, dh_v, gain_v = x_ref[...], dh_ref[...].astype(F32), g_ref[...]
        r = lax.rsqrt(jnp.mean(xv * xv, axis=-1, keepdims=True) + EPS)
        xn = xv * r
        hn = xn * gain_v
        dhn = dh_v * (1.0 + sc_ref[...])
        dxn = dhn * gain_v
        dx = dres_ref[...] + r * (dxn - xn * jnp.mean(dxn * xn, axis=-1, keepdims=True))
        dx_ref[...] = dx

        @pl.when(pl.program_id(0) == 0)
        def _():
            st_ref[...] = jnp.zeros_like(st_ref)

        st_ref[0:1, :] += jnp.sum(dh_v, axis=0, keepdims=True)
        st_ref[1:2, :] += jnp.sum(dh_v * hn, axis=0, keepdims=True)
        st_ref[2:3, :] += jnp.sum(dhn * xn, axis=0, keepdims=True)
        if below:
            _through_gate(dx, rest[0], rest[1], rest[-1], st_ref, 3)

    vec = pl.BlockSpec((1, D), lambda i: (0, 0))
    blk = pl.BlockSpec((tr, D), lambda i: (i, 0))
    return pl.pallas_call(
        body, name=name, grid=(T // tr,),
        out_shape=[jax.ShapeDtypeStruct((T, D), F32), jax.ShapeDtypeStruct((8, D), F32)]
        + ([jax.ShapeDtypeStruct((T, D), BF16)] if below else []),
        in_specs=[blk, blk, blk, vec, vec] + ([blk, vec] if below else []),
        out_specs=[blk, pl.BlockSpec((8, D), lambda i: (0, 0))] + ([blk] if below else []),
        compiler_params=_params("arbitrary"),
    )(x, dh, dres, gain, sc, *(below or ()))


def _loss_head(y, target, below, name):
    T, D = y.shape
    tr = _row_tile(T)

    def body(y_ref, t_ref, b_ref, g_ref, dy_ref, l_ref, st_ref, db_ref):
        err = y_ref[...] - t_ref[...]
        dy = err * (1.0 / D)
        dy_ref[...] = dy

        @pl.when(pl.program_id(0) == 0)
        def _():
            l_ref[...] = jnp.zeros_like(l_ref)
            st_ref[...] = jnp.zeros_like(st_ref)

        part = jnp.sum(jnp.mean(err * err, axis=-1, keepdims=True), axis=0, keepdims=True)
        l_ref[...] += jnp.broadcast_to(0.5 * part, l_ref.shape)
        _through_gate(dy, b_ref, g_ref, db_ref, st_ref, 0)

    blk = pl.BlockSpec((tr, D), lambda i: (i, 0))
    return pl.pallas_call(
        body, name=name, grid=(T // tr,),
        out_shape=[jax.ShapeDtypeStruct((T, D), F32), jax.ShapeDtypeStruct((8, 128), F32),
                   jax.ShapeDtypeStruct((8, D), F32), jax.ShapeDtypeStruct((T, D), BF16)],
        in_specs=[blk, blk, blk, pl.BlockSpec((1, D), lambda i: (0, 0))],
        out_specs=[blk, pl.BlockSpec((8, 128), lambda i: (0, 0)), pl.BlockSpec((8, D), lambda i: (0, 0)), blk],
        compiler_params=_params("arbitrary"),
    )(y, target, *below)


def _silu(v):
    return v * jax.nn.sigmoid(v)


def _mod_fwd(c_all, mod_w, mod_b_cols, name):
    L, D, n = mod_w.shape
    tn = 512

    def body(c_ref, w_ref, b_ref, o_ref):
        cond = _silu(c_ref[...]).astype(BF16)
        o_ref[0] = jnp.dot(cond, w_ref[0].astype(BF16), preferred_element_type=F32) + b_ref[0]

    return pl.pallas_call(
        body, name=name, grid=(L, n // tn), out_shape=jax.ShapeDtypeStruct((L, N_DEV, n), F32),
        in_specs=[pl.BlockSpec((N_DEV, D), lambda l, j: (0, 0)), pl.BlockSpec((1, D, tn), lambda l, j: (l, 0, j)),
                  pl.BlockSpec((1, 1, tn), lambda l, j: (l, 0, j))],
        out_specs=pl.BlockSpec((1, N_DEV, tn), lambda l, j: (l, 0, j)),
        compiler_params=_params("parallel", "parallel"),
    )(c_all, mod_w, mod_b_cols)


def _adamw(g, w, m, v):
    m = ADAM_B1 * m + (1.0 - ADAM_B1) * g
    v = ADAM_B2 * v + (1.0 - ADAM_B2) * (g * g)
    m_hat = m / (1.0 - ADAM_B1 ** ADAM_STEP)
    v_hat = v / (1.0 - ADAM_B2 ** ADAM_STEP)
    delta = -ADAM_LR * (m_hat / (jnp.sqrt(v_hat) + ADAM_EPS) + ADAM_WD * w)
    return delta, m, v


def _mod_w_update(c_t, dmod_cols, w, m, v, name):
    L, D, n = w.shape
    tr = 256

    def body(c_ref, dm_ref, w_ref, m_ref, v_ref, g_ref, d_ref, nm_ref, nv_ref):
        cond = _silu(c_ref[...])
        dm = dm_ref[0]
        g = cond[:, 0:1] * dm[0:1, :]
        for b in range(1, N_DEV):
            g = g + cond[:, b:b + 1] * dm[b:b + 1, :]
        delta, nm, nv = _adamw(g, w_ref[0], m_ref[0], v_ref[0])
        g_ref[0], d_ref[0], nm_ref[0], nv_ref[0] = g, delta, nm, nv

    blk = pl.BlockSpec((1, tr, n), lambda l, i: (l, i, 0))
    return pl.pallas_call(
        body, name=name, grid=(L, D // tr), out_shape=[jax.ShapeDtypeStruct(w.shape, F32)] * 4,
        in_specs=[pl.BlockSpec((tr, N_DEV), lambda l, i: (i, 0)), pl.BlockSpec((1, N_DEV, n), lambda l, i: (l, 0, 0)),
                  blk, blk, blk],
        out_specs=[blk] * 4, compiler_params=_params("parallel", "parallel"),
    )(c_t, dmod_cols, w, m, v)


def _sum_adamw(parts, w, m, v, name):
    R, C = w.shape
    tr = min(R, 256 if C >= 1024 else 1024)
    assert R % tr == 0

    def body(p_ref, w_ref, m_ref, v_ref, g_ref, d_ref, nm_ref, nv_ref):
        g = p_ref[0].astype(F32)
        for s in range(1, N_DEV):
            g = g + p_ref[s].astype(F32)
        delta, nm, nv = _adamw(g, w_ref[...], m_ref[...], v_ref[...])
        g_ref[...], d_ref[...], nm_ref[...], nv_ref[...] = g, delta, nm, nv

    blk = pl.BlockSpec((tr, C), lambda i: (i, 0))
    return pl.pallas_call(
        body, name=name, grid=(R // tr,), out_shape=[jax.ShapeDtypeStruct((R, C), F32)] * 4,
        in_specs=[pl.BlockSpec((N_DEV, tr, C), lambda i: (0, i, 0)), blk, blk, blk], out_specs=[blk] * 4,
        compiler_params=_params("parallel"),
    )(parts, w, m, v)


def _sum_adamw_layers(parts, w, m, v, name):
    L, R, C = w.shape
    tr = min(R, 256 * 1024 // C)
    assert R % tr == 0 and len(parts) == L
    ni = R // tr

    def body(*refs):
        p_refs, (w_ref, m_ref, v_ref), outs = refs[:L], refs[L:L + 3], refs[L + 3:]
        for layer in range(L):
            @pl.when(pl.program_id(0) == layer)
            def _(p_ref=p_refs[layer]):
                g = p_ref[0].astype(F32)
                for s in range(1, N_DEV):
                    g = g + p_ref[s].astype(F32)
                delta, nm, nv = _adamw(g, w_ref[...], m_ref[...], v_ref[...])
                for o_ref, val in zip(outs, (g, delta, nm, nv)):
                    o_ref[...] = val

    def shares(layer):
        park = 0 if layer else ni - 1
        return pl.BlockSpec((N_DEV, tr, C), lambda l, i: (0, jnp.where(l == layer, i, park), 0))

    blk = pl.BlockSpec((None, tr, C), lambda l, i: (l, i, 0))
    return pl.pallas_call(
        body, name=name, grid=(L, ni), out_shape=[jax.ShapeDtypeStruct((L, R, C), F32)] * 4,
        in_specs=[shares(layer) for layer in range(L)] + [blk] * 3, out_specs=[blk] * 4,
        compiler_params=_params("arbitrary", "arbitrary"),
    )(*parts, w, m, v)


def _lower_bound_row1(l0, l1):
    mx = lax.stop_gradient(jnp.maximum(l0, l1))
    e0, e1 = jnp.exp(l0 - mx), jnp.exp(l1 - mx)
    p0, p1 = e0 / (e0 + e1), e1 / (e0 + e1)
    return (p0 + p1) - p0


def _lb_fwd(logits, name):
    def body(l_ref, o_ref):
        o_ref[...] = _lower_bound_row1(l_ref[0:1, :], l_ref[1:2, :])

    return pl.pallas_call(body, name=name, out_shape=jax.ShapeDtypeStruct((1, logits.shape[1]), F32))(logits)


def _lb_bwd(logits, dlb, name):
    def body(l_ref, d_ref, o_ref):
        _, vjp = jax.vjp(_lower_bound_row1, l_ref[0:1, :], l_ref[1:2, :])
        d0, d1 = vjp(d_ref[...])
        o_ref[0:1, :] = d0
        o_ref[1:2, :] = d1

    return pl.pallas_call(body, name=name, out_shape=jax.ShapeDtypeStruct(logits.shape, F32))(logits, dlb)


def _bdot(a, b, dims):
    return lax.dot_general(a.astype(BF16), b.astype(BF16), _DIMS[dims], preferred_element_type=F32)


def _rms(x, gain):
    r = lax.rsqrt(jnp.mean(x * x, axis=-1, keepdims=True) + EPS)
    xhat = x * r
    return xhat, r, xhat * gain


def _attn_band(n):
    qi = lax.broadcasted_iota(jnp.int32, (ATTN_BLOCK, 2 * ATTN_BLOCK), 0)
    ki = lax.broadcasted_iota(jnp.int32, (ATTN_BLOCK, 2 * ATTN_BLOCK), 1)
    dist = qi + ATTN_BLOCK - ki
    first_key = jnp.where(n > 0, 0, ATTN_BLOCK)
    valid = (dist >= 0) & (dist < ATTN_BLOCK) & (ki >= first_key)
    return valid, jnp.abs(dist).astype(F32)


def _attn_head_probs(qn, kn_b, valid, absdist, slope, sink):
    s = lax.dot_general(qn.astype(BF16), kn_b, _DIMS["nt"], preferred_element_type=F32) * ATTN_SCALE
    s = jnp.where(valid, s - slope * absdist, NEG_BIG)
    mx = jnp.maximum(jnp.max(s, axis=-1, keepdims=True), sink)
    e = jnp.exp(s - mx)
    es = jnp.exp(sink - mx)
    inv = 1.0 / (jnp.sum(e, axis=-1, keepdims=True) + es)
    return e * inv, es * inv


def _attn_fwd(qh, kh, vh, qg, kg, slopes, sinks, name, comm=None):
    T = qh.shape[1]
    nb = T // ATTN_BLOCK

    def body(q_ref, kp_ref, kc_ref, vp_ref, vc_ref, qg_ref, kg_ref, sl_ref, sk_ref, o_ref):
        n = pl.program_id(0)
        valid, absdist = _attn_band(n)
        for kv in range(ATTN_KV):
            _, _, kn = _rms(jnp.concatenate([kp_ref[kv], kc_ref[kv]], axis=0), kg_ref[...])
            kn_b = kn.astype(BF16)
            v_b = jnp.concatenate([vp_ref[kv], vc_ref[kv]], axis=0).astype(BF16)
            for g in range(ATTN_GROUP):
                head = kv * ATTN_GROUP + g
                _, _, qn = _rms(q_ref[head], qg_ref[...])
                p, _ = _attn_head_probs(qn, kn_b, valid, absdist, sl_ref[head], sk_ref[head])
                o_ref[head] = jnp.dot(p.astype(BF16), v_b, preferred_element_type=F32).astype(BF16)

    qspec = pl.BlockSpec((ATTN_HEADS, ATTN_BLOCK, ATTN_HD), lambda n: (0, n, 0))
    prev = pl.BlockSpec((ATTN_KV, ATTN_BLOCK, ATTN_HD), lambda n: (0, jnp.maximum(n - 1, 0), 0))
    cur = pl.BlockSpec((ATTN_KV, ATTN_BLOCK, ATTN_HD), lambda n: (0, n, 0))
    gain = pl.BlockSpec((1, ATTN_HD), lambda n: (0, 0))
    scalars = pl.BlockSpec(memory_space=pltpu.SMEM)
    return _call(body, name=name, grid=(nb,), out_shape=[jax.ShapeDtypeStruct(qh.shape, BF16)],
                 in_specs=[qspec, prev, cur, prev, cur, gain, gain, scalars, scalars], out_specs=[qspec],
                 sem=("parallel",), args=(qh, kh, kh, vh, vh, qg, kg, slopes, sinks), comm=comm)


def _rms_bwd(dy, xhat, r, gain):
    dxh = dy * gain
    dx = r * (dxh - xhat * jnp.mean(dxh * xhat, axis=-1, keepdims=True))
    return dx, jnp.sum(dy * xhat, axis=0, keepdims=True)


def _attn_bwd(qh, kh, vh, doh, qg, kg, slope_col, sink_col, name, comm=None):
    T = qh.shape[1]
    nb = T // ATTN_BLOCK
    rows = ATTN_GROUP * ATTN_BLOCK

    def body(q_ref, kp_ref, kc_ref, vp_ref, vc_ref, do_ref, qg_ref, kg_ref, sl_ref, sk_ref,
             dq_ref, dk_ref, dv_ref, dqg_ref, dkg_ref, dsk_ref, carry_ref, sk_acc):
        step = pl.program_id(0)
        n = nb - 1 - step
        qg_v, kg_v = qg_ref[...], kg_ref[...]

        @pl.when(step == 0)
        def _():
            dqg_ref[...] = jnp.zeros_like(dqg_ref)
            dkg_ref[...] = jnp.zeros_like(dkg_ref)
            carry_ref[...] = jnp.zeros_like(carry_ref)
            sk_acc[...] = jnp.zeros_like(sk_acc)

        for kv in range(ATTN_KV):
            heads = slice(kv * ATTN_GROUP, (kv + 1) * ATTN_GROUP)
            qhat, rq, qn = _rms(q_ref[heads].reshape(rows, ATTN_HD), qg_v)
            khat, rk, kn = _rms(jnp.concatenate([kp_ref[kv], kc_ref[kv]], axis=0), kg_v)
            s = _bdot(qn, kn, "nt") * ATTN_SCALE
            qi = lax.broadcasted_iota(jnp.int32, s.shape, 0) & (ATTN_BLOCK - 1)
            ki = lax.broadcasted_iota(jnp.int32, s.shape, 1)
            dist = qi + ATTN_BLOCK - ki
            first_key = jnp.where(n > 0, 0, ATTN_BLOCK)
            valid = (dist >= 0) & (dist < ATTN_BLOCK) & (ki >= first_key)
            s = jnp.where(valid, s - sl_ref[kv] * jnp.abs(dist).astype(F32), NEG_BIG)
            sink = sk_ref[kv]
            mx = jnp.maximum(jnp.max(s, axis=-1, keepdims=True), sink)
            e = jnp.exp(s - mx)
            es = jnp.exp(sink - mx)
            den = jnp.sum(e, axis=-1, keepdims=True) + es
            p, ps = e / den, es / den
            v = jnp.concatenate([vp_ref[kv], vc_ref[kv]], axis=0)
            do = do_ref[heads].reshape(rows, ATTN_HD)
            dp = _bdot(do, v, "nt")
            delta = jnp.sum(p * dp, axis=-1, keepdims=True)
            ds = p * (dp - delta)
            dqn = _bdot(ds, kn, "nn") * ATTN_SCALE
            dkn = _bdot(ds, qn, "tn") * ATTN_SCALE
            dv = _bdot(p, do, "tn")
            dq, dqg = _rms_bwd(dqn, qhat, rq, qg_v)
            dk, dkg = _rms_bwd(dkn, khat, rk, kg_v)
            dq_ref[heads] = dq.reshape(ATTN_GROUP, ATTN_BLOCK, ATTN_HD).astype(BF16)
            dqg_ref[0:1, :] += dqg
            dkg_ref[0:1, :] += dkg
            dk_ref[kv] = (dk[ATTN_BLOCK:, :] + carry_ref[kv, 0]).astype(BF16)
            dv_ref[kv] = (dv[ATTN_BLOCK:, :] + carry_ref[kv, 1]).astype(BF16)
            carry_ref[kv, 0] = dk[:ATTN_BLOCK, :]
            carry_ref[kv, 1] = dv[:ATTN_BLOCK, :]
            sk_acc[kv] += -ps * delta

        @pl.when(step == nb - 1)
        def _():
            for head in range(ATTN_HEADS):
                kv, g = divmod(head, ATTN_GROUP)
                tot = jnp.sum(sk_acc[kv, g * ATTN_BLOCK:(g + 1) * ATTN_BLOCK, :], axis=0, keepdims=True)
                dsk_ref[head:head + 1, :] = jnp.broadcast_to(tot, (1, 128))

    qspec = pl.BlockSpec((ATTN_HEADS, ATTN_BLOCK, ATTN_HD), lambda s: (0, nb - 1 - s, 0))
    prev = pl.BlockSpec((ATTN_KV, ATTN_BLOCK, ATTN_HD), lambda s: (0, jnp.maximum(nb - 2 - s, 0), 0))
    cur = pl.BlockSpec((ATTN_KV, ATTN_BLOCK, ATTN_HD), lambda s: (0, nb - 1 - s, 0))
    gain = pl.BlockSpec((1, ATTN_HD), lambda s: (0, 0))
    col = pl.BlockSpec((ATTN_KV, rows, 1), lambda s: (0, 0, 0))
    acc = pl.BlockSpec((8, ATTN_HD), lambda s: (0, 0))
    return _call(
        body, name=name, grid=(nb,),
        out_shape=[jax.ShapeDtypeStruct(qh.shape, BF16), jax.ShapeDtypeStruct(kh.shape, BF16),
                   jax.ShapeDtypeStruct(kh.shape, BF16), jax.ShapeDtypeStruct((8, ATTN_HD), F32),
                   jax.ShapeDtypeStruct((8, ATTN_HD), F32), jax.ShapeDtypeStruct((ATTN_HEADS, 128), F32)],
        in_specs=[qspec, prev, cur, prev, cur, qspec, gain, gain, col, col],
        out_specs=[qspec, cur, cur, acc, acc, pl.BlockSpec((ATTN_HEADS, 128), lambda s: (0, 0))],
        scratch_shapes=[pltpu.VMEM((ATTN_KV, 2, ATTN_BLOCK, ATTN_HD), F32), pltpu.VMEM((ATTN_KV, rows, 1), F32)],
        sem=("arbitrary",), args=(qh, kh, kh, vh, vh, doh, qg, kg, slope_col, sink_col), comm=comm)


@functools.partial(jax.custom_vjp, nondiff_argnums=(2,))
def _mm(a, b, dims):
    return _bdot(a, b, dims)


def _mm_fwd(a, b, dims):
    return _bdot(a, b, dims), (a, b)


def _mm_bwd(dims, res, ct):
    a, b = res
    if dims == "nn":
        return _bdot(ct, b, "nt"), _bdot(a, ct, "tn")
    if dims == "nt":
        return _bdot(ct, b, "nn"), _bdot(ct, a, "tn")
    return _bdot(b, ct, "nt"), _bdot(a, ct, "nn")


_mm.defvjp(_mm_fwd, _mm_bwd)


def _same_chunk_mask(rows, upper):
    ri = lax.broadcasted_iota(jnp.int32, (rows, rows), 0)
    ci = lax.broadcasted_iota(jnp.int32, (rows, rows), 1)
    same = (ri >> HGRN_CHUNK_SHIFT) == (ci >> HGRN_CHUNK_SHIFT)
    return same & ((ri <= ci) if upper else (ri >= ci))


@jax.custom_vjp
def _chunk_cumsum(x, tri, tri_t):
    hi = x.astype(BF16)
    rest = x - hi.astype(F32)
    mid = rest.astype(BF16)
    lo = (rest - mid.astype(F32)).astype(BF16)
    out = jnp.dot(tri, jnp.concatenate([hi, mid, lo], axis=1), preferred_element_type=F32)
    w = x.shape[1]
    return out[:, :w] + out[:, w:2 * w] + out[:, 2 * w:]


def _chunk_cumsum_fwd(x, tri, tri_t):
    return _chunk_cumsum(x, tri, tri_t), (tri, tri_t)


def _chunk_cumsum_bwd(res, ct):
    tri, tri_t = res
    return _chunk_cumsum(ct, tri_t, tri), jnp.zeros_like(tri), jnp.zeros_like(tri_t)


_chunk_cumsum.defvjp(_chunk_cumsum_fwd, _chunk_cumsum_bwd)


def _hgrn_masks(rows):
    nc = rows // HGRN_CHUNK
    lower = _same_chunk_mask(rows, False)
    chunk_of_row = lax.broadcasted_iota(jnp.int32, (rows, HGRN_DK), 0) >> HGRN_CHUNK_SHIFT
    row_in_chunk = lax.broadcasted_iota(jnp.int32, (nc, HGRN_CHUNK, HGRN_DK), 1)
    return dict(lower=lower, tri=lower.astype(BF16), tri_t=_same_chunk_mask(rows, True).astype(BF16),
                in_chunk=[chunk_of_row == c for c in range(nc)],
                mid_row=row_in_chunk == HGRN_CHUNK // 2 - 1, last_row=row_in_chunk == HGRN_CHUNK - 1)


def _hgrn_block(masks, st, qr, fr, v, gr, lb, og):
    rows = qr.shape[0]
    nc = rows // HGRN_CHUNK
    per_chunk = lambda m: m.reshape(nc, HGRN_CHUNK, HGRN_DK)
    flat = lambda m: m.reshape(rows, HGRN_DK)
    by_chunk = lambda m: jnp.concatenate([jnp.where(masks["in_chunk"][c], m, 0.0) for c in range(nc)], axis=1)

    forget = lb + (1.0 - lb) * jax.nn.sigmoid(fr)
    k = 1.0 - forget
    b = _chunk_cumsum(jnp.log(forget), masks["tri"], masks["tri_t"])
    b3 = per_chunk(b)
    piv = jnp.sum(jnp.where(masks["mid_row"], b3, 0.0), axis=1, keepdims=True)
    b_last = jnp.sum(jnp.where(masks["last_row"], b3, 0.0), axis=1, keepdims=True)
    q = _silu(qr) * HGRN_SCALE
    a = _mm(q * flat(jnp.exp(b3 - piv)), k * flat(jnp.exp(piv - b3)), "nt")
    o = _mm(jnp.where(masks["lower"], a, 0.0), v, "nn")
    updates = _mm(v, by_chunk(k * flat(jnp.exp(b_last - b3))), "tn")
    decay = jnp.exp(b_last)
    before = []
    for c in range(nc):
        before.append(st)
        st = st * decay[c] + updates[:, c * HGRN_DK:(c + 1) * HGRN_DK]
    o = o + _mm(by_chunk(q * jnp.exp(b)), jnp.concatenate(before, axis=1), "nt")
    y = (o * lax.rsqrt(jnp.mean(o * o, axis=-1, keepdims=True) + EPS)) * og * _silu(gr)
    return y, st


def _hgrn_tile(T):
    return min(T, 256)


HGRN_HEADS_PER_STEP = 16
HGRN_GROUPS = HGRN_HEADS // HGRN_HEADS_PER_STEP


def _hgrn_fwd(proj, lb, og, name, comm=None):
    T = proj.shape[0]
    tb = _hgrn_tile(T)
    hp, wide = HGRN_HEADS_PER_STEP, HGRN_HEADS_PER_STEP * HGRN_DK

    def body(q_ref, f_ref, v_ref, g_ref, lb_ref, og_ref, o_ref, s_ref, st_ref):
        @pl.when(pl.program_id(1) == 0)
        def _():
            st_ref[...] = jnp.zeros_like(st_ref)

        masks = _hgrn_masks(tb)
        for j in range(hp):
            ln = slice(j * HGRN_DK, (j + 1) * HGRN_DK)
            st = st_ref[j]
            s_ref[j, 0] = st
            y, st_ref[j] = _hgrn_block(masks, st, q_ref[:, ln], f_ref[:, ln], v_ref[:, ln], g_ref[:, ln], lb_ref[j],
                                       og_ref[j])
            o_ref[:, ln] = y.astype(BF16)

    part = lambda p: pl.BlockSpec((tb, wide), lambda h, t: (t, p * HGRN_GROUPS + h))
    vec = pl.BlockSpec((hp, 1, HGRN_DK), lambda h, t: (h, 0, 0))
    return _call(
        body, name=name, grid=(HGRN_GROUPS, T // tb),
        out_shape=[jax.ShapeDtypeStruct((T, D_MODEL), BF16),
                   jax.ShapeDtypeStruct((HGRN_HEADS, T // tb, HGRN_DK, HGRN_DK), F32)],
        in_specs=[part(0), part(1), part(2), part(3), vec, vec],
        out_specs=[pl.BlockSpec((tb, wide), lambda h, t: (t, h)),
                   pl.BlockSpec((hp, 1, HGRN_DK, HGRN_DK), lambda h, t: (h, t, 0, 0))],
        scratch_shapes=[pltpu.VMEM((hp, HGRN_DK, HGRN_DK), F32)], sem=("parallel", "arbitrary"),
        args=(proj, proj, proj, proj, lb, og), comm=comm)


def _hgrn_bwd(proj, states, do, lb, og, name, comm=None):
    T = proj.shape[0]
    tb = _hgrn_tile(T)
    nt, hp, wide = T // tb, HGRN_HEADS_PER_STEP, HGRN_HEADS_PER_STEP * HGRN_DK

    def body(q_ref, f_ref, v_ref, g_ref, s_ref, do_ref, lb_ref, og_ref, dp_ref, dlb_ref, dog_ref, dst_ref):
        @pl.when(pl.program_id(1) == 0)
        def _():
            dst_ref[...] = jnp.zeros_like(dst_ref)
            dlb_ref[...] = jnp.zeros_like(dlb_ref)
            dog_ref[...] = jnp.zeros_like(dog_ref)

        block = functools.partial(_hgrn_block, _hgrn_masks(tb))
        for j in range(hp):
            ln = slice(j * HGRN_DK, (j + 1) * HGRN_DK)
            _, vjp = jax.vjp(block, s_ref[j, 0], q_ref[:, ln], f_ref[:, ln], v_ref[:, ln], g_ref[:, ln],
                             lb_ref[j], og_ref[j])
            dst_ref[j], dq, df, dv, dg, dlb, dog = vjp((do_ref[:, ln], dst_ref[j]))
            for p, part_grad in enumerate((dq, df, dv, dg)):
                dp_ref[p, :, ln] = part_grad.astype(BF16)
            dlb_ref[j] += dlb
            dog_ref[j] += dog

    part = lambda p: pl.BlockSpec((tb, wide), lambda h, t: (nt - 1 - t, p * HGRN_GROUPS + h))
    vec = pl.BlockSpec((hp, 1, HGRN_DK), lambda h, t: (h, 0, 0))
    head = pl.BlockSpec((tb, wide), lambda h, t: (nt - 1 - t, h))
    return _call(
        body, name=name, grid=(HGRN_GROUPS, nt),
        out_shape=[jax.ShapeDtypeStruct((4, T, D_MODEL), BF16)] + [jax.ShapeDtypeStruct((HGRN_HEADS, 1, HGRN_DK), F32)] * 2,
        in_specs=[part(0), part(1), part(2), part(3),
                  pl.BlockSpec((hp, 1, HGRN_DK, HGRN_DK), lambda h, t: (h, nt - 1 - t, 0, 0)), head, vec, vec],
        out_specs=[pl.BlockSpec((4, tb, wide), lambda h, t: (0, nt - 1 - t, h)), vec, vec],
        scratch_shapes=[pltpu.VMEM((hp, HGRN_DK, HGRN_DK), F32)], sem=("parallel", "arbitrary"),
        args=(proj, proj, proj, proj, states, do, lb, og), comm=comm)


def _cols_to_blocks(g, n8):
    K = g.shape[0]
    return g.reshape(K, N_DEV, n8).transpose(1, 0, 2)


def _blocks_to_cols(wg):
    _, K, n8 = wg.shape
    return wg.transpose(1, 0, 2).reshape(K, N_DEV * n8)


def _pack(parts):
    flat = []
    for p in parts:
        v = p.reshape(-1)
        flat.append(jnp.pad(v, (0, (-v.shape[0]) % 1024)))
    return jnp.concatenate(flat).reshape(-1, 128)


def _unpack(packed, like):
    flat, out, off = packed.reshape(-1), [], 0
    for p in like:
        size = math.prod(p.shape)
        out.append(flat[off:off + size].reshape(p.shape))
        off += size + (-size) % 1024
    return out


def _heads_major(a, heads):
    T = a.shape[0]
    return a.reshape(T, heads, ATTN_HD).transpose(1, 0, 2)


def _heads_minor(a):
    heads, T, _ = a.shape
    return a.transpose(1, 0, 2).reshape(T, heads * ATTN_HD)


def kernel(x, c, mod_w, mod_b, norm_mix, norm_mlp, attn_w_in, attn_w_out, attn_q_gain, attn_k_gain, attn_sinks, hgrn_w_in, hgrn_w_out, hgrn_o_gain, hgrn_lb_logits, mlp_w1, mlp_w2, loss_target, m_mod_w, m_mod_b, m_norm_mix, m_norm_mlp, m_attn_w_in, m_attn_w_out, m_attn_q_gain, m_attn_k_gain, m_attn_sinks, m_hgrn_w_in, m_hgrn_w_out, m_hgrn_o_gain, m_hgrn_lb_logits, m_mlp_w1, m_mlp_w2, v_mod_w, v_mod_b, v_norm_mix, v_norm_mlp, v_attn_w_in, v_attn_w_out, v_attn_q_gain, v_attn_k_gain, v_attn_sinks, v_hgrn_w_in, v_hgrn_w_out, v_hgrn_o_gain, v_hgrn_lb_logits, v_mlp_w1, v_mlp_w2):
    T = x.shape[1]
    me = 4 * lax.axis_index("x") + 2 * lax.axis_index("y") + lax.axis_index("c")
    x0, target = x[0], loss_target[0]
    n_mod = mod_w.shape[2]

    shards = [attn_w_in[0], attn_w_out[0], hgrn_w_in[0], hgrn_w_out[0], mlp_w1[0], mlp_w1[1], mlp_w2[0], mlp_w2[1]]
    sb = [s.astype(BF16) for s in shards]
    gather = lambda *items: _Exchange([("gather_by_chip", a, axis) for a, axis in items])

    c_all = _exchange([("gather", c.reshape(16, 128), None)], vmem=True, name="gather_c")[0].reshape(N_DEV, D_MODEL)
    mod_b_cols = lax.dynamic_slice_in_dim(mod_b, me * n_mod, n_mod, axis=1).reshape(2, 1, n_mod)
    mod_cols = _mod_fwd(c_all, mod_w, mod_b_cols, "mod_fwd")
    mod_all = _exchange([("gather", mod_cols.reshape(-1, 128), None)], vmem=True, name="gather_mod")[0]
    mod_all = mod_all.reshape(N_DEV, 2, N_DEV, n_mod)
    mod_mine = lax.dynamic_index_in_dim(mod_all, me, axis=2, keepdims=False)
    mod_mine = mod_mine.transpose(1, 0, 2).reshape(2, N_MOD, 1, D_MODEL)

    lb = _lb_fwd(hgrn_lb_logits, "lb_fwd").reshape(HGRN_HEADS, 1, HGRN_DK)
    og = hgrn_o_gain.reshape(HGRN_HEADS, 1, HGRN_DK)
    slopes = jnp.exp2(-8.0 * jnp.arange(1, ATTN_HEADS + 1, dtype=F32) / ATTN_HEADS)
    sinks = attn_sinks[0]
    per_row = lambda vals: jnp.repeat(vals.reshape(ATTN_KV, ATTN_GROUP), ATTN_BLOCK, axis=1).reshape(
        ATTN_KV, ATTN_GROUP * ATTN_BLOCK, 1)
    slope_col, sink_col = per_row(slopes), per_row(sinks)

    saved = []
    xi = x0
    w_mlp1, w_mlp2 = [None, None], [None, None]
    for i in range(2):
        sh1, sc1, g1, sh2, sc2, g2 = [mod_mine[i, j] for j in range(N_MOD)]
        if i == 0:
            h, w_attn_in = _norm_mod_fwd(xi, norm_mix[i:i + 1], sc1, sh1, "norm_mix_fwd0", comm=gather((sb[0], None)))
            w_attn_in = _blocks_to_cols(w_attn_in)
            qh, kh, vh, w_attn_out = _matmul(h, w_attn_in, dims="nn", tm=1024, tn=512, tk=2048, name="attn_in_fwd",
                                             epilogue="heads", comm=gather((sb[1], 0)))
            o, w_mlp1[0] = _attn_fwd(qh, kh, vh, attn_q_gain, attn_k_gain, slopes, sinks, "attn_fwd",
                                     comm=gather((sb[4], 1)))
            mix = (qh, kh, vh)
            w_out = w_attn_out
        else:
            h = _norm_mod_fwd(xi, norm_mix[i:i + 1], sc1, sh1, "norm_mix_fwd1")
            proj, w_hgrn_out = _matmul(h, w_hgrn_in, dims="nn", tm=1024, tn=2048, tk=2048, name="hgrn_in_fwd",
                                       comm=gather((sb[3], 0)))
            o, states, w_mlp1[1] = _hgrn_fwd(proj, lb, og, "hgrn_fwd", comm=gather((sb[5], 1)))
            mix = (proj, states)
            w_out = w_hgrn_out
        y, x1 = _matmul(o, w_out, dims="nn", tm=1024, tn=1024, tk=2048, name=f"mix_out_fwd{i}", epilogue="resgate",
                        extras=(xi, g1), a_heads=i == 0)
        h2 = _norm_mod_fwd(x1, norm_mlp[i:i + 1], sc2, sh2, f"norm_mlp_fwd{i}")
        if i == 0:
            act, act2, w_mlp2[0] = _matmul(h2, w_mlp1[0], dims="nn", tm=1024, tn=2048, tk=2048, name="mlp1_fwd0",
                                           epilogue="relu2", comm=gather((sb[6], 0)))
            z, x2, w_hgrn_in = _matmul(act2, w_mlp2[0], dims="nn", tm=1024, tn=1024, tk=2048, name="mlp2_fwd0",
                                       epilogue="resgate", extras=(x1, g2), comm=gather((sb[2], 1)))
        else:
            act, act2, w_mlp2[1] = _matmul(h2, w_mlp1[1], dims="nn", tm=1024, tn=2048, tk=2048, name="mlp1_fwd1",
                                           epilogue="relu2", comm=gather((sb[7], 0)))
            z, x2 = _matmul(act2, w_mlp2[1], dims="nn", tm=1024, tn=1024, tk=2048, name="mlp2_fwd1", epilogue="resgate",
                            extras=(x1, g2))
        saved.append((xi, h, o, y, x1, h2, act, act2, z, mix))
        xi = x2

    dx, loss_tile, st_g2, dz = _loss_head(xi, target, (saved[1][8], mod_mine[1, 5]), "loss_head")
    loss = lax.psum(loss_tile[0, 0], ("x", "y", "c"))

    scatter = lambda *items: _Exchange([("scatter", a, axis) for a, axis in items])
    shares, dmods, dnorm_mix, dnorm_mlp = {}, [None, None], [None, None], [None, None]
    wgrad = lambda a, b, name, tn=1024: _matmul(a, b, dims="tn", tm=2048, tn=tn, tk=2048, name=name, out_dtype=BF16)
    for i in (1, 0):
        sh1, sc1, g1, sh2, sc2, g2 = [mod_mine[i, j] for j in range(N_MOD)]
        xin, h, o, y, x1, h2, act, act2, z, mix = saved[i]
        if i == 1:
            dpre = _matmul(dz, w_mlp2[1], dims="nt", tm=1024, tn=2048, tk=2048, name="mlp2_bwd1", epilogue="mul2a",
                           extras=(act,))
        else:
            dpre, shares["hgrn_w_in"] = _matmul(dz, w_mlp2[0], dims="nt", tm=1024, tn=2048, tk=2048, name="mlp2_bwd0",
                                                epilogue="mul2a", extras=(act,), comm=scatter((g_hgrn_in, 1)))
        g_mlp2 = wgrad(act2, dz, f"mlp2_wgrad{i}")
        if i == 1:
            dh2 = _matmul(dpre, w_mlp1[1], dims="nt", tm=1024, tn=1024, tk=4096, name="mlp1_bwd1", out_dtype=BF16)
        else:
            dh2, shares["mlp_w2_0"] = _matmul(dpre, w_mlp1[0], dims="nt", tm=1024, tn=1024, tk=4096, name="mlp1_bwd0",
                                              out_dtype=BF16, comm=scatter((g_mlp2, 0)))
        g_mlp1 = wgrad(h2, dpre, f"mlp1_wgrad{i}")
        dx1, st_mlp, dy = _norm_mod_bwd(x1, dh2, dx, norm_mlp[i:i + 1], sc2, f"norm_mlp_bwd{i}", below=(y, g1))
        w_out = w_attn_out if i == 0 else w_hgrn_out
        if i == 0:
            qh, kh, vh = mix
            doh = _matmul(dy, w_out, dims="nt", tm=1024, tn=512, tk=2048, name="mix_out_bwd0", epilogue="heads",
                          out_dtype=BF16)
            g_out = _matmul(o, dy, dims="tn", tm=1024, tn=1024, tk=2048, name="mix_out_wgrad0", out_dtype=BF16,
                            a_heads=True)
            dqh, dkh, dvh, dqg, dkg, dsk, shares["mlp_w1_0"] = _attn_bwd(
                qh, kh, vh, doh, attn_q_gain, attn_k_gain, slope_col, sink_col, "attn_bwd",
                comm=scatter((g_mlp1, 1)))
            dproj = jnp.concatenate([dqh, dkh, dvh], axis=0)
            d_q_gain, d_k_gain, d_sinks = dqg[0:1], dkg[0:1], dsk[:, 0].reshape(1, ATTN_HEADS)
            g_attn_in, shares["attn_w_out"] = _matmul(h, dproj, dims="tn", tm=2048, tn=640, tk=2048, name="mix_in_wgrad0",
                                                      out_dtype=BF16, b_heads=True, comm=scatter((g_out, 0)))
            g_attn_in = _cols_to_blocks(g_attn_in, attn_w_in.shape[2])
            dh, shares["attn_w_in"] = _matmul(dproj, w_attn_in, dims="nt", tm=1024, tn=1024, tk=2560, out_dtype=BF16,
                                              name="mix_in_bwd0", comm=scatter((g_attn_in, None)), a_heads=True)
        else:
            do = _matmul(dy, w_out, dims="nt", tm=1024, tn=1024, tk=2048, name="mix_out_bwd1")
            g_out = wgrad(o, dy, "mix_out_wgrad1")
            proj, states = mix
            dproj, dlb, d_o_gain, shares["mlp_w2_1"], shares["mlp_w1_1"], shares["hgrn_w_out"] = _hgrn_bwd(
                proj, states, do, lb, og, "hgrn_bwd", comm=scatter((g_mlp2, 0), (g_mlp1, 1), (g_out, 0)))
            d_lb_logits = _lb_bwd(hgrn_lb_logits, dlb.reshape(1, D_MODEL), "lb_bwd")
            dh = _matmul(dproj, w_hgrn_in, dims="nt", tm=1024, tn=2048, tk=2048, name="mix_in_bwd1", out_dtype=BF16)
            g_hgrn_in = wgrad(h, dproj, "mix_in_wgrad1")
        d_gate2 = st_g2[0:1] if i == 1 else st_mix_above[3:4]
        if i == 1:
            dx, st_mix, dz = _norm_mod_bwd(xin, dh, dx1, norm_mix[i:i + 1], sc1, "norm_mix_bwd1",
                                           below=(saved[0][8], mod_mine[0, 5]))
            st_mix_above = st_mix
        else:
            dx, st_mix = _norm_mod_bwd(xin, dh, dx1, norm_mix[i:i + 1], sc1, "norm_mix_bwd0")
        dmods[i] = jnp.concatenate([st_mix[0:1], st_mix[1:2], st_mlp[3:4], st_mlp[0:1], st_mlp[1:2], d_gate2], axis=1)
        dnorm_mix[i], dnorm_mlp[i] = st_mix[2:3], st_mlp[2:3]

    single = {"attn_w_in": (attn_w_in, m_attn_w_in, v_attn_w_in), "attn_w_out": (attn_w_out, m_attn_w_out, v_attn_w_out),
              "hgrn_w_in": (hgrn_w_in, m_hgrn_w_in, v_hgrn_w_in), "hgrn_w_out": (hgrn_w_out, m_hgrn_w_out, v_hgrn_w_out)}
    big = {nm: _sum_adamw(shares[nm], w[0], m[0], v[0], f"adamw_{nm}") for nm, (w, m, v) in single.items()}
    big["mlp_w1"] = _sum_adamw_layers([shares["mlp_w1_0"], shares["mlp_w1_1"]], mlp_w1, m_mlp_w1, v_mlp_w1, "adamw_mlp_w1")
    big["mlp_w2"] = _sum_adamw_layers([shares["mlp_w2_0"], shares["mlp_w2_1"]], mlp_w2, m_mlp_w2, v_mlp_w2, "adamw_mlp_w2")

    small_w = [mod_b, norm_mix, norm_mlp, attn_q_gain, attn_k_gain, attn_sinks, hgrn_o_gain, hgrn_lb_logits]
    small_m = [m_mod_b, m_norm_mix, m_norm_mlp, m_attn_q_gain, m_attn_k_gain, m_attn_sinks, m_hgrn_o_gain, m_hgrn_lb_logits]
    small_v = [v_mod_b, v_norm_mix, v_norm_mlp, v_attn_q_gain, v_attn_k_gain, v_attn_sinks, v_hgrn_o_gain, v_hgrn_lb_logits]
    small_g = [jnp.concatenate(dmods, axis=0), jnp.concatenate(dnorm_mix, axis=0), jnp.concatenate(dnorm_mlp, axis=0),
               d_q_gain, d_k_gain, d_sinks, d_o_gain.reshape(hgrn_o_gain.shape), d_lb_logits]
    packed_g = _pack(small_g)
    pad_rows = (-packed_g.shape[0]) % 8
    pad8 = lambda a: jnp.pad(a, ((0, pad_rows), (0, 0)))
    all_small = _exchange([("gather", pad8(packed_g), None)], vmem=True, name="gather_small_grads")[0]
    sg, sd, sm, sv = _sum_adamw(all_small, pad8(_pack(small_w)), pad8(_pack(small_m)), pad8(_pack(small_v)),
                                "adamw_small")
    small = [_unpack(t, small_w) for t in (sg, sd, sm, sv)]

    n_modb = N_MOD * D_MODEL
    dmod_all = all_small[:, :2 * n_modb // 128, :].reshape(N_DEV, 2, n_modb)
    dmod_cols = lax.dynamic_slice_in_dim(dmod_all, me * n_mod, n_mod, axis=2).transpose(1, 0, 2)
    modw = _mod_w_update(c_all.T, dmod_cols, mod_w, m_mod_w, v_mod_w, "adamw_mod_w")

    def leaf(k):
        one = lambda a: big[a][k][None]
        s = small[k]
        return [modw[k], s[0], s[1], s[2], one("attn_w_in"), one("attn_w_out"), s[3], s[4], s[5], one("hgrn_w_in"),
                one("hgrn_w_out"), s[6], s[7], big["mlp_w1"][k], big["mlp_w2"][k]]

    return (loss, dx[None], *leaf(0), *leaf(1), *leaf(2), *leaf(3))
```

```python
import functools
import math

import jax
import jax.numpy as jnp
from jax import lax
from jax.experimental import pallas as pl
from jax.experimental.pallas import tpu as pltpu

F32, BF16 = jnp.float32, jnp.bfloat16
N_DEV = 8
D_MODEL = 2048
N_MOD = 6
EPS = 1e-6
ATTN_HD, ATTN_HEADS, ATTN_KV, ATTN_GROUP, ATTN_BLOCK = 64, 32, 4, 8, 128
ATTN_SCALE = 1.0 / math.sqrt(ATTN_HD)
HGRN_HEADS, HGRN_DK, HGRN_CHUNK = 16, 128, 64
HGRN_SCALE = 1.0 / math.sqrt(HGRN_DK)
HGRN_CHUNK_SHIFT = HGRN_CHUNK.bit_length() - 1
assert 1 << HGRN_CHUNK_SHIFT == HGRN_CHUNK
D_FF = 4 * D_MODEL
ADAM_LR, ADAM_B1, ADAM_B2, ADAM_EPS, ADAM_WD, ADAM_STEP = 0.001, 0.9, 0.999, 1e-08, 0.01, 10
NEG_BIG = -1e30
VMEM_LIMIT = 56 * 1024 * 1024
MESH_ID = pl.DeviceIdType.MESH


def _params(*sem):
    return pltpu.CompilerParams(dimension_semantics=sem, vmem_limit_bytes=VMEM_LIMIT)


class _Exchange:
    def __init__(self, items):
        self.items = items
        self.n = len(items)
        self.out_shape = []
        for kind, a, axis in items:
            assert kind in ("gather", "gather_by_chip", "scatter")
            if kind != "scatter":
                shape = (N_DEV,) + a.shape if axis is None else tuple(
                    d * N_DEV if i == axis else d for i, d in enumerate(a.shape))
            else:
                shape = a.shape if axis is None else (N_DEV,) + tuple(
                    d // N_DEV if i == axis else d for i, d in enumerate(a.shape))
            self.out_shape.append(jax.ShapeDtypeStruct(shape, a.dtype))
        self.scratch = [pltpu.SemaphoreType.DMA((7 * self.n,)), pltpu.SemaphoreType.DMA((7 * self.n,)),
                        pltpu.SemaphoreType.DMA((self.n,))]
        self.arrays = [a for _, a, _ in items]

    @staticmethod
    def _block(ref, b, axis, size):
        if axis is None:
            return ref.at[b]
        sl = pl.ds(pl.multiple_of(b * size, size), size)
        return ref.at[sl, :] if axis == 0 else ref.at[:, sl]

    def _plan(self, ins, outs, sems):
        send_sems, recv_sems, loc_sems = sems
        x, y, c = lax.axis_index("x"), lax.axis_index("y"), lax.axis_index("c")
        me = 4 * x + 2 * y + c
        begin, middle, end = [], [], []

        def peer(d):
            px = 1 - x if d & 4 else x
            py = 1 - y if d & 2 else y
            pc = 1 - c if d & 1 else c
            return (px, py, pc), 4 * px + 2 * py + pc

        for a, (kind, arr, axis) in enumerate(self.items):
            gather = kind != "scatter"
            size = None if axis is None else (arr.shape[axis] if gather else arr.shape[axis] // N_DEV)

            def src(b):
                return ins[a] if gather else self._block(ins[a], b, axis, size)

            def dst(b):
                return self._block(outs[a], b, axis, size) if gather else outs[a].at[b]

            def remote(src_ref, dst_ref, slot, dev):
                return pltpu.make_async_remote_copy(
                    src_ref=src_ref, dst_ref=dst_ref, send_sem=send_sems.at[a * 7 + slot],
                    recv_sem=recv_sems.at[a * 7 + slot], device_id=dev, device_id_type=MESH_ID)

            local = pltpu.make_async_copy(src(me), dst(me), loc_sems.at[a])
            begin.append(local)
            end.append(local.wait)
            if kind == "gather_by_chip":
                sib_dev, sib_id = peer(1)
                to_sib = remote(ins[a], dst(me), 0, sib_dev)
                begin.append(to_sib)
                end += [to_sib.wait_send, remote(ins[a], dst(sib_id), 0, sib_dev).wait_recv]
                for j, d in enumerate((2, 4, 6)):
                    dev, pid = peer(d)
                    over_ici = remote(ins[a], dst(me), 1 + j, dev)
                    begin.append(over_ici)
                    passed_on = remote(dst(pid), dst(pid), 4 + j, sib_dev)
                    middle.append((remote(ins[a], dst(pid), 1 + j, dev), passed_on))
                    end += [over_ici.wait_send, passed_on.wait_send,
                            remote(ins[a], dst(peer(d ^ 1)[1]), 4 + j, sib_dev).wait_recv]
            else:
                for d in range(1, N_DEV):
                    dev, pid = peer(d)
                    begin.append(remote(src(pid), dst(me), d - 1, dev))
                    end.append(remote(src(pid), dst(pid), d - 1, dev).wait)
        return begin, middle, end

    def start(self, ins, outs, sems):
        for cp in self._plan(ins, outs, sems)[0]:
            cp.start()

    def pass_on(self, ins, outs, sems):
        for arrived, onward in self._plan(ins, outs, sems)[1]:
            arrived.wait_recv()
            onward.start()

    def wait(self, ins, outs, sems):
        for wait in self._plan(ins, outs, sems)[2]:
            wait()


def _call(body, *, name, grid, in_specs, out_specs, out_shape, args, scratch_shapes=(), sem=None, comm=None):
    n_in, n_out, n_scr = len(in_specs), len(out_specs), len(scratch_shapes)
    if comm is None:
        return pl.pallas_call(
            body, name=name, grid=grid, out_shape=list(out_shape), in_specs=list(in_specs), out_specs=list(out_specs),
            scratch_shapes=list(scratch_shapes), compiler_params=_params(*sem))(*args)
    hbm = pl.BlockSpec(memory_space=pltpu.HBM)

    def carrier(*refs):
        bounds = [0, n_in, n_in + comm.n, n_in + comm.n + n_out, n_in + 2 * comm.n + n_out, len(refs) - 3, len(refs)]
        ins, cin, outs, cout, scr, sems = [refs[lo:hi] for lo, hi in zip(bounds[:-1], bounds[1:])]
        assert len(scr) == n_scr
        step = functools.reduce(lambda lin, ax: lin * grid[ax] + pl.program_id(ax), range(len(grid)), 0)
        steps = math.prod(grid)

        @pl.when(step == 0)
        def _():
            comm.start(cin, cout, sems)

        body(*ins, *outs, *scr)

        @pl.when(step == (2 * steps) // 3)
        def _():
            comm.pass_on(cin, cout, sems)

        @pl.when(step == steps - 1)
        def _():
            comm.wait(cin, cout, sems)

    return pl.pallas_call(
        carrier, name=name, grid=grid, out_shape=list(out_shape) + comm.out_shape,
        in_specs=list(in_specs) + [hbm] * comm.n, out_specs=list(out_specs) + [hbm] * comm.n,
        scratch_shapes=list(scratch_shapes) + comm.scratch,
        compiler_params=_params(*["arbitrary"] * len(grid)))(*args, *comm.arrays)


def _exchange(items, *, name, vmem=False):
    comm = _Exchange(items)
    space = pl.BlockSpec(memory_space=pltpu.VMEM if vmem else pltpu.HBM)

    def body(*refs):
        ins, outs, sems = refs[:comm.n], refs[comm.n:2 * comm.n], refs[2 * comm.n:]
        comm.start(ins, outs, sems)
        comm.pass_on(ins, outs, sems)
        comm.wait(ins, outs, sems)

    return pl.pallas_call(
        body, name=name, out_shape=comm.out_shape, in_specs=[space] * comm.n, out_specs=[space] * comm.n,
        scratch_shapes=comm.scratch, compiler_params=pltpu.CompilerParams(vmem_limit_bytes=VMEM_LIMIT))(*comm.arrays)


_DIMS = {"nn": (((1,), (0,)), ((), ())), "nt": (((1,), (1,)), ((), ())), "tn": (((0,), (0,)), ((), ()))}


def _matmul(a, b, *, dims, tm, tn, tk, name, epilogue="plain", out_dtype=F32, extras=(), comm=None,
            a_heads=False, b_heads=False):
    a_parts = a.shape[0] if a.ndim == 3 else 0
    b_parts = b.shape[0] if b.ndim == 3 else 0
    assert not (a_parts and dims != "nt" and not a_heads) and not (b_parts and dims != "tn")
    a2 = (a.shape[1], a.shape[2] * a_parts) if a_parts else a.shape
    b2 = (b.shape[1], b.shape[2] * b_parts) if b_parts else b.shape
    if dims == "tn":
        (K, M), N = a2, b2[1]
    else:
        (M, K), N = a2, (b2[1] if dims == "nn" else b2[0])
    tm, tn, tk = min(tm, M), min(tn, N), min(tk, K)
    assert M % tm == 0 and N % tn == 0 and K % tk == 0, (name, M, N, K, tm, tn, tk)
    if a_heads and dims == "tn":
        a_spec = pl.BlockSpec((tm // ATTN_HD, tk, ATTN_HD), lambda i, j, k: (i, k, 0))
    elif a_heads:
        a_spec = pl.BlockSpec((tk // ATTN_HD, tm, ATTN_HD), lambda i, j, k: (k, i, 0))
    elif a_parts:
        per = K // a_parts // tk
        a_spec = pl.BlockSpec((None, tm, tk), lambda i, j, k: (k // per, i, k % per))
    elif dims == "tn":
        a_spec = pl.BlockSpec((tk, tm), lambda i, j, k: (k, i))
    else:
        a_spec = pl.BlockSpec((tm, tk), lambda i, j, k: (i, k))
    if b_heads:
        b_spec = pl.BlockSpec((tn // ATTN_HD, tk, ATTN_HD), lambda i, j, k: (j, k, 0))
    elif b_parts:
        per_n = N // b_parts // tn
        b_spec = pl.BlockSpec((None, tk, tn), lambda i, j, k: (j // per_n, k, j % per_n))
    elif dims == "nt":
        b_spec = pl.BlockSpec((tn, tk), lambda i, j, k: (j, k))
    else:
        b_spec = pl.BlockSpec((tk, tn), lambda i, j, k: (k, j))
    side_by_side = lambda ref: jnp.concatenate([ref[g] for g in range(ref.shape[0])], axis=1)
    nk = K // tk
    tile = pl.BlockSpec((tm, tn), lambda i, j, k: (i, j))
    row = pl.BlockSpec((1, tn), lambda i, j, k: (0, j))
    if epilogue == "plain":
        extra_specs, out_shape, out_specs = [], [jax.ShapeDtypeStruct((M, N), out_dtype)], [tile]
    elif epilogue == "relu2":
        extra_specs, out_shape, out_specs = [], [jax.ShapeDtypeStruct((M, N), BF16)] * 2, [tile, tile]
    elif epilogue == "resgate":
        extra_specs, out_specs = [tile, row], [tile, tile]
        out_shape = [jax.ShapeDtypeStruct((M, N), BF16), jax.ShapeDtypeStruct((M, N), F32)]
    elif epilogue == "mul2a":
        extra_specs, out_shape, out_specs = [tile], [jax.ShapeDtypeStruct((M, N), BF16)], [tile]
    elif epilogue == "heads":
        per_tile = tn // ATTN_HD
        q_tiles = ATTN_HEADS // per_tile
        assert tn == 2 * ATTN_KV * ATTN_HD and N // tn in (q_tiles, q_tiles + 1)
        extra_specs = []
        out_shape = [jax.ShapeDtypeStruct((ATTN_HEADS, M, ATTN_HD), out_dtype)]
        out_specs = [pl.BlockSpec((per_tile, tm, ATTN_HD), lambda i, j, k: (jnp.minimum(j, q_tiles - 1), i, 0))]
        if N // tn > q_tiles:
            out_shape += [jax.ShapeDtypeStruct((ATTN_KV, M, ATTN_HD), out_dtype)] * 2
            out_specs += [pl.BlockSpec((ATTN_KV, tm, ATTN_HD), lambda i, j, k: (0, i, 0))] * 2
    else:
        raise ValueError(epilogue)
    n_extra = len(extra_specs)

    def body(a_ref, b_ref, *rest):
        ex, outs, acc_ref = rest[:n_extra], rest[n_extra:-1], rest[-1]
        k = pl.program_id(2)

        @pl.when(k == 0)
        def _():
            acc_ref[...] = jnp.zeros_like(acc_ref)

        a_tile = side_by_side(a_ref) if a_heads else a_ref[...]
        b_tile = side_by_side(b_ref) if b_heads else b_ref[...]
        acc_ref[...] += lax.dot_general(a_tile, b_tile, _DIMS[dims], preferred_element_type=F32)

        if epilogue == "heads":
            j = pl.program_id(1)

            @pl.when((k == nk - 1) & (j < q_tiles))
            def _():
                for g in range(per_tile):
                    outs[0][g] = acc_ref[:, g * ATTN_HD:(g + 1) * ATTN_HD].astype(out_dtype)

            if len(outs) > 1:
                @pl.when((k == nk - 1) & (j == q_tiles))
                def _():
                    for g in range(ATTN_KV):
                        outs[1][g] = acc_ref[:, g * ATTN_HD:(g + 1) * ATTN_HD].astype(out_dtype)
                        outs[2][g] = acc_ref[:, (ATTN_KV + g) * ATTN_HD:(ATTN_KV + g + 1) * ATTN_HD].astype(out_dtype)
            return

        @pl.when(k == nk - 1)
        def _():
            acc = acc_ref[...]
            if epilogue == "plain":
                outs[0][...] = acc.astype(out_dtype)
            elif epilogue == "relu2":
                act = jnp.maximum(acc, 0.0)
                outs[0][...] = act.astype(BF16)
                outs[1][...] = (act * act).astype(BF16)
            elif epilogue == "resgate":
                outs[0][...] = acc.astype(BF16)
                outs[1][...] = ex[0][...] + ex[1][...] * acc
            else:
                outs[0][...] = (acc * (2.0 * ex[0][...].astype(F32))).astype(BF16)

    res = _call(body, name=name, grid=(M // tm, N // tn, nk), out_shape=out_shape,
                in_specs=[a_spec, b_spec] + extra_specs, out_specs=out_specs,
                scratch_shapes=[pltpu.VMEM((tm, tn), F32)],
                sem=("parallel", "arbitrary" if epilogue == "heads" else "parallel", "arbitrary"),
                args=(a, b, *extras), comm=comm)
    return res[0] if len(res) == 1 else res


def _row_tile(T):
    return min(T, 256)


def _norm_mod_fwd(x, gain, sc, sh, name, comm=None):
    T, D = x.shape
    tr = min(T, 2 * _row_tile(T))

    def body(x_ref, g_ref, sc_ref, sh_ref, h_ref):
        xv = x_ref[...]
        r = lax.rsqrt(jnp.mean(xv * xv, axis=-1, keepdims=True) + EPS)
        hn = (xv * r) * g_ref[...]
        h_ref[...] = (hn * (1.0 + sc_ref[...]) + sh_ref[...]).astype(BF16)

    vec = pl.BlockSpec((1, D), lambda i: (0, 0))
    res = _call(body, name=name, grid=(T // tr,), out_shape=[jax.ShapeDtypeStruct((T, D), BF16)],
                in_specs=[pl.BlockSpec((tr, D), lambda i: (i, 0)), vec, vec, vec],
                out_specs=[pl.BlockSpec((tr, D), lambda i: (i, 0))], sem=("parallel",), args=(x, gain, sc, sh),
                comm=comm)
    return res[0] if comm is None else res


def _through_gate(dx, branch_ref, gate_ref, dbranch_ref, st_ref, row):
    dbranch_ref[...] = (dx * gate_ref[...]).astype(BF16)
    st_ref[row:row + 1, :] += jnp.sum(dx * branch_ref[...].astype(F32), axis=0, keepdims=True)


def _norm_mod_bwd(x, dh, dres, gain, sc, name, below=None):
    T, D = x.shape
    tr = _row_tile(T)

    def body(x_ref, dh_ref, dres_ref, g_ref, sc_ref, *rest):
        dx_ref, st_ref = rest[-3:-1] if below else rest[-2:]
        xv, dh_v, gain_v = x_ref[...], dh_ref[...].astype(F32), g_ref[...]
        r = lax.rsqrt(jnp.mean(xv * xv, axis=-1, keepdims=True) + EPS)
        xn = xv * r
        hn = xn * gain_v
        dhn = dh_v * (1.0 + sc_ref[...])
        dxn = dhn * gain_v
        dx = dres_ref[...] + r * (dxn - xn * jnp.mean(dxn * xn, axis=-1, keepdims=True))
        dx_ref[...] = dx

        @pl.when(pl.program_id(0) == 0)
        def _():
            st_ref[...] = jnp.zeros_like(st_ref)

        st_ref[0:1, :] += jnp.sum(dh_v, axis=0, keepdims=True)
        st_ref[1:2, :] += jnp.sum(dh_v * hn, axis=0, keepdims=True)
        st_ref[2:3, :] += jnp.sum(dhn * xn, axis=0, keepdims=True)
        if below:
            _through_gate(dx, rest[0], rest[1], rest[-1], st_ref, 3)

    vec = pl.BlockSpec((1, D), lambda i: (0, 0))
    blk = pl.BlockSpec((tr, D), lambda i: (i, 0))
    return pl.pallas_call(
        body, name=name, grid=(T // tr,),
        out_shape=[jax.ShapeDtypeStruct((T, D), F32), jax.ShapeDtypeStruct((8, D), F32)]
        + ([jax.ShapeDtypeStruct((T, D), BF16)] if below else []),
        in_specs=[blk, blk, blk, vec, vec] + ([blk, vec] if below else []),
        out_specs=[blk, pl.BlockSpec((8, D), lambda i: (0, 0))] + ([blk] if below else []),
        compiler_params=_params("arbitrary"),
    )(x, dh, dres, gain, sc, *(below or ()))


def _loss_head(y, target, below, name):
    T, D = y.shape
    tr = _row_tile(T)

    def body(y_ref, t_ref, b_ref, g_ref, dy_ref, l_ref, st_ref, db_ref):
        err = y_ref[...] - t_ref[...]
        dy = err * (1.0 / D)
        dy_ref[...] = dy

        @pl.when(pl.program_id(0) == 0)
        def _():
            l_ref[...] = jnp.zeros_like(l_ref)
            st_ref[...] = jnp.zeros_like(st_ref)

        part = jnp.sum(jnp.mean(err * err, axis=-1, keepdims=True), axis=0, keepdims=True)
        l_ref[...] += jnp.broadcast_to(0.5 * part, l_ref.shape)
        _through_gate(dy, b_ref, g_ref, db_ref, st_ref, 0)

    blk = pl.BlockSpec((tr, D), lambda i: (i, 0))
    return pl.pallas_call(
        body, name=name, grid=(T // tr,),
        out_shape=[jax.ShapeDtypeStruct((T, D), F32), jax.ShapeDtypeStruct((8, 128), F32),
                   jax.ShapeDtypeStruct((8, D), F32), jax.ShapeDtypeStruct((T, D), BF16)],
        in_specs=[blk, blk, blk, pl.BlockSpec((1, D), lambda i: (0, 0))],
        out_specs=[blk, pl.BlockSpec((8, 128), lambda i: (0, 0)), pl.BlockSpec((8, D), lambda i: (0, 0)), blk],
        compiler_params=_params("arbitrary"),
    )(y, target, *below)


def _silu(v):
    return v * jax.nn.sigmoid(v)


def _mod_fwd(c_all, mod_w, mod_b_cols, name):
    L, D, n = mod_w.shape
    tn = 512

    def body(c_ref, w_ref, b_ref, o_ref):
        cond = _silu(c_ref[...]).astype(BF16)
        o_ref[0] = jnp.dot(cond, w_ref[0].astype(BF16), preferred_element_type=F32) + b_ref[0]

    return pl.pallas_call(
        body, name=name, grid=(L, n // tn), out_shape=jax.ShapeDtypeStruct((L, N_DEV, n), F32),
        in_specs=[pl.BlockSpec((N_DEV, D), lambda l, j: (0, 0)), pl.BlockSpec((1, D, tn), lambda l, j: (l, 0, j)),
                  pl.BlockSpec((1, 1, tn), lambda l, j: (l, 0, j))],
        out_specs=pl.BlockSpec((1, N_DEV, tn), lambda l, j: (l, 0, j)),
        compiler_params=_params("parallel", "parallel"),
    )(c_all, mod_w, mod_b_cols)


def _adamw(g, w, m, v):
    m = ADAM_B1 * m + (1.0 - ADAM_B1) * g
    v = ADAM_B2 * v + (1.0 - ADAM_B2) * (g * g)
    m_hat = m / (1.0 - ADAM_B1 ** ADAM_STEP)
    v_hat = v / (1.0 - ADAM_B2 ** ADAM_STEP)
    delta = -ADAM_LR * (m_hat / (jnp.sqrt(v_hat) + ADAM_EPS) + ADAM_WD * w)
    return delta, m, v


def _mod_w_update(c_t, dmod_cols, w, m, v, name):
    L, D, n = w.shape
    tr = 256

    def body(c_ref, dm_ref, w_ref, m_ref, v_ref, g_ref, d_ref, nm_ref, nv_ref):
        cond = _silu(c_ref[...])
        dm = dm_ref[0]
        g = cond[:, 0:1] * dm[0:1, :]
        for b in range(1, N_DEV):
            g = g + cond[:, b:b + 1] * dm[b:b + 1, :]
        delta, nm, nv = _adamw(g, w_ref[0], m_ref[0], v_ref[0])
        g_ref[0], d_ref[0], nm_ref[0], nv_ref[0] = g, delta, nm, nv

    blk = pl.BlockSpec((1, tr, n), lambda l, i: (l, i, 0))
    return pl.pallas_call(
        body, name=name, grid=(L, D // tr), out_shape=[jax.ShapeDtypeStruct(w.shape, F32)] * 4,
        in_specs=[pl.BlockSpec((tr, N_DEV), lambda l, i: (i, 0)), pl.BlockSpec((1, N_DEV, n), lambda l, i: (l, 0, 0)),
                  blk, blk, blk],
        out_specs=[blk] * 4, compiler_params=_params("parallel", "parallel"),
    )(c_t, dmod_cols, w, m, v)


def _sum_adamw(parts, w, m, v, name):
    R, C = w.shape
    tr = min(R, 256 if C >= 1024 else 1024)
    assert R % tr == 0

    def body(p_ref, w_ref, m_ref, v_ref, g_ref, d_ref, nm_ref, nv_ref):
        g = p_ref[0].astype(F32)
        for s in range(1, N_DEV):
            g = g + p_ref[s].astype(F32)
        delta, nm, nv = _adamw(g, w_ref[...], m_ref[...], v_ref[...])
        g_ref[...], d_ref[...], nm_ref[...], nv_ref[...] = g, delta, nm, nv

    blk = pl.BlockSpec((tr, C), lambda i: (i, 0))
    return pl.pallas_call(
        body, name=name, grid=(R // tr,), out_shape=[jax.ShapeDtypeStruct((R, C), F32)] * 4,
        in_specs=[pl.BlockSpec((N_DEV, tr, C), lambda i: (0, i, 0)), blk, blk, blk], out_specs=[blk] * 4,
        compiler_params=_params("parallel"),
    )(parts, w, m, v)


def _sum_adamw_layers(parts, w, m, v, name):
    L, R, C = w.shape
    tr = min(R, 256 * 1024 // C)
    assert R % tr == 0 and len(parts) == L
    ni = R // tr

    def body(*refs):
        p_refs, (w_ref, m_ref, v_ref), outs = refs[:L], refs[L:L + 3], refs[L + 3:]
        for layer in range(L):
            @pl.when(pl.program_id(0) == layer)
            def _(p_ref=p_refs[layer]):
                g = p_ref[0].astype(F32)
                for s in range(1, N_DEV):
                    g = g + p_ref[s].astype(F32)
                delta, nm, nv = _adamw(g, w_ref[...], m_ref[...], v_ref[...])
                for o_ref, val in zip(outs, (g, delta, nm, nv)):
                    o_ref[...] = val

    def shares(layer):
        park = 0 if layer else ni - 1
        return pl.BlockSpec((N_DEV, tr, C), lambda l, i: (0, jnp.where(l == layer, i, park), 0))

    blk = pl.BlockSpec((None, tr, C), lambda l, i: (l, i, 0))
    return pl.pallas_call(
        body, name=name, grid=(L, ni), out_shape=[jax.ShapeDtypeStruct((L, R, C), F32)] * 4,
        in_specs=[shares(layer) for layer in range(L)] + [blk] * 3, out_specs=[blk] * 4,
        compiler_params=_params("arbitrary", "arbitrary"),
    )(*parts, w, m, v)


def _lower_bound_row1(l0, l1):
    mx = lax.stop_gradient(jnp.maximum(l0, l1))
    e0, e1 = jnp.exp(l0 - mx), jnp.exp(l1 - mx)
    p0, p1 = e0 / (e0 + e1), e1 / (e0 + e1)
    return (p0 + p1) - p0


def _lb_fwd(logits, name):
    def body(l_ref, o_ref):
        o_ref[...] = _lower_bound_row1(l_ref[0:1, :], l_ref[1:2, :])

    return pl.pallas_call(body, name=name, out_shape=jax.ShapeDtypeStruct((1, logits.shape[1]), F32))(logits)


def _lb_bwd(logits, dlb, name):
    def body(l_ref, d_ref, o_ref):
        _, vjp = jax.vjp(_lower_bound_row1, l_ref[0:1, :], l_ref[1:2, :])
        d0, d1 = vjp(d_ref[...])
        o_ref[0:1, :] = d0
        o_ref[1:2, :] = d1

    return pl.pallas_call(body, name=name, out_shape=jax.ShapeDtypeStruct(logits.shape, F32))(logits, dlb)


def _bdot(a, b, dims):
    return lax.dot_general(a.astype(BF16), b.astype(BF16), _DIMS[dims], preferred_element_type=F32)


def _rms(x, gain):
    r = lax.rsqrt(jnp.mean(x * x, axis=-1, keepdims=True) + EPS)
    xhat = x * r
    return xhat, r, xhat * gain


def _attn_band(n):
    qi = lax.broadcasted_iota(jnp.int32, (ATTN_BLOCK, 2 * ATTN_BLOCK), 0)
    ki = lax.broadcasted_iota(jnp.int32, (ATTN_BLOCK, 2 * ATTN_BLOCK), 1)
    dist = qi + ATTN_BLOCK - ki
    first_key = jnp.where(n > 0, 0, ATTN_BLOCK)
    valid = (dist >= 0) & (dist < ATTN_BLOCK) & (ki >= first_key)
    return valid, jnp.abs(dist).astype(F32)


def _attn_head_probs(qn, kn_b, valid, absdist, slope, sink):
    s = lax.dot_general(qn.astype(BF16), kn_b, _DIMS["nt"], preferred_element_type=F32) * ATTN_SCALE
    s = jnp.where(valid, s - slope * absdist, NEG_BIG)
    mx = jnp.maximum(jnp.max(s, axis=-1, keepdims=True), sink)
    e = jnp.exp(s - mx)
    es = jnp.exp(sink - mx)
    inv = 1.0 / (jnp.sum(e, axis=-1, keepdims=True) + es)
    return e * inv, es * inv


def _attn_fwd(qh, kh, vh, qg, kg, slopes, sinks, name, comm=None):
    T = qh.shape[1]
    nb = T // ATTN_BLOCK

    def body(q_ref, kp_ref, kc_ref, vp_ref, vc_ref, qg_ref, kg_ref, sl_ref, sk_ref, o_ref):
        n = pl.program_id(0)
        valid, absdist = _attn_band(n)
        for kv in range(ATTN_KV):
            _, _, kn = _rms(jnp.concatenate([kp_ref[kv], kc_ref[kv]], axis=0), kg_ref[...])
            kn_b = kn.astype(BF16)
            v_b = jnp.concatenate([vp_ref[kv], vc_ref[kv]], axis=0).astype(BF16)
            for g in range(ATTN_GROUP):
                head = kv * ATTN_GROUP + g
                _, _, qn = _rms(q_ref[head], qg_ref[...])
                p, _ = _attn_head_probs(qn, kn_b, valid, absdist, sl_ref[head], sk_ref[head])
                o_ref[head] = jnp.dot(p.astype(BF16), v_b, preferred_element_type=F32).astype(BF16)

    qspec = pl.BlockSpec((ATTN_HEADS, ATTN_BLOCK, ATTN_HD), lambda n: (0, n, 0))
    prev = pl.BlockSpec((ATTN_KV, ATTN_BLOCK, ATTN_HD), lambda n: (0, jnp.maximum(n - 1, 0), 0))
    cur = pl.BlockSpec((ATTN_KV, ATTN_BLOCK, ATTN_HD), lambda n: (0, n, 0))
    gain = pl.BlockSpec((1, ATTN_HD), lambda n: (0, 0))
    scalars = pl.BlockSpec(memory_space=pltpu.SMEM)
    return _call(body, name=name, grid=(nb,), out_shape=[jax.ShapeDtypeStruct(qh.shape, BF16)],
                 in_specs=[qspec, prev, cur, prev, cur, gain, gain, scalars, scalars], out_specs=[qspec],
                 sem=("parallel",), args=(qh, kh, kh, vh, vh, qg, kg, slopes, sinks), comm=comm)


def _rms_bwd(dy, xhat, r, gain):
    dxh = dy * gain
    dx = r * (dxh - xhat * jnp.mean(dxh * xhat, axis=-1, keepdims=True))
    return dx, jnp.sum(dy * xhat, axis=0, keepdims=True)


def _attn_bwd(qh, kh, vh, doh, qg, kg, slope_col, sink_col, name, comm=None):
    T = qh.shape[1]
    nb = T // ATTN_BLOCK
    rows = ATTN_GROUP * ATTN_BLOCK

    def body(q_ref, kp_ref, kc_ref, vp_ref, vc_ref, do_ref, qg_ref, kg_ref, sl_ref, sk_ref,
             dq_ref, dk_ref, dv_ref, dqg_ref, dkg_ref, dsk_ref, carry_ref, sk_acc):
        step = pl.program_id(0)
        n = nb - 1 - step
        qg_v, kg_v = qg_ref[...], kg_ref[...]

        @pl.when(step == 0)
        def _():
            dqg_ref[...] = jnp.zeros_like(dqg_ref)
            dkg_ref[...] = jnp.zeros_like(dkg_ref)
            carry_ref[...] = jnp.zeros_like(carry_ref)
            sk_acc[...] = jnp.zeros_like(sk_acc)

        half_heads = ATTN_GROUP // 2
        half_rows = rows // 2
        for kv in range(ATTN_KV):
            khat, rk, kn = _rms(jnp.concatenate([kp_ref[kv], kc_ref[kv]], axis=0), kg_v)
            v = jnp.concatenate([vp_ref[kv], vc_ref[kv]], axis=0)
            dkn = jnp.zeros((2 * ATTN_BLOCK, ATTN_HD), F32)
            dv = jnp.zeros((2 * ATTN_BLOCK, ATTN_HD), F32)
            for half in range(2):
                heads = slice(kv * ATTN_GROUP + half * half_heads, kv * ATTN_GROUP + (half + 1) * half_heads)
                part = slice(half * half_rows, (half + 1) * half_rows)
                qhat, rq, qn = _rms(q_ref[heads].reshape(half_rows, ATTN_HD), qg_v)
                s = _bdot(qn, kn, "nt") * ATTN_SCALE
                qi = lax.broadcasted_iota(jnp.int32, s.shape, 0) & (ATTN_BLOCK - 1)
                ki = lax.broadcasted_iota(jnp.int32, s.shape, 1)
                dist = qi + ATTN_BLOCK - ki
                first_key = jnp.where(n > 0, 0, ATTN_BLOCK)
                valid = (dist >= 0) & (dist < ATTN_BLOCK) & (ki >= first_key)
                s = jnp.where(valid, s - sl_ref[kv, part, :] * jnp.abs(dist).astype(F32), NEG_BIG)
                sink = sk_ref[kv, part, :]
                mx = jnp.maximum(jnp.max(s, axis=-1, keepdims=True), sink)
                e = jnp.exp(s - mx)
                es = jnp.exp(sink - mx)
                den = jnp.sum(e, axis=-1, keepdims=True) + es
                p, ps = e / den, es / den
                do = do_ref[heads].reshape(half_rows, ATTN_HD)
                dp = _bdot(do, v, "nt")
                delta = jnp.sum(p * dp, axis=-1, keepdims=True)
                ds = p * (dp - delta)
                dqn = _bdot(ds, kn, "nn") * ATTN_SCALE
                dkn = dkn + _bdot(ds, qn, "tn")
                dv = dv + _bdot(p, do, "tn")
                dq, dqg = _rms_bwd(dqn, qhat, rq, qg_v)
                dq_ref[heads] = dq.reshape(half_heads, ATTN_BLOCK, ATTN_HD).astype(BF16)
                dqg_ref[0:1, :] += dqg
                sk_acc[kv, part, :] += -ps * delta
            dk, dkg = _rms_bwd(dkn * ATTN_SCALE, khat, rk, kg_v)
            dkg_ref[0:1, :] += dkg
            dk_ref[kv] = (dk[ATTN_BLOCK:, :] + carry_ref[kv, 0]).astype(BF16)
            dv_ref[kv] = (dv[ATTN_BLOCK:, :] + carry_ref[kv, 1]).astype(BF16)
            carry_ref[kv, 0] = dk[:ATTN_BLOCK, :]
            carry_ref[kv, 1] = dv[:ATTN_BLOCK, :]

        @pl.when(step == nb - 1)
        def _():
            for head in range(ATTN_HEADS):
                kv, g = divmod(head, ATTN_GROUP)
                tot = jnp.sum(sk_acc[kv, g * ATTN_BLOCK:(g + 1) * ATTN_BLOCK, :], axis=0, keepdims=True)
                dsk_ref[head:head + 1, :] = jnp.broadcast_to(tot, (1, 128))

    qspec = pl.BlockSpec((ATTN_HEADS, ATTN_BLOCK, ATTN_HD), lambda s: (0, nb - 1 - s, 0))
    prev = pl.BlockSpec((ATTN_KV, ATTN_BLOCK, ATTN_HD), lambda s: (0, jnp.maximum(nb - 2 - s, 0), 0))
    cur = pl.BlockSpec((ATTN_KV, ATTN_BLOCK, ATTN_HD), lambda s: (0, nb - 1 - s, 0))
    gain = pl.BlockSpec((1, ATTN_HD), lambda s: (0, 0))
    col = pl.BlockSpec((ATTN_KV, rows, 1), lambda s: (0, 0, 0))
    acc = pl.BlockSpec((8, ATTN_HD), lambda s: (0, 0))
    return _call(
        body, name=name, grid=(nb,),
        out_shape=[jax.ShapeDtypeStruct(qh.shape, BF16), jax.ShapeDtypeStruct(kh.shape, BF16),
                   jax.ShapeDtypeStruct(kh.shape, BF16), jax.ShapeDtypeStruct((8, ATTN_HD), F32),
                   jax.ShapeDtypeStruct((8, ATTN_HD), F32), jax.ShapeDtypeStruct((ATTN_HEADS, 128), F32)],
        in_specs=[qspec, prev, cur, prev, cur, qspec, gain, gain, col, col],
        out_specs=[qspec, cur, cur, acc, acc, pl.BlockSpec((ATTN_HEADS, 128), lambda s: (0, 0))],
        scratch_shapes=[pltpu.VMEM((ATTN_KV, 2, ATTN_BLOCK, ATTN_HD), F32), pltpu.VMEM((ATTN_KV, rows, 1), F32)],
        sem=("arbitrary",), args=(qh, kh, kh, vh, vh, doh, qg, kg, slope_col, sink_col), comm=comm)


@functools.partial(jax.custom_vjp, nondiff_argnums=(2,))
def _mm(a, b, dims):
    return _bdot(a, b, dims)


def _mm_fwd(a, b, dims):
    return _bdot(a, b, dims), (a, b)


def _mm_bwd(dims, res, ct):
    a, b = res
    if dims == "nn":
        return _bdot(ct, b, "nt"), _bdot(a, ct, "tn")
    if dims == "nt":
        return _bdot(ct, b, "nn"), _bdot(ct, a, "tn")
    return _bdot(b, ct, "nt"), _bdot(a, ct, "nn")


_mm.defvjp(_mm_fwd, _mm_bwd)


def _same_chunk_mask(rows, upper):
    ri = lax.broadcasted_iota(jnp.int32, (rows, rows), 0)
    ci = lax.broadcasted_iota(jnp.int32, (rows, rows), 1)
    same = (ri >> HGRN_CHUNK_SHIFT) == (ci >> HGRN_CHUNK_SHIFT)
    return same & ((ri <= ci) if upper else (ri >= ci))


@jax.custom_vjp
def _chunk_cumsum(x, tri, tri_t):
    hi = x.astype(BF16)
    rest = x - hi.astype(F32)
    mid = rest.astype(BF16)
    lo = (rest - mid.astype(F32)).astype(BF16)
    out = jnp.dot(tri, jnp.concatenate([hi, mid, lo], axis=1), preferred_element_type=F32)
    w = x.shape[1]
    return out[:, :w] + out[:, w:2 * w] + out[:, 2 * w:]


def _chunk_cumsum_fwd(x, tri, tri_t):
    return _chunk_cumsum(x, tri, tri_t), (tri, tri_t)


def _chunk_cumsum_bwd(res, ct):
    tri, tri_t = res
    return _chunk_cumsum(ct, tri_t, tri), jnp.zeros_like(tri), jnp.zeros_like(tri_t)


_chunk_cumsum.defvjp(_chunk_cumsum_fwd, _chunk_cumsum_bwd)


def _hgrn_masks(rows):
    nc = rows // HGRN_CHUNK
    lower = _same_chunk_mask(rows, False)
    chunk_of_row = lax.broadcasted_iota(jnp.int32, (rows, HGRN_DK), 0) >> HGRN_CHUNK_SHIFT
    row_in_chunk = lax.broadcasted_iota(jnp.int32, (nc, HGRN_CHUNK, HGRN_DK), 1)
    return dict(lower=lower, tri=lower.astype(BF16), tri_t=_same_chunk_mask(rows, True).astype(BF16),
                in_chunk=[chunk_of_row == c for c in range(nc)],
                mid_row=row_in_chunk == HGRN_CHUNK // 2 - 1, last_row=row_in_chunk == HGRN_CHUNK - 1)


def _hgrn_block(masks, st, qr, fr, v, gr, lb, og):
    rows = qr.shape[0]
    nc = rows // HGRN_CHUNK
    per_chunk = lambda m: m.reshape(nc, HGRN_CHUNK, HGRN_DK)
    flat = lambda m: m.reshape(rows, HGRN_DK)
    by_chunk = lambda m: jnp.concatenate([jnp.where(masks["in_chunk"][c], m, 0.0) for c in range(nc)], axis=1)

    forget = lb + (1.0 - lb) * jax.nn.sigmoid(fr)
    k = 1.0 - forget
    b = _chunk_cumsum(jnp.log(forget), masks["tri"], masks["tri_t"])
    b3 = per_chunk(b)
    piv = jnp.sum(jnp.where(masks["mid_row"], b3, 0.0), axis=1, keepdims=True)
    b_last = jnp.sum(jnp.where(masks["last_row"], b3, 0.0), axis=1, keepdims=True)
    q = _silu(qr) * HGRN_SCALE
    a = _mm(q * flat(jnp.exp(b3 - piv)), k * flat(jnp.exp(piv - b3)), "nt")
    o = _mm(jnp.where(masks["lower"], a, 0.0), v, "nn")
    updates = _mm(v, by_chunk(k * flat(jnp.exp(b_last - b3))), "tn")
    decay = jnp.exp(b_last)
    before = []
    for c in range(nc):
        before.append(st)
        st = st * decay[c] + updates[:, c * HGRN_DK:(c + 1) * HGRN_DK]
    o = o + _mm(by_chunk(q * jnp.exp(b)), jnp.concatenate(before, axis=1), "nt")
    y = (o * lax.rsqrt(jnp.mean(o * o, axis=-1, keepdims=True) + EPS)) * og * _silu(gr)
    return y, st


def _hgrn_tile(T):
    return min(T, 256)


HGRN_HEADS_PER_STEP = 16
HGRN_GROUPS = HGRN_HEADS // HGRN_HEADS_PER_STEP


def _hgrn_fwd(proj, lb, og, name, comm=None):
    T = proj.shape[0]
    tb = _hgrn_tile(T)
    hp, wide = HGRN_HEADS_PER_STEP, HGRN_HEADS_PER_STEP * HGRN_DK

    def body(q_ref, f_ref, v_ref, g_ref, lb_ref, og_ref, o_ref, s_ref, st_ref):
        @pl.when(pl.program_id(1) == 0)
        def _():
            st_ref[...] = jnp.zeros_like(st_ref)

        masks = _hgrn_masks(tb)
        for j in range(hp):
            ln = slice(j * HGRN_DK, (j + 1) * HGRN_DK)
            st = st_ref[j]
            s_ref[j, 0] = st
            y, st_ref[j] = _hgrn_block(masks, st, q_ref[:, ln], f_ref[:, ln], v_ref[:, ln], g_ref[:, ln], lb_ref[j],
                                       og_ref[j])
            o_ref[:, ln] = y.astype(BF16)

    part = lambda p: pl.BlockSpec((tb, wide), lambda h, t: (t, p * HGRN_GROUPS + h))
    vec = pl.BlockSpec((hp, 1, HGRN_DK), lambda h, t: (h, 0, 0))
    return _call(
        body, name=name, grid=(HGRN_GROUPS, T // tb),
        out_shape=[jax.ShapeDtypeStruct((T, D_MODEL), BF16),
                   jax.ShapeDtypeStruct((HGRN_HEADS, T // tb, HGRN_DK, HGRN_DK), F32)],
        in_specs=[part(0), part(1), part(2), part(3), vec, vec],
        out_specs=[pl.BlockSpec((tb, wide), lambda h, t: (t, h)),
                   pl.BlockSpec((hp, 1, HGRN_DK, HGRN_DK), lambda h, t: (h, t, 0, 0))],
        scratch_shapes=[pltpu.VMEM((hp, HGRN_DK, HGRN_DK), F32)], sem=("parallel", "arbitrary"),
        args=(proj, proj, proj, proj, lb, og), comm=comm)


def _hgrn_bwd(proj, states, do, lb, og, name, comm=None):
    T = proj.shape[0]
    tb = _hgrn_tile(T)
    nt, hp, wide = T // tb, HGRN_HEADS_PER_STEP, HGRN_HEADS_PER_STEP * HGRN_DK

    def body(q_ref, f_ref, v_ref, g_ref, s_ref, do_ref, lb_ref, og_ref, dp_ref, dlb_ref, dog_ref, dst_ref):
        @pl.when(pl.program_id(1) == 0)
        def _():
            dst_ref[...] = jnp.zeros_like(dst_ref)
            dlb_ref[...] = jnp.zeros_like(dlb_ref)
            dog_ref[...] = jnp.zeros_like(dog_ref)

        block = functools.partial(_hgrn_block, _hgrn_masks(tb))
        for j in range(hp):
            ln = slice(j * HGRN_DK, (j + 1) * HGRN_DK)
            _, vjp = jax.vjp(block, s_ref[j, 0], q_ref[:, ln], f_ref[:, ln], v_ref[:, ln], g_ref[:, ln],
                             lb_ref[j], og_ref[j])
            dst_ref[j], dq, df, dv, dg, dlb, dog = vjp((do_ref[:, ln], dst_ref[j]))
            for p, part_grad in enumerate((dq, df, dv, dg)):
                dp_ref[p, :, ln] = part_grad.astype(BF16)
            dlb_ref[j] += dlb
            dog_ref[j] += dog

    part = lambda p: pl.BlockSpec((tb, wide), lambda h, t: (nt - 1 - t, p * HGRN_GROUPS + h))
    vec = pl.BlockSpec((hp, 1, HGRN_DK), lambda h, t: (h, 0, 0))
    head = pl.BlockSpec((tb, wide), lambda h, t: (nt - 1 - t, h))
    return _call(
        body, name=name, grid=(HGRN_GROUPS, nt),
        out_shape=[jax.ShapeDtypeStruct((4, T, D_MODEL), BF16)] + [jax.ShapeDtypeStruct((HGRN_HEADS, 1, HGRN_DK), F32)] * 2,
        in_specs=[part(0), part(1), part(2), part(3),
                  pl.BlockSpec((hp, 1, HGRN_DK, HGRN_DK), lambda h, t: (h, nt - 1 - t, 0, 0)), head, vec, vec],
        out_specs=[pl.BlockSpec((4, tb, wide), lambda h, t: (0, nt - 1 - t, h)), vec, vec],
        scratch_shapes=[pltpu.VMEM((hp, HGRN_DK, HGRN_DK), F32)], sem=("parallel", "arbitrary"),
        args=(proj, proj, proj, proj, states, do, lb, og), comm=comm)


def _cols_to_blocks(g, n8):
    K = g.shape[0]
    return g.reshape(K, N_DEV, n8).transpose(1, 0, 2)


def _blocks_to_cols(wg):
    _, K, n8 = wg.shape
    return wg.transpose(1, 0, 2).reshape(K, N_DEV * n8)


def _pack(parts):
    flat = []
    for p in parts:
        v = p.reshape(-1)
        flat.append(jnp.pad(v, (0, (-v.shape[0]) % 1024)))
    return jnp.concatenate(flat).reshape(-1, 128)


def _unpack(packed, like):
    flat, out, off = packed.reshape(-1), [], 0
    for p in like:
        size = math.prod(p.shape)
        out.append(flat[off:off + size].reshape(p.shape))
        off += size + (-size) % 1024
    return out


def _heads_major(a, heads):
    T = a.shape[0]
    return a.reshape(T, heads, ATTN_HD).transpose(1, 0, 2)


def _heads_minor(a):
    heads, T, _ = a.shape
    return a.transpose(1, 0, 2).reshape(T, heads * ATTN_HD)


def kernel(x, c, mod_w, mod_b, norm_mix, norm_mlp, attn_w_in, attn_w_out, attn_q_gain, attn_k_gain, attn_sinks, hgrn_w_in, hgrn_w_out, hgrn_o_gain, hgrn_lb_logits, mlp_w1, mlp_w2, loss_target, m_mod_w, m_mod_b, m_norm_mix, m_norm_mlp, m_attn_w_in, m_attn_w_out, m_attn_q_gain, m_attn_k_gain, m_attn_sinks, m_hgrn_w_in, m_hgrn_w_out, m_hgrn_o_gain, m_hgrn_lb_logits, m_mlp_w1, m_mlp_w2, v_mod_w, v_mod_b, v_norm_mix, v_norm_mlp, v_attn_w_in, v_attn_w_out, v_attn_q_gain, v_attn_k_gain, v_attn_sinks, v_hgrn_w_in, v_hgrn_w_out, v_hgrn_o_gain, v_hgrn_lb_logits, v_mlp_w1, v_mlp_w2):
    T = x.shape[1]
    me = 4 * lax.axis_index("x") + 2 * lax.axis_index("y") + lax.axis_index("c")
    x0, target = x[0], loss_target[0]
    n_mod = mod_w.shape[2]

    shards = [attn_w_in[0], attn_w_out[0], hgrn_w_in[0], hgrn_w_out[0], mlp_w1[0], mlp_w1[1], mlp_w2[0], mlp_w2[1]]
    sb = [s.astype(BF16) for s in shards]
    gather = lambda *items: _Exchange([("gather_by_chip", a, axis) for a, axis in items])

    c_all = _exchange([("gather", c.reshape(16, 128), None)], vmem=True, name="gather_c")[0].reshape(N_DEV, D_MODEL)
    mod_b_cols = lax.dynamic_slice_in_dim(mod_b, me * n_mod, n_mod, axis=1).reshape(2, 1, n_mod)
    mod_cols = _mod_fwd(c_all, mod_w, mod_b_cols, "mod_fwd")
    mod_all = _exchange([("gather", mod_cols.reshape(-1, 128), None)], vmem=True, name="gather_mod")[0]
    mod_all = mod_all.reshape(N_DEV, 2, N_DEV, n_mod)
    mod_mine = lax.dynamic_index_in_dim(mod_all, me, axis=2, keepdims=False)
    mod_mine = mod_mine.transpose(1, 0, 2).reshape(2, N_MOD, 1, D_MODEL)

    lb = _lb_fwd(hgrn_lb_logits, "lb_fwd").reshape(HGRN_HEADS, 1, HGRN_DK)
    og = hgrn_o_gain.reshape(HGRN_HEADS, 1, HGRN_DK)
    slopes = jnp.exp2(-8.0 * jnp.arange(1, ATTN_HEADS + 1, dtype=F32) / ATTN_HEADS)
    sinks = attn_sinks[0]
    per_row = lambda vals: jnp.repeat(vals.reshape(ATTN_KV, ATTN_GROUP), ATTN_BLOCK, axis=1).reshape(
        ATTN_KV, ATTN_GROUP * ATTN_BLOCK, 1)
    slope_col, sink_col = per_row(slopes), per_row(sinks)

    saved = []
    xi = x0
    w_mlp1, w_mlp2 = [None, None], [None, None]
    for i in range(2):
        sh1, sc1, g1, sh2, sc2, g2 = [mod_mine[i, j] for j in range(N_MOD)]
        if i == 0:
            h, w_attn_in = _norm_mod_fwd(xi, norm_mix[i:i + 1], sc1, sh1, "norm_mix_fwd0", comm=gather((sb[0], None)))
            w_attn_in = _blocks_to_cols(w_attn_in)
            qh, kh, vh, w_attn_out = _matmul(h, w_attn_in, dims="nn", tm=1024, tn=512, tk=2048, name="attn_in_fwd",
                                             epilogue="heads", comm=gather((sb[1], 0)))
            o, w_mlp1[0] = _attn_fwd(qh, kh, vh, attn_q_gain, attn_k_gain, slopes, sinks, "attn_fwd",
                                     comm=gather((sb[4], 1)))
            mix = (qh, kh, vh)
            w_out = w_attn_out
        else:
            h = _norm_mod_fwd(xi, norm_mix[i:i + 1], sc1, sh1, "norm_mix_fwd1")
            proj, w_hgrn_out = _matmul(h, w_hgrn_in, dims="nn", tm=1024, tn=2048, tk=2048, name="hgrn_in_fwd",
                                       comm=gather((sb[3], 0)))
            o, states, w_mlp1[1] = _hgrn_fwd(proj, lb, og, "hgrn_fwd", comm=gather((sb[5], 1)))
            mix = (proj, states)
            w_out = w_hgrn_out
        y, x1 = _matmul(o, w_out, dims="nn", tm=1024, tn=1024, tk=2048, name=f"mix_out_fwd{i}", epilogue="resgate",
                        extras=(xi, g1), a_heads=i == 0)
        h2 = _norm_mod_fwd(x1, norm_mlp[i:i + 1], sc2, sh2, f"norm_mlp_fwd{i}")
        if i == 0:
            act, act2, w_mlp2[0] = _matmul(h2, w_mlp1[0], dims="nn", tm=1024, tn=2048, tk=2048, name="mlp1_fwd0",
                                           epilogue="relu2", comm=gather((sb[6], 0)))
            z, x2, w_hgrn_in = _matmul(act2, w_mlp2[0], dims="nn", tm=1024, tn=1024, tk=2048, name="mlp2_fwd0",
                                       epilogue="resgate", extras=(x1, g2), comm=gather((sb[2], 1)))
        else:
            act, act2, w_mlp2[1] = _matmul(h2, w_mlp1[1], dims="nn", tm=1024, tn=2048, tk=2048, name="mlp1_fwd1",
                                           epilogue="relu2", comm=gather((sb[7], 0)))
            z, x2 = _matmul(act2, w_mlp2[1], dims="nn", tm=1024, tn=1024, tk=2048, name="mlp2_fwd1", epilogue="resgate",
                            extras=(x1, g2))
        saved.append((xi, h, o, y, x1, h2, act, act2, z, mix))
        xi = x2

    dx, loss_tile, st_g2, dz = _loss_head(xi, target, (saved[1][8], mod_mine[1, 5]), "loss_head")
    loss = lax.psum(loss_tile[0, 0], ("x", "y", "c"))

    scatter = lambda *items: _Exchange([("scatter", a, axis) for a, axis in items])
    shares, dmods, dnorm_mix, dnorm_mlp = {}, [None, None], [None, None], [None, None]
    wgrad = lambda a, b, name, tn=1024: _matmul(a, b, dims="tn", tm=2048, tn=tn, tk=2048, name=name, out_dtype=BF16)
    for i in (1, 0):
        sh1, sc1, g1, sh2, sc2, g2 = [mod_mine[i, j] for j in range(N_MOD)]
        xin, h, o, y, x1, h2, act, act2, z, mix = saved[i]
        if i == 1:
            dpre = _matmul(dz, w_mlp2[1], dims="nt", tm=1024, tn=2048, tk=2048, name="mlp2_bwd1", epilogue="mul2a",
                           extras=(act,))
        else:
            dpre, shares["hgrn_w_in"] = _matmul(dz, w_mlp2[0], dims="nt", tm=1024, tn=2048, tk=2048, name="mlp2_bwd0",
                                                epilogue="mul2a", extras=(act,), comm=scatter((g_hgrn_in, 1)))
        g_mlp2 = wgrad(act2, dz, f"mlp2_wgrad{i}")
        if i == 1:
            dh2 = _matmul(dpre, w_mlp1[1], dims="nt", tm=1024, tn=1024, tk=4096, name="mlp1_bwd1", out_dtype=BF16)
        else:
            dh2, shares["mlp_w2_0"] = _matmul(dpre, w_mlp1[0], dims="nt", tm=1024, tn=1024, tk=4096, name="mlp1_bwd0",
                                              out_dtype=BF16, comm=scatter((g_mlp2, 0)))
        g_mlp1 = wgrad(h2, dpre, f"mlp1_wgrad{i}")
        dx1, st_mlp, dy = _norm_mod_bwd(x1, dh2, dx, norm_mlp[i:i + 1], sc2, f"norm_mlp_bwd{i}", below=(y, g1))
        w_out = w_attn_out if i == 0 else w_hgrn_out
        if i == 0:
            qh, kh, vh = mix
            doh = _matmul(dy, w_out, dims="nt", tm=1024, tn=512, tk=2048, name="mix_out_bwd0", epilogue="heads",
                          out_dtype=BF16)
            g_out = _matmul(o, dy, dims="tn", tm=1024, tn=1024, tk=2048, name="mix_out_wgrad0", out_dtype=BF16,
                            a_heads=True)
            dqh, dkh, dvh, dqg, dkg, dsk, shares["mlp_w1_0"] = _attn_bwd(
                qh, kh, vh, doh, attn_q_gain, attn_k_gain, slope_col, sink_col, "attn_bwd",
                comm=scatter((g_mlp1, 1)))
            dproj = jnp.concatenate([dqh, dkh, dvh], axis=0)
            d_q_gain, d_k_gain, d_sinks = dqg[0:1], dkg[0:1], dsk[:, 0].reshape(1, ATTN_HEADS)
            g_attn_in, shares["attn_w_out"] = _matmul(h, dproj, dims="tn", tm=2048, tn=640, tk=2048, name="mix_in_wgrad0",
                                                      out_dtype=BF16, b_heads=True, comm=scatter((g_out, 0)))
            g_attn_in = _cols_to_blocks(g_attn_in, attn_w_in.shape[2])
            dh, shares["attn_w_in"] = _matmul(dproj, w_attn_in, dims="nt", tm=1024, tn=1024, tk=2560, out_dtype=BF16,
                                              name="mix_in_bwd0", comm=scatter((g_attn_in, None)), a_heads=True)
        else:
            do = _matmul(dy, w_out, dims="nt", tm=1024, tn=1024, tk=2048, name="mix_out_bwd1")
            g_out = wgrad(o, dy, "mix_out_wgrad1")
            proj, states = mix
            dproj, dlb, d_o_gain, shares["mlp_w2_1"], shares["mlp_w1_1"], shares["hgrn_w_out"] = _hgrn_bwd(
                proj, states, do, lb, og, "hgrn_bwd", comm=scatter((g_mlp2, 0), (g_mlp1, 1), (g_out, 0)))
            d_lb_logits = _lb_bwd(hgrn_lb_logits, dlb.reshape(1, D_MODEL), "lb_bwd")
            dh = _matmul(dproj, w_hgrn_in, dims="nt", tm=1024, tn=2048, tk=2048, name="mix_in_bwd1", out_dtype=BF16)
            g_hgrn_in = wgrad(h, dproj, "mix_in_wgrad1")
        d_gate2 = st_g2[0:1] if i == 1 else st_mix_above[3:4]
        if i == 1:
            dx, st_mix, dz = _norm_mod_bwd(xin, dh, dx1, norm_mix[i:i + 1], sc1, "norm_mix_bwd1",
                                           below=(saved[0][8], mod_mine[0, 5]))
            st_mix_above = st_mix
        else:
            dx, st_mix = _norm_mod_bwd(xin, dh, dx1, norm_mix[i:i + 1], sc1, "norm_mix_bwd0")
        dmods[i] = jnp.concatenate([st_mix[0:1], st_mix[1:2], st_mlp[3:4], st_mlp[0:1], st_mlp[1:2], d_gate2], axis=1)
        dnorm_mix[i], dnorm_mlp[i] = st_mix[2:3], st_mlp[2:3]

    single = {"attn_w_in": (attn_w_in, m_attn_w_in, v_attn_w_in), "attn_w_out": (attn_w_out, m_attn_w_out, v_attn_w_out),
              "hgrn_w_in": (hgrn_w_in, m_hgrn_w_in, v_hgrn_w_in), "hgrn_w_out": (hgrn_w_out, m_hgrn_w_out, v_hgrn_w_out)}
    big = {nm: _sum_adamw(shares[nm], w[0], m[0], v[0], f"adamw_{nm}") for nm, (w, m, v) in single.items()}
    big["mlp_w1"] = _sum_adamw_layers([shares["mlp_w1_0"], shares["mlp_w1_1"]], mlp_w1, m_mlp_w1, v_mlp_w1, "adamw_mlp_w1")
    big["mlp_w2"] = _sum_adamw_layers([shares["mlp_w2_0"], shares["mlp_w2_1"]], mlp_w2, m_mlp_w2, v_mlp_w2, "adamw_mlp_w2")

    small_w = [mod_b, norm_mix, norm_mlp, attn_q_gain, attn_k_gain, attn_sinks, hgrn_o_gain, hgrn_lb_logits]
    small_m = [m_mod_b, m_norm_mix, m_norm_mlp, m_attn_q_gain, m_attn_k_gain, m_attn_sinks, m_hgrn_o_gain, m_hgrn_lb_logits]
    small_v = [v_mod_b, v_norm_mix, v_norm_mlp, v_attn_q_gain, v_attn_k_gain, v_attn_sinks, v_hgrn_o_gain, v_hgrn_lb_logits]
    small_g = [jnp.concatenate(dmods, axis=0), jnp.concatenate(dnorm_mix, axis=0), jnp.concatenate(dnorm_mlp, axis=0),
               d_q_gain, d_k_gain, d_sinks, d_o_gain.reshape(hgrn_o_gain.shape), d_lb_logits]
    packed_g = _pack(small_g)
    pad_rows = (-packed_g.shape[0]) % 8
    pad8 = lambda a: jnp.pad(a, ((0, pad_rows), (0, 0)))
    all_small = _exchange([("gather", pad8(packed_g), None)], vmem=True, name="gather_small_grads")[0]
    sg, sd, sm, sv = _sum_adamw(all_small, pad8(_pack(small_w)), pad8(_pack(small_m)), pad8(_pack(small_v)),
                                "adamw_small")
    small = [_unpack(t, small_w) for t in (sg, sd, sm, sv)]

    n_modb = N_MOD * D_MODEL
    dmod_all = all_small[:, :2 * n_modb // 128, :].reshape(N_DEV, 2, n_modb)
    dmod_cols = lax.dynamic_slice_in_dim(dmod_all, me * n_mod, n_mod, axis=2).transpose(1, 0, 2)
    modw = _mod_w_update(c_all.T, dmod_cols, mod_w, m_mod_w, v_mod_w, "adamw_mod_w")

    def leaf(k):
        one = lambda a: big[a][k][None]
        s = small[k]
        return [modw[k], s[0], s[1], s[2], one("attn_w_in"), one("attn_w_out"), s[3], s[4], s[5], one("hgrn_w_in"),
                one("hgrn_w_out"), s[6], s[7], big["mlp_w1"][k], big["mlp_w2"][k]]

    return (loss, dx[None], *leaf(0), *leaf(1), *leaf(2), *leaf(3))
```
